```python
import math
import jax, jax.numpy as jnp
from jax import lax
import numpy as np

D_MODEL = 1024
BATCH = 16
SEQ = 4096
DEPTH = 2
DEC_BATCH = 4
DEC_SEQ = 4096
PAST_LEN = 128

GRID_W = 64
EPS = 1e-6
SSD_HEAD_DIM = 64
D_INNER = D_MODEL // 2
SSD_HEADS = D_INNER // SSD_HEAD_DIM
SSD_GROUPS = 2
SSD_STATE = 64
SSD_CHUNK = 128
CONV_W = 4
CONV_LEFT = (CONV_W - 1) // 2
CONV_CH = D_INNER + 2 * SSD_GROUPS * SSD_STATE
DT_MIN = 0.001
DT_MAX = 0.1
NA_HEADS = 4
NA_HEAD_DIM = D_MODEL // 16
NA_WIDTH = NA_HEADS * NA_HEAD_DIM
NA_MAX_KH = 8
NA_KW = 16
POOL_WINDOWS = (2, 4, 8, 16)
POOL_GROUPS = 4
POOL_WIDTH = D_MODEL - D_INNER - NA_WIDTH
POOL_GROUP_DIM = POOL_WIDTH // POOL_GROUPS
D_MIX = D_INNER + NA_WIDTH + POOL_WIDTH
D_IN = D_INNER + CONV_CH + 2 * SSD_HEADS + 3 * NA_WIDTH + POOL_WIDTH
SPLIT_POINTS = (D_INNER, D_INNER + CONV_CH, D_INNER + CONV_CH + 2 * SSD_HEADS, D_INNER + CONV_CH + 2 * SSD_HEADS + NA_WIDTH, D_INNER + CONV_CH + 2 * SSD_HEADS + 2 * NA_WIDTH, D_INNER + CONV_CH + 2 * SSD_HEADS + 3 * NA_WIDTH)
N_MEM = 256
XA_HEADS = 4
XA_HEAD_DIM = D_MODEL // 8
XA_WIDTH = XA_HEADS * XA_HEAD_DIM
N_EXPERT_GROUPS = 4
EXPERTS_PER_GROUP = 8
N_EXPERTS = N_EXPERT_GROUPS * EXPERTS_PER_GROUP
TOP_K = 2
D_EXPERT = D_MODEL // 4
MOE_BLOCK = 128

kernel_name = 'hybrid_ssd_natten_pool_hmoe_encoder'


def rms_norm(x, g):
    xf = x.astype(jnp.float32)
    y = xf * lax.rsqrt(jnp.mean(xf * xf, axis=-1, keepdims=True) + EPS)
    return (y * g.astype(jnp.float32)).astype(x.dtype)


def centred_depthwise_conv(u, w, b):
    T = u.shape[1]
    up = jnp.pad(u, ((0, 0), (CONV_LEFT, CONV_W - 1 - CONV_LEFT), (0, 0)))
    out = b
    for j in range(CONV_W):
        out = out + up[:, j:j + T] * w[j]
    return out


def ssd_chunked_scan(xh, dt, a, bm, cm):
    b, T, H, P = xh.shape
    G, N = bm.shape[2], bm.shape[3]
    R = H // G
    L = SSD_CHUNK
    nc = T // L
    la = (dt * a).reshape(b, nc, L, G, R)
    xdt = (xh.astype(jnp.float32) * dt[..., None]).reshape(b, nc, L, G, R, P)
    bc = bm.astype(jnp.float32).reshape(b, nc, L, G, N)
    cc = cm.astype(jnp.float32).reshape(b, nc, L, G, N)
    cs = jnp.cumsum(la, axis=2)
    tri = jnp.tril(jnp.ones((L, L), dtype=bool))
    seg = cs[:, :, :, None] - cs[:, :, None, :]
    decay = jnp.exp(jnp.where(tri[None, None, :, :, None, None], seg, -jnp.inf))
    cb = jnp.einsum('bclgn,bcsgn->bclsg', cc, bc)
    y_diag = jnp.einsum('bclsg,bclsgr,bcsgrp->bclgrp', cb, decay, xdt)
    decay_end = jnp.exp(cs[:, :, -1:] - cs)
    states = jnp.einsum('bclgn,bclgr,bclgrp->bcgrpn', bc, decay_end, xdt)
    chunk_decay = jnp.exp(cs[:, :, -1])

    def step(s, inp):
        st, dec = inp
        return s * dec[..., None, None] + st, s

    init = jnp.zeros((b, G, R, P, N), jnp.float32)
    _, s_in = lax.scan(step, init, (jnp.moveaxis(states, 1, 0), jnp.moveaxis(chunk_decay, 1, 0)))
    s_in = jnp.moveaxis(s_in, 0, 1)
    y_off = jnp.einsum('bclgn,bcgrpn,bclgr->bclgrp', cc, s_in, jnp.exp(cs))
    return (y_diag + y_off).reshape(b, T, H, P)


def ssd_heads(z, xbc_raw, dt_raw, conv_w, conv_b, dt_bias, a_log, d_skip, ssd_norm):
    b, T, _ = z.shape
    gn = SSD_GROUPS * SSD_STATE
    xbc = jax.nn.silu(centred_depthwise_conv(xbc_raw, conv_w, conv_b))
    xs = xbc[..., :D_INNER].reshape(b, T, SSD_HEADS, SSD_HEAD_DIM)
    bm = xbc[..., D_INNER:D_INNER + gn].reshape(b, T, SSD_GROUPS, SSD_STATE)
    cm = xbc[..., D_INNER + gn:].reshape(b, T, SSD_GROUPS, SSD_STATE)
    dtf = dt_raw.astype(jnp.float32)
    dt_fwd = jax.nn.softplus(dtf[..., :SSD_HEADS] + dt_bias[0].astype(jnp.float32))
    dt_bwd = jax.nn.softplus(dtf[..., SSD_HEADS:] + dt_bias[1].astype(jnp.float32))
    a_fwd = -jnp.exp(a_log[0].astype(jnp.float32))
    a_bwd = -jnp.exp(a_log[1].astype(jnp.float32))
    rev = lambda u: jnp.flip(u, axis=1)
    y_f = ssd_chunked_scan(xs, dt_fwd, a_fwd, bm, cm)
    y_b = rev(ssd_chunked_scan(rev(xs), rev(dt_bwd), a_bwd, rev(bm), rev(cm)))
    y = y_f + y_b + d_skip.astype(jnp.float32)[:, None] * xs.astype(jnp.float32)
    y = y.reshape(b, T, D_INNER) * jax.nn.silu(z.astype(jnp.float32))
    y = y.reshape(b, T, SSD_GROUPS, D_INNER // SSD_GROUPS)
    y = y * lax.rsqrt(jnp.mean(y * y, axis=-1, keepdims=True) + EPS)
    return (y.reshape(b, T, D_INNER) * ssd_norm.astype(jnp.float32)).astype(z.dtype)


def neighbourhood_attention(q, k, v, q_norm, k_norm, rpb):
    b, T, _ = q.shape
    R = T // GRID_W
    KH = min(NA_MAX_KH, R)
    heads = lambda u: u.reshape(b, R, GRID_W, NA_HEADS, NA_HEAD_DIM)
    qh = rms_norm(heads(q), q_norm)
    kh = rms_norm(heads(k), k_norm)
    vh = heads(v)
    rows = jnp.arange(R)
    row_start = jnp.clip(rows - NA_MAX_KH // 2, 0, R - KH)
    row_idx = row_start[:, None] + jnp.arange(KH)[None, :]
    k_g = kh[:, row_idx]
    v_g = vh[:, row_idx]
    s = jnp.einsum('brqhd,brikhd->bhrqik', qh, k_g).astype(jnp.float32) * (NA_HEAD_DIM ** -0.5)
    cols = jnp.arange(GRID_W)
    col_start = jnp.clip(cols - NA_KW // 2, 0, GRID_W - NA_KW)
    col_mask = (cols[None, :] >= col_start[:, None]) & (cols[None, :] < col_start[:, None] + NA_KW)
    dr = row_idx - rows[:, None] + (NA_MAX_KH - 1)
    dc = jnp.clip(cols[None, :] - cols[:, None] + (NA_KW - 1), 0, 2 * NA_KW - 2)
    bias = rpb.astype(jnp.float32)[:, dr[:, None, :, None], dc[None, :, None, :]]
    s = jnp.where(col_mask[:, None, :], s + bias[None], -jnp.inf)
    p = jax.nn.softmax(s, axis=(-2, -1))
    o = jnp.einsum('bhrqik,brikhd->brqhd', p.astype(v.dtype), v_g)
    return o.reshape(b, T, NA_WIDTH)


def multiscale_pool(u, pool_w, pool_scale):
    b, T, _ = u.shape
    uf = u.astype(jnp.float32).reshape(b, T, POOL_GROUPS, POOL_GROUP_DIM)
    csum = jnp.concatenate([jnp.zeros((b, 1, POOL_GROUPS, POOL_GROUP_DIM), jnp.float32), jnp.cumsum(uf, axis=1)], axis=1)
    t = jnp.arange(T)
    outs = []
    for gi, w in enumerate(POOL_WINDOWS):
        lo = jnp.clip(t - w // 2, 0, T)
        hi = jnp.clip(t + w // 2, 0, T)
        win_sum = csum[:, hi, gi] - csum[:, lo, gi]
        outs.append(win_sum / (hi - lo).astype(jnp.float32)[None, :, None] - uf[:, :, gi])
    d = jnp.stack(outs, axis=2)
    y = jnp.einsum('btgc,gcd->btgd', d, pool_w.astype(jnp.float32)).reshape(b, T, POOL_WIDTH)
    return (y * pool_scale.astype(jnp.float32)).astype(u.dtype)


def memory_cross_attention(hn, memn, w_xq, w_xkv, xq_norm, xk_norm, w_xo):
    b, T, _ = hn.shape
    M = memn.shape[1]
    q = rms_norm((hn @ w_xq).reshape(b, T, XA_HEADS, XA_HEAD_DIM), xq_norm)
    kv = (memn @ w_xkv).reshape(b, M, 2, XA_HEADS, XA_HEAD_DIM)
    k = rms_norm(kv[:, :, 0], xk_norm)
    v = kv[:, :, 1]
    s = jnp.einsum('bthd,bmhd->bhtm', q, k).astype(jnp.float32) * (XA_HEAD_DIM ** -0.5)
    p = jax.nn.softmax(s, axis=-1)
    o = jnp.einsum('bhtm,bmhd->bthd', p.astype(v.dtype), v).reshape(b, T, XA_WIDTH)
    return o @ w_xo


def hier_moe(hn, w_router_group, b_router_group, w_router_expert, b_router_expert, w_e_gate, w_e_up, w_e_down):
    b, T, D = hn.shape
    N = b * T
    A = N * TOP_K
    xt = hn.reshape(N, D)
    g_logits = (xt @ w_router_group).astype(jnp.float32) + b_router_group.astype(jnp.float32)
    g_sel = jnp.argmax(g_logits, axis=-1)
    g_gate = jnp.max(jax.nn.softmax(g_logits, axis=-1), axis=-1, keepdims=True)
    e_logits = ((xt @ w_router_expert).astype(jnp.float32) + b_router_expert.astype(jnp.float32)).reshape(N, N_EXPERT_GROUPS, EXPERTS_PER_GROUP)
    e_in = e_logits[jnp.arange(N), g_sel]
    top_val, top_loc = lax.top_k(e_in, TOP_K)
    gates = jax.nn.softmax(top_val, axis=-1) * g_gate
    e_id = (g_sel[:, None] * EXPERTS_PER_GROUP + top_loc).astype(jnp.int32)
    flat_e = e_id.reshape(A)
    flat_tok = jnp.repeat(jnp.arange(N, dtype=jnp.int32), TOP_K)
    flat_w = gates.reshape(A)
    order = jnp.argsort(flat_e)
    se, stok, sw = flat_e[order], flat_tok[order], flat_w[order]
    sizes = jnp.bincount(flat_e, length=N_EXPERTS)
    start = jnp.cumsum(sizes) - sizes
    psizes = (sizes + MOE_BLOCK - 1) // MOE_BLOCK * MOE_BLOCK
    pend = jnp.cumsum(psizes)
    pstart = pend - psizes
    dest = pstart[se] + (jnp.arange(A) - start[se])
    nb = (A + N_EXPERTS * (MOE_BLOCK - 1) + MOE_BLOCK - 1) // MOE_BLOCK
    P = nb * MOE_BLOCK
    tok_buf = jnp.full((P,), N, jnp.int32).at[dest].set(stok)
    w_buf = jnp.zeros((P,), jnp.float32).at[dest].set(sw)
    blk_expert = jnp.clip(jnp.searchsorted(pend, jnp.arange(nb) * MOE_BLOCK, side='right'), 0, N_EXPERTS - 1)
    x_pad = jnp.concatenate([xt, jnp.zeros((1, D), xt.dtype)], axis=0)

    def expert_block(args):
        idx, wts, e = args
        xb = x_pad[idx]
        hb = jax.nn.silu(xb @ w_e_gate[e]) * (xb @ w_e_up[e])
        return (hb @ w_e_down[e]).astype(jnp.float32) * wts[:, None]

    yb = lax.map(expert_block, (tok_buf.reshape(nb, MOE_BLOCK), w_buf.reshape(nb, MOE_BLOCK), blk_expert))
    out = jnp.zeros((N + 1, D), jnp.float32).at[tok_buf].add(yb.reshape(P, D))[:N]
    return out.reshape(b, T, D).astype(hn.dtype)


def encoder_layer(x, mem, norm_mix, w_in, conv_w, conv_b, dt_bias, a_log, d_skip, ssd_norm, na_q_norm, na_k_norm, na_rpb, pool_w, pool_scale, w_out, norm_xa, norm_mem, w_xq, w_xkv, xq_norm, xk_norm, w_xo, norm_ffn, w_router_group, b_router_group, w_router_expert, b_router_expert, w_e_gate, w_e_up, w_e_down):
    h = rms_norm(x, norm_mix)
    proj = h @ w_in
    z, xbc, dt_raw, q, k, v, u = jnp.split(proj, SPLIT_POINTS, axis=-1)
    y_ssd = ssd_heads(z, xbc, dt_raw, conv_w, conv_b, dt_bias, a_log, d_skip, ssd_norm)
    y_na = neighbourhood_attention(q, k, v, na_q_norm, na_k_norm, na_rpb)
    y_pool = multiscale_pool(u, pool_w, pool_scale)
    mix = jnp.concatenate([y_ssd, y_na.astype(x.dtype), y_pool], axis=-1) @ w_out
    x = x + mix.astype(x.dtype)
    x = x + memory_cross_attention(rms_norm(x, norm_xa), rms_norm(mem, norm_mem), w_xq, w_xkv, xq_norm, xk_norm, w_xo).astype(x.dtype)
    x = x + hier_moe(rms_norm(x, norm_ffn), w_router_group, b_router_group, w_router_expert, b_router_expert, w_e_gate, w_e_up, w_e_down)
    return x


def setup_inputs(seed: int = 0) -> dict:
    key = jax.random.key(seed)
    ks = jax.random.split(key, 40)
    f32 = jnp.float32

    def nrm(k, shape, scale):
        return jax.random.normal(k, shape, f32) * scale

    def gain(k, shape):
        return 1.0 + nrm(k, shape, 0.02)

    dt0 = jnp.exp(jax.random.uniform(ks[8], (DEPTH, 2, SSD_HEADS), f32, math.log(DT_MIN), math.log(DT_MAX)))
    dt_bias = dt0 + jnp.log(-jnp.expm1(-dt0))
    a_log = jnp.log(jax.random.uniform(ks[9], (DEPTH, 2, SSD_HEADS), f32, 1.0, 16.0))
    return {
        'x_prompt': nrm(ks[0], (BATCH, SEQ, D_MODEL), 1.0),
        'x_sample': nrm(ks[1], (DEC_BATCH, DEC_SEQ, D_MODEL), 1.0),
        'mem_prompt': nrm(ks[2], (BATCH, N_MEM, D_MODEL), 1.0),
        'mem_sample': nrm(ks[3], (DEC_BATCH, N_MEM, D_MODEL), 1.0),
        'norm_mix': gain(ks[4], (DEPTH, D_MODEL)),
        'w_in': nrm(ks[5], (DEPTH, D_MODEL, D_IN), D_MODEL ** -0.5),
        'conv_w': nrm(ks[6], (DEPTH, CONV_W, CONV_CH), CONV_W ** -0.5),
        'conv_b': nrm(ks[7], (DEPTH, CONV_CH), 0.02),
        'dt_bias': dt_bias,
        'a_log': a_log,
        'd_skip': 1.0 + nrm(ks[10], (DEPTH, SSD_HEADS), 0.1),
        'ssd_norm': gain(ks[11], (DEPTH, D_INNER)),
        'na_q_norm': gain(ks[12], (DEPTH, NA_HEAD_DIM)),
        'na_k_norm': gain(ks[13], (DEPTH, NA_HEAD_DIM)),
        'na_rpb': nrm(ks[14], (DEPTH, NA_HEADS, 2 * NA_MAX_KH - 1, 2 * NA_KW - 1), 0.1),
        'pool_w': nrm(ks[15], (DEPTH, POOL_GROUPS, POOL_GROUP_DIM, POOL_GROUP_DIM), POOL_GROUP_DIM ** -0.5),
        'pool_scale': gain(ks[16], (DEPTH, POOL_WIDTH)),
        'w_out': nrm(ks[17], (DEPTH, D_MIX, D_MODEL), D_MIX ** -0.5),
        'norm_xa': gain(ks[18], (DEPTH, D_MODEL)),
        'norm_mem': gain(ks[19], (DEPTH, D_MODEL)),
        'w_xq': nrm(ks[20], (DEPTH, D_MODEL, XA_WIDTH), D_MODEL ** -0.5),
        'w_xkv': nrm(ks[21], (DEPTH, D_MODEL, 2 * XA_WIDTH), D_MODEL ** -0.5),
        'xq_norm': gain(ks[22], (DEPTH, XA_HEAD_DIM)),
        'xk_norm': gain(ks[23], (DEPTH, XA_HEAD_DIM)),
        'w_xo': nrm(ks[24], (DEPTH, XA_WIDTH, D_MODEL), XA_WIDTH ** -0.5),
        'norm_ffn': gain(ks[25], (DEPTH, D_MODEL)),
        'w_router_group': nrm(ks[26], (DEPTH, D_MODEL, N_EXPERT_GROUPS), D_MODEL ** -0.5),
        'b_router_group': nrm(ks[27], (DEPTH, N_EXPERT_GROUPS), 0.01),
        'w_router_expert': nrm(ks[28], (DEPTH, D_MODEL, N_EXPERTS), D_MODEL ** -0.5),
        'b_router_expert': nrm(ks[29], (DEPTH, N_EXPERTS), 0.01),
        'w_e_gate': nrm(ks[30], (DEPTH, N_EXPERTS, D_MODEL, D_EXPERT), D_MODEL ** -0.5),
        'w_e_up': nrm(ks[31], (DEPTH, N_EXPERTS, D_MODEL, D_EXPERT), D_MODEL ** -0.5),
        'w_e_down': nrm(ks[32], (DEPTH, N_EXPERTS, D_EXPERT, D_MODEL), D_EXPERT ** -0.5),
    }


def reference(x_prompt, x_sample, mem_prompt, mem_sample, norm_mix, w_in, conv_w, conv_b, dt_bias, a_log, d_skip, ssd_norm, na_q_norm, na_k_norm, na_rpb, pool_w, pool_scale, w_out, norm_xa, norm_mem, w_xq, w_xkv, xq_norm, xk_norm, w_xo, norm_ffn, w_router_group, b_router_group, w_router_expert, b_router_expert, w_e_gate, w_e_up, w_e_down):
    layer_params = (norm_mix, w_in, conv_w, conv_b, dt_bias, a_log, d_skip, ssd_norm, na_q_norm, na_k_norm, na_rpb, pool_w, pool_scale, w_out, norm_xa, norm_mem, w_xq, w_xkv, xq_norm, xk_norm, w_xo, norm_ffn, w_router_group, b_router_group, w_router_expert, b_router_expert, w_e_gate, w_e_up, w_e_down)

    def trunk(x, mem):
        for layer in range(DEPTH):
            x = encoder_layer(x, mem, *[p[layer] for p in layer_params])
        return x

    y_prompt = trunk(x_prompt, mem_prompt)
    y_sample = trunk(x_sample, mem_sample)
    return (y_prompt, y_sample)
```

```python
import functools

import jax
import jax.numpy as jnp
from jax import lax
from jax.experimental import pallas as pl
from jax.experimental.pallas import tpu as pltpu

F32 = jnp.float32
BF16 = jnp.bfloat16
HIGHEST = lax.Precision.HIGHEST

D_MODEL = 1024
GRID_W = 64
EPS = 1e-6
SSD_HEAD_DIM = 64
D_INNER = D_MODEL // 2
SSD_HEADS = D_INNER // SSD_HEAD_DIM
SSD_GROUPS = 2
SSD_STATE = 64
SSD_CHUNK = 128
CONV_W = 4
CONV_CH = D_INNER + 2 * SSD_GROUPS * SSD_STATE
NA_HEADS = 4
NA_HEAD_DIM = D_MODEL // 16
NA_WIDTH = NA_HEADS * NA_HEAD_DIM
NA_MAX_KH = 8
NA_KW = 16
POOL_WINDOWS = (2, 4, 8, 16)
POOL_GROUPS = 4
POOL_WIDTH = D_MODEL - D_INNER - NA_WIDTH
POOL_GROUP_DIM = POOL_WIDTH // POOL_GROUPS
XA_HEADS = 4
XA_HEAD_DIM = D_MODEL // 8
XA_WIDTH = XA_HEADS * XA_HEAD_DIM
N_EXPERT_GROUPS = 4
EXPERTS_PER_GROUP = 8
N_EXPERTS = N_EXPERT_GROUPS * EXPERTS_PER_GROUP
D_EXPERT = D_MODEL // 4

LANES = 128
BF16_SUBLANES = 16
VMEM_LIMIT_BYTES = 56 * 1024 * 1024

TOKEN_TILE = 512
NA_QUERY_ROWS = 8
NA_KEY_ROWS = 16
EXPERT_BLOCK = 256
DT_PAD = LANES
NEG_BIG = -1e30


def _cparams(sem):
    return pltpu.CompilerParams(dimension_semantics=sem, vmem_limit_bytes=VMEM_LIMIT_BYTES)


def _sigmoid(x):
    return 1.0 / (1.0 + jnp.exp(-x))


def _silu(x):
    return x * _sigmoid(x)


def _softplus(x):
    return jnp.maximum(x, 0.0) + jnp.log(1.0 + jnp.exp(-jnp.abs(x)))


def _full(shape):
    n = len(shape)
    return pl.BlockSpec(shape, lambda *_: (0,) * n)


def _inproj_kernel(x_ref, g_ref, w_ref, z_ref, xbc_ref, qkv_ref, u_ref, dt_ref):
    x = x_ref[...]
    ms = jnp.mean(x * x, axis=-1, keepdims=True)
    h = (x * lax.rsqrt(ms + EPS) * g_ref[...]).astype(BF16)
    o = 0
    for ref in (z_ref, xbc_ref, qkv_ref, u_ref, dt_ref):
        w = ref.shape[-1]
        ref[...] = jnp.dot(h, w_ref[:, o:o + w], preferred_element_type=F32).astype(ref.dtype)
        o += w


def _inproj(x2d, gain, w_cat):
    n = x2d.shape[0]
    tm = TOKEN_TILE
    widths = (D_INNER, CONV_CH, 3 * NA_WIDTH, POOL_WIDTH, DT_PAD)
    dtypes = (BF16, BF16, BF16, BF16, F32)
    return pl.pallas_call(
        _inproj_kernel,
        grid=(n // tm,),
        in_specs=[pl.BlockSpec((tm, D_MODEL), lambda i: (i, 0)),
                  _full((1, D_MODEL)),
                  _full(w_cat.shape)],
        out_specs=[pl.BlockSpec((tm, w), lambda i: (i, 0)) for w in widths],
        out_shape=[jax.ShapeDtypeStruct((n, w), d) for w, d in zip(widths, dtypes)],
        compiler_params=_cparams(("arbitrary",)),
    )(x2d, gain.reshape(1, D_MODEL), w_cat)


def _ssd_kernel(xc_ref, xp_ref, xn_ref, dt_ref, z_ref, cw_ref, cb_ref, dtb_ref, alog_ref, dsk_ref, nrm_ref,
                y_ref, state_ref, yf_ref, *, nc):
    L = SSD_CHUNK
    P = SSD_HEAD_DIM
    NS = SSD_STATE
    HG = SSD_HEADS // SSD_GROUPS
    j = pl.program_id(1)
    c = jnp.where(j < nc, j, 2 * nc - 1 - j)

    cur = xc_ref[0].astype(F32)
    prev = xp_ref[0].astype(F32)
    nxt = xn_ref[0].astype(F32)
    has_prev = (c > 0).astype(F32)
    has_next = (c < nc - 1).astype(F32)
    p_last = prev[BF16_SUBLANES - 1:BF16_SUBLANES, :] * has_prev
    n0 = nxt[0:1, :] * has_next
    n1 = nxt[1:2, :] * has_next
    row = lax.broadcasted_iota(jnp.int32, (L, 1), 0)
    um1 = jnp.where(row == 0, p_last, pltpu.roll(cur, 1, 0))
    up1 = jnp.where(row == L - 1, n0, pltpu.roll(cur, L - 1, 0))
    up2 = jnp.where(row == L - 2, n0, jnp.where(row == L - 1, n1, pltpu.roll(cur, L - 2, 0)))
    cw = cw_ref[...]
    acc = cb_ref[...] + um1 * cw[0:1, :] + cur * cw[1:2, :] + up1 * cw[2:3, :] + up2 * cw[3:4, :]
    xbc = _silu(acc)
    xs = xbc[:, :D_INNER]
    gn = SSD_GROUPS * NS
    lane1 = lax.broadcasted_iota(jnp.int32, (1, LANES), 1)
    lo_half = lane1 < P
    bfull = xbc[:, D_INNER:D_INNER + gn].astype(BF16)
    cfull = xbc[:, D_INNER + gn:D_INNER + 2 * gn]
    cg = [jnp.where(lane1 // NS == g, cfull, 0.0).astype(BF16) for g in range(SSD_GROUPS)]
    cb_mat = [lax.dot_general(cg[g], bfull, (((1,), (1,)), ((), ())), preferred_element_type=F32)
              for g in range(SSD_GROUPS)]

    dt = _softplus(dt_ref[0] + dtb_ref[...])
    la = dt * (-jnp.exp(alog_ref[...]))
    ti = lax.broadcasted_iota(jnp.int32, (L, L), 0)
    si = lax.broadcasted_iota(jnp.int32, (L, L), 1)

    def scan_chunk(direction):
        if direction == 0:
            mask = ti >= si
            edge = L - 1
        else:
            mask = si >= ti
            edge = 0
        tri = mask.astype(F32)
        csum = jnp.dot(tri, la, precision=HIGHEST, preferred_element_type=F32)
        csum_t = csum.T
        tot = csum[edge:edge + 1, :]
        e_tot = jnp.exp(tot)
        e_in = jnp.exp(csum)
        e_out = jnp.exp(tot - csum)
        ys = []
        for g in range(SSD_GROUPS):
            s_old = state_ref[g]
            y_off = lax.dot_general(cg[g], s_old.astype(BF16), (((1,), (1,)), ((), ())),
                                    preferred_element_type=F32)
            xw = []
            for pr in range(HG // 2):
                h0 = g * HG + 2 * pr
                l0 = direction * SSD_HEADS + h0

                def col(a, l0=l0):
                    return jnp.where(lo_half, a[:, l0:l0 + 1], a[:, l0 + 1:l0 + 2])

                xdt = xs[:, h0 * P:(h0 + 2) * P] * col(dt)
                y_pair = y_off[:, 2 * pr * P:(2 * pr + 2) * P] * col(e_in)
                for ll, half in ((l0, lo_half), (l0 + 1, jnp.logical_not(lo_half))):
                    seg = csum[:, ll:ll + 1] - csum_t[ll:ll + 1, :]
                    dec = jnp.exp(jnp.where(mask, seg, NEG_BIG))
                    m = (cb_mat[g] * dec).astype(BF16)
                    y_pair += jnp.dot(m, jnp.where(half, xdt, 0.0).astype(BF16), preferred_element_type=F32)
                ys.append(y_pair)
                xw.append(xdt * col(e_out))
            xw = jnp.concatenate(xw, axis=1).astype(BF16)
            s_new = lax.dot_general(xw, bfull, (((0,), (0,)), ((), ())), preferred_element_type=F32)
            s_scaled = []
            for hl in range(HG):
                lane = direction * SSD_HEADS + g * HG + hl
                s_scaled.append(s_old[hl * P:(hl + 1) * P, :] * e_tot[:, lane:lane + 1])
            state_ref[g] = jnp.concatenate(s_scaled, axis=0) + s_new
        return jnp.concatenate(ys, axis=1)

    @pl.when(jnp.logical_or(j == 0, j == nc))
    def _():
        state_ref[...] = jnp.zeros_like(state_ref)

    row0 = pl.multiple_of(c * L, L)

    @pl.when(j < nc)
    def _():
        yf_ref[pl.ds(row0, L), :] = scan_chunk(0)

    @pl.when(j >= nc)
    def _():
        y = yf_ref[pl.ds(row0, L), :] + scan_chunk(1) + dsk_ref[...] * xs
        y = y * _silu(z_ref[0].astype(F32))
        gw = D_INNER // SSD_GROUPS
        outs = []
        for g in range(SSD_GROUPS):
            yg = y[:, g * gw:(g + 1) * gw]
            outs.append(yg * lax.rsqrt(jnp.mean(yg * yg, axis=-1, keepdims=True) + EPS))
        y_ref[0] = (jnp.concatenate(outs, axis=1) * nrm_ref[...]).astype(y_ref.dtype)


def _ssd(z, xbc, dt, conv_w, conv_b, dt_bias, a_log, d_skip, ssd_norm):
    b, t, _ = z.shape
    L = SSD_CHUNK
    nc = t // L
    hb = L // BF16_SUBLANES
    nhalo = t // BF16_SUBLANES

    def chunk(j):
        return jnp.where(j < nc, j, 2 * nc - 1 - j)

    pad = DT_PAD - 2 * SSD_HEADS
    dtb = jnp.pad(dt_bias.reshape(1, -1).astype(F32), ((0, 0), (0, pad)))
    alog = jnp.pad(a_log.reshape(1, -1).astype(F32), ((0, 0), (0, pad)))
    dsk = jnp.repeat(d_skip.astype(F32), SSD_HEAD_DIM).reshape(1, D_INNER)
    return pl.pallas_call(
        functools.partial(_ssd_kernel, nc=nc),
        grid=(b, 2 * nc),
        in_specs=[
            pl.BlockSpec((1, L, CONV_CH), lambda i, j: (i, chunk(j), 0)),
            pl.BlockSpec((1, BF16_SUBLANES, CONV_CH), lambda i, j: (i, jnp.maximum(chunk(j) * hb - 1, 0), 0)),
            pl.BlockSpec((1, BF16_SUBLANES, CONV_CH),
                         lambda i, j: (i, jnp.minimum((chunk(j) + 1) * hb, nhalo - 1), 0)),
            pl.BlockSpec((1, L, DT_PAD), lambda i, j: (i, chunk(j), 0)),
            pl.BlockSpec((1, L, D_INNER), lambda i, j: (i, chunk(j), 0)),
            _full((CONV_W, CONV_CH)), _full((1, CONV_CH)), _full((1, DT_PAD)), _full((1, DT_PAD)),
            _full((1, D_INNER)), _full((1, D_INNER)),
        ],
        out_specs=pl.BlockSpec((1, L, D_INNER), lambda i, j: (i, jnp.where(j < nc, nc - 1, 2 * nc - 1 - j), 0)),
        out_shape=jax.ShapeDtypeStruct((b, t, D_INNER), BF16),
        scratch_shapes=[pltpu.VMEM((SSD_GROUPS, (SSD_HEADS // SSD_GROUPS) * SSD_HEAD_DIM, LANES), F32),
                        pltpu.VMEM((t, D_INNER), F32)],
        compiler_params=_cparams(("arbitrary", "arbitrary")),
    )(xbc, xbc, xbc, dt, z, conv_w.astype(F32), conv_b.reshape(1, CONV_CH).astype(F32), dtb, alog, dsk,
      ssd_norm.reshape(1, D_INNER).astype(F32))


def _na_bias(rpb, t):
    r = t // GRID_W
    kh = min(NA_MAX_KH, r)
    nrb = r // NA_QUERY_ROWS
    rows = jnp.arange(r)
    row_start = jnp.clip(rows - NA_MAX_KH // 2, 0, r - kh)
    r0 = jnp.arange(nrb) * NA_QUERY_ROWS
    kr0 = jnp.clip(r0 - NA_MAX_KH // 2, 0, r - NA_KEY_ROWS)
    qrow = r0[:, None] + jnp.arange(NA_QUERY_ROWS)[None, :]
    krow = kr0[:, None] + jnp.arange(NA_KEY_ROWS)[None, :]
    rs = row_start[qrow]
    row_ok = (krow[:, None, :] >= rs[:, :, None]) & (krow[:, None, :] < rs[:, :, None] + kh)
    dr = jnp.clip(krow[:, None, :] - qrow[:, :, None] + (NA_MAX_KH - 1), 0, 2 * NA_MAX_KH - 2)
    cols = jnp.arange(GRID_W)
    col_start = jnp.clip(cols - NA_KW // 2, 0, GRID_W - NA_KW)
    col_ok = (cols[None, :] >= col_start[:, None]) & (cols[None, :] < col_start[:, None] + NA_KW)
    dc = jnp.clip(cols[None, :] - cols[:, None] + (NA_KW - 1), 0, 2 * NA_KW - 2)
    bias = rpb.astype(F32)[:, dr[:, :, None, :, None], dc[None, None, :, None, :]]
    ok = row_ok[:, :, None, :, None] & col_ok[None, None, :, None, :]
    bias = jnp.where(ok[None], bias, NEG_BIG)
    bias = jnp.moveaxis(bias, 0, 1)
    return bias.reshape(nrb, NA_HEADS, NA_QUERY_ROWS * GRID_W, NA_KEY_ROWS * GRID_W).astype(BF16)


def _natten_kernel(qkv_ref, bias_ref, qg_ref, kg_ref, seg_ref, o_ref, *, grid_rows):
    nq = NA_QUERY_ROWS * GRID_W
    nk = NA_KEY_ROWS * GRID_W
    rb = pl.program_id(0)
    r0 = rb * NA_QUERY_ROWS
    kr0 = jnp.clip(r0 - NA_MAX_KH // 2, 0, grid_rows - NA_KEY_ROWS)
    q0 = pl.multiple_of(r0 * GRID_W, nq)
    k0 = pl.multiple_of(kr0 * GRID_W, NA_MAX_KH // 2 * GRID_W)
    q = qkv_ref[0, pl.ds(q0, nq), 0:NA_WIDTH].astype(F32)
    k = qkv_ref[0, pl.ds(k0, nk), NA_WIDTH:2 * NA_WIDTH].astype(F32)
    v = qkv_ref[0, pl.ds(k0, nk), 2 * NA_WIDTH:3 * NA_WIDTH]
    seg = seg_ref[...]
    qms = jnp.dot(q * q, seg, precision=HIGHEST, preferred_element_type=F32)
    kms = jnp.dot(k * k, seg, precision=HIGHEST, preferred_element_type=F32)
    qn = q * lax.rsqrt(qms + EPS) * qg_ref[...]
    kn = (k * lax.rsqrt(kms + EPS) * kg_ref[...]).astype(BF16)
    scale = NA_HEAD_DIM ** -0.5
    lane_h = lax.broadcasted_iota(jnp.int32, (1, NA_WIDTH), 1) // NA_HEAD_DIM
    acc = jnp.zeros((nq, NA_WIDTH), F32)
    for h in range(NA_HEADS):
        hm = lane_h == h
        s = lax.dot_general(jnp.where(hm, qn, 0.0).astype(BF16), kn, (((1,), (1,)), ((), ())),
                            preferred_element_type=F32)
        s = s * scale + bias_ref[0, h].astype(F32)
        p = jnp.exp(s - jnp.max(s, axis=-1, keepdims=True))
        l = jnp.sum(p, axis=-1, keepdims=True)
        o = jnp.dot(p.astype(BF16), v, preferred_element_type=F32)
        acc += jnp.where(hm, o / l, 0.0)
    o_ref[0] = acc.astype(o_ref.dtype)


def _natten(qkv, bias, q_norm, k_norm):
    b, t, _ = qkv.shape
    r = t // GRID_W
    nrb = r // NA_QUERY_ROWS
    nq = NA_QUERY_ROWS * GRID_W
    nk = NA_KEY_ROWS * GRID_W
    head = jnp.arange(NA_WIDTH) // NA_HEAD_DIM
    seg = (head[:, None] == head[None, :]).astype(F32) / NA_HEAD_DIM
    return pl.pallas_call(
        functools.partial(_natten_kernel, grid_rows=r),
        grid=(nrb, b),
        in_specs=[pl.BlockSpec((1, t, 3 * NA_WIDTH), lambda i, j: (j, 0, 0)),
                  pl.BlockSpec((1, NA_HEADS, nq, nk), lambda i, j: (i, 0, 0, 0)),
                  _full((1, NA_WIDTH)), _full((1, NA_WIDTH)), _full((NA_WIDTH, NA_WIDTH))],
        out_specs=pl.BlockSpec((1, nq, NA_WIDTH), lambda i, j: (j, i, 0)),
        out_shape=jax.ShapeDtypeStruct((b, t, NA_WIDTH), BF16),
        compiler_params=_cparams(("arbitrary", "arbitrary")),
    )(qkv, bias, jnp.tile(q_norm.astype(F32), NA_HEADS).reshape(1, NA_WIDTH),
      jnp.tile(k_norm.astype(F32), NA_HEADS).reshape(1, NA_WIDTH), seg)


def _memkv_kernel(m_ref, g_ref, w_ref, kg_ref, k_ref, v_ref):
    x = m_ref[0]
    ms = jnp.mean(x * x, axis=-1, keepdims=True)
    h = (x * lax.rsqrt(ms + EPS) * g_ref[...]).astype(BF16)
    kv = jnp.dot(h, w_ref[...], preferred_element_type=F32)
    ks = []
    for hd in range(XA_HEADS):
        kh = kv[:, hd * XA_HEAD_DIM:(hd + 1) * XA_HEAD_DIM]
        ks.append(kh * lax.rsqrt(jnp.mean(kh * kh, axis=-1, keepdims=True) + EPS))
    k_ref[0] = (jnp.concatenate(ks, axis=1) * kg_ref[...]).astype(k_ref.dtype)
    v_ref[0] = kv[:, XA_WIDTH:].astype(v_ref.dtype)


def _memkv(mem, norm_mem, w_xkv, xk_norm):
    b, m, _ = mem.shape
    return pl.pallas_call(
        _memkv_kernel,
        grid=(b,),
        in_specs=[pl.BlockSpec((1, m, D_MODEL), lambda i: (i, 0, 0)),
                  _full((1, D_MODEL)), _full((D_MODEL, 2 * XA_WIDTH)), _full((1, XA_WIDTH))],
        out_specs=[pl.BlockSpec((1, m, XA_WIDTH), lambda i: (i, 0, 0))] * 2,
        out_shape=[jax.ShapeDtypeStruct((b, m, XA_WIDTH), BF16)] * 2,
        compiler_params=_cparams(("arbitrary",)),
    )(mem, norm_mem.reshape(1, D_MODEL).astype(F32), w_xkv,
      jnp.tile(xk_norm.astype(F32), XA_HEADS).reshape(1, XA_WIDTH))


def _mixer_kernel(x_ref, ys_ref, yn_ref, u_ref, up_ref, un_ref, k_ref, v_ref,
                  pw_ref, psc_ref, wo_ref, gxa_ref, wq_ref, qg_ref, wxo_ref, gff_ref,
                  wrh_ref, wrl_ref, br_ref, ltri_ref,
                  x2_ref, hf_ref, meta_ref, gate_ref, cnt_ref, carry_ref, *, seq):
    tm = x_ref.shape[1]
    halo = BF16_SUBLANES
    bi = pl.program_id(0)
    i = pl.program_id(1)
    nt = pl.num_programs(1)

    @pl.when(jnp.logical_and(bi == 0, i == 0))
    def _():
        carry_ref[...] = jnp.zeros_like(carry_ref)

    u = u_ref[0].astype(F32)
    up = up_ref[0].astype(F32) * (i > 0).astype(F32)
    un = un_ref[0].astype(F32) * (i < nt - 1).astype(F32)
    cat = jnp.concatenate([up, u, un], axis=0)
    n = tm + 2 * halo

    def sh(a, k):
        return pltpu.roll(a, (-k) % n, 0)

    a2 = cat + sh(cat, -1)
    a4 = sh(a2, 1) + sh(a2, -1)
    a8 = sh(a4, 2) + sh(a4, -2)
    a16 = sh(a8, 4) + sh(a8, -4)
    lane_g = lax.broadcasted_iota(jnp.int32, (1, POOL_WIDTH), 1) // POOL_GROUP_DIM
    wsum = jnp.where(lane_g == 0, a2, jnp.where(lane_g == 1, a4, jnp.where(lane_g == 2, a8, a16)))
    wsum = wsum[halo:halo + tm, :]
    half = jnp.where(lane_g == 0, POOL_WINDOWS[0] // 2,
                     jnp.where(lane_g == 1, POOL_WINDOWS[1] // 2,
                               jnp.where(lane_g == 2, POOL_WINDOWS[2] // 2, POOL_WINDOWS[3] // 2)))
    tpos = i * tm + lax.broadcasted_iota(jnp.int32, (tm, 1), 0)
    cnt = (jnp.minimum(tpos + half, seq) - jnp.maximum(tpos - half, 0)).astype(F32)
    d = wsum / cnt - u
    ypool = jnp.dot(d.astype(BF16), pw_ref[...], preferred_element_type=F32) * psc_ref[...]

    mix = jnp.dot(ys_ref[0], wo_ref[0:D_INNER, :], preferred_element_type=F32)
    mix += jnp.dot(yn_ref[0], wo_ref[D_INNER:D_INNER + NA_WIDTH, :], preferred_element_type=F32)
    mix += jnp.dot(ypool.astype(BF16), wo_ref[D_INNER + NA_WIDTH:, :], preferred_element_type=F32)
    x1 = x_ref[0] + mix

    hn = (x1 * lax.rsqrt(jnp.mean(x1 * x1, axis=-1, keepdims=True) + EPS) * gxa_ref[...]).astype(BF16)
    q = jnp.dot(hn, wq_ref[...], preferred_element_type=F32)
    kk = k_ref[0]
    vv = v_ref[0]
    scale = XA_HEAD_DIM ** -0.5
    outs = []
    for hd in range(XA_HEADS):
        sl = slice(hd * XA_HEAD_DIM, (hd + 1) * XA_HEAD_DIM)
        qh = q[:, sl]
        qh = (qh * lax.rsqrt(jnp.mean(qh * qh, axis=-1, keepdims=True) + EPS) * qg_ref[:, sl]).astype(BF16)
        s = lax.dot_general(qh, kk[:, sl], (((1,), (1,)), ((), ())), preferred_element_type=F32) * scale
        p = jnp.exp(s - jnp.max(s, axis=-1, keepdims=True))
        l = jnp.sum(p, axis=-1, keepdims=True)
        outs.append(jnp.dot(p.astype(BF16), vv[:, sl], preferred_element_type=F32) / l)
    att = jnp.concatenate(outs, axis=1).astype(BF16)
    x2 = x1 + jnp.dot(att, wxo_ref[...], preferred_element_type=F32)
    x2_ref[0] = x2

    hf = x2 * lax.rsqrt(jnp.mean(x2 * x2, axis=-1, keepdims=True) + EPS) * gff_ref[...]
    hf_ref[0] = hf
    h_hi = hf.astype(BF16)
    h_lo = (hf - h_hi.astype(F32)).astype(BF16)
    logits = (jnp.dot(h_hi, wrh_ref[...], preferred_element_type=F32)
              + jnp.dot(h_lo, wrh_ref[...], preferred_element_type=F32)
              + jnp.dot(h_hi, wrl_ref[...], preferred_element_type=F32)) + br_ref[...]
    lane = lax.broadcasted_iota(jnp.int32, (1, LANES), 1)
    lane_f = lane.astype(F32)
    lane_grp = (lane // EXPERTS_PER_GROUP).astype(F32)
    is_g = jnp.logical_and(lane >= N_EXPERTS, lane < N_EXPERTS + N_EXPERT_GROUPS)
    gl = jnp.where(is_g, logits, NEG_BIG)
    gmax = jnp.max(gl, axis=-1, keepdims=True)
    g_sel = jnp.min(jnp.where(gl == gmax, lane_f, float(LANES)), axis=-1, keepdims=True) - N_EXPERTS
    g_gate = 1.0 / jnp.sum(jnp.where(is_g, jnp.exp(gl - gmax), 0.0), axis=-1, keepdims=True)
    in_grp = jnp.logical_and(lane < N_EXPERTS, lane_grp == g_sel)
    el = jnp.where(in_grp, logits, NEG_BIG)
    v1 = jnp.max(el, axis=-1, keepdims=True)
    e0 = jnp.min(jnp.where(el == v1, lane_f, float(LANES)), axis=-1, keepdims=True)
    el2 = jnp.where(lane_f == e0, NEG_BIG, el)
    v2 = jnp.max(el2, axis=-1, keepdims=True)
    e1 = jnp.min(jnp.where(el2 == v2, lane_f, float(LANES)), axis=-1, keepdims=True)
    w1 = jnp.exp(v2 - v1)
    gate0 = g_gate / (1.0 + w1)
    gate1 = g_gate * w1 / (1.0 + w1)

    oh0 = lane_f == e0
    oh1 = lane_f == e1
    cnt_tok = oh0.astype(F32) + oh1.astype(F32)
    before = jnp.dot(ltri_ref[...], cnt_tok.astype(BF16), preferred_element_type=F32) + carry_ref[0:1, :]
    rank0 = jnp.sum(jnp.where(oh0, before, 0.0), axis=-1, keepdims=True)
    rank1 = jnp.sum(jnp.where(oh1, before, 0.0), axis=-1, keepdims=True)
    new_carry = carry_ref[0:1, :] + jnp.sum(cnt_tok, axis=0, keepdims=True)
    carry_ref[...] = jnp.broadcast_to(new_carry, carry_ref.shape)
    cnt_ref[...] = jnp.broadcast_to(new_carry, cnt_ref.shape)

    slab = (jnp.where(lane == 0, e0, 0.0) + jnp.where(lane == 1, e1, 0.0)
            + jnp.where(lane == 2, rank0, 0.0) + jnp.where(lane == 3, rank1, 0.0))
    meta_ref[0] = slab.T[0:8, :]
    gate_ref[0] = jnp.where(lane == 0, gate0, 0.0) + jnp.where(lane == 1, gate1, 0.0)


def _mixer(x, y_ssd, y_na, u, kmem, vmem, p):
    b, t, _ = x.shape
    tm = TOKEN_TILE
    nt = t // tm
    hb = tm // BF16_SUBLANES
    nhalo = t // BF16_SUBLANES
    ltri = (jnp.arange(tm)[:, None] > jnp.arange(tm)[None, :]).astype(BF16)
    tok = lambda w: pl.BlockSpec((1, tm, w), lambda i, j: (i, j, 0))
    mem = pl.BlockSpec((1, kmem.shape[1], XA_WIDTH), lambda i, j: (i, 0, 0))
    weights = (p["pool_bd"], p["pool_scale"], p["w_out"], p["norm_xa"], p["w_xq"], p["xq_norm"], p["w_xo"],
               p["norm_ffn"], p["w_r_hi"], p["w_r_lo"], p["b_r"], ltri)
    return pl.pallas_call(
        functools.partial(_mixer_kernel, seq=t),
        grid=(b, nt),
        in_specs=[tok(D_MODEL), tok(D_INNER), tok(NA_WIDTH), tok(POOL_WIDTH),
                  pl.BlockSpec((1, BF16_SUBLANES, POOL_WIDTH), lambda i, j: (i, jnp.maximum(j * hb - 1, 0), 0)),
                  pl.BlockSpec((1, BF16_SUBLANES, POOL_WIDTH),
                               lambda i, j: (i, jnp.minimum((j + 1) * hb, nhalo - 1), 0)),
                  mem, mem] + [_full(w.shape) for w in weights],
        out_specs=[tok(D_MODEL), tok(D_MODEL),
                   pl.BlockSpec((1, 8, tm), lambda i, j: (i * nt + j, 0, 0)),
                   tok(LANES),
                   pl.BlockSpec((8, LANES), lambda i, j: (0, 0))],
        out_shape=[jax.ShapeDtypeStruct((b, t, D_MODEL), F32),
                   jax.ShapeDtypeStruct((b, t, D_MODEL), F32),
                   jax.ShapeDtypeStruct((b * nt, 8, tm), F32),
                   jax.ShapeDtypeStruct((b, t, LANES), F32),
                   jax.ShapeDtypeStruct((8, LANES), F32)],
        scratch_shapes=[pltpu.VMEM((8, LANES), F32)],
        compiler_params=_cparams(("arbitrary", "arbitrary")),
    )(x, y_ssd, y_na, u, u, u, kmem, vmem, *weights)


def _dispatch_kernel(pstart_ref, cnt_ref, e_ref, rk_ref, hf_ref, zero_ref, xs_ref, sem):
    tm = hf_ref.shape[0]
    i = pl.program_id(0)

    def row_copy(src_ref, src_row, dst_row):
        return pltpu.make_async_copy(src_ref.at[pl.ds(src_row, 1)], xs_ref.at[pl.ds(dst_row, 1)], sem)

    @pl.when(i == 0)
    def _():
        for e in range(N_EXPERTS):
            n_e = cnt_ref[e]
            npad = (EXPERT_BLOCK - n_e % EXPERT_BLOCK) % EXPERT_BLOCK
            base = pstart_ref[e] + n_e

            def start(r, _):
                row_copy(zero_ref, 0, base + r).start()
                return 0

            def wait(r, _):
                row_copy(zero_ref, 0, base + r).wait()
                return 0

            lax.fori_loop(0, npad, start, 0)
            lax.fori_loop(0, npad, wait, 0)

    def dest(tk, slot):
        return pstart_ref[e_ref[0, 0, slot * tm + tk]] + rk_ref[0, 0, slot * tm + tk]

    def start(tk, _):
        row_copy(hf_ref, tk, dest(tk, 0)).start()
        row_copy(hf_ref, tk, dest(tk, 1)).start()
        return 0

    def wait(tk, _):
        row_copy(hf_ref, tk, dest(tk, 0)).wait()
        row_copy(hf_ref, tk, dest(tk, 1)).wait()
        return 0

    lax.fori_loop(0, tm, start, 0)
    lax.fori_loop(0, tm, wait, 0)


def _dispatch(hf2d, e_idx, rank, pstart, counts, n_rows):
    n = hf2d.shape[0]
    tm = TOKEN_TILE
    smem_blk = pl.BlockSpec((1, 1, 2 * tm), lambda i, *_: (i, 0, 0), memory_space=pltpu.SMEM)
    return pl.pallas_call(
        _dispatch_kernel,
        grid_spec=pltpu.PrefetchScalarGridSpec(
            num_scalar_prefetch=2,
            grid=(n // tm,),
            in_specs=[smem_blk, smem_blk,
                      pl.BlockSpec((tm, D_MODEL), lambda i, *_: (i, 0)),
                      pl.BlockSpec((8, D_MODEL), lambda i, *_: (0, 0))],
            out_specs=pl.BlockSpec(memory_space=pl.ANY),
            scratch_shapes=[pltpu.SemaphoreType.DMA(())]),
        out_shape=jax.ShapeDtypeStruct((n_rows, D_MODEL), F32),
        compiler_params=_cparams(("arbitrary",)),
    )(pstart, counts, e_idx, rank, hf2d, jnp.zeros((8, D_MODEL), F32))


def _experts_kernel(blk_e_ref, nused_ref, x_ref, wg_ref, wu_ref, wd_ref, y_ref):
    x = x_ref[...].astype(BF16)
    hg = jnp.dot(x, wg_ref[0], preferred_element_type=F32)
    hu = jnp.dot(x, wu_ref[0], preferred_element_type=F32)
    h = (_silu(hg) * hu).astype(BF16)
    y_ref[...] = jnp.dot(h, wd_ref[0], preferred_element_type=F32)


def _experts(xs, blk_e, nused, w_gate, w_up, w_down):
    nblk = blk_e.shape[0]
    bm = EXPERT_BLOCK
    row = lambda j, be, nu: (jnp.minimum(j, nu[0] - 1), 0)
    wsel = lambda j, be, nu: (be[j], 0, 0)
    return pl.pallas_call(
        _experts_kernel,
        grid_spec=pltpu.PrefetchScalarGridSpec(
            num_scalar_prefetch=2,
            grid=(nblk,),
            in_specs=[pl.BlockSpec((bm, D_MODEL), row),
                      pl.BlockSpec((1, D_MODEL, D_EXPERT), wsel),
                      pl.BlockSpec((1, D_MODEL, D_EXPERT), wsel),
                      pl.BlockSpec((1, D_EXPERT, D_MODEL), wsel)],
            out_specs=pl.BlockSpec((bm, D_MODEL), row)),
        out_shape=jax.ShapeDtypeStruct((nblk * bm, D_MODEL), F32),
        compiler_params=_cparams(("arbitrary",)),
    )(blk_e, nused, xs, w_gate, w_up, w_down)


def _combine_kernel(pstart_ref, e_ref, rk_ref, x_ref, gate_ref, y_hbm, o_ref, ybuf, sem):
    tm = x_ref.shape[0]

    def row_copy(tk, slot):
        d = pstart_ref[e_ref[0, 0, slot * tm + tk]] + rk_ref[0, 0, slot * tm + tk]
        return pltpu.make_async_copy(y_hbm.at[pl.ds(d, 1)], ybuf.at[slot, pl.ds(tk, 1)], sem)

    def start(tk, _):
        row_copy(tk, 0).start()
        row_copy(tk, 1).start()
        return 0

    def wait(tk, _):
        row_copy(tk, 0).wait()
        row_copy(tk, 1).wait()
        return 0

    lax.fori_loop(0, tm, start, 0)
    lax.fori_loop(0, tm, wait, 0)
    g = gate_ref[...]
    o_ref[...] = x_ref[...] + g[:, 0:1] * ybuf[0] + g[:, 1:2] * ybuf[1]


def _combine(x2d, gates2d, y, e_idx, rank, pstart):
    n = x2d.shape[0]
    tm = TOKEN_TILE
    smem_blk = pl.BlockSpec((1, 1, 2 * tm), lambda i, *_: (i, 0, 0), memory_space=pltpu.SMEM)
    return pl.pallas_call(
        _combine_kernel,
        grid_spec=pltpu.PrefetchScalarGridSpec(
            num_scalar_prefetch=1,
            grid=(n // tm,),
            in_specs=[smem_blk, smem_blk,
                      pl.BlockSpec((tm, D_MODEL), lambda i, *_: (i, 0)),
                      pl.BlockSpec((tm, LANES), lambda i, *_: (i, 0)),
                      pl.BlockSpec(memory_space=pl.ANY)],
            out_specs=pl.BlockSpec((tm, D_MODEL), lambda i, *_: (i, 0)),
            scratch_shapes=[pltpu.VMEM((2, tm, D_MODEL), F32), pltpu.SemaphoreType.DMA(())]),
        out_shape=jax.ShapeDtypeStruct((n, D_MODEL), F32),
        compiler_params=_cparams(("arbitrary",)),
    )(pstart, e_idx, rank, x2d, gates2d, y)


def _moe(x2, hf, meta, gates, counts, w_gate, w_up, w_down):
    b, t, _ = x2.shape
    n = b * t
    tm = TOKEN_TILE
    bm = EXPERT_BLOCK
    nblk = (2 * n + N_EXPERTS * (bm - 1) + bm - 1) // bm
    cnt = counts[0, :N_EXPERTS].astype(jnp.int32)
    psz = (cnt + bm - 1) // bm * bm
    pend = jnp.cumsum(psz)
    pstart = (pend - psz).astype(jnp.int32)
    nused = jnp.maximum(pend[-1] // bm, 1).astype(jnp.int32).reshape(1)
    blk = jnp.minimum(jnp.arange(nblk, dtype=jnp.int32), nused[0] - 1)
    blk_e = jnp.clip(jnp.searchsorted(pend, blk * bm, side="right"), 0, N_EXPERTS - 1).astype(jnp.int32)
    ids = meta.astype(jnp.int32)
    e_idx = ids[:, 0:2, :].reshape(n // tm, 1, 2 * tm)
    rank = ids[:, 2:4, :].reshape(n // tm, 1, 2 * tm)
    xs = _dispatch(hf.reshape(n, D_MODEL), e_idx, rank, pstart, cnt, nblk * bm)
    y = _experts(xs, blk_e, nused, w_gate, w_up, w_down)
    out = _combine(x2.reshape(n, D_MODEL), gates.reshape(n, LANES), y, e_idx, rank, pstart)
    return out.reshape(b, t, D_MODEL)


def _prep_layer(lp):
    w_in = lp["w_in"]
    c0 = D_INNER + CONV_CH
    c1 = c0 + 2 * SSD_HEADS
    w_cat = jnp.concatenate([w_in[:, :c0], w_in[:, c1:], w_in[:, c0:c1],
                             jnp.zeros((D_MODEL, DT_PAD - 2 * SSD_HEADS), w_in.dtype)], axis=1).astype(BF16)
    pool_bd = jnp.zeros((POOL_WIDTH, POOL_WIDTH), F32)
    for g in range(POOL_GROUPS):
        sl = slice(g * POOL_GROUP_DIM, (g + 1) * POOL_GROUP_DIM)
        pool_bd = pool_bd.at[sl, sl].set(lp["pool_w"][g].astype(F32))
    w_r = jnp.concatenate([lp["w_router_expert"], lp["w_router_group"],
                           jnp.zeros((D_MODEL, LANES - N_EXPERTS - N_EXPERT_GROUPS), F32)], axis=1).astype(F32)
    w_r_hi = w_r.astype(BF16)
    w_r_lo = (w_r - w_r_hi.astype(F32)).astype(BF16)
    b_r = jnp.concatenate([lp["b_router_expert"], lp["b_router_group"],
                           jnp.zeros((LANES - N_EXPERTS - N_EXPERT_GROUPS,), F32)]).reshape(1, LANES).astype(F32)
    row = lambda a, w: a.reshape(1, w).astype(F32)
    return dict(
        norm_mix=lp["norm_mix"], w_cat=w_cat,
        conv_w=lp["conv_w"], conv_b=lp["conv_b"], dt_bias=lp["dt_bias"], a_log=lp["a_log"],
        d_skip=lp["d_skip"], ssd_norm=lp["ssd_norm"],
        na_q_norm=lp["na_q_norm"], na_k_norm=lp["na_k_norm"], na_rpb=lp["na_rpb"],
        pool_bd=pool_bd.astype(BF16), pool_scale=row(lp["pool_scale"], POOL_WIDTH),
        w_out=lp["w_out"].astype(BF16), norm_xa=row(lp["norm_xa"], D_MODEL),
        norm_mem=lp["norm_mem"], w_xq=lp["w_xq"].astype(BF16), w_xkv=lp["w_xkv"].astype(BF16),
        xq_norm=row(jnp.tile(lp["xq_norm"], XA_HEADS), XA_WIDTH), xk_norm=lp["xk_norm"],
        w_xo=lp["w_xo"].astype(BF16), norm_ffn=row(lp["norm_ffn"], D_MODEL),
        w_r_hi=w_r_hi, w_r_lo=w_r_lo, b_r=b_r,
        w_e_gate=lp["w_e_gate"].astype(BF16), w_e_up=lp["w_e_up"].astype(BF16),
        w_e_down=lp["w_e_down"].astype(BF16),
    )


def _layer(x, mem, p, na_bias):
    b, t, _ = x.shape
    z, xbc, qkv, u, dt = _inproj(x.reshape(b * t, D_MODEL), p["norm_mix"], p["w_cat"])
    r3 = lambda a: a.reshape(b, t, a.shape[-1])
    y_ssd = _ssd(r3(z), r3(xbc), r3(dt), p["conv_w"], p["conv_b"], p["dt_bias"], p["a_log"], p["d_skip"],
                 p["ssd_norm"])
    y_na = _natten(r3(qkv), na_bias, p["na_q_norm"], p["na_k_norm"])
    kmem, vmem = _memkv(mem, p["norm_mem"], p["w_xkv"], p["xk_norm"])
    x2, hf, meta, gates, counts = _mixer(x, y_ssd, y_na, r3(u), kmem, vmem, p)
    return _moe(x2, hf, meta, gates, counts, p["w_e_gate"], p["w_e_up"], p["w_e_down"])


_LAYER_KEYS = ("norm_mix", "w_in", "conv_w", "conv_b", "dt_bias", "a_log", "d_skip", "ssd_norm", "na_q_norm",
               "na_k_norm", "na_rpb", "pool_w", "pool_scale", "w_out", "norm_xa", "norm_mem", "w_xq", "w_xkv",
               "xq_norm", "xk_norm", "w_xo", "norm_ffn", "w_router_group", "b_router_group", "w_router_expert",
               "b_router_expert", "w_e_gate", "w_e_up", "w_e_down")


def kernel(x_prompt, x_sample, mem_prompt, mem_sample, norm_mix, w_in, conv_w, conv_b, dt_bias, a_log, d_skip, ssd_norm, na_q_norm, na_k_norm, na_rpb, pool_w, pool_scale, w_out, norm_xa, norm_mem, w_xq, w_xkv, xq_norm, xk_norm, w_xo, norm_ffn, w_router_group, b_router_group, w_router_expert, b_router_expert, w_e_gate, w_e_up, w_e_down):
    stacked = dict(zip(_LAYER_KEYS, (norm_mix, w_in, conv_w, conv_b, dt_bias, a_log, d_skip, ssd_norm, na_q_norm,
                                     na_k_norm, na_rpb, pool_w, pool_scale, w_out, norm_xa, norm_mem, w_xq, w_xkv,
                                     xq_norm, xk_norm, w_xo, norm_ffn, w_router_group, b_router_group,
                                     w_router_expert, b_router_expert, w_e_gate, w_e_up, w_e_down)))
    depth = w_in.shape[0]
    layers = [_prep_layer({k: v[l] for k, v in stacked.items()}) for l in range(depth)]

    def trunk(x, mem):
        biases = [_na_bias(lp["na_rpb"], x.shape[1]) for lp in layers]
        for lp, nb in zip(layers, biases):
            x = _layer(x, mem, lp, nb)
        return x

    return trunk(x_prompt, mem_prompt), trunk(x_sample, mem_sample)
```

```python
import functools

import jax
import jax.numpy as jnp
import numpy as np
from jax import lax
from jax.experimental import pallas as pl
from jax.experimental.pallas import tpu as pltpu

F32 = jnp.float32
BF16 = jnp.bfloat16
HIGHEST = lax.Precision.HIGHEST

D_MODEL = 1024
GRID_W = 64
EPS = 1e-6
SSD_HEAD_DIM = 64
D_INNER = D_MODEL // 2
SSD_HEADS = D_INNER // SSD_HEAD_DIM
SSD_GROUPS = 2
SSD_STATE = 64
SSD_CHUNK = 128
CONV_W = 4
CONV_CH = D_INNER + 2 * SSD_GROUPS * SSD_STATE
NA_HEADS = 4
NA_HEAD_DIM = D_MODEL // 16
NA_WIDTH = NA_HEADS * NA_HEAD_DIM
NA_MAX_KH = 8
NA_KW = 16
POOL_WINDOWS = (2, 4, 8, 16)
POOL_GROUPS = 4
POOL_WIDTH = D_MODEL - D_INNER - NA_WIDTH
POOL_GROUP_DIM = POOL_WIDTH // POOL_GROUPS
XA_HEADS = 4
XA_HEAD_DIM = D_MODEL // 8
XA_WIDTH = XA_HEADS * XA_HEAD_DIM
N_EXPERT_GROUPS = 4
EXPERTS_PER_GROUP = 8
N_EXPERTS = N_EXPERT_GROUPS * EXPERTS_PER_GROUP
D_EXPERT = D_MODEL // 4

LANES = 128
BF16_SUBLANES = 16
VMEM_LIMIT_BYTES = 56 * 1024 * 1024

TOKEN_TILE = 512
NA_QUERY_ROWS = 8
NA_KEY_ROWS = 16
EXPERT_BLOCK = 256
DT_PAD = LANES
DMA_UNROLL = 8
NEG_BIG = -1e30


def _cparams(sem):
    return pltpu.CompilerParams(dimension_semantics=sem, vmem_limit_bytes=VMEM_LIMIT_BYTES)


def _sigmoid(x):
    return 1.0 / (1.0 + jnp.exp(-x))


def _silu(x):
    return x * _sigmoid(x)


def _softplus(x):
    return jnp.maximum(x, 0.0) + jnp.log(1.0 + jnp.exp(-jnp.abs(x)))


def _full(shape):
    n = len(shape)
    return pl.BlockSpec(shape, lambda *_: (0,) * n)


def _inproj_kernel(x_ref, g_ref, w_ref, z_ref, xbc_ref, qkv_ref, u_ref, dt_ref):
    x = x_ref[...]
    ms = jnp.mean(x * x, axis=-1, keepdims=True)
    h = (x * lax.rsqrt(ms + EPS) * g_ref[...]).astype(BF16)
    o = 0
    for ref in (z_ref, xbc_ref, qkv_ref, u_ref, dt_ref):
        w = ref.shape[-1]
        ref[...] = jnp.dot(h, w_ref[:, o:o + w], preferred_element_type=F32).astype(ref.dtype)
        o += w


def _inproj(x2d, gain, w_cat):
    n = x2d.shape[0]
    tm = TOKEN_TILE
    widths = (D_INNER, CONV_CH, 3 * NA_WIDTH, POOL_WIDTH, DT_PAD)
    dtypes = (BF16, BF16, BF16, BF16, F32)
    return pl.pallas_call(
        _inproj_kernel,
        grid=(n // tm,),
        in_specs=[pl.BlockSpec((tm, D_MODEL), lambda i: (i, 0)),
                  _full((1, D_MODEL)),
                  _full(w_cat.shape)],
        out_specs=[pl.BlockSpec((tm, w), lambda i: (i, 0)) for w in widths],
        out_shape=[jax.ShapeDtypeStruct((n, w), d) for w, d in zip(widths, dtypes)],
        compiler_params=_cparams(("arbitrary",)),
    )(x2d, gain.reshape(1, D_MODEL), w_cat)


def _ssd_kernel(xc_ref, xp_ref, xn_ref, dt_ref, z_ref, cw_ref, cb_ref, dtb_ref, alog_ref, dsk_ref, nrm_ref,
                y_ref, state_ref, yf_ref, *, nc):
    L = SSD_CHUNK
    P = SSD_HEAD_DIM
    NS = SSD_STATE
    HG = SSD_HEADS // SSD_GROUPS
    j = pl.program_id(1)
    c = jnp.where(j < nc, j, 2 * nc - 1 - j)

    cur = xc_ref[0].astype(F32)
    prev = xp_ref[0].astype(F32)
    nxt = xn_ref[0].astype(F32)
    has_prev = (c > 0).astype(F32)
    has_next = (c < nc - 1).astype(F32)
    p_last = prev[BF16_SUBLANES - 1:BF16_SUBLANES, :] * has_prev
    n0 = nxt[0:1, :] * has_next
    n1 = nxt[1:2, :] * has_next
    row = lax.broadcasted_iota(jnp.int32, (L, 1), 0)
    um1 = jnp.where(row == 0, p_last, pltpu.roll(cur, 1, 0))
    up1 = jnp.where(row == L - 1, n0, pltpu.roll(cur, L - 1, 0))
    up2 = jnp.where(row == L - 2, n0, jnp.where(row == L - 1, n1, pltpu.roll(cur, L - 2, 0)))
    cw = cw_ref[...]
    acc = cb_ref[...] + um1 * cw[0:1, :] + cur * cw[1:2, :] + up1 * cw[2:3, :] + up2 * cw[3:4, :]
    xbc = _silu(acc)
    xs = xbc[:, :D_INNER]
    gn = SSD_GROUPS * NS
    lane1 = lax.broadcasted_iota(jnp.int32, (1, LANES), 1)
    lo_half = lane1 < P
    bfull = xbc[:, D_INNER:D_INNER + gn].astype(BF16)
    cfull = xbc[:, D_INNER + gn:D_INNER + 2 * gn]
    cg = [jnp.where(lane1 // NS == g, cfull, 0.0).astype(BF16) for g in range(SSD_GROUPS)]
    cb_mat = [lax.dot_general(cg[g], bfull, (((1,), (1,)), ((), ())), preferred_element_type=F32)
              for g in range(SSD_GROUPS)]

    dt = _softplus(dt_ref[0] + dtb_ref[...])
    la = dt * (-jnp.exp(alog_ref[...]))
    ti = lax.broadcasted_iota(jnp.int32, (L, L), 0)
    si = lax.broadcasted_iota(jnp.int32, (L, L), 1)

    def scan_chunk(direction):
        if direction == 0:
            mask = ti >= si
            edge = L - 1
        else:
            mask = si >= ti
            edge = 0
        tri = mask.astype(F32)
        csum = jnp.dot(tri, la, precision=HIGHEST, preferred_element_type=F32)
        csum_t = csum.T
        tot = csum[edge:edge + 1, :]
        e_tot = jnp.exp(tot)
        e_in = jnp.exp(csum)
        e_out = jnp.exp(tot - csum)
        ys = []
        for g in range(SSD_GROUPS):
            s_old = state_ref[g]
            y_off = lax.dot_general(cg[g], s_old.astype(BF16), (((1,), (1,)), ((), ())),
                                    preferred_element_type=F32)
            xw = []
            for pr in range(HG // 2):
                h0 = g * HG + 2 * pr
                l0 = direction * SSD_HEADS + h0

                def col(a, l0=l0):
                    return jnp.where(lo_half, a[:, l0:l0 + 1], a[:, l0 + 1:l0 + 2])

                xdt = xs[:, h0 * P:(h0 + 2) * P] * col(dt)
                y_pair = y_off[:, 2 * pr * P:(2 * pr + 2) * P] * col(e_in)
                for ll, half in ((l0, lo_half), (l0 + 1, jnp.logical_not(lo_half))):
                    seg = csum[:, ll:ll + 1] - csum_t[ll:ll + 1, :]
                    dec = jnp.exp(jnp.where(mask, seg, NEG_BIG))
                    m = (cb_mat[g] * dec).astype(BF16)
                    y_pair += jnp.dot(m, jnp.where(half, xdt, 0.0).astype(BF16), preferred_element_type=F32)
                ys.append(y_pair)
                xw.append(xdt * col(e_out))
            xw = jnp.concatenate(xw, axis=1).astype(BF16)
            s_new = lax.dot_general(xw, bfull, (((0,), (0,)), ((), ())), preferred_element_type=F32)
            s_scaled = []
            for hl in range(HG):
                lane = direction * SSD_HEADS + g * HG + hl
                s_scaled.append(s_old[hl * P:(hl + 1) * P, :] * e_tot[:, lane:lane + 1])
            state_ref[g] = jnp.concatenate(s_scaled, axis=0) + s_new
        return jnp.concatenate(ys, axis=1)

    @pl.when(jnp.logical_or(j == 0, j == nc))
    def _():
        state_ref[...] = jnp.zeros_like(state_ref)

    row0 = pl.multiple_of(c * L, L)

    @pl.when(j < nc)
    def _():
        yf_ref[pl.ds(row0, L), :] = scan_chunk(0)

    @pl.when(j >= nc)
    def _():
        y = yf_ref[pl.ds(row0, L), :] + scan_chunk(1) + dsk_ref[...] * xs
        y = y * _silu(z_ref[0].astype(F32))
        gw = D_INNER // SSD_GROUPS
        outs = []
        for g in range(SSD_GROUPS):
            yg = y[:, g * gw:(g + 1) * gw]
            outs.append(yg * lax.rsqrt(jnp.mean(yg * yg, axis=-1, keepdims=True) + EPS))
        y_ref[0] = (jnp.concatenate(outs, axis=1) * nrm_ref[...]).astype(y_ref.dtype)


def _ssd(z, xbc, dt, conv_w, conv_b, dt_bias, a_log, d_skip, ssd_norm):
    b, t, _ = z.shape
    L = SSD_CHUNK
    nc = t // L
    hb = L // BF16_SUBLANES
    nhalo = t // BF16_SUBLANES

    def chunk(j):
        return jnp.where(j < nc, j, 2 * nc - 1 - j)

    pad = DT_PAD - 2 * SSD_HEADS
    dtb = jnp.pad(dt_bias.reshape(1, -1).astype(F32), ((0, 0), (0, pad)))
    alog = jnp.pad(a_log.reshape(1, -1).astype(F32), ((0, 0), (0, pad)))
    dsk = jnp.repeat(d_skip.astype(F32), SSD_HEAD_DIM).reshape(1, D_INNER)
    return pl.pallas_call(
        functools.partial(_ssd_kernel, nc=nc),
        grid=(b, 2 * nc),
        in_specs=[
            pl.BlockSpec((1, L, CONV_CH), lambda i, j: (i, chunk(j), 0)),
            pl.BlockSpec((1, BF16_SUBLANES, CONV_CH), lambda i, j: (i, jnp.maximum(chunk(j) * hb - 1, 0), 0)),
            pl.BlockSpec((1, BF16_SUBLANES, CONV_CH),
                         lambda i, j: (i, jnp.minimum((chunk(j) + 1) * hb, nhalo - 1), 0)),
            pl.BlockSpec((1, L, DT_PAD), lambda i, j: (i, chunk(j), 0)),
            pl.BlockSpec((1, L, D_INNER), lambda i, j: (i, chunk(j), 0)),
            _full((CONV_W, CONV_CH)), _full((1, CONV_CH)), _full((1, DT_PAD)), _full((1, DT_PAD)),
            _full((1, D_INNER)), _full((1, D_INNER)),
        ],
        out_specs=pl.BlockSpec((1, L, D_INNER), lambda i, j: (i, jnp.where(j < nc, nc - 1, 2 * nc - 1 - j), 0)),
        out_shape=jax.ShapeDtypeStruct((b, t, D_INNER), BF16),
        scratch_shapes=[pltpu.VMEM((SSD_GROUPS, (SSD_HEADS // SSD_GROUPS) * SSD_HEAD_DIM, LANES), F32),
                        pltpu.VMEM((t, D_INNER), F32)],
        compiler_params=_cparams(("arbitrary", "arbitrary")),
    )(xbc, xbc, xbc, dt, z, conv_w.astype(F32), conv_b.reshape(1, CONV_CH).astype(F32), dtb, alog, dsk,
      ssd_norm.reshape(1, D_INNER).astype(F32))


def _na_bias(rpb, t):
    r = t // GRID_W
    kh = min(NA_MAX_KH, r)
    nrb = r // NA_QUERY_ROWS
    rows = np.arange(r)
    row_start = np.clip(rows - NA_MAX_KH // 2, 0, r - kh)
    r0 = np.arange(nrb) * NA_QUERY_ROWS
    kr0 = np.clip(r0 - NA_MAX_KH // 2, 0, r - NA_KEY_ROWS)
    qrow = r0[:, None] + np.arange(NA_QUERY_ROWS)[None, :]
    krow = kr0[:, None] + np.arange(NA_KEY_ROWS)[None, :]
    rs = row_start[qrow]
    row_ok = (krow[:, None, :] >= rs[:, :, None]) & (krow[:, None, :] < rs[:, :, None] + kh)
    dr = np.clip(krow[:, None, :] - qrow[:, :, None] + (NA_MAX_KH - 1), 0, 2 * NA_MAX_KH - 2)
    cols = np.arange(GRID_W)
    col_start = np.clip(cols - NA_KW // 2, 0, GRID_W - NA_KW)
    col_ok = (cols[None, :] >= col_start[:, None]) & (cols[None, :] < col_start[:, None] + NA_KW)
    dc = np.clip(cols[None, :] - cols[:, None] + (NA_KW - 1), 0, 2 * NA_KW - 2)
    sel_r = (dr[..., None] == np.arange(2 * NA_MAX_KH - 1)) & row_ok[..., None]
    sel_c = (dc[..., None] == np.arange(2 * NA_KW - 1)) & col_ok[..., None]
    sel_r = jnp.asarray(sel_r, F32)
    sel_c = jnp.asarray(sel_c, F32)
    bias = jnp.einsum("hrc,bqkr,xyc->bhqxky", rpb.astype(F32), sel_r, sel_c, precision=HIGHEST)
    ok = jnp.einsum("bqk,xy->bqxky", jnp.asarray(row_ok, F32), jnp.asarray(col_ok, F32))
    bias = jnp.where(ok[:, None] > 0.5, bias, NEG_BIG)
    return bias.reshape(nrb, NA_HEADS, NA_QUERY_ROWS * GRID_W, NA_KEY_ROWS * GRID_W).astype(BF16)


def _natten_kernel(qkv_ref, bias_ref, qg_ref, kg_ref, seg_ref, o_ref, *, grid_rows):
    nq = NA_QUERY_ROWS * GRID_W
    nk = NA_KEY_ROWS * GRID_W
    rb = pl.program_id(0)
    r0 = rb * NA_QUERY_ROWS
    kr0 = jnp.clip(r0 - NA_MAX_KH // 2, 0, grid_rows - NA_KEY_ROWS)
    q0 = pl.multiple_of(r0 * GRID_W, nq)
    k0 = pl.multiple_of(kr0 * GRID_W, NA_MAX_KH // 2 * GRID_W)
    q = qkv_ref[0, pl.ds(q0, nq), 0:NA_WIDTH].astype(F32)
    k = qkv_ref[0, pl.ds(k0, nk), NA_WIDTH:2 * NA_WIDTH].astype(F32)
    v = qkv_ref[0, pl.ds(k0, nk), 2 * NA_WIDTH:3 * NA_WIDTH]
    seg = seg_ref[...]
    qms = jnp.dot(q * q, seg, precision=HIGHEST, preferred_element_type=F32)
    kms = jnp.dot(k * k, seg, precision=HIGHEST, preferred_element_type=F32)
    qn = q * lax.rsqrt(qms + EPS) * qg_ref[...]
    kn = (k * lax.rsqrt(kms + EPS) * kg_ref[...]).astype(BF16)
    scale = NA_HEAD_DIM ** -0.5
    lane_h = lax.broadcasted_iota(jnp.int32, (1, NA_WIDTH), 1) // NA_HEAD_DIM
    acc = jnp.zeros((nq, NA_WIDTH), F32)
    for h in range(NA_HEADS):
        hm = lane_h == h
        s = lax.dot_general(jnp.where(hm, qn, 0.0).astype(BF16), kn, (((1,), (1,)), ((), ())),
                            preferred_element_type=F32)
        s = s * scale + bias_ref[0, h].astype(F32)
        p = jnp.exp(s - jnp.max(s, axis=-1, keepdims=True))
        l = jnp.sum(p, axis=-1, keepdims=True)
        o = jnp.dot(p.astype(BF16), v, preferred_element_type=F32)
        acc += jnp.where(hm, o / l, 0.0)
    o_ref[0] = acc.astype(o_ref.dtype)


def _natten(qkv, bias, q_norm, k_norm):
    b, t, _ = qkv.shape
    r = t // GRID_W
    nrb = r // NA_QUERY_ROWS
    nq = NA_QUERY_ROWS * GRID_W
    nk = NA_KEY_ROWS * GRID_W
    head = jnp.arange(NA_WIDTH) // NA_HEAD_DIM
    seg = (head[:, None] == head[None, :]).astype(F32) / NA_HEAD_DIM
    return pl.pallas_call(
        functools.partial(_natten_kernel, grid_rows=r),
        grid=(nrb, b),
        in_specs=[pl.BlockSpec((1, t, 3 * NA_WIDTH), lambda i, j: (j, 0, 0)),
                  pl.BlockSpec((1, NA_HEADS, nq, nk), lambda i, j: (i, 0, 0, 0)),
                  _full((1, NA_WIDTH)), _full((1, NA_WIDTH)), _full((NA_WIDTH, NA_WIDTH))],
        out_specs=pl.BlockSpec((1, nq, NA_WIDTH), lambda i, j: (j, i, 0)),
        out_shape=jax.ShapeDtypeStruct((b, t, NA_WIDTH), BF16),
        compiler_params=_cparams(("arbitrary", "arbitrary")),
    )(qkv, bias, jnp.tile(q_norm.astype(F32), NA_HEADS).reshape(1, NA_WIDTH),
      jnp.tile(k_norm.astype(F32), NA_HEADS).reshape(1, NA_WIDTH), seg)


def _memkv_kernel(m_ref, g_ref, w_ref, kg_ref, k_ref, v_ref):
    x = m_ref[0]
    ms = jnp.mean(x * x, axis=-1, keepdims=True)
    h = (x * lax.rsqrt(ms + EPS) * g_ref[...]).astype(BF16)
    kv = jnp.dot(h, w_ref[...], preferred_element_type=F32)
    ks = []
    for hd in range(XA_HEADS):
        kh = kv[:, hd * XA_HEAD_DIM:(hd + 1) * XA_HEAD_DIM]
        ks.append(kh * lax.rsqrt(jnp.mean(kh * kh, axis=-1, keepdims=True) + EPS))
    k_ref[0] = (jnp.concatenate(ks, axis=1) * kg_ref[...]).astype(k_ref.dtype)
    v_ref[0] = kv[:, XA_WIDTH:].astype(v_ref.dtype)


def _memkv(mem, norm_mem, w_xkv, xk_norm):
    b, m, _ = mem.shape
    return pl.pallas_call(
        _memkv_kernel,
        grid=(b,),
        in_specs=[pl.BlockSpec((1, m, D_MODEL), lambda i: (i, 0, 0)),
                  _full((1, D_MODEL)), _full((D_MODEL, 2 * XA_WIDTH)), _full((1, XA_WIDTH))],
        out_specs=[pl.BlockSpec((1, m, XA_WIDTH), lambda i: (i, 0, 0))] * 2,
        out_shape=[jax.ShapeDtypeStruct((b, m, XA_WIDTH), BF16)] * 2,
        compiler_params=_cparams(("arbitrary",)),
    )(mem, norm_mem.reshape(1, D_MODEL).astype(F32), w_xkv,
      jnp.tile(xk_norm.astype(F32), XA_HEADS).reshape(1, XA_WIDTH))


def _mixer_kernel(x_ref, ys_ref, yn_ref, u_ref, up_ref, un_ref, k_ref, v_ref,
                  pw_ref, psc_ref, wo_ref, gxa_ref, wq_ref, qg_ref, wxo_ref, gff_ref,
                  wrh_ref, wrl_ref, br_ref, ltri_ref,
                  x2_ref, hf_ref, meta_ref, gate_ref, cnt_ref, carry_ref, *, seq):
    tm = x_ref.shape[1]
    halo = BF16_SUBLANES
    bi = pl.program_id(0)
    i = pl.program_id(1)
    nt = pl.num_programs(1)

    @pl.when(jnp.logical_and(bi == 0, i == 0))
    def _():
        carry_ref[...] = jnp.zeros_like(carry_ref)

    u = u_ref[0].astype(F32)
    up = up_ref[0].astype(F32) * (i > 0).astype(F32)
    un = un_ref[0].astype(F32) * (i < nt - 1).astype(F32)
    cat = jnp.concatenate([up, u, un], axis=0)
    n = tm + 2 * halo

    def sh(a, k):
        return pltpu.roll(a, (-k) % n, 0)

    a2 = cat + sh(cat, -1)
    a4 = sh(a2, 1) + sh(a2, -1)
    a8 = sh(a4, 2) + sh(a4, -2)
    a16 = sh(a8, 4) + sh(a8, -4)
    lane_g = lax.broadcasted_iota(jnp.int32, (1, POOL_WIDTH), 1) // POOL_GROUP_DIM
    wsum = jnp.where(lane_g == 0, a2, jnp.where(lane_g == 1, a4, jnp.where(lane_g == 2, a8, a16)))
    wsum = wsum[halo:halo + tm, :]
    half = jnp.where(lane_g == 0, POOL_WINDOWS[0] // 2,
                     jnp.where(lane_g == 1, POOL_WINDOWS[1] // 2,
                               jnp.where(lane_g == 2, POOL_WINDOWS[2] // 2, POOL_WINDOWS[3] // 2)))
    tpos = i * tm + lax.broadcasted_iota(jnp.int32, (tm, 1), 0)
    cnt = (jnp.minimum(tpos + half, seq) - jnp.maximum(tpos - half, 0)).astype(F32)
    d = wsum / cnt - u
    ypool = jnp.dot(d.astype(BF16), pw_ref[...], preferred_element_type=F32) * psc_ref[...]

    mix = jnp.dot(ys_ref[0], wo_ref[0:D_INNER, :], preferred_element_type=F32)
    mix += jnp.dot(yn_ref[0], wo_ref[D_INNER:D_INNER + NA_WIDTH, :], preferred_element_type=F32)
    mix += jnp.dot(ypool.astype(BF16), wo_ref[D_INNER + NA_WIDTH:, :], preferred_element_type=F32)
    x1 = x_ref[0] + mix

    hn = (x1 * lax.rsqrt(jnp.mean(x1 * x1, axis=-1, keepdims=True) + EPS) * gxa_ref[...]).astype(BF16)
    q = jnp.dot(hn, wq_ref[...], preferred_element_type=F32)
    kk = k_ref[0]
    vv = v_ref[0]
    scale = XA_HEAD_DIM ** -0.5
    outs = []
    for hd in range(XA_HEADS):
        sl = slice(hd * XA_HEAD_DIM, (hd + 1) * XA_HEAD_DIM)
        qh = q[:, sl]
        qh = (qh * lax.rsqrt(jnp.mean(qh * qh, axis=-1, keepdims=True) + EPS) * qg_ref[:, sl]).astype(BF16)
        s = lax.dot_general(qh, kk[:, sl], (((1,), (1,)), ((), ())), preferred_element_type=F32) * scale
        p = jnp.exp(s - jnp.max(s, axis=-1, keepdims=True))
        l = jnp.sum(p, axis=-1, keepdims=True)
        outs.append(jnp.dot(p.astype(BF16), vv[:, sl], preferred_element_type=F32) / l)
    att = jnp.concatenate(outs, axis=1).astype(BF16)
    x2 = x1 + jnp.dot(att, wxo_ref[...], preferred_element_type=F32)
    x2_ref[0] = x2

    hf = x2 * lax.rsqrt(jnp.mean(x2 * x2, axis=-1, keepdims=True) + EPS) * gff_ref[...]
    hf_ref[0] = hf
    h_hi = hf.astype(BF16)
    h_lo = (hf - h_hi.astype(F32)).astype(BF16)
    logits = (jnp.dot(h_hi, wrh_ref[...], preferred_element_type=F32)
              + jnp.dot(h_lo, wrh_ref[...], preferred_element_type=F32)
              + jnp.dot(h_hi, wrl_ref[...], preferred_element_type=F32)) + br_ref[...]
    lane = lax.broadcasted_iota(jnp.int32, (1, LANES), 1)
    lane_f = lane.astype(F32)
    lane_grp = (lane // EXPERTS_PER_GROUP).astype(F32)
    is_g = jnp.logical_and(lane >= N_EXPERTS, lane < N_EXPERTS + N_EXPERT_GROUPS)
    gl = jnp.where(is_g, logits, NEG_BIG)
    gmax = jnp.max(gl, axis=-1, keepdims=True)
    g_sel = jnp.min(jnp.where(gl == gmax, lane_f, float(LANES)), axis=-1, keepdims=True) - N_EXPERTS
    g_gate = 1.0 / jnp.sum(jnp.where(is_g, jnp.exp(gl - gmax), 0.0), axis=-1, keepdims=True)
    in_grp = jnp.logical_and(lane < N_EXPERTS, lane_grp == g_sel)
    el = jnp.where(in_grp, logits, NEG_BIG)
    v1 = jnp.max(el, axis=-1, keepdims=True)
    e0 = jnp.min(jnp.where(el == v1, lane_f, float(LANES)), axis=-1, keepdims=True)
    el2 = jnp.where(lane_f == e0, NEG_BIG, el)
    v2 = jnp.max(el2, axis=-1, keepdims=True)
    e1 = jnp.min(jnp.where(el2 == v2, lane_f, float(LANES)), axis=-1, keepdims=True)
    w1 = jnp.exp(v2 - v1)
    gate0 = g_gate / (1.0 + w1)
    gate1 = g_gate * w1 / (1.0 + w1)

    oh0 = lane_f == e0
    oh1 = lane_f == e1
    cnt_tok = oh0.astype(F32) + oh1.astype(F32)
    before = jnp.dot(ltri_ref[...], cnt_tok.astype(BF16), preferred_element_type=F32) + carry_ref[0:1, :]
    rank0 = jnp.sum(jnp.where(oh0, before, 0.0), axis=-1, keepdims=True)
    rank1 = jnp.sum(jnp.where(oh1, before, 0.0), axis=-1, keepdims=True)
    new_carry = carry_ref[0:1, :] + jnp.sum(cnt_tok, axis=0, keepdims=True)
    carry_ref[...] = jnp.broadcast_to(new_carry, carry_ref.shape)
    cnt_ref[...] = jnp.broadcast_to(new_carry, cnt_ref.shape)

    slab = (jnp.where(lane == 0, e0, 0.0) + jnp.where(lane == 1, e1, 0.0)
            + jnp.where(lane == 2, rank0, 0.0) + jnp.where(lane == 3, rank1, 0.0))
    meta_ref[0] = slab.T[0:8, :]
    gate_ref[0] = jnp.where(lane == 0, gate0, 0.0) + jnp.where(lane == 1, gate1, 0.0)


def _mixer(x, y_ssd, y_na, u, kmem, vmem, p):
    b, t, _ = x.shape
    tm = TOKEN_TILE
    nt = t // tm
    hb = tm // BF16_SUBLANES
    nhalo = t // BF16_SUBLANES
    ltri = (jnp.arange(tm)[:, None] > jnp.arange(tm)[None, :]).astype(BF16)
    tok = lambda w: pl.BlockSpec((1, tm, w), lambda i, j: (i, j, 0))
    mem = pl.BlockSpec((1, kmem.shape[1], XA_WIDTH), lambda i, j: (i, 0, 0))
    weights = (p["pool_bd"], p["pool_scale"], p["w_out"], p["norm_xa"], p["w_xq"], p["xq_norm"], p["w_xo"],
               p["norm_ffn"], p["w_r_hi"], p["w_r_lo"], p["b_r"], ltri)
    return pl.pallas_call(
        functools.partial(_mixer_kernel, seq=t),
        grid=(b, nt),
        in_specs=[tok(D_MODEL), tok(D_INNER), tok(NA_WIDTH), tok(POOL_WIDTH),
                  pl.BlockSpec((1, BF16_SUBLANES, POOL_WIDTH), lambda i, j: (i, jnp.maximum(j * hb - 1, 0), 0)),
                  pl.BlockSpec((1, BF16_SUBLANES, POOL_WIDTH),
                               lambda i, j: (i, jnp.minimum((j + 1) * hb, nhalo - 1), 0)),
                  mem, mem] + [_full(w.shape) for w in weights],
        out_specs=[tok(D_MODEL), tok(D_MODEL),
                   pl.BlockSpec((1, 8, tm), lambda i, j: (i * nt + j, 0, 0)),
                   tok(LANES),
                   pl.BlockSpec((8, LANES), lambda i, j: (0, 0))],
        out_shape=[jax.ShapeDtypeStruct((b, t, D_MODEL), F32),
                   jax.ShapeDtypeStruct((b, t, D_MODEL), F32),
                   jax.ShapeDtypeStruct((b * nt, 8, tm), F32),
                   jax.ShapeDtypeStruct((b, t, LANES), F32),
                   jax.ShapeDtypeStruct((8, LANES), F32)],
        scratch_shapes=[pltpu.VMEM((8, LANES), F32)],
        compiler_params=_cparams(("arbitrary", "arbitrary")),
    )(x, y_ssd, y_na, u, u, u, kmem, vmem, *weights)


def _dispatch_kernel(pstart_ref, cnt_ref, e_ref, rk_ref, hf_ref, zero_ref, xs_ref, sem):
    tm = hf_ref.shape[0]
    i = pl.program_id(0)

    def row_copy(src_ref, src_row, dst_row):
        return pltpu.make_async_copy(src_ref.at[pl.ds(src_row, 1)], xs_ref.at[pl.ds(dst_row, 1)], sem)

    @pl.when(i == 0)
    def _():
        for e in range(N_EXPERTS):
            n_e = cnt_ref[e]
            npad = (EXPERT_BLOCK - n_e % EXPERT_BLOCK) % EXPERT_BLOCK
            base = pstart_ref[e] + n_e

            def start(r, _):
                row_copy(zero_ref, 0, base + r).start()
                return 0

            def wait(r, _):
                row_copy(zero_ref, 0, base + r).wait()
                return 0

            lax.fori_loop(0, npad, start, 0)
            lax.fori_loop(0, npad, wait, 0)

    def dest(tk, slot):
        return pstart_ref[e_ref[0, 0, slot * tm + tk]] + rk_ref[0, 0, slot * tm + tk]

    def start(tk, _):
        row_copy(hf_ref, tk, dest(tk, 0)).start(priority=0)
        row_copy(hf_ref, tk, dest(tk, 1)).start(priority=1)
        return 0

    lax.fori_loop(0, tm, start, 0, unroll=DMA_UNROLL)
    for _ in range(2):
        pltpu.make_async_copy(hf_ref, xs_ref.at[pl.ds(0, tm)], sem).wait()


def _dispatch(hf2d, e_idx, rank, pstart, counts, n_rows):
    n = hf2d.shape[0]
    tm = TOKEN_TILE
    smem_blk = pl.BlockSpec((1, 1, 2 * tm), lambda i, *_: (i, 0, 0), memory_space=pltpu.SMEM)
    return pl.pallas_call(
        _dispatch_kernel,
        grid_spec=pltpu.PrefetchScalarGridSpec(
            num_scalar_prefetch=2,
            grid=(n // tm,),
            in_specs=[smem_blk, smem_blk,
                      pl.BlockSpec((tm, D_MODEL), lambda i, *_: (i, 0)),
                      pl.BlockSpec((8, D_MODEL), lambda i, *_: (0, 0))],
            out_specs=pl.BlockSpec(memory_space=pl.ANY),
            scratch_shapes=[pltpu.SemaphoreType.DMA(())]),
        out_shape=jax.ShapeDtypeStruct((n_rows, D_MODEL), F32),
        compiler_params=_cparams(("arbitrary",)),
    )(pstart, counts, e_idx, rank, hf2d, jnp.zeros((8, D_MODEL), F32))


def _experts_kernel(blk_e_ref, nused_ref, x_ref, wg_ref, wu_ref, wd_ref, y_ref):
    x = x_ref[...].astype(BF16)
    hg = jnp.dot(x, wg_ref[0], preferred_element_type=F32)
    hu = jnp.dot(x, wu_ref[0], preferred_element_type=F32)
    h = (_silu(hg) * hu).astype(BF16)
    y_ref[...] = jnp.dot(h, wd_ref[0], preferred_element_type=F32)


def _experts(xs, blk_e, nused, w_gate, w_up, w_down):
    nblk = blk_e.shape[0]
    bm = EXPERT_BLOCK
    row = lambda j, be, nu: (jnp.minimum(j, nu[0] - 1), 0)
    wsel = lambda j, be, nu: (be[j], 0, 0)
    return pl.pallas_call(
        _experts_kernel,
        grid_spec=pltpu.PrefetchScalarGridSpec(
            num_scalar_prefetch=2,
            grid=(nblk,),
            in_specs=[pl.BlockSpec((bm, D_MODEL), row),
                      pl.BlockSpec((1, D_MODEL, D_EXPERT), wsel),
                      pl.BlockSpec((1, D_MODEL, D_EXPERT), wsel),
                      pl.BlockSpec((1, D_EXPERT, D_MODEL), wsel)],
            out_specs=pl.BlockSpec((bm, D_MODEL), row)),
        out_shape=jax.ShapeDtypeStruct((nblk * bm, D_MODEL), F32),
        compiler_params=_cparams(("arbitrary",)),
    )(blk_e, nused, xs, w_gate, w_up, w_down)


def _combine_kernel(pstart_ref, e_ref, rk_ref, x_ref, gate_ref, y_hbm, o_ref, ybuf, sem):
    tm = x_ref.shape[0]

    def row_copy(tk, slot):
        d = pstart_ref[e_ref[0, 0, slot * tm + tk]] + rk_ref[0, 0, slot * tm + tk]
        return pltpu.make_async_copy(y_hbm.at[pl.ds(d, 1)], ybuf.at[slot, pl.ds(tk, 1)], sem)

    def start(tk, _):
        row_copy(tk, 0).start(priority=0)
        row_copy(tk, 1).start(priority=1)
        return 0

    lax.fori_loop(0, tm, start, 0, unroll=DMA_UNROLL)
    for slot in range(2):
        pltpu.make_async_copy(y_hbm.at[pl.ds(0, tm)], ybuf.at[slot], sem).wait()
    g = gate_ref[...]
    o_ref[...] = x_ref[...] + g[:, 0:1] * ybuf[0] + g[:, 1:2] * ybuf[1]


def _combine(x2d, gates2d, y, e_idx, rank, pstart):
    n = x2d.shape[0]
    tm = TOKEN_TILE
    smem_blk = pl.BlockSpec((1, 1, 2 * tm), lambda i, *_: (i, 0, 0), memory_space=pltpu.SMEM)
    return pl.pallas_call(
        _combine_kernel,
        grid_spec=pltpu.PrefetchScalarGridSpec(
            num_scalar_prefetch=1,
            grid=(n // tm,),
            in_specs=[smem_blk, smem_blk,
                      pl.BlockSpec((tm, D_MODEL), lambda i, *_: (i, 0)),
                      pl.BlockSpec((tm, LANES), lambda i, *_: (i, 0)),
                      pl.BlockSpec(memory_space=pl.ANY)],
            out_specs=pl.BlockSpec((tm, D_MODEL), lambda i, *_: (i, 0)),
            scratch_shapes=[pltpu.VMEM((2, tm, D_MODEL), F32), pltpu.SemaphoreType.DMA(())]),
        out_shape=jax.ShapeDtypeStruct((n, D_MODEL), F32),
        compiler_params=_cparams(("arbitrary",)),
    )(pstart, e_idx, rank, x2d, gates2d, y)


def _moe(x2, hf, meta, gates, counts, w_gate, w_up, w_down):
    b, t, _ = x2.shape
    n = b * t
    tm = TOKEN_TILE
    bm = EXPERT_BLOCK
    nblk = (2 * n + N_EXPERTS * (bm - 1) + bm - 1) // bm
    cnt = counts[0, :N_EXPERTS].astype(jnp.int32)
    psz = (cnt + bm - 1) // bm * bm
    pend = jnp.cumsum(psz)
    pstart = (pend - psz).astype(jnp.int32)
    nused = jnp.maximum(pend[-1] // bm, 1).astype(jnp.int32).reshape(1)
    blk = jnp.minimum(jnp.arange(nblk, dtype=jnp.int32), nused[0] - 1)
    blk_e = jnp.minimum(jnp.sum(pend[None, :] <= (blk * bm)[:, None], axis=1), N_EXPERTS - 1).astype(jnp.int32)
    ids = meta.astype(jnp.int32)
    e_idx = ids[:, 0:2, :].reshape(n // tm, 1, 2 * tm)
    rank = ids[:, 2:4, :].reshape(n // tm, 1, 2 * tm)
    xs = _dispatch(hf.reshape(n, D_MODEL), e_idx, rank, pstart, cnt, nblk * bm)
    y = _experts(xs, blk_e, nused, w_gate, w_up, w_down)
    out = _combine(x2.reshape(n, D_MODEL), gates.reshape(n, LANES), y, e_idx, rank, pstart)
    return out.reshape(b, t, D_MODEL)


def _prep_layer(lp):
    w_in = lp["w_in"]
    c0 = D_INNER + CONV_CH
    c1 = c0 + 2 * SSD_HEADS
    w_cat = jnp.concatenate([w_in[:, :c0], w_in[:, c1:], w_in[:, c0:c1],
                             jnp.zeros((D_MODEL, DT_PAD - 2 * SSD_HEADS), w_in.dtype)], axis=1).astype(BF16)
    pool_bd = jnp.zeros((POOL_WIDTH, POOL_WIDTH), F32)
    for g in range(POOL_GROUPS):
        sl = slice(g * POOL_GROUP_DIM, (g + 1) * POOL_GROUP_DIM)
        pool_bd = pool_bd.at[sl, sl].set(lp["pool_w"][g].astype(F32))
    w_r = jnp.concatenate([lp["w_router_expert"], lp["w_router_group"],
                           jnp.zeros((D_MODEL, LANES - N_EXPERTS - N_EXPERT_GROUPS), F32)], axis=1).astype(F32)
    w_r_hi = w_r.astype(BF16)
    w_r_lo = (w_r - w_r_hi.astype(F32)).astype(BF16)
    b_r = jnp.concatenate([lp["b_router_expert"], lp["b_router_group"],
                           jnp.zeros((LANES - N_EXPERTS - N_EXPERT_GROUPS,), F32)]).reshape(1, LANES).astype(F32)
    row = lambda a, w: a.reshape(1, w).astype(F32)
    return dict(
        norm_mix=lp["norm_mix"], w_cat=w_cat,
        conv_w=lp["conv_w"], conv_b=lp["conv_b"], dt_bias=lp["dt_bias"], a_log=lp["a_log"],
        d_skip=lp["d_skip"], ssd_norm=lp["ssd_norm"],
        na_q_norm=lp["na_q_norm"], na_k_norm=lp["na_k_norm"], na_rpb=lp["na_rpb"],
        pool_bd=pool_bd.astype(BF16), pool_scale=row(lp["pool_scale"], POOL_WIDTH),
        w_out=lp["w_out"].astype(BF16), norm_xa=row(lp["norm_xa"], D_MODEL),
        norm_mem=lp["norm_mem"], w_xq=lp["w_xq"].astype(BF16), w_xkv=lp["w_xkv"].astype(BF16),
        xq_norm=row(jnp.tile(lp["xq_norm"], XA_HEADS), XA_WIDTH), xk_norm=lp["xk_norm"],
        w_xo=lp["w_xo"].astype(BF16), norm_ffn=row(lp["norm_ffn"], D_MODEL),
        w_r_hi=w_r_hi, w_r_lo=w_r_lo, b_r=b_r,
        w_e_gate=lp["w_e_gate"].astype(BF16), w_e_up=lp["w_e_up"].astype(BF16),
        w_e_down=lp["w_e_down"].astype(BF16),
    )


def _layer(x, mem, p, na_bias):
    b, t, _ = x.shape
    z, xbc, qkv, u, dt = _inproj(x.reshape(b * t, D_MODEL), p["norm_mix"], p["w_cat"])
    r3 = lambda a: a.reshape(b, t, a.shape[-1])
    y_ssd = _ssd(r3(z), r3(xbc), r3(dt), p["conv_w"], p["conv_b"], p["dt_bias"], p["a_log"], p["d_skip"],
                 p["ssd_norm"])
    y_na = _natten(r3(qkv), na_bias, p["na_q_norm"], p["na_k_norm"])
    kmem, vmem = _memkv(mem, p["norm_mem"], p["w_xkv"], p["xk_norm"])
    x2, hf, meta, gates, counts = _mixer(x, y_ssd, y_na, r3(u), kmem, vmem, p)
    return _moe(x2, hf, meta, gates, counts, p["w_e_gate"], p["w_e_up"], p["w_e_down"])


_LAYER_KEYS = ("norm_mix", "w_in", "conv_w", "conv_b", "dt_bias", "a_log", "d_skip", "ssd_norm", "na_q_norm",
               "na_k_norm", "na_rpb", "pool_w", "pool_scale", "w_out", "norm_xa", "norm_mem", "w_xq", "w_xkv",
               "xq_norm", "xk_norm", "w_xo", "norm_ffn", "w_router_group", "b_router_group", "w_router_expert",
               "b_router_expert", "w_e_gate", "w_e_up", "w_e_down")


def kernel(x_prompt, x_sample, mem_prompt, mem_sample, norm_mix, w_in, conv_w, conv_b, dt_bias, a_log, d_skip, ssd_norm, na_q_norm, na_k_norm, na_rpb, pool_w, pool_scale, w_out, norm_xa, norm_mem, w_xq, w_xkv, xq_norm, xk_norm, w_xo, norm_ffn, w_router_group, b_router_group, w_router_expert, b_router_expert, w_e_gate, w_e_up, w_e_down):
    stacked = dict(zip(_LAYER_KEYS, (norm_mix, w_in, conv_w, conv_b, dt_bias, a_log, d_skip, ssd_norm, na_q_norm,
                                     na_k_norm, na_rpb, pool_w, pool_scale, w_out, norm_xa, norm_mem, w_xq, w_xkv,
                                     xq_norm, xk_norm, w_xo, norm_ffn, w_router_group, b_router_group,
                                     w_router_expert, b_router_expert, w_e_gate, w_e_up, w_e_down)))
    depth = w_in.shape[0]
    layers = [_prep_layer({k: v[l] for k, v in stacked.items()}) for l in range(depth)]

    bias_cache = {}

    def trunk(x, mem):
        t = x.shape[1]
        for l, lp in enumerate(layers):
            if (l, t) not in bias_cache:
                bias_cache[(l, t)] = _na_bias(lp["na_rpb"], t)
            x = _layer(x, mem, lp, bias_cache[(l, t)])
        return x

    return trunk(x_prompt, mem_prompt), trunk(x_sample, mem_sample)
```

```python
import functools

import jax
import jax.numpy as jnp
import numpy as np
from jax import lax
from jax.experimental import pallas as pl
from jax.experimental.pallas import tpu as pltpu

F32 = jnp.float32
BF16 = jnp.bfloat16
HIGHEST = lax.Precision.HIGHEST

D_MODEL = 1024
GRID_W = 64
EPS = 1e-6
SSD_HEAD_DIM = 64
D_INNER = D_MODEL // 2
SSD_HEADS = D_INNER // SSD_HEAD_DIM
SSD_GROUPS = 2
SSD_STATE = 64
SSD_CHUNK = 128
CONV_W = 4
CONV_CH = D_INNER + 2 * SSD_GROUPS * SSD_STATE
NA_HEADS = 4
NA_HEAD_DIM = D_MODEL // 16
NA_WIDTH = NA_HEADS * NA_HEAD_DIM
NA_MAX_KH = 8
NA_KW = 16
POOL_WINDOWS = (2, 4, 8, 16)
POOL_GROUPS = 4
POOL_WIDTH = D_MODEL - D_INNER - NA_WIDTH
POOL_GROUP_DIM = POOL_WIDTH // POOL_GROUPS
XA_HEADS = 4
XA_HEAD_DIM = D_MODEL // 8
XA_WIDTH = XA_HEADS * XA_HEAD_DIM
N_EXPERT_GROUPS = 4
EXPERTS_PER_GROUP = 8
N_EXPERTS = N_EXPERT_GROUPS * EXPERTS_PER_GROUP
D_EXPERT = D_MODEL // 4

LANES = 128
BF16_SUBLANES = 16
VMEM_LIMIT_BYTES = 56 * 1024 * 1024

TOKEN_TILE = 512
NA_QUERY_ROWS = 8
NA_KEY_ROWS = 16
EXPERT_BLOCK = 512
DT_PAD = LANES
DMA_UNROLL = 8
NEG_BIG = -1e30


def _cparams(sem):
    return pltpu.CompilerParams(dimension_semantics=sem, vmem_limit_bytes=VMEM_LIMIT_BYTES)


def _sigmoid(x):
    return 1.0 / (1.0 + jnp.exp(-x))


def _silu(x):
    return x * _sigmoid(x)


def _softplus(x):
    return jnp.maximum(x, 0.0) + jnp.log(1.0 + jnp.exp(-jnp.abs(x)))


def _full(shape):
    n = len(shape)
    return pl.BlockSpec(shape, lambda *_: (0,) * n)


def _inproj_kernel(x_ref, g_ref, w_ref, z_ref, xbc_ref, qkv_ref, u_ref, dt_ref):
    x = x_ref[...]
    ms = jnp.mean(x * x, axis=-1, keepdims=True)
    h = (x * lax.rsqrt(ms + EPS) * g_ref[...]).astype(BF16)
    o = 0
    for ref in (z_ref, xbc_ref, qkv_ref, u_ref, dt_ref):
        w = ref.shape[-1]
        ref[...] = jnp.dot(h, w_ref[:, o:o + w], preferred_element_type=F32).astype(ref.dtype)
        o += w


def _inproj(x2d, gain, w_cat):
    n = x2d.shape[0]
    tm = TOKEN_TILE
    widths = (D_INNER, CONV_CH, 3 * NA_WIDTH, POOL_WIDTH, DT_PAD)
    dtypes = (BF16, BF16, BF16, BF16, F32)
    return pl.pallas_call(
        _inproj_kernel,
        grid=(n // tm,),
        in_specs=[pl.BlockSpec((tm, D_MODEL), lambda i: (i, 0)),
                  _full((1, D_MODEL)),
                  _full(w_cat.shape)],
        out_specs=[pl.BlockSpec((tm, w), lambda i: (i, 0)) for w in widths],
        out_shape=[jax.ShapeDtypeStruct((n, w), d) for w, d in zip(widths, dtypes)],
        compiler_params=_cparams(("arbitrary",)),
    )(x2d, gain.reshape(1, D_MODEL), w_cat)


def _split3(a):
    hi = a.astype(BF16)
    r = a - hi.astype(F32)
    mid = r.astype(BF16)
    lo = (r - mid.astype(F32)).astype(BF16)
    return hi, mid, lo


def _ssd_kernel(xc_ref, xp_ref, xn_ref, dt_ref, z_ref, cw_ref, cb_ref, dtb_ref, alog_ref, dsk_ref, nrm_ref, emat_ref,
                y_ref, state_ref, yf_ref, xs_c, bc_c, cbm_c, dt_c, *, nc):
    L = SSD_CHUNK
    P = SSD_HEAD_DIM
    NS = SSD_STATE
    HG = SSD_HEADS // SSD_GROUPS
    gn = SSD_GROUPS * NS
    j = pl.program_id(1)
    c = jnp.where(j < nc, j, 2 * nc - 1 - j)
    rows = pl.ds(pl.multiple_of(c * L, L), L)
    lane1 = lax.broadcasted_iota(jnp.int32, (1, LANES), 1)
    lo_half = lane1 < P
    ti = lax.broadcasted_iota(jnp.int32, (L, L), 0)
    si = lax.broadcasted_iota(jnp.int32, (L, L), 1)

    def prepare():
        cur = xc_ref[0].astype(F32)
        prev = xp_ref[0].astype(F32)
        nxt = xn_ref[0].astype(F32)
        has_prev = (c > 0).astype(F32)
        has_next = (c < nc - 1).astype(F32)
        p_last = prev[BF16_SUBLANES - 1:BF16_SUBLANES, :] * has_prev
        n0 = nxt[0:1, :] * has_next
        n1 = nxt[1:2, :] * has_next
        row = lax.broadcasted_iota(jnp.int32, (L, 1), 0)
        um1 = jnp.where(row == 0, p_last, pltpu.roll(cur, 1, 0))
        up1 = jnp.where(row == L - 1, n0, pltpu.roll(cur, L - 1, 0))
        up2 = jnp.where(row == L - 2, n0, jnp.where(row == L - 1, n1, pltpu.roll(cur, L - 2, 0)))
        cw = cw_ref[...]
        acc = cb_ref[...] + um1 * cw[0:1, :] + cur * cw[1:2, :] + up1 * cw[2:3, :] + up2 * cw[3:4, :]
        xbc = _silu(acc)
        xs = xbc[:, :D_INNER]
        bc = xbc[:, D_INNER:D_INNER + 2 * gn].astype(BF16)
        bfull = bc[:, :gn]
        cg = [jnp.where(lane1 // NS == g, bc[:, gn:], 0.0).astype(BF16) for g in range(SSD_GROUPS)]
        cb_mat = [lax.dot_general(cg[g], bfull, (((1,), (1,)), ((), ())), preferred_element_type=F32)
                  for g in range(SSD_GROUPS)]
        dt = _softplus(dt_ref[0] + dtb_ref[...])
        xs_c[rows, :] = xs
        bc_c[rows, :] = bc
        cbm_c[rows, :] = jnp.concatenate(cb_mat, axis=1)
        dt_c[rows, :] = dt
        return xs, bfull, cg, cb_mat, dt

    def recall():
        bc = bc_c[rows, :]
        cg = [jnp.where(lane1 // NS == g, bc[:, gn:], 0.0).astype(BF16) for g in range(SSD_GROUPS)]
        cbm = cbm_c[rows, :]
        return xs_c[rows, :], bc[:, :gn], cg, [cbm[:, g * L:(g + 1) * L] for g in range(SSD_GROUPS)], dt_c[rows, :]

    def scan_chunk(direction, xs, bfull, cg, cb_mat, dt):
        if direction == 0:
            mask = ti >= si
            edge = L - 1
        else:
            mask = si >= ti
            edge = 0
        la = dt * (-jnp.exp(alog_ref[...]))
        tri = mask.astype(BF16)
        csum = sum(jnp.dot(tri, part, preferred_element_type=F32) for part in _split3(la))
        csum_t = csum.T
        emat = emat_ref[direction]
        colb = sum(jnp.dot(part, emat, preferred_element_type=F32) for part in _split3(csum))
        tot = csum[edge:edge + 1, :]
        e_tot = jnp.exp(tot)
        e_in = jnp.exp(csum)
        e_out = jnp.exp(tot - csum)
        ys = []
        for g in range(SSD_GROUPS):
            s_old = state_ref[g]
            y_off = lax.dot_general(cg[g], s_old.astype(BF16), (((1,), (1,)), ((), ())),
                                    preferred_element_type=F32)
            xw = []
            for pr in range(HG // 2):
                h0 = g * HG + 2 * pr
                l0 = direction * SSD_HEADS + h0

                def col(a, l0=l0):
                    return jnp.where(lo_half, a[:, l0:l0 + 1], a[:, l0 + 1:l0 + 2])

                xdt = xs[:, h0 * P:(h0 + 2) * P] * col(dt)
                y_pair = y_off[:, 2 * pr * P:(2 * pr + 2) * P] * col(e_in)
                for hh, half in ((h0, lo_half), (h0 + 1, jnp.logical_not(lo_half))):
                    ll = direction * SSD_HEADS + hh
                    seg = colb[:, hh * L:(hh + 1) * L] - csum_t[ll:ll + 1, :]
                    dec = jnp.exp(jnp.where(mask, seg, NEG_BIG))
                    m = (cb_mat[g] * dec).astype(BF16)
                    y_pair += jnp.dot(m, jnp.where(half, xdt, 0.0).astype(BF16), preferred_element_type=F32)
                ys.append(y_pair)
                xw.append(xdt * col(e_out))
            xw = jnp.concatenate(xw, axis=1).astype(BF16)
            s_new = lax.dot_general(xw, bfull, (((0,), (0,)), ((), ())), preferred_element_type=F32)
            s_scaled = []
            for hl in range(HG):
                lane = direction * SSD_HEADS + g * HG + hl
                s_scaled.append(s_old[hl * P:(hl + 1) * P, :] * e_tot[:, lane:lane + 1])
            state_ref[g] = jnp.concatenate(s_scaled, axis=0) + s_new
        return jnp.concatenate(ys, axis=1)

    @pl.when(jnp.logical_or(j == 0, j == nc))
    def _():
        state_ref[...] = jnp.zeros_like(state_ref)

    @pl.when(j < nc)
    def _():
        yf_ref[rows, :] = scan_chunk(0, *prepare())

    @pl.when(j >= nc)
    def _():
        ops = recall()
        y = yf_ref[rows, :] + scan_chunk(1, *ops) + dsk_ref[...] * ops[0]
        y = y * _silu(z_ref[0].astype(F32))
        gw = D_INNER // SSD_GROUPS
        outs = []
        for g in range(SSD_GROUPS):
            yg = y[:, g * gw:(g + 1) * gw]
            outs.append(yg * lax.rsqrt(jnp.mean(yg * yg, axis=-1, keepdims=True) + EPS))
        y_ref[0] = (jnp.concatenate(outs, axis=1) * nrm_ref[...]).astype(y_ref.dtype)


def _ssd(z, xbc, dt, conv_w, conv_b, dt_bias, a_log, d_skip, ssd_norm):
    b, t, _ = z.shape
    L = SSD_CHUNK
    nc = t // L
    hb = L // BF16_SUBLANES
    nhalo = t // BF16_SUBLANES

    def chunk(j):
        return jnp.where(j < nc, j, 2 * nc - 1 - j)

    pad = DT_PAD - 2 * SSD_HEADS
    dtb = jnp.pad(dt_bias.reshape(1, -1).astype(F32), ((0, 0), (0, pad)))
    alog = jnp.pad(a_log.reshape(1, -1).astype(F32), ((0, 0), (0, pad)))
    dsk = jnp.repeat(d_skip.astype(F32), SSD_HEAD_DIM).reshape(1, D_INNER)
    sel = np.arange(DT_PAD)[None, :, None] == (np.arange(2)[:, None, None] * SSD_HEADS
                                                + np.arange(SSD_HEADS)[None, None, :])
    emat = jnp.asarray(np.repeat(sel, L, axis=2), BF16)
    return pl.pallas_call(
        functools.partial(_ssd_kernel, nc=nc),
        grid=(b, 2 * nc),
        in_specs=[
            pl.BlockSpec((1, L, CONV_CH), lambda i, j: (i, chunk(j), 0)),
            pl.BlockSpec((1, BF16_SUBLANES, CONV_CH), lambda i, j: (i, jnp.maximum(chunk(j) * hb - 1, 0), 0)),
            pl.BlockSpec((1, BF16_SUBLANES, CONV_CH),
                         lambda i, j: (i, jnp.minimum((chunk(j) + 1) * hb, nhalo - 1), 0)),
            pl.BlockSpec((1, L, DT_PAD), lambda i, j: (i, chunk(j), 0)),
            pl.BlockSpec((1, L, D_INNER), lambda i, j: (i, chunk(j), 0)),
            _full((CONV_W, CONV_CH)), _full((1, CONV_CH)), _full((1, DT_PAD)), _full((1, DT_PAD)),
            _full((1, D_INNER)), _full((1, D_INNER)), _full((2, DT_PAD, SSD_HEADS * L)),
        ],
        out_specs=pl.BlockSpec((1, L, D_INNER), lambda i, j: (i, jnp.where(j < nc, nc - 1, 2 * nc - 1 - j), 0)),
        out_shape=jax.ShapeDtypeStruct((b, t, D_INNER), BF16),
        scratch_shapes=[pltpu.VMEM((SSD_GROUPS, (SSD_HEADS // SSD_GROUPS) * SSD_HEAD_DIM, LANES), F32),
                        pltpu.VMEM((t, D_INNER), F32),
                        pltpu.VMEM((t, D_INNER), F32),
                        pltpu.VMEM((t, 2 * SSD_GROUPS * SSD_STATE), BF16),
                        pltpu.VMEM((t, SSD_GROUPS * L), F32),
                        pltpu.VMEM((t, DT_PAD), F32)],
        compiler_params=_cparams(("arbitrary", "arbitrary")),
    )(xbc, xbc, xbc, dt, z, conv_w.astype(F32), conv_b.reshape(1, CONV_CH).astype(F32), dtb, alog, dsk,
      ssd_norm.reshape(1, D_INNER).astype(F32), emat)


def _na_bias(rpb, t):
    r = t // GRID_W
    kh = min(NA_MAX_KH, r)
    nrb = r // NA_QUERY_ROWS
    rows = np.arange(r)
    row_start = np.clip(rows - NA_MAX_KH // 2, 0, r - kh)
    r0 = np.arange(nrb) * NA_QUERY_ROWS
    kr0 = np.clip(r0 - NA_MAX_KH // 2, 0, r - NA_KEY_ROWS)
    qrow = r0[:, None] + np.arange(NA_QUERY_ROWS)[None, :]
    krow = kr0[:, None] + np.arange(NA_KEY_ROWS)[None, :]
    rs = row_start[qrow]
    row_ok = (krow[:, None, :] >= rs[:, :, None]) & (krow[:, None, :] < rs[:, :, None] + kh)
    dr = np.clip(krow[:, None, :] - qrow[:, :, None] + (NA_MAX_KH - 1), 0, 2 * NA_MAX_KH - 2)
    cols = np.arange(GRID_W)
    col_start = np.clip(cols - NA_KW // 2, 0, GRID_W - NA_KW)
    col_ok = (cols[None, :] >= col_start[:, None]) & (cols[None, :] < col_start[:, None] + NA_KW)
    dc = np.clip(cols[None, :] - cols[:, None] + (NA_KW - 1), 0, 2 * NA_KW - 2)
    sel_r = (dr[..., None] == np.arange(2 * NA_MAX_KH - 1)) & row_ok[..., None]
    sel_c = (dc[..., None] == np.arange(2 * NA_KW - 1)) & col_ok[..., None]
    sel_r = jnp.asarray(sel_r, F32)
    sel_c = jnp.asarray(sel_c, F32)
    bias = jnp.einsum("hrc,bqkr,xyc->bhqxky", rpb.astype(F32), sel_r, sel_c, precision=HIGHEST)
    ok = jnp.einsum("bqk,xy->bqxky", jnp.asarray(row_ok, F32), jnp.asarray(col_ok, F32))
    bias = jnp.where(ok[:, None] > 0.5, bias, NEG_BIG)
    return bias.reshape(nrb, NA_HEADS, NA_QUERY_ROWS * GRID_W, NA_KEY_ROWS * GRID_W).astype(BF16)


def _natten_kernel(qkv_ref, bias_ref, qg_ref, kg_ref, seg_ref, o_ref, *, grid_rows):
    nq = NA_QUERY_ROWS * GRID_W
    nk = NA_KEY_ROWS * GRID_W
    rb = pl.program_id(0)
    r0 = rb * NA_QUERY_ROWS
    kr0 = jnp.clip(r0 - NA_MAX_KH // 2, 0, grid_rows - NA_KEY_ROWS)
    q0 = pl.multiple_of(r0 * GRID_W, nq)
    k0 = pl.multiple_of(kr0 * GRID_W, NA_MAX_KH // 2 * GRID_W)
    q = qkv_ref[0, pl.ds(q0, nq), 0:NA_WIDTH].astype(F32)
    k = qkv_ref[0, pl.ds(k0, nk), NA_WIDTH:2 * NA_WIDTH].astype(F32)
    v = qkv_ref[0, pl.ds(k0, nk), 2 * NA_WIDTH:3 * NA_WIDTH]
    seg = seg_ref[...]
    qms = jnp.dot(q * q, seg, precision=HIGHEST, preferred_element_type=F32)
    kms = jnp.dot(k * k, seg, precision=HIGHEST, preferred_element_type=F32)
    qn = q * lax.rsqrt(qms + EPS) * (qg_ref[...] * NA_HEAD_DIM ** -0.5)
    kn = (k * lax.rsqrt(kms + EPS) * kg_ref[...]).astype(BF16)
    lane_h = lax.broadcasted_iota(jnp.int32, (1, NA_WIDTH), 1) // NA_HEAD_DIM
    acc = jnp.zeros((nq, NA_WIDTH), F32)
    for h in range(NA_HEADS):
        hm = lane_h == h
        s = lax.dot_general(jnp.where(hm, qn, 0.0).astype(BF16), kn, (((1,), (1,)), ((), ())),
                            preferred_element_type=F32)
        s = s + bias_ref[0, h].astype(F32)
        p = jnp.exp(s - jnp.max(s, axis=-1, keepdims=True))
        l = jnp.sum(p, axis=-1, keepdims=True)
        o = jnp.dot(p.astype(BF16), v, preferred_element_type=F32)
        acc += jnp.where(hm, o / l, 0.0)
    o_ref[0] = acc.astype(o_ref.dtype)


def _natten(qkv, bias, q_norm, k_norm):
    b, t, _ = qkv.shape
    r = t // GRID_W
    nrb = r // NA_QUERY_ROWS
    nq = NA_QUERY_ROWS * GRID_W
    nk = NA_KEY_ROWS * GRID_W
    head = jnp.arange(NA_WIDTH) // NA_HEAD_DIM
    seg = (head[:, None] == head[None, :]).astype(F32) / NA_HEAD_DIM
    return pl.pallas_call(
        functools.partial(_natten_kernel, grid_rows=r),
        grid=(nrb, b),
        in_specs=[pl.BlockSpec((1, t, 3 * NA_WIDTH), lambda i, j: (j, 0, 0)),
                  pl.BlockSpec((1, NA_HEADS, nq, nk), lambda i, j: (i, 0, 0, 0)),
                  _full((1, NA_WIDTH)), _full((1, NA_WIDTH)), _full((NA_WIDTH, NA_WIDTH))],
        out_specs=pl.BlockSpec((1, nq, NA_WIDTH), lambda i, j: (j, i, 0)),
        out_shape=jax.ShapeDtypeStruct((b, t, NA_WIDTH), BF16),
        compiler_params=_cparams(("arbitrary", "arbitrary")),
    )(qkv, bias, jnp.tile(q_norm.astype(F32), NA_HEADS).reshape(1, NA_WIDTH),
      jnp.tile(k_norm.astype(F32), NA_HEADS).reshape(1, NA_WIDTH), seg)


def _memkv_kernel(m_ref, g_ref, w_ref, kg_ref, k_ref, v_ref):
    x = m_ref[0]
    ms = jnp.mean(x * x, axis=-1, keepdims=True)
    h = (x * lax.rsqrt(ms + EPS) * g_ref[...]).astype(BF16)
    kv = jnp.dot(h, w_ref[...], preferred_element_type=F32)
    ks = []
    for hd in range(XA_HEADS):
        kh = kv[:, hd * XA_HEAD_DIM:(hd + 1) * XA_HEAD_DIM]
        ks.append(kh * lax.rsqrt(jnp.mean(kh * kh, axis=-1, keepdims=True) + EPS))
    k_ref[0] = (jnp.concatenate(ks, axis=1) * kg_ref[...]).astype(k_ref.dtype)
    v_ref[0] = kv[:, XA_WIDTH:].astype(v_ref.dtype)


def _memkv(mem, norm_mem, w_xkv, xk_norm):
    b, m, _ = mem.shape
    return pl.pallas_call(
        _memkv_kernel,
        grid=(b,),
        in_specs=[pl.BlockSpec((1, m, D_MODEL), lambda i: (i, 0, 0)),
                  _full((1, D_MODEL)), _full((D_MODEL, 2 * XA_WIDTH)), _full((1, XA_WIDTH))],
        out_specs=[pl.BlockSpec((1, m, XA_WIDTH), lambda i: (i, 0, 0))] * 2,
        out_shape=[jax.ShapeDtypeStruct((b, m, XA_WIDTH), BF16)] * 2,
        compiler_params=_cparams(("arbitrary",)),
    )(mem, norm_mem.reshape(1, D_MODEL).astype(F32), w_xkv,
      jnp.tile(xk_norm.astype(F32), XA_HEADS).reshape(1, XA_WIDTH))


def _mixer_kernel(x_ref, ys_ref, yn_ref, u_ref, up_ref, un_ref, k_ref, v_ref,
                  pw_ref, psc_ref, wo_ref, gxa_ref, wq_ref, qg_ref, wxo_ref, gff_ref,
                  wrb_ref, br_ref, ltri_ref,
                  x2_ref, hf_ref, meta_ref, gate_ref, cnt_ref, carry_ref, *, seq):
    tm = x_ref.shape[1]
    halo = BF16_SUBLANES
    bi = pl.program_id(0)
    i = pl.program_id(1)
    nt = pl.num_programs(1)

    @pl.when(jnp.logical_and(bi == 0, i == 0))
    def _():
        carry_ref[...] = jnp.zeros_like(carry_ref)

    u = u_ref[0].astype(F32)
    up = up_ref[0].astype(F32) * (i > 0).astype(F32)
    un = un_ref[0].astype(F32) * (i < nt - 1).astype(F32)
    cat = jnp.concatenate([up, u, un], axis=0)
    n = tm + 2 * halo

    def sh(a, k):
        return pltpu.roll(a, (-k) % n, 0)

    a2 = cat + sh(cat, -1)
    a4 = sh(a2, 1) + sh(a2, -1)
    a8 = sh(a4, 2) + sh(a4, -2)
    a16 = sh(a8, 4) + sh(a8, -4)
    lane_g = lax.broadcasted_iota(jnp.int32, (1, POOL_WIDTH), 1) // POOL_GROUP_DIM
    wsum = jnp.where(lane_g == 0, a2, jnp.where(lane_g == 1, a4, jnp.where(lane_g == 2, a8, a16)))
    wsum = wsum[halo:halo + tm, :]
    half = jnp.where(lane_g == 0, POOL_WINDOWS[0] // 2,
                     jnp.where(lane_g == 1, POOL_WINDOWS[1] // 2,
                               jnp.where(lane_g == 2, POOL_WINDOWS[2] // 2, POOL_WINDOWS[3] // 2)))
    tpos = i * tm + lax.broadcasted_iota(jnp.int32, (tm, 1), 0)
    cnt = (jnp.minimum(tpos + half, seq) - jnp.maximum(tpos - half, 0)).astype(F32)
    d = wsum / cnt - u
    ypool = jnp.dot(d.astype(BF16), pw_ref[...], preferred_element_type=F32) * psc_ref[...]

    mix = jnp.dot(ys_ref[0], wo_ref[0:D_INNER, :], preferred_element_type=F32)
    mix += jnp.dot(yn_ref[0], wo_ref[D_INNER:D_INNER + NA_WIDTH, :], preferred_element_type=F32)
    mix += jnp.dot(ypool.astype(BF16), wo_ref[D_INNER + NA_WIDTH:, :], preferred_element_type=F32)
    x1 = x_ref[0] + mix

    hn = (x1 * lax.rsqrt(jnp.mean(x1 * x1, axis=-1, keepdims=True) + EPS) * gxa_ref[...]).astype(BF16)
    q = jnp.dot(hn, wq_ref[...], preferred_element_type=F32)
    kk = k_ref[0]
    vv = v_ref[0]
    scale = XA_HEAD_DIM ** -0.5
    outs = []
    for hd in range(XA_HEADS):
        sl = slice(hd * XA_HEAD_DIM, (hd + 1) * XA_HEAD_DIM)
        qh = q[:, sl]
        qh = (qh * lax.rsqrt(jnp.mean(qh * qh, axis=-1, keepdims=True) + EPS) * qg_ref[:, sl]).astype(BF16)
        s = lax.dot_general(qh, kk[:, sl], (((1,), (1,)), ((), ())), preferred_element_type=F32) * scale
        p = jnp.exp(s - jnp.max(s, axis=-1, keepdims=True))
        l = jnp.sum(p, axis=-1, keepdims=True)
        outs.append(jnp.dot(p.astype(BF16), vv[:, sl], preferred_element_type=F32) / l)
    att = jnp.concatenate(outs, axis=1).astype(BF16)
    x2 = x1 + jnp.dot(att, wxo_ref[...], preferred_element_type=F32)
    x2_ref[0] = x2

    hf = x2 * lax.rsqrt(jnp.mean(x2 * x2, axis=-1, keepdims=True) + EPS) * gff_ref[...]
    hf_ref[0] = hf
    h_hi = hf.astype(BF16)
    h_lo = (hf - h_hi.astype(F32)).astype(BF16)
    both = jnp.dot(h_hi, wrb_ref[...], preferred_element_type=F32)
    logits = (both[:, :LANES] + both[:, LANES:]
              + jnp.dot(h_lo, wrb_ref[:, :LANES], preferred_element_type=F32)) + br_ref[...]
    lane = lax.broadcasted_iota(jnp.int32, (1, LANES), 1)
    lane_f = lane.astype(F32)
    lane_grp = (lane // EXPERTS_PER_GROUP).astype(F32)
    is_g = jnp.logical_and(lane >= N_EXPERTS, lane < N_EXPERTS + N_EXPERT_GROUPS)
    gl = jnp.where(is_g, logits, NEG_BIG)
    gmax = jnp.max(gl, axis=-1, keepdims=True)
    g_sel = jnp.min(jnp.where(gl == gmax, lane_f, float(LANES)), axis=-1, keepdims=True) - N_EXPERTS
    g_gate = 1.0 / jnp.sum(jnp.where(is_g, jnp.exp(gl - gmax), 0.0), axis=-1, keepdims=True)
    in_grp = jnp.logical_and(lane < N_EXPERTS, lane_grp == g_sel)
    el = jnp.where(in_grp, logits, NEG_BIG)
    v1 = jnp.max(el, axis=-1, keepdims=True)
    e0 = jnp.min(jnp.where(el == v1, lane_f, float(LANES)), axis=-1, keepdims=True)
    el2 = jnp.where(lane_f == e0, NEG_BIG, el)
    v2 = jnp.max(el2, axis=-1, keepdims=True)
    e1 = jnp.min(jnp.where(el2 == v2, lane_f, float(LANES)), axis=-1, keepdims=True)
    w1 = jnp.exp(v2 - v1)
    gate0 = g_gate / (1.0 + w1)
    gate1 = g_gate * w1 / (1.0 + w1)

    oh0 = lane_f == e0
    oh1 = lane_f == e1
    cnt_tok = oh0.astype(F32) + oh1.astype(F32)
    before = jnp.dot(ltri_ref[...], cnt_tok.astype(BF16), preferred_element_type=F32) + carry_ref[0:1, :]
    rank0 = jnp.sum(jnp.where(oh0, before, 0.0), axis=-1, keepdims=True)
    rank1 = jnp.sum(jnp.where(oh1, before, 0.0), axis=-1, keepdims=True)
    new_carry = carry_ref[0:1, :] + jnp.sum(cnt_tok, axis=0, keepdims=True)
    carry_ref[...] = jnp.broadcast_to(new_carry, carry_ref.shape)
    cnt_ref[...] = jnp.broadcast_to(new_carry, cnt_ref.shape)

    slab = (jnp.where(lane == 0, e0, 0.0) + jnp.where(lane == 1, e1, 0.0)
            + jnp.where(lane == 2, rank0, 0.0) + jnp.where(lane == 3, rank1, 0.0))
    meta_ref[0] = slab.T[0:8, :]
    gate_ref[0] = jnp.where(lane == 0, gate0, 0.0) + jnp.where(lane == 1, gate1, 0.0)


def _mixer(x, y_ssd, y_na, u, kmem, vmem, p):
    b, t, _ = x.shape
    tm = TOKEN_TILE
    nt = t // tm
    hb = tm // BF16_SUBLANES
    nhalo = t // BF16_SUBLANES
    ltri = (jnp.arange(tm)[:, None] > jnp.arange(tm)[None, :]).astype(BF16)
    tok = lambda w: pl.BlockSpec((1, tm, w), lambda i, j: (i, j, 0))
    mem = pl.BlockSpec((1, kmem.shape[1], XA_WIDTH), lambda i, j: (i, 0, 0))
    weights = (p["pool_bd"], p["pool_scale"], p["w_out"], p["norm_xa"], p["w_xq"], p["xq_norm"], p["w_xo"],
               p["norm_ffn"], p["w_r_both"], p["b_r"], ltri)
    return pl.pallas_call(
        functools.partial(_mixer_kernel, seq=t),
        grid=(b, nt),
        in_specs=[tok(D_MODEL), tok(D_INNER), tok(NA_WIDTH), tok(POOL_WIDTH),
                  pl.BlockSpec((1, BF16_SUBLANES, POOL_WIDTH), lambda i, j: (i, jnp.maximum(j * hb - 1, 0), 0)),
                  pl.BlockSpec((1, BF16_SUBLANES, POOL_WIDTH),
                               lambda i, j: (i, jnp.minimum((j + 1) * hb, nhalo - 1), 0)),
                  mem, mem] + [_full(w.shape) for w in weights],
        out_specs=[tok(D_MODEL), tok(D_MODEL),
                   pl.BlockSpec((1, 8, tm), lambda i, j: (i * nt + j, 0, 0)),
                   tok(LANES),
                   pl.BlockSpec((8, LANES), lambda i, j: (0, 0))],
        out_shape=[jax.ShapeDtypeStruct((b, t, D_MODEL), F32),
                   jax.ShapeDtypeStruct((b, t, D_MODEL), F32),
                   jax.ShapeDtypeStruct((b * nt, 8, tm), F32),
                   jax.ShapeDtypeStruct((b, t, LANES), F32),
                   jax.ShapeDtypeStruct((8, LANES), F32)],
        scratch_shapes=[pltpu.VMEM((8, LANES), F32)],
        compiler_params=_cparams(("arbitrary", "arbitrary")),
    )(x, y_ssd, y_na, u, u, u, kmem, vmem, *weights)


def _dispatch_kernel(pstart_ref, cnt_ref, e_ref, rk_ref, hf_ref, zero_ref, xs_ref, sem):
    tm = hf_ref.shape[0]
    i = pl.program_id(0)

    def row_copy(src_ref, src_row, dst_row):
        return pltpu.make_async_copy(src_ref.at[pl.ds(src_row, 1)], xs_ref.at[pl.ds(dst_row, 1)], sem)

    @pl.when(i == 0)
    def _():
        for e in range(N_EXPERTS):
            n_e = cnt_ref[e]
            npad = (EXPERT_BLOCK - n_e % EXPERT_BLOCK) % EXPERT_BLOCK
            base = pstart_ref[e] + n_e

            def start(r, _):
                row_copy(zero_ref, 0, base + r).start()
                return 0

            def wait(r, _):
                row_copy(zero_ref, 0, base + r).wait()
                return 0

            lax.fori_loop(0, npad, start, 0)
            lax.fori_loop(0, npad, wait, 0)

    def dest(tk, slot):
        return pstart_ref[e_ref[0, 0, slot * tm + tk]] + rk_ref[0, 0, slot * tm + tk]

    def start(tk, _):
        row_copy(hf_ref, tk, dest(tk, 0)).start(priority=0)
        row_copy(hf_ref, tk, dest(tk, 1)).start(priority=1)
        return 0

    lax.fori_loop(0, tm, start, 0, unroll=DMA_UNROLL)
    for _ in range(2):
        pltpu.make_async_copy(hf_ref, xs_ref.at[pl.ds(0, tm)], sem).wait()


def _dispatch(hf2d, e_idx, rank, pstart, counts, n_rows):
    n = hf2d.shape[0]
    tm = TOKEN_TILE
    smem_blk = pl.BlockSpec((1, 1, 2 * tm), lambda i, *_: (i, 0, 0), memory_space=pltpu.SMEM)
    return pl.pallas_call(
        _dispatch_kernel,
        grid_spec=pltpu.PrefetchScalarGridSpec(
            num_scalar_prefetch=2,
            grid=(n // tm,),
            in_specs=[smem_blk, smem_blk,
                      pl.BlockSpec((tm, D_MODEL), lambda i, *_: (i, 0)),
                      pl.BlockSpec((8, D_MODEL), lambda i, *_: (0, 0))],
            out_specs=pl.BlockSpec(memory_space=pl.ANY),
            scratch_shapes=[pltpu.SemaphoreType.DMA(())]),
        out_shape=jax.ShapeDtypeStruct((n_rows, D_MODEL), F32),
        compiler_params=_cparams(("arbitrary",)),
    )(pstart, counts, e_idx, rank, hf2d, jnp.zeros((8, D_MODEL), F32))


def _experts_kernel(blk_e_ref, nused_ref, x_ref, wg_ref, wu_ref, wd_ref, y_ref):
    x = x_ref[...].astype(BF16)
    hg = jnp.dot(x, wg_ref[0], preferred_element_type=F32)
    hu = jnp.dot(x, wu_ref[0], preferred_element_type=F32)
    h = (_silu(hg) * hu).astype(BF16)
    y_ref[...] = jnp.dot(h, wd_ref[0], preferred_element_type=F32)


def _experts(xs, blk_e, nused, w_gate, w_up, w_down):
    nblk = blk_e.shape[0]
    bm = EXPERT_BLOCK
    row = lambda j, be, nu: (jnp.minimum(j, nu[0] - 1), 0)
    wsel = lambda j, be, nu: (be[j], 0, 0)
    return pl.pallas_call(
        _experts_kernel,
        grid_spec=pltpu.PrefetchScalarGridSpec(
            num_scalar_prefetch=2,
            grid=(nblk,),
            in_specs=[pl.BlockSpec((bm, D_MODEL), row),
                      pl.BlockSpec((1, D_MODEL, D_EXPERT), wsel),
                      pl.BlockSpec((1, D_MODEL, D_EXPERT), wsel),
                      pl.BlockSpec((1, D_EXPERT, D_MODEL), wsel)],
            out_specs=pl.BlockSpec((bm, D_MODEL), row)),
        out_shape=jax.ShapeDtypeStruct((nblk * bm, D_MODEL), F32),
        compiler_params=_cparams(("arbitrary",)),
    )(blk_e, nused, xs, w_gate, w_up, w_down)


def _combine_kernel(pstart_ref, e_ref, rk_ref, x_ref, gate_ref, y_hbm, o_ref, ybuf, sem):
    tm = x_ref.shape[0]

    def row_copy(tk, slot):
        d = pstart_ref[e_ref[0, 0, slot * tm + tk]] + rk_ref[0, 0, slot * tm + tk]
        return pltpu.make_async_copy(y_hbm.at[pl.ds(d, 1)], ybuf.at[slot, pl.ds(tk, 1)], sem)

    def start(tk, _):
        row_copy(tk, 0).start(priority=0)
        row_copy(tk, 1).start(priority=1)
        return 0

    lax.fori_loop(0, tm, start, 0, unroll=DMA_UNROLL)
    for slot in range(2):
        pltpu.make_async_copy(y_hbm.at[pl.ds(0, tm)], ybuf.at[slot], sem).wait()
    g = gate_ref[...]
    o_ref[...] = x_ref[...] + g[:, 0:1] * ybuf[0] + g[:, 1:2] * ybuf[1]


def _combine(x2d, gates2d, y, e_idx, rank, pstart):
    n = x2d.shape[0]
    tm = TOKEN_TILE
    smem_blk = pl.BlockSpec((1, 1, 2 * tm), lambda i, *_: (i, 0, 0), memory_space=pltpu.SMEM)
    return pl.pallas_call(
        _combine_kernel,
        grid_spec=pltpu.PrefetchScalarGridSpec(
            num_scalar_prefetch=1,
            grid=(n // tm,),
            in_specs=[smem_blk, smem_blk,
                      pl.BlockSpec((tm, D_MODEL), lambda i, *_: (i, 0)),
                      pl.BlockSpec((tm, LANES), lambda i, *_: (i, 0)),
                      pl.BlockSpec(memory_space=pl.ANY)],
            out_specs=pl.BlockSpec((tm, D_MODEL), lambda i, *_: (i, 0)),
            scratch_shapes=[pltpu.VMEM((2, tm, D_MODEL), F32), pltpu.SemaphoreType.DMA(())]),
        out_shape=jax.ShapeDtypeStruct((n, D_MODEL), F32),
        compiler_params=_cparams(("arbitrary",)),
    )(pstart, e_idx, rank, x2d, gates2d, y)


def _moe(x2, hf, meta, gates, counts, w_gate, w_up, w_down):
    b, t, _ = x2.shape
    n = b * t
    tm = TOKEN_TILE
    bm = EXPERT_BLOCK
    nblk = (2 * n + N_EXPERTS * (bm - 1) + bm - 1) // bm
    cnt = counts[0, :N_EXPERTS].astype(jnp.int32)
    psz = (cnt + bm - 1) // bm * bm
    pend = jnp.cumsum(psz)
    pstart = (pend - psz).astype(jnp.int32)
    nused = jnp.maximum(pend[-1] // bm, 1).astype(jnp.int32).reshape(1)
    blk = jnp.minimum(jnp.arange(nblk, dtype=jnp.int32), nused[0] - 1)
    blk_e = jnp.minimum(jnp.sum(pend[None, :] <= (blk * bm)[:, None], axis=1), N_EXPERTS - 1).astype(jnp.int32)
    ids = meta.astype(jnp.int32)
    e_idx = ids[:, 0:2, :].reshape(n // tm, 1, 2 * tm)
    rank = ids[:, 2:4, :].reshape(n // tm, 1, 2 * tm)
    xs = _dispatch(hf.reshape(n, D_MODEL), e_idx, rank, pstart, cnt, nblk * bm)
    y = _experts(xs, blk_e, nused, w_gate, w_up, w_down)
    out = _combine(x2.reshape(n, D_MODEL), gates.reshape(n, LANES), y, e_idx, rank, pstart)
    return out.reshape(b, t, D_MODEL)


def _prep_layer(lp):
    w_in = lp["w_in"]
    c0 = D_INNER + CONV_CH
    c1 = c0 + 2 * SSD_HEADS
    w_cat = jnp.concatenate([w_in[:, :c0], w_in[:, c1:], w_in[:, c0:c1],
                             jnp.zeros((D_MODEL, DT_PAD - 2 * SSD_HEADS), w_in.dtype)], axis=1).astype(BF16)
    pool_bd = jnp.zeros((POOL_WIDTH, POOL_WIDTH), F32)
    for g in range(POOL_GROUPS):
        sl = slice(g * POOL_GROUP_DIM, (g + 1) * POOL_GROUP_DIM)
        pool_bd = pool_bd.at[sl, sl].set(lp["pool_w"][g].astype(F32))
    w_r = jnp.concatenate([lp["w_router_expert"], lp["w_router_group"],
                           jnp.zeros((D_MODEL, LANES - N_EXPERTS - N_EXPERT_GROUPS), F32)], axis=1).astype(F32)
    w_r_hi = w_r.astype(BF16)
    w_r_lo = (w_r - w_r_hi.astype(F32)).astype(BF16)
    b_r = jnp.concatenate([lp["b_router_expert"], lp["b_router_group"],
                           jnp.zeros((LANES - N_EXPERTS - N_EXPERT_GROUPS,), F32)]).reshape(1, LANES).astype(F32)
    row = lambda a, w: a.reshape(1, w).astype(F32)
    return dict(
        norm_mix=lp["norm_mix"], w_cat=w_cat,
        conv_w=lp["conv_w"], conv_b=lp["conv_b"], dt_bias=lp["dt_bias"], a_log=lp["a_log"],
        d_skip=lp["d_skip"], ssd_norm=lp["ssd_norm"],
        na_q_norm=lp["na_q_norm"], na_k_norm=lp["na_k_norm"], na_rpb=lp["na_rpb"],
        pool_bd=pool_bd.astype(BF16), pool_scale=row(lp["pool_scale"], POOL_WIDTH),
        w_out=lp["w_out"].astype(BF16), norm_xa=row(lp["norm_xa"], D_MODEL),
        norm_mem=lp["norm_mem"], w_xq=lp["w_xq"].astype(BF16), w_xkv=lp["w_xkv"].astype(BF16),
        xq_norm=row(jnp.tile(lp["xq_norm"], XA_HEADS), XA_WIDTH), xk_norm=lp["xk_norm"],
        w_xo=lp["w_xo"].astype(BF16), norm_ffn=row(lp["norm_ffn"], D_MODEL),
        w_r_both=jnp.concatenate([w_r_hi, w_r_lo], axis=1), b_r=b_r,
        w_e_gate=lp["w_e_gate"].astype(BF16), w_e_up=lp["w_e_up"].astype(BF16),
        w_e_down=lp["w_e_down"].astype(BF16),
    )


def _layer(x, mem, p, na_bias):
    b, t, _ = x.shape
    z, xbc, qkv, u, dt = _inproj(x.reshape(b * t, D_MODEL), p["norm_mix"], p["w_cat"])
    r3 = lambda a: a.reshape(b, t, a.shape[-1])
    y_ssd = _ssd(r3(z), r3(xbc), r3(dt), p["conv_w"], p["conv_b"], p["dt_bias"], p["a_log"], p["d_skip"],
                 p["ssd_norm"])
    y_na = _natten(r3(qkv), na_bias, p["na_q_norm"], p["na_k_norm"])
    kmem, vmem = _memkv(mem, p["norm_mem"], p["w_xkv"], p["xk_norm"])
    x2, hf, meta, gates, counts = _mixer(x, y_ssd, y_na, r3(u), kmem, vmem, p)
    return _moe(x2, hf, meta, gates, counts, p["w_e_gate"], p["w_e_up"], p["w_e_down"])


_LAYER_KEYS = ("norm_mix", "w_in", "conv_w", "conv_b", "dt_bias", "a_log", "d_skip", "ssd_norm", "na_q_norm",
               "na_k_norm", "na_rpb", "pool_w", "pool_scale", "w_out", "norm_xa", "norm_mem", "w_xq", "w_xkv",
               "xq_norm", "xk_norm", "w_xo", "norm_ffn", "w_router_group", "b_router_group", "w_router_expert",
               "b_router_expert", "w_e_gate", "w_e_up", "w_e_down")


def kernel(x_prompt, x_sample, mem_prompt, mem_sample, norm_mix, w_in, conv_w, conv_b, dt_bias, a_log, d_skip, ssd_norm, na_q_norm, na_k_norm, na_rpb, pool_w, pool_scale, w_out, norm_xa, norm_mem, w_xq, w_xkv, xq_norm, xk_norm, w_xo, norm_ffn, w_router_group, b_router_group, w_router_expert, b_router_expert, w_e_gate, w_e_up, w_e_down):
    stacked = dict(zip(_LAYER_KEYS, (norm_mix, w_in, conv_w, conv_b, dt_bias, a_log, d_skip, ssd_norm, na_q_norm,
                                     na_k_norm, na_rpb, pool_w, pool_scale, w_out, norm_xa, norm_mem, w_xq, w_xkv,
                                     xq_norm, xk_norm, w_xo, norm_ffn, w_router_group, b_router_group,
                                     w_router_expert, b_router_expert, w_e_gate, w_e_up, w_e_down)))
    depth = w_in.shape[0]
    layers = [_prep_layer({k: v[l] for k, v in stacked.items()}) for l in range(depth)]

    bias_cache = {}

    def trunk(x, mem):
        t = x.shape[1]
        for l, lp in enumerate(layers):
            if (l, t) not in bias_cache:
                bias_cache[(l, t)] = _na_bias(lp["na_rpb"], t)
            x = _layer(x, mem, lp, bias_cache[(l, t)])
        return x

    return trunk(x_prompt, mem_prompt), trunk(x_sample, mem_sample)
```

```python
import functools

import jax
import jax.numpy as jnp
import numpy as np
from jax import lax
from jax.experimental import pallas as pl
from jax.experimental.pallas import tpu as pltpu

F32 = jnp.float32
BF16 = jnp.bfloat16
HIGHEST = lax.Precision.HIGHEST

D_MODEL = 1024
GRID_W = 64
EPS = 1e-6
SSD_HEAD_DIM = 64
D_INNER = D_MODEL // 2
SSD_HEADS = D_INNER // SSD_HEAD_DIM
SSD_GROUPS = 2
SSD_STATE = 64
SSD_CHUNK = 128
CONV_W = 4
CONV_CH = D_INNER + 2 * SSD_GROUPS * SSD_STATE
NA_HEADS = 4
NA_HEAD_DIM = D_MODEL // 16
NA_WIDTH = NA_HEADS * NA_HEAD_DIM
NA_MAX_KH = 8
NA_KW = 16
POOL_WINDOWS = (2, 4, 8, 16)
POOL_GROUPS = 4
POOL_WIDTH = D_MODEL - D_INNER - NA_WIDTH
POOL_GROUP_DIM = POOL_WIDTH // POOL_GROUPS
XA_HEADS = 4
XA_HEAD_DIM = D_MODEL // 8
XA_WIDTH = XA_HEADS * XA_HEAD_DIM
N_EXPERT_GROUPS = 4
EXPERTS_PER_GROUP = 8
N_EXPERTS = N_EXPERT_GROUPS * EXPERTS_PER_GROUP
D_EXPERT = D_MODEL // 4
PAIRS_PER_GROUP = EXPERTS_PER_GROUP * (EXPERTS_PER_GROUP - 1) // 2
N_COMBOS = N_EXPERT_GROUPS * PAIRS_PER_GROUP

LANES = 128
BF16_SUBLANES = 16
VMEM_LIMIT_BYTES = 56 * 1024 * 1024

TOKEN_TILE = 512
NA_QUERY_ROWS = 8
NA_KEY_ROWS = 16
EXPERT_BLOCK = 128
ROW_EXT = D_MODEL + LANES
DT_PAD = LANES
DMA_UNROLL = 8
NEG_BIG = -1e30


def _cparams(sem):
    return pltpu.CompilerParams(dimension_semantics=sem, vmem_limit_bytes=VMEM_LIMIT_BYTES)


def _sigmoid(x):
    return 1.0 / (1.0 + jnp.exp(-x))


def _silu(x):
    return x * _sigmoid(x)


def _softplus(x):
    return jnp.maximum(x, 0.0) + jnp.log(1.0 + jnp.exp(-jnp.abs(x)))


def _full(shape):
    n = len(shape)
    return pl.BlockSpec(shape, lambda *_: (0,) * n)


def _inproj_kernel(x_ref, g_ref, w_ref, z_ref, xbc_ref, qkv_ref, u_ref, dt_ref):
    x = x_ref[...]
    ms = jnp.mean(x * x, axis=-1, keepdims=True)
    h = (x * lax.rsqrt(ms + EPS) * g_ref[...]).astype(BF16)
    o = 0
    for ref in (z_ref, xbc_ref, qkv_ref, u_ref, dt_ref):
        w = ref.shape[-1]
        ref[...] = jnp.dot(h, w_ref[:, o:o + w], preferred_element_type=F32).astype(ref.dtype)
        o += w


def _inproj(x2d, gain, w_cat):
    n = x2d.shape[0]
    tm = TOKEN_TILE
    widths = (D_INNER, CONV_CH, 3 * NA_WIDTH, POOL_WIDTH, DT_PAD)
    dtypes = (BF16, BF16, BF16, BF16, F32)
    return pl.pallas_call(
        _inproj_kernel,
        grid=(n // tm,),
        in_specs=[pl.BlockSpec((tm, D_MODEL), lambda i: (i, 0)),
                  _full((1, D_MODEL)),
                  _full(w_cat.shape)],
        out_specs=[pl.BlockSpec((tm, w), lambda i: (i, 0)) for w in widths],
        out_shape=[jax.ShapeDtypeStruct((n, w), d) for w, d in zip(widths, dtypes)],
        compiler_params=_cparams(("arbitrary",)),
    )(x2d, gain.reshape(1, D_MODEL), w_cat)


def _split3(a):
    hi = a.astype(BF16)
    r = a - hi.astype(F32)
    mid = r.astype(BF16)
    lo = (r - mid.astype(F32)).astype(BF16)
    return hi, mid, lo


def _ssd_kernel(xc_ref, xp_ref, xn_ref, dt_ref, z_ref, cw_ref, cb_ref, dtb_ref, alog_ref, dsk_ref, nrm_ref, emat_ref,
                y_ref, state_ref, yf_ref, xs_c, bc_c, cbm_c, dt_c, *, nc):
    L = SSD_CHUNK
    P = SSD_HEAD_DIM
    NS = SSD_STATE
    HG = SSD_HEADS // SSD_GROUPS
    gn = SSD_GROUPS * NS
    j = pl.program_id(1)
    c = jnp.where(j < nc, j, 2 * nc - 1 - j)
    rows = pl.ds(pl.multiple_of(c * L, L), L)
    lane1 = lax.broadcasted_iota(jnp.int32, (1, LANES), 1)
    lo_half = lane1 < P
    ti = lax.broadcasted_iota(jnp.int32, (L, L), 0)
    si = lax.broadcasted_iota(jnp.int32, (L, L), 1)

    def prepare():
        cur = xc_ref[0].astype(F32)
        prev = xp_ref[0].astype(F32)
        nxt = xn_ref[0].astype(F32)
        has_prev = (c > 0).astype(F32)
        has_next = (c < nc - 1).astype(F32)
        p_last = prev[BF16_SUBLANES - 1:BF16_SUBLANES, :] * has_prev
        n0 = nxt[0:1, :] * has_next
        n1 = nxt[1:2, :] * has_next
        row = lax.broadcasted_iota(jnp.int32, (L, 1), 0)
        um1 = jnp.where(row == 0, p_last, pltpu.roll(cur, 1, 0))
        up1 = jnp.where(row == L - 1, n0, pltpu.roll(cur, L - 1, 0))
        up2 = jnp.where(row == L - 2, n0, jnp.where(row == L - 1, n1, pltpu.roll(cur, L - 2, 0)))
        cw = cw_ref[...]
        acc = cb_ref[...] + um1 * cw[0:1, :] + cur * cw[1:2, :] + up1 * cw[2:3, :] + up2 * cw[3:4, :]
        xbc = _silu(acc)
        xs = xbc[:, :D_INNER]
        bc = xbc[:, D_INNER:D_INNER + 2 * gn].astype(BF16)
        bfull = bc[:, :gn]
        cg = [jnp.where(lane1 // NS == g, bc[:, gn:], 0.0).astype(BF16) for g in range(SSD_GROUPS)]
        cb_mat = [lax.dot_general(cg[g], bfull, (((1,), (1,)), ((), ())), preferred_element_type=F32)
                  for g in range(SSD_GROUPS)]
        dt = _softplus(dt_ref[0] + dtb_ref[...])
        xs_c[rows, :] = xs
        bc_c[rows, :] = bc
        cbm_c[rows, :] = jnp.concatenate(cb_mat, axis=1)
        dt_c[rows, :] = dt
        return xs, bfull, cg, cb_mat, dt

    def recall():
        bc = bc_c[rows, :]
        cg = [jnp.where(lane1 // NS == g, bc[:, gn:], 0.0).astype(BF16) for g in range(SSD_GROUPS)]
        cbm = cbm_c[rows, :]
        return xs_c[rows, :], bc[:, :gn], cg, [cbm[:, g * L:(g + 1) * L] for g in range(SSD_GROUPS)], dt_c[rows, :]

    def scan_chunk(direction, xs, bfull, cg, cb_mat, dt):
        if direction == 0:
            mask = ti >= si
            edge = L - 1
        else:
            mask = si >= ti
            edge = 0
        la = dt * (-jnp.exp(alog_ref[...]))
        tri = mask.astype(BF16)
        csum = sum(jnp.dot(tri, part, preferred_element_type=F32) for part in _split3(la))
        csum_t = csum.T
        emat = emat_ref[direction]
        colb = sum(jnp.dot(part, emat, preferred_element_type=F32) for part in _split3(csum))
        tot = csum[edge:edge + 1, :]
        e_tot = jnp.exp(tot)
        e_in = jnp.exp(csum)
        e_out = jnp.exp(tot - csum)
        ys = []
        for g in range(SSD_GROUPS):
            s_old = state_ref[g]
            y_off = lax.dot_general(cg[g], s_old.astype(BF16), (((1,), (1,)), ((), ())),
                                    preferred_element_type=F32)
            xw = []
            for pr in range(HG // 2):
                h0 = g * HG + 2 * pr
                l0 = direction * SSD_HEADS + h0

                def col(a, l0=l0):
                    return jnp.where(lo_half, a[:, l0:l0 + 1], a[:, l0 + 1:l0 + 2])

                xdt = xs[:, h0 * P:(h0 + 2) * P] * col(dt)
                y_pair = y_off[:, 2 * pr * P:(2 * pr + 2) * P] * col(e_in)
                for hh, half in ((h0, lo_half), (h0 + 1, jnp.logical_not(lo_half))):
                    ll = direction * SSD_HEADS + hh
                    seg = colb[:, hh * L:(hh + 1) * L] - csum_t[ll:ll + 1, :]
                    dec = jnp.exp(jnp.where(mask, seg, NEG_BIG))
                    m = (cb_mat[g] * dec).astype(BF16)
                    y_pair += jnp.dot(m, jnp.where(half, xdt, 0.0).astype(BF16), preferred_element_type=F32)
                ys.append(y_pair)
                xw.append(xdt * col(e_out))
            xw = jnp.concatenate(xw, axis=1).astype(BF16)
            s_new = lax.dot_general(xw, bfull, (((0,), (0,)), ((), ())), preferred_element_type=F32)
            s_scaled = []
            for hl in range(HG):
                lane = direction * SSD_HEADS + g * HG + hl
                s_scaled.append(s_old[hl * P:(hl + 1) * P, :] * e_tot[:, lane:lane + 1])
            state_ref[g] = jnp.concatenate(s_scaled, axis=0) + s_new
        return jnp.concatenate(ys, axis=1)

    @pl.when(jnp.logical_or(j == 0, j == nc))
    def _():
        state_ref[...] = jnp.zeros_like(state_ref)

    @pl.when(j < nc)
    def _():
        yf_ref[rows, :] = scan_chunk(0, *prepare())

    @pl.when(j >= nc)
    def _():
        ops = recall()
        y = yf_ref[rows, :] + scan_chunk(1, *ops) + dsk_ref[...] * ops[0]
        y = y * _silu(z_ref[0].astype(F32))
        gw = D_INNER // SSD_GROUPS
        outs = []
        for g in range(SSD_GROUPS):
            yg = y[:, g * gw:(g + 1) * gw]
            outs.append(yg * lax.rsqrt(jnp.mean(yg * yg, axis=-1, keepdims=True) + EPS))
        y_ref[0] = (jnp.concatenate(outs, axis=1) * nrm_ref[...]).astype(y_ref.dtype)


def _ssd(z, xbc, dt, conv_w, conv_b, dt_bias, a_log, d_skip, ssd_norm):
    b, t, _ = z.shape
    L = SSD_CHUNK
    nc = t // L
    hb = L // BF16_SUBLANES
    nhalo = t // BF16_SUBLANES

    def chunk(j):
        return jnp.where(j < nc, j, 2 * nc - 1 - j)

    pad = DT_PAD - 2 * SSD_HEADS
    dtb = jnp.pad(dt_bias.reshape(1, -1).astype(F32), ((0, 0), (0, pad)))
    alog = jnp.pad(a_log.reshape(1, -1).astype(F32), ((0, 0), (0, pad)))
    dsk = jnp.repeat(d_skip.astype(F32), SSD_HEAD_DIM).reshape(1, D_INNER)
    sel = np.arange(DT_PAD)[None, :, None] == (np.arange(2)[:, None, None] * SSD_HEADS
                                                + np.arange(SSD_HEADS)[None, None, :])
    emat = jnp.asarray(np.repeat(sel, L, axis=2), BF16)
    return pl.pallas_call(
        functools.partial(_ssd_kernel, nc=nc),
        grid=(b, 2 * nc),
        in_specs=[
            pl.BlockSpec((1, L, CONV_CH), lambda i, j: (i, chunk(j), 0)),
            pl.BlockSpec((1, BF16_SUBLANES, CONV_CH), lambda i, j: (i, jnp.maximum(chunk(j) * hb - 1, 0), 0)),
            pl.BlockSpec((1, BF16_SUBLANES, CONV_CH),
                         lambda i, j: (i, jnp.minimum((chunk(j) + 1) * hb, nhalo - 1), 0)),
            pl.BlockSpec((1, L, DT_PAD), lambda i, j: (i, chunk(j), 0)),
            pl.BlockSpec((1, L, D_INNER), lambda i, j: (i, chunk(j), 0)),
            _full((CONV_W, CONV_CH)), _full((1, CONV_CH)), _full((1, DT_PAD)), _full((1, DT_PAD)),
            _full((1, D_INNER)), _full((1, D_INNER)), _full((2, DT_PAD, SSD_HEADS * L)),
        ],
        out_specs=pl.BlockSpec((1, L, D_INNER), lambda i, j: (i, jnp.where(j < nc, nc - 1, 2 * nc - 1 - j), 0)),
        out_shape=jax.ShapeDtypeStruct((b, t, D_INNER), BF16),
        scratch_shapes=[pltpu.VMEM((SSD_GROUPS, (SSD_HEADS // SSD_GROUPS) * SSD_HEAD_DIM, LANES), F32),
                        pltpu.VMEM((t, D_INNER), F32),
                        pltpu.VMEM((t, D_INNER), F32),
                        pltpu.VMEM((t, 2 * SSD_GROUPS * SSD_STATE), BF16),
                        pltpu.VMEM((t, SSD_GROUPS * L), F32),
                        pltpu.VMEM((t, DT_PAD), F32)],
        compiler_params=_cparams(("arbitrary", "arbitrary")),
    )(xbc, xbc, xbc, dt, z, conv_w.astype(F32), conv_b.reshape(1, CONV_CH).astype(F32), dtb, alog, dsk,
      ssd_norm.reshape(1, D_INNER).astype(F32), emat)


def _na_bias(rpb, t):
    r = t // GRID_W
    kh = min(NA_MAX_KH, r)
    nrb = r // NA_QUERY_ROWS
    rows = np.arange(r)
    row_start = np.clip(rows - NA_MAX_KH // 2, 0, r - kh)
    r0 = np.arange(nrb) * NA_QUERY_ROWS
    kr0 = np.clip(r0 - NA_MAX_KH // 2, 0, r - NA_KEY_ROWS)
    qrow = r0[:, None] + np.arange(NA_QUERY_ROWS)[None, :]
    krow = kr0[:, None] + np.arange(NA_KEY_ROWS)[None, :]
    rs = row_start[qrow]
    row_ok = (krow[:, None, :] >= rs[:, :, None]) & (krow[:, None, :] < rs[:, :, None] + kh)
    dr = np.clip(krow[:, None, :] - qrow[:, :, None] + (NA_MAX_KH - 1), 0, 2 * NA_MAX_KH - 2)
    cols = np.arange(GRID_W)
    col_start = np.clip(cols - NA_KW // 2, 0, GRID_W - NA_KW)
    col_ok = (cols[None, :] >= col_start[:, None]) & (cols[None, :] < col_start[:, None] + NA_KW)
    dc = np.clip(cols[None, :] - cols[:, None] + (NA_KW - 1), 0, 2 * NA_KW - 2)
    sel_r = (dr[..., None] == np.arange(2 * NA_MAX_KH - 1)) & row_ok[..., None]
    sel_c = (dc[..., None] == np.arange(2 * NA_KW - 1)) & col_ok[..., None]
    sel_r = jnp.asarray(sel_r, F32)
    sel_c = jnp.asarray(sel_c, F32)
    bias = jnp.einsum("hrc,bqkr,xyc->bhqxky", rpb.astype(F32), sel_r, sel_c, precision=HIGHEST)
    ok = jnp.einsum("bqk,xy->bqxky", jnp.asarray(row_ok, F32), jnp.asarray(col_ok, F32))
    bias = jnp.where(ok[:, None] > 0.5, bias, NEG_BIG)
    return bias.reshape(nrb, NA_HEADS, NA_QUERY_ROWS * GRID_W, NA_KEY_ROWS * GRID_W).astype(BF16)


def _natten_kernel(qkv_ref, bias_ref, qg_ref, kg_ref, seg_ref, o_ref, *, grid_rows):
    nq = NA_QUERY_ROWS * GRID_W
    nk = NA_KEY_ROWS * GRID_W
    rb = pl.program_id(0)
    r0 = rb * NA_QUERY_ROWS
    kr0 = jnp.clip(r0 - NA_MAX_KH // 2, 0, grid_rows - NA_KEY_ROWS)
    q0 = pl.multiple_of(r0 * GRID_W, nq)
    k0 = pl.multiple_of(kr0 * GRID_W, NA_MAX_KH // 2 * GRID_W)
    q = qkv_ref[0, pl.ds(q0, nq), 0:NA_WIDTH].astype(F32)
    k = qkv_ref[0, pl.ds(k0, nk), NA_WIDTH:2 * NA_WIDTH].astype(F32)
    v = qkv_ref[0, pl.ds(k0, nk), 2 * NA_WIDTH:3 * NA_WIDTH]
    seg = seg_ref[...]
    qms = jnp.dot(q * q, seg, precision=HIGHEST, preferred_element_type=F32)
    kms = jnp.dot(k * k, seg, precision=HIGHEST, preferred_element_type=F32)
    qn = q * lax.rsqrt(qms + EPS) * (qg_ref[...] * NA_HEAD_DIM ** -0.5)
    kn = (k * lax.rsqrt(kms + EPS) * kg_ref[...]).astype(BF16)
    lane_h = lax.broadcasted_iota(jnp.int32, (1, NA_WIDTH), 1) // NA_HEAD_DIM
    acc = jnp.zeros((nq, NA_WIDTH), F32)
    for h in range(NA_HEADS):
        hm = lane_h == h
        s = lax.dot_general(jnp.where(hm, qn, 0.0).astype(BF16), kn, (((1,), (1,)), ((), ())),
                            preferred_element_type=F32)
        s = s + bias_ref[0, h].astype(F32)
        p = jnp.exp(s - jnp.max(s, axis=-1, keepdims=True))
        l = jnp.sum(p, axis=-1, keepdims=True)
        o = jnp.dot(p.astype(BF16), v, preferred_element_type=F32)
        acc += jnp.where(hm, o / l, 0.0)
    o_ref[0] = acc.astype(o_ref.dtype)


def _natten(qkv, bias, q_norm, k_norm):
    b, t, _ = qkv.shape
    r = t // GRID_W
    nrb = r // NA_QUERY_ROWS
    nq = NA_QUERY_ROWS * GRID_W
    nk = NA_KEY_ROWS * GRID_W
    head = jnp.arange(NA_WIDTH) // NA_HEAD_DIM
    seg = (head[:, None] == head[None, :]).astype(F32) / NA_HEAD_DIM
    return pl.pallas_call(
        functools.partial(_natten_kernel, grid_rows=r),
        grid=(nrb, b),
        in_specs=[pl.BlockSpec((1, t, 3 * NA_WIDTH), lambda i, j: (j, 0, 0)),
                  pl.BlockSpec((1, NA_HEADS, nq, nk), lambda i, j: (i, 0, 0, 0)),
                  _full((1, NA_WIDTH)), _full((1, NA_WIDTH)), _full((NA_WIDTH, NA_WIDTH))],
        out_specs=pl.BlockSpec((1, nq, NA_WIDTH), lambda i, j: (j, i, 0)),
        out_shape=jax.ShapeDtypeStruct((b, t, NA_WIDTH), BF16),
        compiler_params=_cparams(("arbitrary", "arbitrary")),
    )(qkv, bias, jnp.tile(q_norm.astype(F32), NA_HEADS).reshape(1, NA_WIDTH),
      jnp.tile(k_norm.astype(F32), NA_HEADS).reshape(1, NA_WIDTH), seg)


def _memkv_kernel(m_ref, g_ref, w_ref, kg_ref, k_ref, v_ref):
    x = m_ref[0]
    ms = jnp.mean(x * x, axis=-1, keepdims=True)
    h = (x * lax.rsqrt(ms + EPS) * g_ref[...]).astype(BF16)
    kv = jnp.dot(h, w_ref[...], preferred_element_type=F32)
    ks = []
    for hd in range(XA_HEADS):
        kh = kv[:, hd * XA_HEAD_DIM:(hd + 1) * XA_HEAD_DIM]
        ks.append(kh * lax.rsqrt(jnp.mean(kh * kh, axis=-1, keepdims=True) + EPS))
    k_ref[0] = (jnp.concatenate(ks, axis=1) * kg_ref[...]).astype(k_ref.dtype)
    v_ref[0] = kv[:, XA_WIDTH:].astype(v_ref.dtype)


def _memkv(mem, norm_mem, w_xkv, xk_norm):
    b, m, _ = mem.shape
    return pl.pallas_call(
        _memkv_kernel,
        grid=(b,),
        in_specs=[pl.BlockSpec((1, m, D_MODEL), lambda i: (i, 0, 0)),
                  _full((1, D_MODEL)), _full((D_MODEL, 2 * XA_WIDTH)), _full((1, XA_WIDTH))],
        out_specs=[pl.BlockSpec((1, m, XA_WIDTH), lambda i: (i, 0, 0))] * 2,
        out_shape=[jax.ShapeDtypeStruct((b, m, XA_WIDTH), BF16)] * 2,
        compiler_params=_cparams(("arbitrary",)),
    )(mem, norm_mem.reshape(1, D_MODEL).astype(F32), w_xkv,
      jnp.tile(xk_norm.astype(F32), XA_HEADS).reshape(1, XA_WIDTH))


def _mixer_kernel(x_ref, ys_ref, yn_ref, u_ref, up_ref, un_ref, k_ref, v_ref,
                  pw_ref, psc_ref, wo_ref, gxa_ref, wq_ref, qg_ref, wxo_ref, gff_ref,
                  wrb_ref, br_ref, ltri_ref,
                  x2_ref, hf_ref, meta_ref, cnt_ref, carry_ref, *, seq):
    tm = x_ref.shape[1]
    halo = BF16_SUBLANES
    bi = pl.program_id(0)
    i = pl.program_id(1)
    nt = pl.num_programs(1)

    @pl.when(jnp.logical_and(bi == 0, i == 0))
    def _():
        carry_ref[...] = jnp.zeros_like(carry_ref)

    u = u_ref[0].astype(F32)
    up = up_ref[0].astype(F32) * (i > 0).astype(F32)
    un = un_ref[0].astype(F32) * (i < nt - 1).astype(F32)
    cat = jnp.concatenate([up, u, un], axis=0)
    n = tm + 2 * halo

    def sh(a, k):
        return pltpu.roll(a, (-k) % n, 0)

    a2 = cat + sh(cat, -1)
    a4 = sh(a2, 1) + sh(a2, -1)
    a8 = sh(a4, 2) + sh(a4, -2)
    a16 = sh(a8, 4) + sh(a8, -4)
    lane_g = lax.broadcasted_iota(jnp.int32, (1, POOL_WIDTH), 1) // POOL_GROUP_DIM
    wsum = jnp.where(lane_g == 0, a2, jnp.where(lane_g == 1, a4, jnp.where(lane_g == 2, a8, a16)))
    wsum = wsum[halo:halo + tm, :]
    half = jnp.where(lane_g == 0, POOL_WINDOWS[0] // 2,
                     jnp.where(lane_g == 1, POOL_WINDOWS[1] // 2,
                               jnp.where(lane_g == 2, POOL_WINDOWS[2] // 2, POOL_WINDOWS[3] // 2)))
    tpos = i * tm + lax.broadcasted_iota(jnp.int32, (tm, 1), 0)
    cnt = (jnp.minimum(tpos + half, seq) - jnp.maximum(tpos - half, 0)).astype(F32)
    d = wsum / cnt - u
    ypool = jnp.dot(d.astype(BF16), pw_ref[...], preferred_element_type=F32) * psc_ref[...]

    mix = jnp.dot(ys_ref[0], wo_ref[0:D_INNER, :], preferred_element_type=F32)
    mix += jnp.dot(yn_ref[0], wo_ref[D_INNER:D_INNER + NA_WIDTH, :], preferred_element_type=F32)
    mix += jnp.dot(ypool.astype(BF16), wo_ref[D_INNER + NA_WIDTH:, :], preferred_element_type=F32)
    x1 = x_ref[0] + mix

    hn = (x1 * lax.rsqrt(jnp.mean(x1 * x1, axis=-1, keepdims=True) + EPS) * gxa_ref[...]).astype(BF16)
    q = jnp.dot(hn, wq_ref[...], preferred_element_type=F32)
    kk = k_ref[0]
    vv = v_ref[0]
    scale = XA_HEAD_DIM ** -0.5
    outs = []
    for hd in range(XA_HEADS):
        sl = slice(hd * XA_HEAD_DIM, (hd + 1) * XA_HEAD_DIM)
        qh = q[:, sl]
        qh = (qh * lax.rsqrt(jnp.mean(qh * qh, axis=-1, keepdims=True) + EPS) * qg_ref[:, sl]).astype(BF16)
        s = lax.dot_general(qh, kk[:, sl], (((1,), (1,)), ((), ())), preferred_element_type=F32) * scale
        p = jnp.exp(s - jnp.max(s, axis=-1, keepdims=True))
        l = jnp.sum(p, axis=-1, keepdims=True)
        outs.append(jnp.dot(p.astype(BF16), vv[:, sl], preferred_element_type=F32) / l)
    att = jnp.concatenate(outs, axis=1).astype(BF16)
    x2 = x1 + jnp.dot(att, wxo_ref[...], preferred_element_type=F32)
    x2_ref[0] = x2

    hf = x2 * lax.rsqrt(jnp.mean(x2 * x2, axis=-1, keepdims=True) + EPS) * gff_ref[...]
    hf_ref[0, :, :D_MODEL] = hf
    h_hi = hf.astype(BF16)
    h_lo = (hf - h_hi.astype(F32)).astype(BF16)
    both = jnp.dot(h_hi, wrb_ref[...], preferred_element_type=F32)
    logits = (both[:, :LANES] + both[:, LANES:]
              + jnp.dot(h_lo, wrb_ref[:, :LANES], preferred_element_type=F32)) + br_ref[...]
    lane = lax.broadcasted_iota(jnp.int32, (1, LANES), 1)
    lane_f = lane.astype(F32)
    lane_grp = (lane // EXPERTS_PER_GROUP).astype(F32)
    is_g = jnp.logical_and(lane >= N_EXPERTS, lane < N_EXPERTS + N_EXPERT_GROUPS)
    gl = jnp.where(is_g, logits, NEG_BIG)
    gmax = jnp.max(gl, axis=-1, keepdims=True)
    g_sel = jnp.min(jnp.where(gl == gmax, lane_f, float(LANES)), axis=-1, keepdims=True) - N_EXPERTS
    g_gate = 1.0 / jnp.sum(jnp.where(is_g, jnp.exp(gl - gmax), 0.0), axis=-1, keepdims=True)
    in_grp = jnp.logical_and(lane < N_EXPERTS, lane_grp == g_sel)
    el = jnp.where(in_grp, logits, NEG_BIG)
    v1 = jnp.max(el, axis=-1, keepdims=True)
    e0 = jnp.min(jnp.where(el == v1, lane_f, float(LANES)), axis=-1, keepdims=True)
    el2 = jnp.where(lane_f == e0, NEG_BIG, el)
    v2 = jnp.max(el2, axis=-1, keepdims=True)
    e1 = jnp.min(jnp.where(el2 == v2, lane_f, float(LANES)), axis=-1, keepdims=True)
    w1 = jnp.exp(v2 - v1)
    gate0 = g_gate / (1.0 + w1)
    gate1 = g_gate * w1 / (1.0 + w1)

    base = g_sel * EXPERTS_PER_GROUP
    ea = jnp.minimum(e0, e1) - base
    eb = jnp.maximum(e0, e1) - base
    combo = g_sel * PAIRS_PER_GROUP + ea * EXPERTS_PER_GROUP - ea * (ea + 1.0) * 0.5 + (eb - ea - 1.0)
    gate_a = jnp.where(e0 < e1, gate0, gate1)
    gate_b = jnp.where(e0 < e1, gate1, gate0)
    hf_ref[0, :, D_MODEL:] = jnp.where(lane == 0, gate_a, 0.0) + jnp.where(lane == 1, gate_b, 0.0)

    oh = lane_f == combo
    cnt_tok = oh.astype(F32)
    before = jnp.dot(ltri_ref[...], cnt_tok.astype(BF16), preferred_element_type=F32) + carry_ref[0:1, :]
    rank = jnp.sum(jnp.where(oh, before, 0.0), axis=-1, keepdims=True)
    new_carry = carry_ref[0:1, :] + jnp.sum(cnt_tok, axis=0, keepdims=True)
    carry_ref[...] = jnp.broadcast_to(new_carry, carry_ref.shape)
    cnt_ref[...] = jnp.broadcast_to(new_carry, cnt_ref.shape)

    slab = jnp.where(lane == 0, combo, 0.0) + jnp.where(lane == 1, rank, 0.0)
    meta_ref[0] = slab.T[0:8, :]


def _mixer(x, y_ssd, y_na, u, kmem, vmem, p):
    b, t, _ = x.shape
    tm = TOKEN_TILE
    nt = t // tm
    hb = tm // BF16_SUBLANES
    nhalo = t // BF16_SUBLANES
    ltri = (jnp.arange(tm)[:, None] > jnp.arange(tm)[None, :]).astype(BF16)
    tok = lambda w: pl.BlockSpec((1, tm, w), lambda i, j: (i, j, 0))
    mem = pl.BlockSpec((1, kmem.shape[1], XA_WIDTH), lambda i, j: (i, 0, 0))
    weights = (p["pool_bd"], p["pool_scale"], p["w_out"], p["norm_xa"], p["w_xq"], p["xq_norm"], p["w_xo"],
               p["norm_ffn"], p["w_r_both"], p["b_r"], ltri)
    return pl.pallas_call(
        functools.partial(_mixer_kernel, seq=t),
        grid=(b, nt),
        in_specs=[tok(D_MODEL), tok(D_INNER), tok(NA_WIDTH), tok(POOL_WIDTH),
                  pl.BlockSpec((1, BF16_SUBLANES, POOL_WIDTH), lambda i, j: (i, jnp.maximum(j * hb - 1, 0), 0)),
                  pl.BlockSpec((1, BF16_SUBLANES, POOL_WIDTH),
                               lambda i, j: (i, jnp.minimum((j + 1) * hb, nhalo - 1), 0)),
                  mem, mem] + [_full(w.shape) for w in weights],
        out_specs=[tok(D_MODEL), tok(ROW_EXT),
                   pl.BlockSpec((1, 8, tm), lambda i, j: (i * nt + j, 0, 0)),
                   pl.BlockSpec((8, LANES), lambda i, j: (0, 0))],
        out_shape=[jax.ShapeDtypeStruct((b, t, D_MODEL), F32),
                   jax.ShapeDtypeStruct((b, t, ROW_EXT), F32),
                   jax.ShapeDtypeStruct((b * nt, 8, tm), F32),
                   jax.ShapeDtypeStruct((8, LANES), F32)],
        scratch_shapes=[pltpu.VMEM((8, LANES), F32)],
        compiler_params=_cparams(("arbitrary", "arbitrary")),
    )(x, y_ssd, y_na, u, u, u, kmem, vmem, *weights)


def _dispatch_kernel(pstart_ref, cnt_ref, cmb_ref, rk_ref, hf_ref, zero_ref, xs_ref, sem):
    tm = hf_ref.shape[0]
    i = pl.program_id(0)

    def row_copy(src_ref, src_row, dst_row):
        return pltpu.make_async_copy(src_ref.at[pl.ds(src_row, 1)], xs_ref.at[pl.ds(dst_row, 1)], sem)

    @pl.when(i == 0)
    def _():
        def per_combo(cm, _):
            n_c = cnt_ref[cm]
            npad = (EXPERT_BLOCK - n_c % EXPERT_BLOCK) % EXPERT_BLOCK
            base = pstart_ref[cm] + n_c

            def start(r, _):
                row_copy(zero_ref, 0, base + r).start()
                return 0

            def wait(r, _):
                row_copy(zero_ref, 0, base + r).wait()
                return 0

            lax.fori_loop(0, npad, start, 0)
            lax.fori_loop(0, npad, wait, 0)
            return 0

        lax.fori_loop(0, N_COMBOS, per_combo, 0)

    def start(pair, _):
        for u in range(2):
            tk = 2 * pair + u
            row_copy(hf_ref, tk, pstart_ref[cmb_ref[0, 0, tk]] + rk_ref[0, 0, tk]).start(priority=u)
        return 0

    lax.fori_loop(0, tm // 2, start, 0, unroll=DMA_UNROLL // 2)
    pltpu.make_async_copy(hf_ref, xs_ref.at[pl.ds(0, tm)], sem).wait()


def _dispatch(hf2d, combo, rank, pstart, counts, n_rows):
    n = hf2d.shape[0]
    tm = TOKEN_TILE
    smem_blk = pl.BlockSpec((1, 1, tm), lambda i, *_: (i, 0, 0), memory_space=pltpu.SMEM)
    return pl.pallas_call(
        _dispatch_kernel,
        grid_spec=pltpu.PrefetchScalarGridSpec(
            num_scalar_prefetch=2,
            grid=(n // tm,),
            in_specs=[smem_blk, smem_blk,
                      pl.BlockSpec((tm, ROW_EXT), lambda i, *_: (i, 0)),
                      pl.BlockSpec((8, ROW_EXT), lambda i, *_: (0, 0))],
            out_specs=pl.BlockSpec(memory_space=pl.ANY),
            scratch_shapes=[pltpu.SemaphoreType.DMA(())]),
        out_shape=jax.ShapeDtypeStruct((n_rows, ROW_EXT), F32),
        compiler_params=_cparams(("arbitrary",)),
    )(pstart, counts, combo, rank, hf2d, jnp.zeros((8, ROW_EXT), F32))


def _experts_kernel(ea_ref, eb_ref, nused_ref, x_ref, wga_ref, wua_ref, wda_ref, wgb_ref, wub_ref, wdb_ref, y_ref):
    x = x_ref[:, :D_MODEL].astype(BF16)
    gates = x_ref[:, D_MODEL:]

    def mlp(wg_ref, wu_ref, wd_ref):
        hg = jnp.dot(x, wg_ref[0], preferred_element_type=F32)
        hu = jnp.dot(x, wu_ref[0], preferred_element_type=F32)
        return jnp.dot((_silu(hg) * hu).astype(BF16), wd_ref[0], preferred_element_type=F32)

    y_ref[...] = mlp(wga_ref, wua_ref, wda_ref) * gates[:, 0:1] + mlp(wgb_ref, wub_ref, wdb_ref) * gates[:, 1:2]


def _experts(xs, blk_ea, blk_eb, nused, w_gate, w_up, w_down):
    nblk = blk_ea.shape[0]
    bm = EXPERT_BLOCK
    row = lambda j, ea, eb, nu: (jnp.minimum(j, nu[0] - 1), 0)
    sel_a = lambda j, ea, eb, nu: (ea[j], 0, 0)
    sel_b = lambda j, ea, eb, nu: (eb[j], 0, 0)
    up = lambda sel: pl.BlockSpec((1, D_MODEL, D_EXPERT), sel)
    down = lambda sel: pl.BlockSpec((1, D_EXPERT, D_MODEL), sel)
    return pl.pallas_call(
        _experts_kernel,
        grid_spec=pltpu.PrefetchScalarGridSpec(
            num_scalar_prefetch=3,
            grid=(nblk,),
            in_specs=[pl.BlockSpec((bm, ROW_EXT), row),
                      up(sel_a), up(sel_a), down(sel_a), up(sel_b), up(sel_b), down(sel_b)],
            out_specs=pl.BlockSpec((bm, D_MODEL), row)),
        out_shape=jax.ShapeDtypeStruct((nblk * bm, D_MODEL), F32),
        compiler_params=_cparams(("arbitrary",)),
    )(blk_ea, blk_eb, nused, xs, w_gate, w_up, w_down, w_gate, w_up, w_down)


def _combine_kernel(pstart_ref, cmb_ref, rk_ref, x_ref, y_hbm, o_ref, ybuf, sem):
    tm = x_ref.shape[0]

    def start(pair, _):
        for u in range(2):
            tk = 2 * pair + u
            d = pstart_ref[cmb_ref[0, 0, tk]] + rk_ref[0, 0, tk]
            pltpu.make_async_copy(y_hbm.at[pl.ds(d, 1)], ybuf.at[pl.ds(tk, 1)], sem).start(priority=u)
        return 0

    lax.fori_loop(0, tm // 2, start, 0, unroll=DMA_UNROLL // 2)
    pltpu.make_async_copy(y_hbm.at[pl.ds(0, tm)], ybuf, sem).wait()
    o_ref[...] = x_ref[...] + ybuf[...]


def _combine(x2d, y, combo, rank, pstart):
    n = x2d.shape[0]
    tm = TOKEN_TILE
    smem_blk = pl.BlockSpec((1, 1, tm), lambda i, *_: (i, 0, 0), memory_space=pltpu.SMEM)
    return pl.pallas_call(
        _combine_kernel,
        grid_spec=pltpu.PrefetchScalarGridSpec(
            num_scalar_prefetch=1,
            grid=(n // tm,),
            in_specs=[smem_blk, smem_blk,
                      pl.BlockSpec((tm, D_MODEL), lambda i, *_: (i, 0)),
                      pl.BlockSpec(memory_space=pl.ANY)],
            out_specs=pl.BlockSpec((tm, D_MODEL), lambda i, *_: (i, 0)),
            scratch_shapes=[pltpu.VMEM((tm, D_MODEL), F32), pltpu.SemaphoreType.DMA(())]),
        out_shape=jax.ShapeDtypeStruct((n, D_MODEL), F32),
        compiler_params=_cparams(("arbitrary",)),
    )(pstart, combo, rank, x2d, y)


_PAIR_A = np.array([a for a in range(EXPERTS_PER_GROUP) for _ in range(a + 1, EXPERTS_PER_GROUP)], np.int32)
_PAIR_B = np.array([b for a in range(EXPERTS_PER_GROUP) for b in range(a + 1, EXPERTS_PER_GROUP)], np.int32)


def _moe(x2, hf, meta, counts, w_gate, w_up, w_down):
    b, t, _ = x2.shape
    n = b * t
    bm = EXPERT_BLOCK
    nblk = (n + N_COMBOS * (bm - 1) + bm - 1) // bm
    cnt = counts[0, :N_COMBOS].astype(jnp.int32)
    psz = (cnt + bm - 1) // bm * bm
    pend = jnp.cumsum(psz)
    pstart = (pend - psz).astype(jnp.int32)
    nused = jnp.maximum(pend[-1] // bm, 1).astype(jnp.int32).reshape(1)
    blk = jnp.minimum(jnp.arange(nblk, dtype=jnp.int32), nused[0] - 1)
    blk_c = jnp.minimum(jnp.sum(pend[None, :] <= (blk * bm)[:, None], axis=1), N_COMBOS - 1).astype(jnp.int32)
    grp = blk_c // PAIRS_PER_GROUP
    blk_ea = (grp * EXPERTS_PER_GROUP + jnp.asarray(_PAIR_A)[blk_c % PAIRS_PER_GROUP]).astype(jnp.int32)
    blk_eb = (grp * EXPERTS_PER_GROUP + jnp.asarray(_PAIR_B)[blk_c % PAIRS_PER_GROUP]).astype(jnp.int32)
    ids = meta.astype(jnp.int32)
    combo = ids[:, 0:1, :]
    rank = ids[:, 1:2, :]
    xs = _dispatch(hf.reshape(n, ROW_EXT), combo, rank, pstart, cnt, nblk * bm)
    y = _experts(xs, blk_ea, blk_eb, nused, w_gate, w_up, w_down)
    out = _combine(x2.reshape(n, D_MODEL), y, combo, rank, pstart)
    return out.reshape(b, t, D_MODEL)


def _prep_layer(lp):
    w_in = lp["w_in"]
    c0 = D_INNER + CONV_CH
    c1 = c0 + 2 * SSD_HEADS
    w_cat = jnp.concatenate([w_in[:, :c0], w_in[:, c1:], w_in[:, c0:c1],
                             jnp.zeros((D_MODEL, DT_PAD - 2 * SSD_HEADS), w_in.dtype)], axis=1).astype(BF16)
    pool_bd = jnp.zeros((POOL_WIDTH, POOL_WIDTH), F32)
    for g in range(POOL_GROUPS):
        sl = slice(g * POOL_GROUP_DIM, (g + 1) * POOL_GROUP_DIM)
        pool_bd = pool_bd.at[sl, sl].set(lp["pool_w"][g].astype(F32))
    w_r = jnp.concatenate([lp["w_router_expert"], lp["w_router_group"],
                           jnp.zeros((D_MODEL, LANES - N_EXPERTS - N_EXPERT_GROUPS), F32)], axis=1).astype(F32)
    w_r_hi = w_r.astype(BF16)
    w_r_lo = (w_r - w_r_hi.astype(F32)).astype(BF16)
    b_r = jnp.concatenate([lp["b_router_expert"], lp["b_router_group"],
                           jnp.zeros((LANES - N_EXPERTS - N_EXPERT_GROUPS,), F32)]).reshape(1, LANES).astype(F32)
    row = lambda a, w: a.reshape(1, w).astype(F32)
    return dict(
        norm_mix=lp["norm_mix"], w_cat=w_cat,
        conv_w=lp["conv_w"], conv_b=lp["conv_b"], dt_bias=lp["dt_bias"], a_log=lp["a_log"],
        d_skip=lp["d_skip"], ssd_norm=lp["ssd_norm"],
        na_q_norm=lp["na_q_norm"], na_k_norm=lp["na_k_norm"], na_rpb=lp["na_rpb"],
        pool_bd=pool_bd.astype(BF16), pool_scale=row(lp["pool_scale"], POOL_WIDTH),
        w_out=lp["w_out"].astype(BF16), norm_xa=row(lp["norm_xa"], D_MODEL),
        norm_mem=lp["norm_mem"], w_xq=lp["w_xq"].astype(BF16), w_xkv=lp["w_xkv"].astype(BF16),
        xq_norm=row(jnp.tile(lp["xq_norm"], XA_HEADS), XA_WIDTH), xk_norm=lp["xk_norm"],
        w_xo=lp["w_xo"].astype(BF16), norm_ffn=row(lp["norm_ffn"], D_MODEL),
        w_r_both=jnp.concatenate([w_r_hi, w_r_lo], axis=1), b_r=b_r,
        w_e_gate=lp["w_e_gate"].astype(BF16), w_e_up=lp["w_e_up"].astype(BF16),
        w_e_down=lp["w_e_down"].astype(BF16),
    )


def _layer(x, mem, p, na_bias):
    b, t, _ = x.shape
    z, xbc, qkv, u, dt = _inproj(x.reshape(b * t, D_MODEL), p["norm_mix"], p["w_cat"])
    r3 = lambda a: a.reshape(b, t, a.shape[-1])
    y_ssd = _ssd(r3(z), r3(xbc), r3(dt), p["conv_w"], p["conv_b"], p["dt_bias"], p["a_log"], p["d_skip"],
                 p["ssd_norm"])
    y_na = _natten(r3(qkv), na_bias, p["na_q_norm"], p["na_k_norm"])
    kmem, vmem = _memkv(mem, p["norm_mem"], p["w_xkv"], p["xk_norm"])
    x2, hf, meta, counts = _mixer(x, y_ssd, y_na, r3(u), kmem, vmem, p)
    return _moe(x2, hf, meta, counts, p["w_e_gate"], p["w_e_up"], p["w_e_down"])


_LAYER_KEYS = ("norm_mix", "w_in", "conv_w", "conv_b", "dt_bias", "a_log", "d_skip", "ssd_norm", "na_q_norm",
               "na_k_norm", "na_rpb", "pool_w", "pool_scale", "w_out", "norm_xa", "norm_mem", "w_xq", "w_xkv",
               "xq_norm", "xk_norm", "w_xo", "norm_ffn", "w_router_group", "b_router_group", "w_router_expert",
               "b_router_expert", "w_e_gate", "w_e_up", "w_e_down")


def kernel(x_prompt, x_sample, mem_prompt, mem_sample, norm_mix, w_in, conv_w, conv_b, dt_bias, a_log, d_skip, ssd_norm, na_q_norm, na_k_norm, na_rpb, pool_w, pool_scale, w_out, norm_xa, norm_mem, w_xq, w_xkv, xq_norm, xk_norm, w_xo, norm_ffn, w_router_group, b_router_group, w_router_expert, b_router_expert, w_e_gate, w_e_up, w_e_down):
    stacked = dict(zip(_LAYER_KEYS, (norm_mix, w_in, conv_w, conv_b, dt_bias, a_log, d_skip, ssd_norm, na_q_norm,
                                     na_k_norm, na_rpb, pool_w, pool_scale, w_out, norm_xa, norm_mem, w_xq, w_xkv,
                                     xq_norm, xk_norm, w_xo, norm_ffn, w_router_group, b_router_group,
                                     w_router_expert, b_router_expert, w_e_gate, w_e_up, w_e_down)))
    depth = w_in.shape[0]
    layers = [_prep_layer({k: v[l] for k, v in stacked.items()}) for l in range(depth)]

    bias_cache = {}

    def trunk(x, mem):
        t = x.shape[1]
        for l, lp in enumerate(layers):
            if (l, t) not in bias_cache:
                bias_cache[(l, t)] = _na_bias(lp["na_rpb"], t)
            x = _layer(x, mem, lp, bias_cache[(l, t)])
        return x

    return trunk(x_prompt, mem_prompt), trunk(x_sample, mem_sample)
```

```python
import functools

import jax
import jax.numpy as jnp
import numpy as np
from jax import lax
from jax.experimental import pallas as pl
from jax.experimental.pallas import tpu as pltpu

F32 = jnp.float32
BF16 = jnp.bfloat16
HIGHEST = lax.Precision.HIGHEST

D_MODEL = 1024
GRID_W = 64
EPS = 1e-6
SSD_HEAD_DIM = 64
D_INNER = D_MODEL // 2
SSD_HEADS = D_INNER // SSD_HEAD_DIM
SSD_GROUPS = 2
SSD_STATE = 64
SSD_CHUNK = 128
CONV_W = 4
CONV_CH = D_INNER + 2 * SSD_GROUPS * SSD_STATE
NA_HEADS = 4
NA_HEAD_DIM = D_MODEL // 16
NA_WIDTH = NA_HEADS * NA_HEAD_DIM
NA_MAX_KH = 8
NA_KW = 16
POOL_WINDOWS = (2, 4, 8, 16)
POOL_GROUPS = 4
POOL_WIDTH = D_MODEL - D_INNER - NA_WIDTH
POOL_GROUP_DIM = POOL_WIDTH // POOL_GROUPS
XA_HEADS = 4
XA_HEAD_DIM = D_MODEL // 8
XA_WIDTH = XA_HEADS * XA_HEAD_DIM
N_EXPERT_GROUPS = 4
EXPERTS_PER_GROUP = 8
N_EXPERTS = N_EXPERT_GROUPS * EXPERTS_PER_GROUP
D_EXPERT = D_MODEL // 4
PAIRS_PER_GROUP = EXPERTS_PER_GROUP * (EXPERTS_PER_GROUP - 1) // 2
N_COMBOS = N_EXPERT_GROUPS * PAIRS_PER_GROUP

LANES = 128
BF16_SUBLANES = 16
VMEM_LIMIT_BYTES = 56 * 1024 * 1024

TOKEN_TILE = 512
NA_QUERY_ROWS = 8
NA_SUB_ROWS = 8
NA_SUBS = NA_QUERY_ROWS // NA_SUB_ROWS
NA_KEY_ROWS = NA_SUB_ROWS + NA_MAX_KH
EXPERT_BLOCK = 128
ROW_EXT = D_MODEL + LANES
SSD_STEP_CHUNKS = 2
DT_PAD = LANES
DMA_UNROLL = 8
NEG_BIG = -1e30


def _cparams(sem):
    return pltpu.CompilerParams(dimension_semantics=sem, vmem_limit_bytes=VMEM_LIMIT_BYTES)


def _sigmoid(x):
    return 1.0 / (1.0 + jnp.exp(-x))


def _silu(x):
    return x * _sigmoid(x)


def _softplus(x):
    return jnp.maximum(x, 0.0) + jnp.log(1.0 + jnp.exp(-jnp.abs(x)))


def _full(shape):
    n = len(shape)
    return pl.BlockSpec(shape, lambda *_: (0,) * n)


def _inproj_kernel(x_ref, g_ref, w_ref, z_ref, xbc_ref, qkv_ref, u_ref, dt_ref):
    x = x_ref[...]
    ms = jnp.mean(x * x, axis=-1, keepdims=True)
    h = (x * lax.rsqrt(ms + EPS) * g_ref[...]).astype(BF16)
    o = 0
    for ref in (z_ref, xbc_ref, qkv_ref, u_ref, dt_ref):
        w = ref.shape[-1]
        ref[...] = jnp.dot(h, w_ref[:, o:o + w], preferred_element_type=F32).astype(ref.dtype)
        o += w


def _inproj(x2d, gain, w_cat):
    n = x2d.shape[0]
    tm = TOKEN_TILE
    widths = (D_INNER, CONV_CH, 3 * NA_WIDTH, POOL_WIDTH, DT_PAD)
    dtypes = (BF16, BF16, BF16, BF16, F32)
    return pl.pallas_call(
        _inproj_kernel,
        grid=(n // tm,),
        in_specs=[pl.BlockSpec((tm, D_MODEL), lambda i: (i, 0)),
                  _full((1, D_MODEL)),
                  _full(w_cat.shape)],
        out_specs=[pl.BlockSpec((tm, w), lambda i: (i, 0)) for w in widths],
        out_shape=[jax.ShapeDtypeStruct((n, w), d) for w, d in zip(widths, dtypes)],
        compiler_params=_cparams(("arbitrary",)),
    )(x2d, gain.reshape(1, D_MODEL), w_cat)


def _split3(a):
    hi = a.astype(BF16)
    r = a - hi.astype(F32)
    mid = r.astype(BF16)
    lo = (r - mid.astype(F32)).astype(BF16)
    return hi, mid, lo


def _ssd_kernel(xc_ref, xp_ref, xn_ref, dt_ref, z_ref, cw_ref, cb_ref, dtb_ref, alog_ref, dsk_ref, nrm_ref, emat_ref,
                y_ref, state_ref, yf_ref, xs_c, bc_c, cbm_c, dt_c, *, nblocks):
    L = SSD_CHUNK
    LB = SSD_STEP_CHUNKS * L
    P = SSD_HEAD_DIM
    NS = SSD_STATE
    HG = SSD_HEADS // SSD_GROUPS
    gn = SSD_GROUPS * NS
    j = pl.program_id(1)
    c = jnp.where(j < nblocks, j, 2 * nblocks - 1 - j)
    row0 = pl.multiple_of(c * LB, LB)
    rows = pl.ds(row0, LB)
    lane1 = lax.broadcasted_iota(jnp.int32, (1, LANES), 1)
    lo_half = lane1 < P
    ti = lax.broadcasted_iota(jnp.int32, (L, L), 0)
    si = lax.broadcasted_iota(jnp.int32, (L, L), 1)

    def masked_c(bc):
        return [jnp.where(lane1 // NS == g, bc[:, gn:], 0.0).astype(BF16) for g in range(SSD_GROUPS)]

    def prepare():
        cur = xc_ref[0].astype(F32)
        prev = xp_ref[0].astype(F32)
        nxt = xn_ref[0].astype(F32)
        has_prev = (c > 0).astype(F32)
        has_next = (c < nblocks - 1).astype(F32)
        p_last = prev[BF16_SUBLANES - 1:BF16_SUBLANES, :] * has_prev
        n0 = nxt[0:1, :] * has_next
        n1 = nxt[1:2, :] * has_next
        row = lax.broadcasted_iota(jnp.int32, (LB, 1), 0)
        um1 = jnp.where(row == 0, p_last, pltpu.roll(cur, 1, 0))
        up1 = jnp.where(row == LB - 1, n0, pltpu.roll(cur, LB - 1, 0))
        up2 = jnp.where(row == LB - 2, n0, jnp.where(row == LB - 1, n1, pltpu.roll(cur, LB - 2, 0)))
        cw = cw_ref[...]
        acc = cb_ref[...] + um1 * cw[0:1, :] + cur * cw[1:2, :] + up1 * cw[2:3, :] + up2 * cw[3:4, :]
        xbc = _silu(acc)
        xs = xbc[:, :D_INNER]
        bc = xbc[:, D_INNER:D_INNER + 2 * gn].astype(BF16)
        dt = _softplus(dt_ref[0] + dtb_ref[...])
        xs_c[rows, :] = xs
        bc_c[rows, :] = bc
        dt_c[rows, :] = dt
        ops = []
        for sub in range(SSD_STEP_CHUNKS):
            sl = slice(sub * L, (sub + 1) * L)
            cg = masked_c(bc[sl])
            cb_mat = [lax.dot_general(cg[g], bc[sl, :gn], (((1,), (1,)), ((), ())), preferred_element_type=F32)
                      for g in range(SSD_GROUPS)]
            cbm_c[pl.ds(row0 + sub * L, L), :] = jnp.concatenate(cb_mat, axis=1)
            ops.append((xs[sl], bc[sl, :gn], cg, cb_mat, dt[sl]))
        return ops

    def recall():
        ops = []
        for sub in range(SSD_STEP_CHUNKS):
            r = pl.ds(row0 + sub * L, L)
            bc = bc_c[r, :]
            cbm = cbm_c[r, :]
            ops.append((xs_c[r, :], bc[:, :gn], masked_c(bc), [cbm[:, g * L:(g + 1) * L] for g in range(SSD_GROUPS)],
                        dt_c[r, :]))
        return ops

    def scan_chunk(direction, xs, bfull, cg, cb_mat, dt):
        if direction == 0:
            mask = ti >= si
            edge = L - 1
        else:
            mask = si >= ti
            edge = 0
        la = dt * (-jnp.exp(alog_ref[...]))
        tri = mask.astype(BF16)
        csum = sum(jnp.dot(tri, part, preferred_element_type=F32) for part in _split3(la))
        csum_t = csum.T
        emat = emat_ref[direction]
        colb = sum(jnp.dot(part, emat, preferred_element_type=F32) for part in _split3(csum))
        tot = csum[edge:edge + 1, :]
        e_tot = jnp.exp(tot)
        e_in = jnp.exp(csum)
        e_out = jnp.exp(tot - csum)
        ys = []
        for g in range(SSD_GROUPS):
            s_old = state_ref[g]
            y_off = lax.dot_general(cg[g], s_old.astype(BF16), (((1,), (1,)), ((), ())),
                                    preferred_element_type=F32)
            xw = []
            for pr in range(HG // 2):
                h0 = g * HG + 2 * pr
                l0 = direction * SSD_HEADS + h0

                def col(a, l0=l0):
                    return jnp.where(lo_half, a[:, l0:l0 + 1], a[:, l0 + 1:l0 + 2])

                xdt = xs[:, h0 * P:(h0 + 2) * P] * col(dt)
                y_pair = y_off[:, 2 * pr * P:(2 * pr + 2) * P] * col(e_in)
                for hh, half in ((h0, lo_half), (h0 + 1, jnp.logical_not(lo_half))):
                    ll = direction * SSD_HEADS + hh
                    seg = colb[:, hh * L:(hh + 1) * L] - csum_t[ll:ll + 1, :]
                    dec = jnp.exp(jnp.where(mask, seg, NEG_BIG))
                    m = (cb_mat[g] * dec).astype(BF16)
                    y_pair += jnp.dot(m, jnp.where(half, xdt, 0.0).astype(BF16), preferred_element_type=F32)
                ys.append(y_pair)
                xw.append(xdt * col(e_out))
            xw = jnp.concatenate(xw, axis=1).astype(BF16)
            s_new = lax.dot_general(xw, bfull, (((0,), (0,)), ((), ())), preferred_element_type=F32)
            s_scaled = []
            for hl in range(HG):
                lane = direction * SSD_HEADS + g * HG + hl
                s_scaled.append(s_old[hl * P:(hl + 1) * P, :] * e_tot[:, lane:lane + 1])
            state_ref[g] = jnp.concatenate(s_scaled, axis=0) + s_new
        return jnp.concatenate(ys, axis=1)

    @pl.when(jnp.logical_or(j == 0, j == nblocks))
    def _():
        state_ref[...] = jnp.zeros_like(state_ref)

    @pl.when(j < nblocks)
    def _():
        ops = prepare()
        for sub in range(SSD_STEP_CHUNKS):
            yf_ref[pl.ds(row0 + sub * L, L), :] = scan_chunk(0, *ops[sub])

    @pl.when(j >= nblocks)
    def _():
        ops = recall()
        for sub in reversed(range(SSD_STEP_CHUNKS)):
            sl = slice(sub * L, (sub + 1) * L)
            y = yf_ref[pl.ds(row0 + sub * L, L), :] + scan_chunk(1, *ops[sub]) + dsk_ref[...] * ops[sub][0]
            y = y * _silu(z_ref[0, sl, :].astype(F32))
            gw = D_INNER // SSD_GROUPS
            outs = []
            for g in range(SSD_GROUPS):
                yg = y[:, g * gw:(g + 1) * gw]
                outs.append(yg * lax.rsqrt(jnp.mean(yg * yg, axis=-1, keepdims=True) + EPS))
            y_ref[0, sl, :] = (jnp.concatenate(outs, axis=1) * nrm_ref[...]).astype(y_ref.dtype)


def _ssd(z, xbc, dt, conv_w, conv_b, dt_bias, a_log, d_skip, ssd_norm):
    b, t, _ = z.shape
    L = SSD_CHUNK
    lb = SSD_STEP_CHUNKS * L
    nb = t // lb
    hb = lb // BF16_SUBLANES
    nhalo = t // BF16_SUBLANES

    def blk(j):
        return jnp.where(j < nb, j, 2 * nb - 1 - j)

    pad = DT_PAD - 2 * SSD_HEADS
    dtb = jnp.pad(dt_bias.reshape(1, -1).astype(F32), ((0, 0), (0, pad)))
    alog = jnp.pad(a_log.reshape(1, -1).astype(F32), ((0, 0), (0, pad)))
    dsk = jnp.repeat(d_skip.astype(F32), SSD_HEAD_DIM).reshape(1, D_INNER)
    sel = np.arange(DT_PAD)[None, :, None] == (np.arange(2)[:, None, None] * SSD_HEADS
                                                + np.arange(SSD_HEADS)[None, None, :])
    emat = jnp.asarray(np.repeat(sel, L, axis=2), BF16)
    return pl.pallas_call(
        functools.partial(_ssd_kernel, nblocks=nb),
        grid=(b, 2 * nb),
        in_specs=[
            pl.BlockSpec((1, lb, CONV_CH), lambda i, j: (i, blk(j), 0)),
            pl.BlockSpec((1, BF16_SUBLANES, CONV_CH), lambda i, j: (i, jnp.maximum(blk(j) * hb - 1, 0), 0)),
            pl.BlockSpec((1, BF16_SUBLANES, CONV_CH),
                         lambda i, j: (i, jnp.minimum((blk(j) + 1) * hb, nhalo - 1), 0)),
            pl.BlockSpec((1, lb, DT_PAD), lambda i, j: (i, blk(j), 0)),
            pl.BlockSpec((1, lb, D_INNER), lambda i, j: (i, blk(j), 0)),
            _full((CONV_W, CONV_CH)), _full((1, CONV_CH)), _full((1, DT_PAD)), _full((1, DT_PAD)),
            _full((1, D_INNER)), _full((1, D_INNER)), _full((2, DT_PAD, SSD_HEADS * L)),
        ],
        out_specs=pl.BlockSpec((1, lb, D_INNER), lambda i, j: (i, jnp.where(j < nb, nb - 1, 2 * nb - 1 - j), 0)),
        out_shape=jax.ShapeDtypeStruct((b, t, D_INNER), BF16),
        scratch_shapes=[pltpu.VMEM((SSD_GROUPS, (SSD_HEADS // SSD_GROUPS) * SSD_HEAD_DIM, LANES), F32),
                        pltpu.VMEM((t, D_INNER), F32),
                        pltpu.VMEM((t, D_INNER), F32),
                        pltpu.VMEM((t, 2 * SSD_GROUPS * SSD_STATE), BF16),
                        pltpu.VMEM((t, SSD_GROUPS * L), F32),
                        pltpu.VMEM((t, DT_PAD), F32)],
        compiler_params=_cparams(("arbitrary", "arbitrary")),
    )(xbc, xbc, xbc, dt, z, conv_w.astype(F32), conv_b.reshape(1, CONV_CH).astype(F32), dtb, alog, dsk,
      ssd_norm.reshape(1, D_INNER).astype(F32), emat)


def _na_bias(rpb, t):
    r = t // GRID_W
    kh = min(NA_MAX_KH, r)
    nsb = r // NA_SUB_ROWS
    rows = np.arange(r)
    row_start = np.clip(rows - NA_MAX_KH // 2, 0, r - kh)
    r0 = np.arange(nsb) * NA_SUB_ROWS
    kr0 = np.clip(r0 - NA_MAX_KH // 2, 0, r - NA_KEY_ROWS)
    qrow = r0[:, None] + np.arange(NA_SUB_ROWS)[None, :]
    krow = kr0[:, None] + np.arange(NA_KEY_ROWS)[None, :]
    rs = row_start[qrow]
    row_ok = (krow[:, None, :] >= rs[:, :, None]) & (krow[:, None, :] < rs[:, :, None] + kh)
    dr = np.clip(krow[:, None, :] - qrow[:, :, None] + (NA_MAX_KH - 1), 0, 2 * NA_MAX_KH - 2)
    cols = np.arange(GRID_W)
    col_start = np.clip(cols - NA_KW // 2, 0, GRID_W - NA_KW)
    col_ok = (cols[None, :] >= col_start[:, None]) & (cols[None, :] < col_start[:, None] + NA_KW)
    dc = np.clip(cols[None, :] - cols[:, None] + (NA_KW - 1), 0, 2 * NA_KW - 2)
    sel_r = (dr[..., None] == np.arange(2 * NA_MAX_KH - 1)) & row_ok[..., None]
    sel_c = (dc[..., None] == np.arange(2 * NA_KW - 1)) & col_ok[..., None]
    sel_r = jnp.asarray(sel_r, F32)
    sel_c = jnp.asarray(sel_c, F32)
    bias = jnp.einsum("hrc,bqkr,xyc->bhqxky", rpb.astype(F32), sel_r, sel_c, precision=HIGHEST)
    ok = jnp.einsum("bqk,xy->bqxky", jnp.asarray(row_ok, F32), jnp.asarray(col_ok, F32))
    bias = jnp.where(ok[:, None] > 0.5, bias, NEG_BIG)
    return bias.reshape(nsb // NA_SUBS, NA_SUBS, NA_HEADS, NA_SUB_ROWS * GRID_W, NA_KEY_ROWS * GRID_W).astype(BF16)


def _natten_kernel(qkv_ref, bias_ref, qg_ref, kg_ref, seg_ref, o_ref, *, grid_rows):
    nq = NA_SUB_ROWS * GRID_W
    nk = NA_KEY_ROWS * GRID_W
    rb = pl.program_id(0)
    seg = seg_ref[...]
    lane_h = lax.broadcasted_iota(jnp.int32, (1, NA_WIDTH), 1) // NA_HEAD_DIM
    for sub in range(NA_SUBS):
        r0 = rb * NA_QUERY_ROWS + sub * NA_SUB_ROWS
        kr0 = jnp.clip(r0 - NA_MAX_KH // 2, 0, grid_rows - NA_KEY_ROWS)
        q0 = pl.multiple_of(r0 * GRID_W, nq)
        k0 = pl.multiple_of(kr0 * GRID_W, NA_MAX_KH // 2 * GRID_W)
        q = qkv_ref[0, pl.ds(q0, nq), 0:NA_WIDTH].astype(F32)
        k = qkv_ref[0, pl.ds(k0, nk), NA_WIDTH:2 * NA_WIDTH].astype(F32)
        v = qkv_ref[0, pl.ds(k0, nk), 2 * NA_WIDTH:3 * NA_WIDTH]
        qms = jnp.dot(q * q, seg, precision=HIGHEST, preferred_element_type=F32)
        kms = jnp.dot(k * k, seg, precision=HIGHEST, preferred_element_type=F32)
        qn = q * lax.rsqrt(qms + EPS) * (qg_ref[...] * NA_HEAD_DIM ** -0.5)
        kn = (k * lax.rsqrt(kms + EPS) * kg_ref[...]).astype(BF16)
        acc = jnp.zeros((nq, NA_WIDTH), F32)
        for h in range(NA_HEADS):
            hm = lane_h == h
            s = lax.dot_general(jnp.where(hm, qn, 0.0).astype(BF16), kn, (((1,), (1,)), ((), ())),
                                preferred_element_type=F32)
            s = s + bias_ref[0, sub, h].astype(F32)
            p = jnp.exp(s - jnp.max(s, axis=-1, keepdims=True))
            l = jnp.sum(p, axis=-1, keepdims=True)
            o = jnp.dot(p.astype(BF16), v, preferred_element_type=F32)
            acc += jnp.where(hm, o / l, 0.0)
        o_ref[0, sub * nq:(sub + 1) * nq, :] = acc.astype(o_ref.dtype)


def _natten(qkv, bias, q_norm, k_norm):
    b, t, _ = qkv.shape
    r = t // GRID_W
    nrb = r // NA_QUERY_ROWS
    nq = NA_QUERY_ROWS * GRID_W
    head = jnp.arange(NA_WIDTH) // NA_HEAD_DIM
    seg = (head[:, None] == head[None, :]).astype(F32) / NA_HEAD_DIM
    return pl.pallas_call(
        functools.partial(_natten_kernel, grid_rows=r),
        grid=(nrb, b),
        in_specs=[pl.BlockSpec((1, t, 3 * NA_WIDTH), lambda i, j: (j, 0, 0)),
                  pl.BlockSpec((1,) + bias.shape[1:], lambda i, j: (i, 0, 0, 0, 0)),
                  _full((1, NA_WIDTH)), _full((1, NA_WIDTH)), _full((NA_WIDTH, NA_WIDTH))],
        out_specs=pl.BlockSpec((1, nq, NA_WIDTH), lambda i, j: (j, i, 0)),
        out_shape=jax.ShapeDtypeStruct((b, t, NA_WIDTH), BF16),
        compiler_params=_cparams(("arbitrary", "arbitrary")),
    )(qkv, bias, jnp.tile(q_norm.astype(F32), NA_HEADS).reshape(1, NA_WIDTH),
      jnp.tile(k_norm.astype(F32), NA_HEADS).reshape(1, NA_WIDTH), seg)


def _memkv_kernel(m_ref, g_ref, w_ref, kg_ref, k_ref, v_ref):
    x = m_ref[0]
    ms = jnp.mean(x * x, axis=-1, keepdims=True)
    h = (x * lax.rsqrt(ms + EPS) * g_ref[...]).astype(BF16)
    kv = jnp.dot(h, w_ref[...], preferred_element_type=F32)
    ks = []
    for hd in range(XA_HEADS):
        kh = kv[:, hd * XA_HEAD_DIM:(hd + 1) * XA_HEAD_DIM]
        ks.append(kh * lax.rsqrt(jnp.mean(kh * kh, axis=-1, keepdims=True) + EPS))
    k_ref[0] = (jnp.concatenate(ks, axis=1) * kg_ref[...]).astype(k_ref.dtype)
    v_ref[0] = kv[:, XA_WIDTH:].astype(v_ref.dtype)


def _memkv(mem, norm_mem, w_xkv, xk_norm):
    b, m, _ = mem.shape
    return pl.pallas_call(
        _memkv_kernel,
        grid=(b,),
        in_specs=[pl.BlockSpec((1, m, D_MODEL), lambda i: (i, 0, 0)),
                  _full((1, D_MODEL)), _full((D_MODEL, 2 * XA_WIDTH)), _full((1, XA_WIDTH))],
        out_specs=[pl.BlockSpec((1, m, XA_WIDTH), lambda i: (i, 0, 0))] * 2,
        out_shape=[jax.ShapeDtypeStruct((b, m, XA_WIDTH), BF16)] * 2,
        compiler_params=_cparams(("arbitrary",)),
    )(mem, norm_mem.reshape(1, D_MODEL).astype(F32), w_xkv,
      jnp.tile(xk_norm.astype(F32), XA_HEADS).reshape(1, XA_WIDTH))


def _mixer_kernel(x_ref, ys_ref, yn_ref, u_ref, up_ref, un_ref, k_ref, v_ref,
                  pw_ref, psc_ref, wo_ref, gxa_ref, wq_ref, qg_ref, wxo_ref, gff_ref,
                  wrb_ref, br_ref, ltri_ref,
                  x2_ref, hf_ref, meta_ref, cnt_ref, carry_ref, *, seq):
    tm = x_ref.shape[1]
    halo = BF16_SUBLANES
    bi = pl.program_id(0)
    i = pl.program_id(1)
    nt = pl.num_programs(1)

    @pl.when(jnp.logical_and(bi == 0, i == 0))
    def _():
        carry_ref[...] = jnp.zeros_like(carry_ref)

    u = u_ref[0].astype(F32)
    up = up_ref[0].astype(F32) * (i > 0).astype(F32)
    un = un_ref[0].astype(F32) * (i < nt - 1).astype(F32)
    cat = jnp.concatenate([up, u, un], axis=0)
    n = tm + 2 * halo

    def sh(a, k):
        return pltpu.roll(a, (-k) % n, 0)

    a2 = cat + sh(cat, -1)
    a4 = sh(a2, 1) + sh(a2, -1)
    a8 = sh(a4, 2) + sh(a4, -2)
    a16 = sh(a8, 4) + sh(a8, -4)
    lane_g = lax.broadcasted_iota(jnp.int32, (1, POOL_WIDTH), 1) // POOL_GROUP_DIM
    wsum = jnp.where(lane_g == 0, a2, jnp.where(lane_g == 1, a4, jnp.where(lane_g == 2, a8, a16)))
    wsum = wsum[halo:halo + tm, :]
    half = jnp.where(lane_g == 0, POOL_WINDOWS[0] // 2,
                     jnp.where(lane_g == 1, POOL_WINDOWS[1] // 2,
                               jnp.where(lane_g == 2, POOL_WINDOWS[2] // 2, POOL_WINDOWS[3] // 2)))
    tpos = i * tm + lax.broadcasted_iota(jnp.int32, (tm, 1), 0)
    cnt = (jnp.minimum(tpos + half, seq) - jnp.maximum(tpos - half, 0)).astype(F32)
    d = wsum / cnt - u
    ypool = jnp.dot(d.astype(BF16), pw_ref[...], preferred_element_type=F32) * psc_ref[...]

    mix = jnp.dot(ys_ref[0], wo_ref[0:D_INNER, :], preferred_element_type=F32)
    mix += jnp.dot(yn_ref[0], wo_ref[D_INNER:D_INNER + NA_WIDTH, :], preferred_element_type=F32)
    mix += jnp.dot(ypool.astype(BF16), wo_ref[D_INNER + NA_WIDTH:, :], preferred_element_type=F32)
    x1 = x_ref[0] + mix

    hn = (x1 * lax.rsqrt(jnp.mean(x1 * x1, axis=-1, keepdims=True) + EPS) * gxa_ref[...]).astype(BF16)
    q = jnp.dot(hn, wq_ref[...], preferred_element_type=F32)
    kk = k_ref[0]
    vv = v_ref[0]
    scale = XA_HEAD_DIM ** -0.5
    outs = []
    for hd in range(XA_HEADS):
        sl = slice(hd * XA_HEAD_DIM, (hd + 1) * XA_HEAD_DIM)
        qh = q[:, sl]
        qh = (qh * lax.rsqrt(jnp.mean(qh * qh, axis=-1, keepdims=True) + EPS) * qg_ref[:, sl]).astype(BF16)
        s = lax.dot_general(qh, kk[:, sl], (((1,), (1,)), ((), ())), preferred_element_type=F32) * scale
        p = jnp.exp(s - jnp.max(s, axis=-1, keepdims=True))
        l = jnp.sum(p, axis=-1, keepdims=True)
        outs.append(jnp.dot(p.astype(BF16), vv[:, sl], preferred_element_type=F32) / l)
    att = jnp.concatenate(outs, axis=1).astype(BF16)
    x2 = x1 + jnp.dot(att, wxo_ref[...], preferred_element_type=F32)
    x2_ref[0] = x2

    hf = x2 * lax.rsqrt(jnp.mean(x2 * x2, axis=-1, keepdims=True) + EPS) * gff_ref[...]
    hf_ref[0, :, :D_MODEL] = hf
    h_hi = hf.astype(BF16)
    h_lo = (hf - h_hi.astype(F32)).astype(BF16)
    both = jnp.dot(h_hi, wrb_ref[...], preferred_element_type=F32)
    logits = (both[:, :LANES] + both[:, LANES:]
              + jnp.dot(h_lo, wrb_ref[:, :LANES], preferred_element_type=F32)) + br_ref[...]
    lane = lax.broadcasted_iota(jnp.int32, (1, LANES), 1)
    lane_f = lane.astype(F32)
    lane_grp = (lane // EXPERTS_PER_GROUP).astype(F32)
    is_g = jnp.logical_and(lane >= N_EXPERTS, lane < N_EXPERTS + N_EXPERT_GROUPS)
    gl = jnp.where(is_g, logits, NEG_BIG)
    gmax = jnp.max(gl, axis=-1, keepdims=True)
    g_sel = jnp.min(jnp.where(gl == gmax, lane_f, float(LANES)), axis=-1, keepdims=True) - N_EXPERTS
    g_gate = 1.0 / jnp.sum(jnp.where(is_g, jnp.exp(gl - gmax), 0.0), axis=-1, keepdims=True)
    in_grp = jnp.logical_and(lane < N_EXPERTS, lane_grp == g_sel)
    el = jnp.where(in_grp, logits, NEG_BIG)
    v1 = jnp.max(el, axis=-1, keepdims=True)
    e0 = jnp.min(jnp.where(el == v1, lane_f, float(LANES)), axis=-1, keepdims=True)
    el2 = jnp.where(lane_f == e0, NEG_BIG, el)
    v2 = jnp.max(el2, axis=-1, keepdims=True)
    e1 = jnp.min(jnp.where(el2 == v2, lane_f, float(LANES)), axis=-1, keepdims=True)
    w1 = jnp.exp(v2 - v1)
    gate0 = g_gate / (1.0 + w1)
    gate1 = g_gate * w1 / (1.0 + w1)

    base = g_sel * EXPERTS_PER_GROUP
    ea = jnp.minimum(e0, e1) - base
    eb = jnp.maximum(e0, e1) - base
    combo = g_sel * PAIRS_PER_GROUP + ea * EXPERTS_PER_GROUP - ea * (ea + 1.0) * 0.5 + (eb - ea - 1.0)
    gate_a = jnp.where(e0 < e1, gate0, gate1)
    gate_b = jnp.where(e0 < e1, gate1, gate0)
    hf_ref[0, :, D_MODEL:] = jnp.where(lane == 0, gate_a, 0.0) + jnp.where(lane == 1, gate_b, 0.0)

    oh = lane_f == combo
    cnt_tok = oh.astype(F32)
    before = jnp.dot(ltri_ref[...], cnt_tok.astype(BF16), preferred_element_type=F32) + carry_ref[0:1, :]
    rank = jnp.sum(jnp.where(oh, before, 0.0), axis=-1, keepdims=True)
    new_carry = carry_ref[0:1, :] + jnp.sum(cnt_tok, axis=0, keepdims=True)
    carry_ref[...] = jnp.broadcast_to(new_carry, carry_ref.shape)
    cnt_ref[...] = jnp.broadcast_to(new_carry, cnt_ref.shape)

    slab = jnp.where(lane == 0, combo, 0.0) + jnp.where(lane == 1, rank, 0.0)
    meta_ref[0] = slab.T[0:8, :]


def _mixer(x, y_ssd, y_na, u, kmem, vmem, p):
    b, t, _ = x.shape
    tm = TOKEN_TILE
    nt = t // tm
    hb = tm // BF16_SUBLANES
    nhalo = t // BF16_SUBLANES
    ltri = (jnp.arange(tm)[:, None] > jnp.arange(tm)[None, :]).astype(BF16)
    tok = lambda w: pl.BlockSpec((1, tm, w), lambda i, j: (i, j, 0))
    mem = pl.BlockSpec((1, kmem.shape[1], XA_WIDTH), lambda i, j: (i, 0, 0))
    weights = (p["pool_bd"], p["pool_scale"], p["w_out"], p["norm_xa"], p["w_xq"], p["xq_norm"], p["w_xo"],
               p["norm_ffn"], p["w_r_both"], p["b_r"], ltri)
    return pl.pallas_call(
        functools.partial(_mixer_kernel, seq=t),
        grid=(b, nt),
        in_specs=[tok(D_MODEL), tok(D_INNER), tok(NA_WIDTH), tok(POOL_WIDTH),
                  pl.BlockSpec((1, BF16_SUBLANES, POOL_WIDTH), lambda i, j: (i, jnp.maximum(j * hb - 1, 0), 0)),
                  pl.BlockSpec((1, BF16_SUBLANES, POOL_WIDTH),
                               lambda i, j: (i, jnp.minimum((j + 1) * hb, nhalo - 1), 0)),
                  mem, mem] + [_full(w.shape) for w in weights],
        out_specs=[tok(D_MODEL), tok(ROW_EXT),
                   pl.BlockSpec((1, 8, tm), lambda i, j: (i * nt + j, 0, 0)),
                   pl.BlockSpec((8, LANES), lambda i, j: (0, 0))],
        out_shape=[jax.ShapeDtypeStruct((b, t, D_MODEL), F32),
                   jax.ShapeDtypeStruct((b, t, ROW_EXT), F32),
                   jax.ShapeDtypeStruct((b * nt, 8, tm), F32),
                   jax.ShapeDtypeStruct((8, LANES), F32)],
        scratch_shapes=[pltpu.VMEM((8, LANES), F32)],
        compiler_params=_cparams(("arbitrary", "arbitrary")),
    )(x, y_ssd, y_na, u, u, u, kmem, vmem, *weights)


def _dispatch_kernel(pstart_ref, cnt_ref, cmb_ref, rk_ref, hf_ref, zero_ref, xs_ref, sem):
    tm = hf_ref.shape[0]
    i = pl.program_id(0)

    def row_copy(src_ref, src_row, dst_row):
        return pltpu.make_async_copy(src_ref.at[pl.ds(src_row, 1)], xs_ref.at[pl.ds(dst_row, 1)], sem)

    @pl.when(i == 0)
    def _():
        def per_combo(cm, _):
            n_c = cnt_ref[cm]
            npad = (EXPERT_BLOCK - n_c % EXPERT_BLOCK) % EXPERT_BLOCK
            base = pstart_ref[cm] + n_c

            def start(r, _):
                row_copy(zero_ref, 0, base + r).start()
                return 0

            def wait(r, _):
                row_copy(zero_ref, 0, base + r).wait()
                return 0

            lax.fori_loop(0, npad, start, 0)
            lax.fori_loop(0, npad, wait, 0)
            return 0

        lax.fori_loop(0, N_COMBOS, per_combo, 0)

    def start(pair, _):
        for u in range(2):
            tk = 2 * pair + u
            row_copy(hf_ref, tk, pstart_ref[cmb_ref[0, 0, tk]] + rk_ref[0, 0, tk]).start(priority=u)
        return 0

    lax.fori_loop(0, tm // 2, start, 0, unroll=DMA_UNROLL // 2)
    pltpu.make_async_copy(hf_ref, xs_ref.at[pl.ds(0, tm)], sem).wait()


def _dispatch(hf2d, combo, rank, pstart, counts, n_rows):
    n = hf2d.shape[0]
    tm = TOKEN_TILE
    smem_blk = pl.BlockSpec((1, 1, tm), lambda i, *_: (i, 0, 0), memory_space=pltpu.SMEM)
    return pl.pallas_call(
        _dispatch_kernel,
        grid_spec=pltpu.PrefetchScalarGridSpec(
            num_scalar_prefetch=2,
            grid=(n // tm,),
            in_specs=[smem_blk, smem_blk,
                      pl.BlockSpec((tm, ROW_EXT), lambda i, *_: (i, 0)),
                      pl.BlockSpec((8, ROW_EXT), lambda i, *_: (0, 0))],
            out_specs=pl.BlockSpec(memory_space=pl.ANY),
            scratch_shapes=[pltpu.SemaphoreType.DMA(())]),
        out_shape=jax.ShapeDtypeStruct((n_rows, ROW_EXT), F32),
        compiler_params=_cparams(("arbitrary",)),
    )(pstart, counts, combo, rank, hf2d, jnp.zeros((8, ROW_EXT), F32))


def _experts_kernel(ea_ref, eb_ref, nused_ref, x_ref, wga_ref, wua_ref, wda_ref, wgb_ref, wub_ref, wdb_ref, y_ref):
    @pl.when(pl.program_id(0) < nused_ref[0])
    def _():
        x = x_ref[:, :D_MODEL].astype(BF16)
        gates = x_ref[:, D_MODEL:]

        def mlp(wg_ref, wu_ref, wd_ref):
            hg = jnp.dot(x, wg_ref[0], preferred_element_type=F32)
            hu = jnp.dot(x, wu_ref[0], preferred_element_type=F32)
            return jnp.dot((_silu(hg) * hu).astype(BF16), wd_ref[0], preferred_element_type=F32)

        y_ref[...] = mlp(wga_ref, wua_ref, wda_ref) * gates[:, 0:1] + mlp(wgb_ref, wub_ref, wdb_ref) * gates[:, 1:2]


def _experts(xs, blk_ea, blk_eb, nused, w_gate, w_up, w_down):
    nblk = blk_ea.shape[0]
    bm = EXPERT_BLOCK
    row = lambda j, ea, eb, nu: (jnp.minimum(j, nu[0] - 1), 0)
    row_out = lambda j, ea, eb, nu: (jnp.where(j < nu[0], j, nblk - 1), 0)
    sel_a = lambda j, ea, eb, nu: (ea[j], 0, 0)
    sel_b = lambda j, ea, eb, nu: (eb[j], 0, 0)
    up = lambda sel: pl.BlockSpec((1, D_MODEL, D_EXPERT), sel)
    down = lambda sel: pl.BlockSpec((1, D_EXPERT, D_MODEL), sel)
    return pl.pallas_call(
        _experts_kernel,
        grid_spec=pltpu.PrefetchScalarGridSpec(
            num_scalar_prefetch=3,
            grid=(nblk,),
            in_specs=[pl.BlockSpec((bm, ROW_EXT), row),
                      up(sel_a), up(sel_a), down(sel_a), up(sel_b), up(sel_b), down(sel_b)],
            out_specs=pl.BlockSpec((bm, D_MODEL), row_out)),
        out_shape=jax.ShapeDtypeStruct((nblk * bm, D_MODEL), F32),
        compiler_params=_cparams(("arbitrary",)),
    )(blk_ea, blk_eb, nused, xs, w_gate, w_up, w_down, w_gate, w_up, w_down)


def _combine_kernel(pstart_ref, cmb_ref, rk_ref, x_ref, y_hbm, o_ref, ybuf, sem):
    tm = x_ref.shape[0]

    def start(pair, _):
        for u in range(2):
            tk = 2 * pair + u
            d = pstart_ref[cmb_ref[0, 0, tk]] + rk_ref[0, 0, tk]
            pltpu.make_async_copy(y_hbm.at[pl.ds(d, 1)], ybuf.at[pl.ds(tk, 1)], sem).start(priority=u)
        return 0

    lax.fori_loop(0, tm // 2, start, 0, unroll=DMA_UNROLL // 2)
    pltpu.make_async_copy(y_hbm.at[pl.ds(0, tm)], ybuf, sem).wait()
    o_ref[...] = x_ref[...] + ybuf[...]


def _combine(x2d, y, combo, rank, pstart):
    n = x2d.shape[0]
    tm = TOKEN_TILE
    smem_blk = pl.BlockSpec((1, 1, tm), lambda i, *_: (i, 0, 0), memory_space=pltpu.SMEM)
    return pl.pallas_call(
        _combine_kernel,
        grid_spec=pltpu.PrefetchScalarGridSpec(
            num_scalar_prefetch=1,
            grid=(n // tm,),
            in_specs=[smem_blk, smem_blk,
                      pl.BlockSpec((tm, D_MODEL), lambda i, *_: (i, 0)),
                      pl.BlockSpec(memory_space=pl.ANY)],
            out_specs=pl.BlockSpec((tm, D_MODEL), lambda i, *_: (i, 0)),
            scratch_shapes=[pltpu.VMEM((tm, D_MODEL), F32), pltpu.SemaphoreType.DMA(())]),
        out_shape=jax.ShapeDtypeStruct((n, D_MODEL), F32),
        compiler_params=_cparams(("arbitrary",)),
    )(pstart, combo, rank, x2d, y)


_PAIR_A = np.array([a for a in range(EXPERTS_PER_GROUP) for _ in range(a + 1, EXPERTS_PER_GROUP)], np.int32)
_PAIR_B = np.array([b for a in range(EXPERTS_PER_GROUP) for b in range(a + 1, EXPERTS_PER_GROUP)], np.int32)


def _moe(x2, hf, meta, counts, w_gate, w_up, w_down):
    b, t, _ = x2.shape
    n = b * t
    bm = EXPERT_BLOCK
    nblk = (n + N_COMBOS * (bm - 1) + bm - 1) // bm
    cnt = counts[0, :N_COMBOS].astype(jnp.int32)
    psz = (cnt + bm - 1) // bm * bm
    pend = jnp.cumsum(psz)
    pstart = (pend - psz).astype(jnp.int32)
    nused = jnp.maximum(pend[-1] // bm, 1).astype(jnp.int32).reshape(1)
    blk = jnp.minimum(jnp.arange(nblk, dtype=jnp.int32), nused[0] - 1)
    blk_c = jnp.minimum(jnp.sum(pend[None, :] <= (blk * bm)[:, None], axis=1), N_COMBOS - 1).astype(jnp.int32)
    grp = blk_c // PAIRS_PER_GROUP
    blk_ea = (grp * EXPERTS_PER_GROUP + jnp.asarray(_PAIR_A)[blk_c % PAIRS_PER_GROUP]).astype(jnp.int32)
    blk_eb = (grp * EXPERTS_PER_GROUP + jnp.asarray(_PAIR_B)[blk_c % PAIRS_PER_GROUP]).astype(jnp.int32)
    ids = meta.astype(jnp.int32)
    combo = ids[:, 0:1, :]
    rank = ids[:, 1:2, :]
    xs = _dispatch(hf.reshape(n, ROW_EXT), combo, rank, pstart, cnt, nblk * bm)
    y = _experts(xs, blk_ea, blk_eb, nused, w_gate, w_up, w_down)
    out = _combine(x2.reshape(n, D_MODEL), y, combo, rank, pstart)
    return out.reshape(b, t, D_MODEL)


def _prep_layer(lp):
    w_in = lp["w_in"]
    c0 = D_INNER + CONV_CH
    c1 = c0 + 2 * SSD_HEADS
    w_cat = jnp.concatenate([w_in[:, :c0], w_in[:, c1:], w_in[:, c0:c1],
                             jnp.zeros((D_MODEL, DT_PAD - 2 * SSD_HEADS), w_in.dtype)], axis=1).astype(BF16)
    pool_bd = jnp.zeros((POOL_WIDTH, POOL_WIDTH), F32)
    for g in range(POOL_GROUPS):
        sl = slice(g * POOL_GROUP_DIM, (g + 1) * POOL_GROUP_DIM)
        pool_bd = pool_bd.at[sl, sl].set(lp["pool_w"][g].astype(F32))
    w_r = jnp.concatenate([lp["w_router_expert"], lp["w_router_group"],
                           jnp.zeros((D_MODEL, LANES - N_EXPERTS - N_EXPERT_GROUPS), F32)], axis=1).astype(F32)
    w_r_hi = w_r.astype(BF16)
    w_r_lo = (w_r - w_r_hi.astype(F32)).astype(BF16)
    b_r = jnp.concatenate([lp["b_router_expert"], lp["b_router_group"],
                           jnp.zeros((LANES - N_EXPERTS - N_EXPERT_GROUPS,), F32)]).reshape(1, LANES).astype(F32)
    row = lambda a, w: a.reshape(1, w).astype(F32)
    return dict(
        norm_mix=lp["norm_mix"], w_cat=w_cat,
        conv_w=lp["conv_w"], conv_b=lp["conv_b"], dt_bias=lp["dt_bias"], a_log=lp["a_log"],
        d_skip=lp["d_skip"], ssd_norm=lp["ssd_norm"],
        na_q_norm=lp["na_q_norm"], na_k_norm=lp["na_k_norm"], na_rpb=lp["na_rpb"],
        pool_bd=pool_bd.astype(BF16), pool_scale=row(lp["pool_scale"], POOL_WIDTH),
        w_out=lp["w_out"].astype(BF16), norm_xa=row(lp["norm_xa"], D_MODEL),
        norm_mem=lp["norm_mem"], w_xq=lp["w_xq"].astype(BF16), w_xkv=lp["w_xkv"].astype(BF16),
        xq_norm=row(jnp.tile(lp["xq_norm"], XA_HEADS), XA_WIDTH), xk_norm=lp["xk_norm"],
        w_xo=lp["w_xo"].astype(BF16), norm_ffn=row(lp["norm_ffn"], D_MODEL),
        w_r_both=jnp.concatenate([w_r_hi, w_r_lo], axis=1), b_r=b_r,
        w_e_gate=lp["w_e_gate"].astype(BF16), w_e_up=lp["w_e_up"].astype(BF16),
        w_e_down=lp["w_e_down"].astype(BF16),
    )


def _layer(x, mem, p, na_bias):
    b, t, _ = x.shape
    z, xbc, qkv, u, dt = _inproj(x.reshape(b * t, D_MODEL), p["norm_mix"], p["w_cat"])
    r3 = lambda a: a.reshape(b, t, a.shape[-1])
    y_ssd = _ssd(r3(z), r3(xbc), r3(dt), p["conv_w"], p["conv_b"], p["dt_bias"], p["a_log"], p["d_skip"],
                 p["ssd_norm"])
    y_na = _natten(r3(qkv), na_bias, p["na_q_norm"], p["na_k_norm"])
    kmem, vmem = _memkv(mem, p["norm_mem"], p["w_xkv"], p["xk_norm"])
    x2, hf, meta, counts = _mixer(x, y_ssd, y_na, r3(u), kmem, vmem, p)
    return _moe(x2, hf, meta, counts, p["w_e_gate"], p["w_e_up"], p["w_e_down"])


_LAYER_KEYS = ("norm_mix", "w_in", "conv_w", "conv_b", "dt_bias", "a_log", "d_skip", "ssd_norm", "na_q_norm",
               "na_k_norm", "na_rpb", "pool_w", "pool_scale", "w_out", "norm_xa", "norm_mem", "w_xq", "w_xkv",
               "xq_norm", "xk_norm", "w_xo", "norm_ffn", "w_router_group", "b_router_group", "w_router_expert",
               "b_router_expert", "w_e_gate", "w_e_up", "w_e_down")


def kernel(x_prompt, x_sample, mem_prompt, mem_sample, norm_mix, w_in, conv_w, conv_b, dt_bias, a_log, d_skip, ssd_norm, na_q_norm, na_k_norm, na_rpb, pool_w, pool_scale, w_out, norm_xa, norm_mem, w_xq, w_xkv, xq_norm, xk_norm, w_xo, norm_ffn, w_router_group, b_router_group, w_router_expert, b_router_expert, w_e_gate, w_e_up, w_e_down):
    stacked = dict(zip(_LAYER_KEYS, (norm_mix, w_in, conv_w, conv_b, dt_bias, a_log, d_skip, ssd_norm, na_q_norm,
                                     na_k_norm, na_rpb, pool_w, pool_scale, w_out, norm_xa, norm_mem, w_xq, w_xkv,
                                     xq_norm, xk_norm, w_xo, norm_ffn, w_router_group, b_router_group,
                                     w_router_expert, b_router_expert, w_e_gate, w_e_up, w_e_down)))
    depth = w_in.shape[0]
    layers = [_prep_layer({k: v[l] for k, v in stacked.items()}) for l in range(depth)]

    bias_cache = {}

    def trunk(x, mem):
        t = x.shape[1]
        for l, lp in enumerate(layers):
            if (l, t) not in bias_cache:
                bias_cache[(l, t)] = _na_bias(lp["na_rpb"], t)
            x = _layer(x, mem, lp, bias_cache[(l, t)])
        return x

    return trunk(x_prompt, mem_prompt), trunk(x_sample, mem_sample)
```

```python
import functools

import jax
import jax.numpy as jnp
import numpy as np
from jax import lax
from jax.experimental import pallas as pl
from jax.experimental.pallas import tpu as pltpu

F32 = jnp.float32
BF16 = jnp.bfloat16
HIGHEST = lax.Precision.HIGHEST

D_MODEL = 1024
GRID_W = 64
EPS = 1e-6
SSD_HEAD_DIM = 64
D_INNER = D_MODEL // 2
SSD_HEADS = D_INNER // SSD_HEAD_DIM
SSD_GROUPS = 2
SSD_STATE = 64
SSD_CHUNK = 128
CONV_W = 4
CONV_CH = D_INNER + 2 * SSD_GROUPS * SSD_STATE
NA_HEADS = 4
NA_HEAD_DIM = D_MODEL // 16
NA_WIDTH = NA_HEADS * NA_HEAD_DIM
NA_MAX_KH = 8
NA_KW = 16
POOL_WINDOWS = (2, 4, 8, 16)
POOL_GROUPS = 4
POOL_WIDTH = D_MODEL - D_INNER - NA_WIDTH
POOL_GROUP_DIM = POOL_WIDTH // POOL_GROUPS
XA_HEADS = 4
XA_HEAD_DIM = D_MODEL // 8
XA_WIDTH = XA_HEADS * XA_HEAD_DIM
N_EXPERT_GROUPS = 4
EXPERTS_PER_GROUP = 8
N_EXPERTS = N_EXPERT_GROUPS * EXPERTS_PER_GROUP
D_EXPERT = D_MODEL // 4
PAIRS_PER_GROUP = EXPERTS_PER_GROUP * (EXPERTS_PER_GROUP - 1) // 2
N_COMBOS = N_EXPERT_GROUPS * PAIRS_PER_GROUP

LANES = 128
BF16_SUBLANES = 16
VMEM_LIMIT_BYTES = 56 * 1024 * 1024

TOKEN_TILE = 512
NA_QUERY_ROWS = 8
NA_SUB_ROWS = 8
NA_SUBS = NA_QUERY_ROWS // NA_SUB_ROWS
NA_KEY_ROWS = NA_SUB_ROWS + NA_MAX_KH
EXPERT_BLOCK = 128
ROW_EXT = D_MODEL + LANES
SSD_STEP_CHUNKS = 2
DT_PAD = LANES
DMA_UNROLL = 8
NEG_BIG = -1e30


def _cparams(sem):
    return pltpu.CompilerParams(dimension_semantics=sem, vmem_limit_bytes=VMEM_LIMIT_BYTES)


def _sigmoid(x):
    return 1.0 / (1.0 + jnp.exp(-x))


def _silu(x):
    return x * _sigmoid(x)


def _softplus(x):
    return jnp.maximum(x, 0.0) + jnp.log(1.0 + jnp.exp(-jnp.abs(x)))


def _full(shape):
    n = len(shape)
    return pl.BlockSpec(shape, lambda *_: (0,) * n)


def _inproj_kernel(x_ref, g_ref, w_ref, z_ref, xbc_ref, qkv_ref, u_ref, dt_ref):
    x = x_ref[...]
    ms = jnp.mean(x * x, axis=-1, keepdims=True)
    h = (x * lax.rsqrt(ms + EPS) * g_ref[...]).astype(BF16)
    o = 0
    for ref in (z_ref, xbc_ref, qkv_ref, u_ref, dt_ref):
        w = ref.shape[-1]
        ref[...] = jnp.dot(h, w_ref[:, o:o + w], preferred_element_type=F32).astype(ref.dtype)
        o += w


def _inproj(x2d, gain, w_cat):
    n = x2d.shape[0]
    tm = TOKEN_TILE
    widths = (D_INNER, CONV_CH, 3 * NA_WIDTH, POOL_WIDTH, DT_PAD)
    dtypes = (BF16, BF16, BF16, BF16, F32)
    return pl.pallas_call(
        _inproj_kernel,
        grid=(n // tm,),
        in_specs=[pl.BlockSpec((tm, D_MODEL), lambda i: (i, 0)),
                  _full((1, D_MODEL)),
                  _full(w_cat.shape)],
        out_specs=[pl.BlockSpec((tm, w), lambda i: (i, 0)) for w in widths],
        out_shape=[jax.ShapeDtypeStruct((n, w), d) for w, d in zip(widths, dtypes)],
        compiler_params=_cparams(("arbitrary",)),
    )(x2d, gain.reshape(1, D_MODEL), w_cat)


def _split3(a):
    hi = a.astype(BF16)
    r = a - hi.astype(F32)
    mid = r.astype(BF16)
    lo = (r - mid.astype(F32)).astype(BF16)
    return hi, mid, lo


def _ssd_kernel(xc_ref, xp_ref, xn_ref, dt_ref, z_ref, cw_ref, cb_ref, dtb_ref, alog_ref, dsk_ref, nrm_ref, emat_ref,
                y_ref, state_ref, yf_ref, xs_c, bc_c, cbm_c, dt_c, *, nblocks):
    L = SSD_CHUNK
    LB = SSD_STEP_CHUNKS * L
    P = SSD_HEAD_DIM
    NS = SSD_STATE
    HG = SSD_HEADS // SSD_GROUPS
    gn = SSD_GROUPS * NS
    j = pl.program_id(1)
    c = jnp.where(j < nblocks, j, 2 * nblocks - 1 - j)
    row0 = pl.multiple_of(c * LB, LB)
    rows = pl.ds(row0, LB)
    lane1 = lax.broadcasted_iota(jnp.int32, (1, LANES), 1)
    lo_half = lane1 < P
    ti = lax.broadcasted_iota(jnp.int32, (L, L), 0)
    si = lax.broadcasted_iota(jnp.int32, (L, L), 1)

    def masked_c(bc):
        return [jnp.where(lane1 // NS == g, bc[:, gn:], 0.0).astype(BF16) for g in range(SSD_GROUPS)]

    def prepare():
        cur = xc_ref[0].astype(F32)
        prev = xp_ref[0].astype(F32)
        nxt = xn_ref[0].astype(F32)
        has_prev = (c > 0).astype(F32)
        has_next = (c < nblocks - 1).astype(F32)
        p_last = prev[BF16_SUBLANES - 1:BF16_SUBLANES, :] * has_prev
        n0 = nxt[0:1, :] * has_next
        n1 = nxt[1:2, :] * has_next
        row = lax.broadcasted_iota(jnp.int32, (LB, 1), 0)
        um1 = jnp.where(row == 0, p_last, pltpu.roll(cur, 1, 0))
        up1 = jnp.where(row == LB - 1, n0, pltpu.roll(cur, LB - 1, 0))
        up2 = jnp.where(row == LB - 2, n0, jnp.where(row == LB - 1, n1, pltpu.roll(cur, LB - 2, 0)))
        cw = cw_ref[...]
        acc = cb_ref[...] + um1 * cw[0:1, :] + cur * cw[1:2, :] + up1 * cw[2:3, :] + up2 * cw[3:4, :]
        xbc = _silu(acc)
        xs = xbc[:, :D_INNER]
        bc = xbc[:, D_INNER:D_INNER + 2 * gn].astype(BF16)
        dt = _softplus(dt_ref[0] + dtb_ref[...])
        xs_c[rows, :] = xs
        bc_c[rows, :] = bc
        dt_c[rows, :] = dt
        ops = []
        for sub in range(SSD_STEP_CHUNKS):
            sl = slice(sub * L, (sub + 1) * L)
            cg = masked_c(bc[sl])
            cb_mat = [lax.dot_general(cg[g], bc[sl, :gn], (((1,), (1,)), ((), ())), preferred_element_type=F32)
                      for g in range(SSD_GROUPS)]
            cbm_c[pl.ds(row0 + sub * L, L), :] = jnp.concatenate(cb_mat, axis=1)
            ops.append((xs[sl], bc[sl, :gn], cg, cb_mat, dt[sl]))
        return ops

    def recall():
        ops = []
        for sub in range(SSD_STEP_CHUNKS):
            r = pl.ds(row0 + sub * L, L)
            bc = bc_c[r, :]
            cbm = cbm_c[r, :]
            ops.append((xs_c[r, :], bc[:, :gn], masked_c(bc), [cbm[:, g * L:(g + 1) * L] for g in range(SSD_GROUPS)],
                        dt_c[r, :]))
        return ops

    def scan_chunk(direction, xs, bfull, cg, cb_mat, dt):
        if direction == 0:
            mask = ti >= si
            edge = L - 1
        else:
            mask = si >= ti
            edge = 0
        la = dt * (-jnp.exp(alog_ref[...]))
        tri = mask.astype(BF16)
        csum = sum(jnp.dot(tri, part, preferred_element_type=F32) for part in _split3(la))
        csum_t = csum.T
        emat = emat_ref[direction]
        colb = sum(jnp.dot(part, emat, preferred_element_type=F32) for part in _split3(csum))
        tot = csum[edge:edge + 1, :]
        e_tot = jnp.exp(tot)
        e_in = jnp.exp(csum)
        e_out = jnp.exp(tot - csum)
        ys = []
        for g in range(SSD_GROUPS):
            s_old = state_ref[g]
            y_off = lax.dot_general(cg[g], s_old.astype(BF16), (((1,), (1,)), ((), ())),
                                    preferred_element_type=F32)
            xw = []
            for pr in range(HG // 2):
                h0 = g * HG + 2 * pr
                l0 = direction * SSD_HEADS + h0

                def col(a, l0=l0):
                    return jnp.where(lo_half, a[:, l0:l0 + 1], a[:, l0 + 1:l0 + 2])

                xdt = xs[:, h0 * P:(h0 + 2) * P] * col(dt)
                y_pair = y_off[:, 2 * pr * P:(2 * pr + 2) * P] * col(e_in)
                for hh, half in ((h0, lo_half), (h0 + 1, jnp.logical_not(lo_half))):
                    ll = direction * SSD_HEADS + hh
                    seg = colb[:, hh * L:(hh + 1) * L] - csum_t[ll:ll + 1, :]
                    dec = jnp.exp(jnp.where(mask, seg, NEG_BIG))
                    m = (cb_mat[g] * dec).astype(BF16)
                    y_pair += jnp.dot(m, jnp.where(half, xdt, 0.0).astype(BF16), preferred_element_type=F32)
                ys.append(y_pair)
                xw.append(xdt * col(e_out))
            xw = jnp.concatenate(xw, axis=1).astype(BF16)
            s_new = lax.dot_general(xw, bfull, (((0,), (0,)), ((), ())), preferred_element_type=F32)
            s_scaled = []
            for hl in range(HG):
                lane = direction * SSD_HEADS + g * HG + hl
                s_scaled.append(s_old[hl * P:(hl + 1) * P, :] * e_tot[:, lane:lane + 1])
            state_ref[g] = jnp.concatenate(s_scaled, axis=0) + s_new
        return jnp.concatenate(ys, axis=1)

    @pl.when(jnp.logical_or(j == 0, j == nblocks))
    def _():
        state_ref[...] = jnp.zeros_like(state_ref)

    @pl.when(j < nblocks)
    def _():
        ops = prepare()
        for sub in range(SSD_STEP_CHUNKS):
            yf_ref[pl.ds(row0 + sub * L, L), :] = scan_chunk(0, *ops[sub])

    @pl.when(j >= nblocks)
    def _():
        ops = recall()
        for sub in reversed(range(SSD_STEP_CHUNKS)):
            sl = slice(sub * L, (sub + 1) * L)
            y = yf_ref[pl.ds(row0 + sub * L, L), :] + scan_chunk(1, *ops[sub]) + dsk_ref[...] * ops[sub][0]
            y = y * _silu(z_ref[0, sl, :].astype(F32))
            gw = D_INNER // SSD_GROUPS
            outs = []
            for g in range(SSD_GROUPS):
                yg = y[:, g * gw:(g + 1) * gw]
                outs.append(yg * lax.rsqrt(jnp.mean(yg * yg, axis=-1, keepdims=True) + EPS))
            y_ref[0, sl, :] = (jnp.concatenate(outs, axis=1) * nrm_ref[...]).astype(y_ref.dtype)


def _ssd(z, xbc, dt, conv_w, conv_b, dt_bias, a_log, d_skip, ssd_norm):
    b, t, _ = z.shape
    L = SSD_CHUNK
    lb = SSD_STEP_CHUNKS * L
    nb = t // lb
    hb = lb // BF16_SUBLANES
    nhalo = t // BF16_SUBLANES

    def blk(j):
        return jnp.where(j < nb, j, 2 * nb - 1 - j)

    pad = DT_PAD - 2 * SSD_HEADS
    dtb = jnp.pad(dt_bias.reshape(1, -1).astype(F32), ((0, 0), (0, pad)))
    alog = jnp.pad(a_log.reshape(1, -1).astype(F32), ((0, 0), (0, pad)))
    dsk = jnp.repeat(d_skip.astype(F32), SSD_HEAD_DIM).reshape(1, D_INNER)
    sel = np.arange(DT_PAD)[None, :, None] == (np.arange(2)[:, None, None] * SSD_HEADS
                                                + np.arange(SSD_HEADS)[None, None, :])
    emat = jnp.asarray(np.repeat(sel, L, axis=2), BF16)
    return pl.pallas_call(
        functools.partial(_ssd_kernel, nblocks=nb),
        grid=(b, 2 * nb),
        in_specs=[
            pl.BlockSpec((1, lb, CONV_CH), lambda i, j: (i, blk(j), 0)),
            pl.BlockSpec((1, BF16_SUBLANES, CONV_CH), lambda i, j: (i, jnp.maximum(blk(j) * hb - 1, 0), 0)),
            pl.BlockSpec((1, BF16_SUBLANES, CONV_CH),
                         lambda i, j: (i, jnp.minimum((blk(j) + 1) * hb, nhalo - 1), 0)),
            pl.BlockSpec((1, lb, DT_PAD), lambda i, j: (i, blk(j), 0)),
            pl.BlockSpec((1, lb, D_INNER), lambda i, j: (i, blk(j), 0)),
            _full((CONV_W, CONV_CH)), _full((1, CONV_CH)), _full((1, DT_PAD)), _full((1, DT_PAD)),
            _full((1, D_INNER)), _full((1, D_INNER)), _full((2, DT_PAD, SSD_HEADS * L)),
        ],
        out_specs=pl.BlockSpec((1, lb, D_INNER), lambda i, j: (i, jnp.where(j < nb, nb - 1, 2 * nb - 1 - j), 0)),
        out_shape=jax.ShapeDtypeStruct((b, t, D_INNER), BF16),
        scratch_shapes=[pltpu.VMEM((SSD_GROUPS, (SSD_HEADS // SSD_GROUPS) * SSD_HEAD_DIM, LANES), F32),
                        pltpu.VMEM((t, D_INNER), F32),
                        pltpu.VMEM((t, D_INNER), F32),
                        pltpu.VMEM((t, 2 * SSD_GROUPS * SSD_STATE), BF16),
                        pltpu.VMEM((t, SSD_GROUPS * L), F32),
                        pltpu.VMEM((t, DT_PAD), F32)],
        compiler_params=_cparams(("arbitrary", "arbitrary")),
    )(xbc, xbc, xbc, dt, z, conv_w.astype(F32), conv_b.reshape(1, CONV_CH).astype(F32), dtb, alog, dsk,
      ssd_norm.reshape(1, D_INNER).astype(F32), emat)


def _na_bias(rpb, t):
    r = t // GRID_W
    kh = min(NA_MAX_KH, r)
    nsb = r // NA_SUB_ROWS
    rows = np.arange(r)
    row_start = np.clip(rows - NA_MAX_KH // 2, 0, r - kh)
    r0 = np.arange(nsb) * NA_SUB_ROWS
    kr0 = np.clip(r0 - NA_MAX_KH // 2, 0, r - NA_KEY_ROWS)
    qrow = r0[:, None] + np.arange(NA_SUB_ROWS)[None, :]
    krow = kr0[:, None] + np.arange(NA_KEY_ROWS)[None, :]
    rs = row_start[qrow]
    row_ok = (krow[:, None, :] >= rs[:, :, None]) & (krow[:, None, :] < rs[:, :, None] + kh)
    dr = np.clip(krow[:, None, :] - qrow[:, :, None] + (NA_MAX_KH - 1), 0, 2 * NA_MAX_KH - 2)
    cols = np.arange(GRID_W)
    col_start = np.clip(cols - NA_KW // 2, 0, GRID_W - NA_KW)
    col_ok = (cols[None, :] >= col_start[:, None]) & (cols[None, :] < col_start[:, None] + NA_KW)
    dc = np.clip(cols[None, :] - cols[:, None] + (NA_KW - 1), 0, 2 * NA_KW - 2)
    kp = NA_KEY_ROWS // 2
    sel_r = (dr[..., None] == np.arange(2 * NA_MAX_KH - 1)) & row_ok[..., None]
    sel_r = sel_r.reshape(nsb, NA_SUB_ROWS, kp, 2, 2 * NA_MAX_KH - 1)
    sel_c = (dc[..., None] == np.arange(2 * NA_KW - 1)) & col_ok[..., None]
    sel_c2 = np.zeros((2, GRID_W, 2 * GRID_W, 2 * NA_KW - 1), bool)
    col_ok2 = np.zeros((2, GRID_W, 2 * GRID_W), bool)
    for par in range(2):
        sel_c2[par, :, par * GRID_W:(par + 1) * GRID_W] = sel_c
        col_ok2[par, :, par * GRID_W:(par + 1) * GRID_W] = col_ok
    bias = jnp.einsum("hrc,bqkpr,pxyc->bhqxky", rpb.astype(F32), jnp.asarray(sel_r, F32), jnp.asarray(sel_c2, F32),
                      precision=HIGHEST)
    ok = jnp.einsum("bqkp,pxy->bqxky", jnp.asarray(row_ok.reshape(nsb, NA_SUB_ROWS, kp, 2), F32),
                    jnp.asarray(col_ok2, F32))
    bias = jnp.where(ok[:, None] > 0.5, bias, NEG_BIG)
    return bias.reshape(nsb // NA_SUBS, NA_SUBS, NA_HEADS, NA_SUB_ROWS * GRID_W, NA_KEY_ROWS * GRID_W).astype(BF16)


def _natten_kernel(qkv_ref, bias_ref, qg_ref, kg_ref, seg_ref, o_ref, *, grid_rows):
    nq = NA_SUB_ROWS * GRID_W
    nk = NA_KEY_ROWS * GRID_W
    rb = pl.program_id(0)
    seg = seg_ref[...]
    lane_h = lax.broadcasted_iota(jnp.int32, (1, NA_WIDTH), 1) // NA_HEAD_DIM
    for sub in range(NA_SUBS):
        r0 = rb * NA_QUERY_ROWS + sub * NA_SUB_ROWS
        kr0 = jnp.clip(r0 - NA_MAX_KH // 2, 0, grid_rows - NA_KEY_ROWS)
        q0 = pl.multiple_of(r0 * GRID_W, nq)
        k0 = pl.multiple_of(kr0 * GRID_W, NA_MAX_KH // 2 * GRID_W)
        q = qkv_ref[0, pl.ds(q0, nq), 0:NA_WIDTH].astype(F32)
        k = qkv_ref[0, pl.ds(k0, nk), NA_WIDTH:2 * NA_WIDTH].astype(F32)
        v = qkv_ref[0, pl.ds(k0, nk), 2 * NA_WIDTH:3 * NA_WIDTH]
        qms = jnp.dot(q * q, seg, precision=HIGHEST, preferred_element_type=F32)
        kms = jnp.dot(k * k, seg, precision=HIGHEST, preferred_element_type=F32)
        qn = q * lax.rsqrt(qms + EPS) * (qg_ref[...] * NA_HEAD_DIM ** -0.5)
        kn = (k * lax.rsqrt(kms + EPS) * kg_ref[...]).astype(BF16)
        acc = jnp.zeros((nq, NA_WIDTH), F32)
        for h in range(NA_HEADS):
            hm = lane_h == h
            s = lax.dot_general(jnp.where(hm, qn, 0.0).astype(BF16), kn, (((1,), (1,)), ((), ())),
                                preferred_element_type=F32)
            s = s + bias_ref[0, sub, h].astype(F32)
            p = jnp.exp(s - jnp.max(s, axis=-1, keepdims=True))
            l = jnp.sum(p, axis=-1, keepdims=True)
            o = jnp.dot(p.astype(BF16), v, preferred_element_type=F32)
            acc += jnp.where(hm, o / l, 0.0)
        o_ref[0, sub * nq:(sub + 1) * nq, :] = acc.astype(o_ref.dtype)


def _natten(qkv, bias, q_norm, k_norm):
    b, t, _ = qkv.shape
    r = t // GRID_W
    nrb = r // NA_QUERY_ROWS
    nq = NA_QUERY_ROWS * GRID_W
    head = jnp.arange(NA_WIDTH) // NA_HEAD_DIM
    seg = (head[:, None] == head[None, :]).astype(F32) / NA_HEAD_DIM
    return pl.pallas_call(
        functools.partial(_natten_kernel, grid_rows=r),
        grid=(nrb, b),
        in_specs=[pl.BlockSpec((1, t, 3 * NA_WIDTH), lambda i, j: (j, 0, 0)),
                  pl.BlockSpec((1,) + bias.shape[1:], lambda i, j: (i, 0, 0, 0, 0)),
                  _full((1, NA_WIDTH)), _full((1, NA_WIDTH)), _full((NA_WIDTH, NA_WIDTH))],
        out_specs=pl.BlockSpec((1, nq, NA_WIDTH), lambda i, j: (j, i, 0)),
        out_shape=jax.ShapeDtypeStruct((b, t, NA_WIDTH), BF16),
        compiler_params=_cparams(("arbitrary", "arbitrary")),
    )(qkv, bias, jnp.tile(q_norm.astype(F32), NA_HEADS).reshape(1, NA_WIDTH),
      jnp.tile(k_norm.astype(F32), NA_HEADS).reshape(1, NA_WIDTH), seg)


def _memkv_kernel(m_ref, g_ref, w_ref, kg_ref, k_ref, v_ref):
    x = m_ref[0]
    ms = jnp.mean(x * x, axis=-1, keepdims=True)
    h = (x * lax.rsqrt(ms + EPS) * g_ref[...]).astype(BF16)
    kv = jnp.dot(h, w_ref[...], preferred_element_type=F32)
    ks = []
    for hd in range(XA_HEADS):
        kh = kv[:, hd * XA_HEAD_DIM:(hd + 1) * XA_HEAD_DIM]
        ks.append(kh * lax.rsqrt(jnp.mean(kh * kh, axis=-1, keepdims=True) + EPS))
    k_ref[0] = (jnp.concatenate(ks, axis=1) * kg_ref[...]).astype(k_ref.dtype)
    v_ref[0] = kv[:, XA_WIDTH:].astype(v_ref.dtype)


def _memkv(mem, norm_mem, w_xkv, xk_norm):
    b, m, _ = mem.shape
    return pl.pallas_call(
        _memkv_kernel,
        grid=(b,),
        in_specs=[pl.BlockSpec((1, m, D_MODEL), lambda i: (i, 0, 0)),
                  _full((1, D_MODEL)), _full((D_MODEL, 2 * XA_WIDTH)), _full((1, XA_WIDTH))],
        out_specs=[pl.BlockSpec((1, m, XA_WIDTH), lambda i: (i, 0, 0))] * 2,
        out_shape=[jax.ShapeDtypeStruct((b, m, XA_WIDTH), BF16)] * 2,
        compiler_params=_cparams(("arbitrary",)),
    )(mem, norm_mem.reshape(1, D_MODEL).astype(F32), w_xkv,
      jnp.tile(xk_norm.astype(F32), XA_HEADS).reshape(1, XA_WIDTH))


def _mixer_kernel(x_ref, ys_ref, yn_ref, u_ref, up_ref, un_ref, k_ref, v_ref,
                  pw_ref, psc_ref, wo_ref, gxa_ref, wq_ref, qg_ref, wxo_ref, gff_ref,
                  wrb_ref, br_ref, ltri_ref,
                  x2_ref, hf_ref, meta_ref, cnt_ref, carry_ref, *, seq):
    tm = x_ref.shape[1]
    halo = BF16_SUBLANES
    bi = pl.program_id(0)
    i = pl.program_id(1)
    nt = pl.num_programs(1)

    @pl.when(jnp.logical_and(bi == 0, i == 0))
    def _():
        carry_ref[...] = jnp.zeros_like(carry_ref)

    u = u_ref[0].astype(F32)
    up = up_ref[0].astype(F32) * (i > 0).astype(F32)
    un = un_ref[0].astype(F32) * (i < nt - 1).astype(F32)
    cat = jnp.concatenate([up, u, un], axis=0)
    n = tm + 2 * halo

    def sh(a, k):
        return pltpu.roll(a, (-k) % n, 0)

    a2 = cat + sh(cat, -1)
    a4 = sh(a2, 1) + sh(a2, -1)
    a8 = sh(a4, 2) + sh(a4, -2)
    a16 = sh(a8, 4) + sh(a8, -4)
    lane_g = lax.broadcasted_iota(jnp.int32, (1, POOL_WIDTH), 1) // POOL_GROUP_DIM
    wsum = jnp.where(lane_g == 0, a2, jnp.where(lane_g == 1, a4, jnp.where(lane_g == 2, a8, a16)))
    wsum = wsum[halo:halo + tm, :]
    half = jnp.where(lane_g == 0, POOL_WINDOWS[0] // 2,
                     jnp.where(lane_g == 1, POOL_WINDOWS[1] // 2,
                               jnp.where(lane_g == 2, POOL_WINDOWS[2] // 2, POOL_WINDOWS[3] // 2)))
    tpos = i * tm + lax.broadcasted_iota(jnp.int32, (tm, 1), 0)
    cnt = (jnp.minimum(tpos + half, seq) - jnp.maximum(tpos - half, 0)).astype(F32)
    d = wsum / cnt - u
    ypool = jnp.dot(d.astype(BF16), pw_ref[...], preferred_element_type=F32) * psc_ref[...]

    mix = jnp.dot(ys_ref[0], wo_ref[0:D_INNER, :], preferred_element_type=F32)
    mix += jnp.dot(yn_ref[0], wo_ref[D_INNER:D_INNER + NA_WIDTH, :], preferred_element_type=F32)
    mix += jnp.dot(ypool.astype(BF16), wo_ref[D_INNER + NA_WIDTH:, :], preferred_element_type=F32)
    x1 = x_ref[0] + mix

    hn = (x1 * lax.rsqrt(jnp.mean(x1 * x1, axis=-1, keepdims=True) + EPS) * gxa_ref[...]).astype(BF16)
    q = jnp.dot(hn, wq_ref[...], preferred_element_type=F32)
    kk = k_ref[0]
    vv = v_ref[0]
    scale = XA_HEAD_DIM ** -0.5
    outs = []
    for hd in range(XA_HEADS):
        sl = slice(hd * XA_HEAD_DIM, (hd + 1) * XA_HEAD_DIM)
        qh = q[:, sl]
        qh = (qh * lax.rsqrt(jnp.mean(qh * qh, axis=-1, keepdims=True) + EPS) * qg_ref[:, sl]).astype(BF16)
        s = lax.dot_general(qh, kk[:, sl], (((1,), (1,)), ((), ())), preferred_element_type=F32) * scale
        p = jnp.exp(s - jnp.max(s, axis=-1, keepdims=True))
        l = jnp.sum(p, axis=-1, keepdims=True)
        outs.append(jnp.dot(p.astype(BF16), vv[:, sl], preferred_element_type=F32) / l)
    att = jnp.concatenate(outs, axis=1).astype(BF16)
    x2 = x1 + jnp.dot(att, wxo_ref[...], preferred_element_type=F32)
    x2_ref[0] = x2

    hf = x2 * lax.rsqrt(jnp.mean(x2 * x2, axis=-1, keepdims=True) + EPS) * gff_ref[...]
    hf_ref[0, :, :D_MODEL] = hf
    h_hi = hf.astype(BF16)
    h_lo = (hf - h_hi.astype(F32)).astype(BF16)
    both = jnp.dot(h_hi, wrb_ref[...], preferred_element_type=F32)
    logits = (both[:, :LANES] + both[:, LANES:]
              + jnp.dot(h_lo, wrb_ref[:, :LANES], preferred_element_type=F32)) + br_ref[...]
    lane = lax.broadcasted_iota(jnp.int32, (1, LANES), 1)
    lane_f = lane.astype(F32)
    lane_grp = (lane // EXPERTS_PER_GROUP).astype(F32)
    is_g = jnp.logical_and(lane >= N_EXPERTS, lane < N_EXPERTS + N_EXPERT_GROUPS)
    gl = jnp.where(is_g, logits, NEG_BIG)
    gmax = jnp.max(gl, axis=-1, keepdims=True)
    g_sel = jnp.min(jnp.where(gl == gmax, lane_f, float(LANES)), axis=-1, keepdims=True) - N_EXPERTS
    g_gate = 1.0 / jnp.sum(jnp.where(is_g, jnp.exp(gl - gmax), 0.0), axis=-1, keepdims=True)
    in_grp = jnp.logical_and(lane < N_EXPERTS, lane_grp == g_sel)
    el = jnp.where(in_grp, logits, NEG_BIG)
    v1 = jnp.max(el, axis=-1, keepdims=True)
    e0 = jnp.min(jnp.where(el == v1, lane_f, float(LANES)), axis=-1, keepdims=True)
    el2 = jnp.where(lane_f == e0, NEG_BIG, el)
    v2 = jnp.max(el2, axis=-1, keepdims=True)
    e1 = jnp.min(jnp.where(el2 == v2, lane_f, float(LANES)), axis=-1, keepdims=True)
    w1 = jnp.exp(v2 - v1)
    gate0 = g_gate / (1.0 + w1)
    gate1 = g_gate * w1 / (1.0 + w1)

    base = g_sel * EXPERTS_PER_GROUP
    ea = jnp.minimum(e0, e1) - base
    eb = jnp.maximum(e0, e1) - base
    combo = g_sel * PAIRS_PER_GROUP + ea * EXPERTS_PER_GROUP - ea * (ea + 1.0) * 0.5 + (eb - ea - 1.0)
    gate_a = jnp.where(e0 < e1, gate0, gate1)
    gate_b = jnp.where(e0 < e1, gate1, gate0)
    hf_ref[0, :, D_MODEL:] = jnp.where(lane == 0, gate_a, 0.0) + jnp.where(lane == 1, gate_b, 0.0)

    oh = lane_f == combo
    cnt_tok = oh.astype(F32)
    before = jnp.dot(ltri_ref[...], cnt_tok.astype(BF16), preferred_element_type=F32) + carry_ref[0:1, :]
    rank = jnp.sum(jnp.where(oh, before, 0.0), axis=-1, keepdims=True)
    new_carry = carry_ref[0:1, :] + jnp.sum(cnt_tok, axis=0, keepdims=True)
    carry_ref[...] = jnp.broadcast_to(new_carry, carry_ref.shape)
    cnt_ref[...] = jnp.broadcast_to(new_carry, cnt_ref.shape)

    slab = jnp.where(lane == 0, combo, 0.0) + jnp.where(lane == 1, rank, 0.0)
    meta_ref[0] = slab.T[0:8, :]


def _mixer(x, y_ssd, y_na, u, kmem, vmem, p):
    b, t, _ = x.shape
    tm = TOKEN_TILE
    nt = t // tm
    hb = tm // BF16_SUBLANES
    nhalo = t // BF16_SUBLANES
    ltri = (jnp.arange(tm)[:, None] > jnp.arange(tm)[None, :]).astype(BF16)
    tok = lambda w: pl.BlockSpec((1, tm, w), lambda i, j: (i, j, 0))
    mem = pl.BlockSpec((1, kmem.shape[1], XA_WIDTH), lambda i, j: (i, 0, 0))
    weights = (p["pool_bd"], p["pool_scale"], p["w_out"], p["norm_xa"], p["w_xq"], p["xq_norm"], p["w_xo"],
               p["norm_ffn"], p["w_r_both"], p["b_r"], ltri)
    return pl.pallas_call(
        functools.partial(_mixer_kernel, seq=t),
        grid=(b, nt),
        in_specs=[tok(D_MODEL), tok(D_INNER), tok(NA_WIDTH), tok(POOL_WIDTH),
                  pl.BlockSpec((1, BF16_SUBLANES, POOL_WIDTH), lambda i, j: (i, jnp.maximum(j * hb - 1, 0), 0)),
                  pl.BlockSpec((1, BF16_SUBLANES, POOL_WIDTH),
                               lambda i, j: (i, jnp.minimum((j + 1) * hb, nhalo - 1), 0)),
                  mem, mem] + [_full(w.shape) for w in weights],
        out_specs=[tok(D_MODEL), tok(ROW_EXT),
                   pl.BlockSpec((1, 8, tm), lambda i, j: (i * nt + j, 0, 0)),
                   pl.BlockSpec((8, LANES), lambda i, j: (0, 0))],
        out_shape=[jax.ShapeDtypeStruct((b, t, D_MODEL), F32),
                   jax.ShapeDtypeStruct((b, t, ROW_EXT), F32),
                   jax.ShapeDtypeStruct((b * nt, 8, tm), F32),
                   jax.ShapeDtypeStruct((8, LANES), F32)],
        scratch_shapes=[pltpu.VMEM((8, LANES), F32)],
        compiler_params=_cparams(("arbitrary", "arbitrary")),
    )(x, y_ssd, y_na, u, u, u, kmem, vmem, *weights)


def _dispatch_kernel(pstart_ref, cnt_ref, cmb_ref, rk_ref, hf_ref, zero_ref, xs_ref, stage, sem):
    tm = hf_ref.shape[0]
    i = pl.program_id(0)
    n = pl.num_programs(0)
    slot = i % 2

    def row_copy(src_ref, src_row, dst_row, s):
        return pltpu.make_async_copy(src_ref.at[pl.ds(src_row, 1)], xs_ref.at[pl.ds(dst_row, 1)], s)

    def wait_tile(sl):
        pltpu.make_async_copy(stage.at[sl], xs_ref.at[pl.ds(0, tm)], sem.at[sl]).wait()

    @pl.when(i == 0)
    def _():
        def per_combo(cm, _):
            n_c = cnt_ref[cm]
            npad = (EXPERT_BLOCK - n_c % EXPERT_BLOCK) % EXPERT_BLOCK
            base = pstart_ref[cm] + n_c

            def start(r, _):
                row_copy(zero_ref, 0, base + r, sem.at[0]).start()
                return 0

            def wait(r, _):
                row_copy(zero_ref, 0, base + r, sem.at[0]).wait()
                return 0

            lax.fori_loop(0, npad, start, 0)
            lax.fori_loop(0, npad, wait, 0)
            return 0

        lax.fori_loop(0, N_COMBOS, per_combo, 0)

    stage[slot] = hf_ref[...]

    def start(pair, _):
        for u in range(2):
            tk = 2 * pair + u
            row_copy(stage.at[slot], tk, pstart_ref[cmb_ref[0, 0, tk]] + rk_ref[0, 0, tk],
                     sem.at[slot]).start(priority=u)
        return 0

    lax.fori_loop(0, tm // 2, start, 0, unroll=DMA_UNROLL // 2)

    @pl.when(i > 0)
    def _():
        wait_tile(1 - slot)

    @pl.when(i == n - 1)
    def _():
        wait_tile(slot)


def _dispatch(hf2d, combo, rank, pstart, counts, n_rows):
    n = hf2d.shape[0]
    tm = TOKEN_TILE
    smem_blk = pl.BlockSpec((1, 1, tm), lambda i, *_: (i, 0, 0), memory_space=pltpu.SMEM)
    return pl.pallas_call(
        _dispatch_kernel,
        grid_spec=pltpu.PrefetchScalarGridSpec(
            num_scalar_prefetch=2,
            grid=(n // tm,),
            in_specs=[smem_blk, smem_blk,
                      pl.BlockSpec((tm, ROW_EXT), lambda i, *_: (i, 0)),
                      pl.BlockSpec((8, ROW_EXT), lambda i, *_: (0, 0))],
            out_specs=pl.BlockSpec(memory_space=pl.ANY),
            scratch_shapes=[pltpu.VMEM((2, tm, ROW_EXT), F32), pltpu.SemaphoreType.DMA((2,))]),
        out_shape=jax.ShapeDtypeStruct((n_rows, ROW_EXT), F32),
        compiler_params=_cparams(("arbitrary",)),
    )(pstart, counts, combo, rank, hf2d, jnp.zeros((8, ROW_EXT), F32))


def _experts_kernel(ea_ref, eb_ref, nused_ref, x_ref, wga_ref, wua_ref, wda_ref, wgb_ref, wub_ref, wdb_ref, y_ref):
    @pl.when(pl.program_id(0) < nused_ref[0])
    def _():
        x = x_ref[:, :D_MODEL].astype(BF16)
        gates = x_ref[:, D_MODEL:]

        def mlp(wg_ref, wu_ref, wd_ref):
            hg = jnp.dot(x, wg_ref[0], preferred_element_type=F32)
            hu = jnp.dot(x, wu_ref[0], preferred_element_type=F32)
            return jnp.dot((_silu(hg) * hu).astype(BF16), wd_ref[0], preferred_element_type=F32)

        y_ref[...] = mlp(wga_ref, wua_ref, wda_ref) * gates[:, 0:1] + mlp(wgb_ref, wub_ref, wdb_ref) * gates[:, 1:2]


def _experts(xs, blk_ea, blk_eb, nused, w_gate, w_up, w_down):
    nblk = blk_ea.shape[0]
    bm = EXPERT_BLOCK
    row = lambda j, ea, eb, nu: (jnp.minimum(j, nu[0] - 1), 0)
    row_out = lambda j, ea, eb, nu: (jnp.where(j < nu[0], j, nblk - 1), 0)
    sel_a = lambda j, ea, eb, nu: (ea[j], 0, 0)
    sel_b = lambda j, ea, eb, nu: (eb[j], 0, 0)
    up = lambda sel: pl.BlockSpec((1, D_MODEL, D_EXPERT), sel)
    down = lambda sel: pl.BlockSpec((1, D_EXPERT, D_MODEL), sel)
    return pl.pallas_call(
        _experts_kernel,
        grid_spec=pltpu.PrefetchScalarGridSpec(
            num_scalar_prefetch=3,
            grid=(nblk,),
            in_specs=[pl.BlockSpec((bm, ROW_EXT), row),
                      up(sel_a), up(sel_a), down(sel_a), up(sel_b), up(sel_b), down(sel_b)],
            out_specs=pl.BlockSpec((bm, D_MODEL), row_out)),
        out_shape=jax.ShapeDtypeStruct((nblk * bm, D_MODEL), F32),
        compiler_params=_cparams(("arbitrary",)),
    )(blk_ea, blk_eb, nused, xs, w_gate, w_up, w_down, w_gate, w_up, w_down)


def _combine_kernel(pstart_ref, cmb_ref, rk_ref, cmb_nx_ref, rk_nx_ref, x_ref, y_hbm, o_ref, ybuf, sem):
    tm = x_ref.shape[0]
    i = pl.program_id(0)
    n = pl.num_programs(0)
    slot = i % 2

    def gather_tile(c_ref, r_ref, sl):
        def start(pair, _):
            for u in range(2):
                tk = 2 * pair + u
                d = pstart_ref[c_ref[0, 0, tk]] + r_ref[0, 0, tk]
                pltpu.make_async_copy(y_hbm.at[pl.ds(d, 1)], ybuf.at[sl, pl.ds(tk, 1)], sem.at[sl]).start(priority=u)
            return 0

        lax.fori_loop(0, tm // 2, start, 0, unroll=DMA_UNROLL // 2)

    @pl.when(i == 0)
    def _():
        gather_tile(cmb_ref, rk_ref, slot)

    @pl.when(i + 1 < n)
    def _():
        gather_tile(cmb_nx_ref, rk_nx_ref, 1 - slot)

    pltpu.make_async_copy(y_hbm.at[pl.ds(0, tm)], ybuf.at[slot], sem.at[slot]).wait()
    o_ref[...] = x_ref[...] + ybuf[slot]


def _combine(x2d, y, combo, rank, pstart):
    n = x2d.shape[0]
    tm = TOKEN_TILE
    nt = n // tm
    smem_blk = pl.BlockSpec((1, 1, tm), lambda i, *_: (i, 0, 0), memory_space=pltpu.SMEM)
    smem_nxt = pl.BlockSpec((1, 1, tm), lambda i, *_: (jnp.minimum(i + 1, nt - 1), 0, 0), memory_space=pltpu.SMEM)
    return pl.pallas_call(
        _combine_kernel,
        grid_spec=pltpu.PrefetchScalarGridSpec(
            num_scalar_prefetch=1,
            grid=(nt,),
            in_specs=[smem_blk, smem_blk, smem_nxt, smem_nxt,
                      pl.BlockSpec((tm, D_MODEL), lambda i, *_: (i, 0)),
                      pl.BlockSpec(memory_space=pl.ANY)],
            out_specs=pl.BlockSpec((tm, D_MODEL), lambda i, *_: (i, 0)),
            scratch_shapes=[pltpu.VMEM((2, tm, D_MODEL), F32), pltpu.SemaphoreType.DMA((2,))]),
        out_shape=jax.ShapeDtypeStruct((n, D_MODEL), F32),
        compiler_params=_cparams(("arbitrary",)),
    )(pstart, combo, rank, combo, rank, x2d, y)


_PAIR_A = np.array([a for a in range(EXPERTS_PER_GROUP) for _ in range(a + 1, EXPERTS_PER_GROUP)], np.int32)
_PAIR_B = np.array([b for a in range(EXPERTS_PER_GROUP) for b in range(a + 1, EXPERTS_PER_GROUP)], np.int32)


def _moe(x2, hf, meta, counts, w_gate, w_up, w_down):
    b, t, _ = x2.shape
    n = b * t
    bm = EXPERT_BLOCK
    nblk = (n + N_COMBOS * (bm - 1) + bm - 1) // bm
    cnt = counts[0, :N_COMBOS].astype(jnp.int32)
    psz = (cnt + bm - 1) // bm * bm
    pend = jnp.cumsum(psz)
    pstart = (pend - psz).astype(jnp.int32)
    nused = jnp.maximum(pend[-1] // bm, 1).astype(jnp.int32).reshape(1)
    blk = jnp.minimum(jnp.arange(nblk, dtype=jnp.int32), nused[0] - 1)
    blk_c = jnp.minimum(jnp.sum(pend[None, :] <= (blk * bm)[:, None], axis=1), N_COMBOS - 1).astype(jnp.int32)
    grp = blk_c // PAIRS_PER_GROUP
    blk_ea = (grp * EXPERTS_PER_GROUP + jnp.asarray(_PAIR_A)[blk_c % PAIRS_PER_GROUP]).astype(jnp.int32)
    blk_eb = (grp * EXPERTS_PER_GROUP + jnp.asarray(_PAIR_B)[blk_c % PAIRS_PER_GROUP]).astype(jnp.int32)
    ids = meta.astype(jnp.int32)
    combo = ids[:, 0:1, :]
    rank = ids[:, 1:2, :]
    xs = _dispatch(hf.reshape(n, ROW_EXT), combo, rank, pstart, cnt, nblk * bm)
    y = _experts(xs, blk_ea, blk_eb, nused, w_gate, w_up, w_down)
    out = _combine(x2.reshape(n, D_MODEL), y, combo, rank, pstart)
    return out.reshape(b, t, D_MODEL)


def _prep_layer(lp):
    w_in = lp["w_in"]
    c0 = D_INNER + CONV_CH
    c1 = c0 + 2 * SSD_HEADS
    w_cat = jnp.concatenate([w_in[:, :c0], w_in[:, c1:], w_in[:, c0:c1],
                             jnp.zeros((D_MODEL, DT_PAD - 2 * SSD_HEADS), w_in.dtype)], axis=1).astype(BF16)
    pool_bd = jnp.zeros((POOL_WIDTH, POOL_WIDTH), F32)
    for g in range(POOL_GROUPS):
        sl = slice(g * POOL_GROUP_DIM, (g + 1) * POOL_GROUP_DIM)
        pool_bd = pool_bd.at[sl, sl].set(lp["pool_w"][g].astype(F32))
    w_r = jnp.concatenate([lp["w_router_expert"], lp["w_router_group"],
                           jnp.zeros((D_MODEL, LANES - N_EXPERTS - N_EXPERT_GROUPS), F32)], axis=1).astype(F32)
    w_r_hi = w_r.astype(BF16)
    w_r_lo = (w_r - w_r_hi.astype(F32)).astype(BF16)
    b_r = jnp.concatenate([lp["b_router_expert"], lp["b_router_group"],
                           jnp.zeros((LANES - N_EXPERTS - N_EXPERT_GROUPS,), F32)]).reshape(1, LANES).astype(F32)
    row = lambda a, w: a.reshape(1, w).astype(F32)
    return dict(
        norm_mix=lp["norm_mix"], w_cat=w_cat,
        conv_w=lp["conv_w"], conv_b=lp["conv_b"], dt_bias=lp["dt_bias"], a_log=lp["a_log"],
        d_skip=lp["d_skip"], ssd_norm=lp["ssd_norm"],
        na_q_norm=lp["na_q_norm"], na_k_norm=lp["na_k_norm"], na_rpb=lp["na_rpb"],
        pool_bd=pool_bd.astype(BF16), pool_scale=row(lp["pool_scale"], POOL_WIDTH),
        w_out=lp["w_out"].astype(BF16), norm_xa=row(lp["norm_xa"], D_MODEL),
        norm_mem=lp["norm_mem"], w_xq=lp["w_xq"].astype(BF16), w_xkv=lp["w_xkv"].astype(BF16),
        xq_norm=row(jnp.tile(lp["xq_norm"], XA_HEADS), XA_WIDTH), xk_norm=lp["xk_norm"],
        w_xo=lp["w_xo"].astype(BF16), norm_ffn=row(lp["norm_ffn"], D_MODEL),
        w_r_both=jnp.concatenate([w_r_hi, w_r_lo], axis=1), b_r=b_r,
        w_e_gate=lp["w_e_gate"].astype(BF16), w_e_up=lp["w_e_up"].astype(BF16),
        w_e_down=lp["w_e_down"].astype(BF16),
    )


def _layer(x, mem, p, na_bias):
    b, t, _ = x.shape
    z, xbc, qkv, u, dt = _inproj(x.reshape(b * t, D_MODEL), p["norm_mix"], p["w_cat"])
    r3 = lambda a: a.reshape(b, t, a.shape[-1])
    y_ssd = _ssd(r3(z), r3(xbc), r3(dt), p["conv_w"], p["conv_b"], p["dt_bias"], p["a_log"], p["d_skip"],
                 p["ssd_norm"])
    y_na = _natten(r3(qkv), na_bias, p["na_q_norm"], p["na_k_norm"])
    kmem, vmem = _memkv(mem, p["norm_mem"], p["w_xkv"], p["xk_norm"])
    x2, hf, meta, counts = _mixer(x, y_ssd, y_na, r3(u), kmem, vmem, p)
    return _moe(x2, hf, meta, counts, p["w_e_gate"], p["w_e_up"], p["w_e_down"])


_LAYER_KEYS = ("norm_mix", "w_in", "conv_w", "conv_b", "dt_bias", "a_log", "d_skip", "ssd_norm", "na_q_norm",
               "na_k_norm", "na_rpb", "pool_w", "pool_scale", "w_out", "norm_xa", "norm_mem", "w_xq", "w_xkv",
               "xq_norm", "xk_norm", "w_xo", "norm_ffn", "w_router_group", "b_router_group", "w_router_expert",
               "b_router_expert", "w_e_gate", "w_e_up", "w_e_down")


def kernel(x_prompt, x_sample, mem_prompt, mem_sample, norm_mix, w_in, conv_w, conv_b, dt_bias, a_log, d_skip, ssd_norm, na_q_norm, na_k_norm, na_rpb, pool_w, pool_scale, w_out, norm_xa, norm_mem, w_xq, w_xkv, xq_norm, xk_norm, w_xo, norm_ffn, w_router_group, b_router_group, w_router_expert, b_router_expert, w_e_gate, w_e_up, w_e_down):
    stacked = dict(zip(_LAYER_KEYS, (norm_mix, w_in, conv_w, conv_b, dt_bias, a_log, d_skip, ssd_norm, na_q_norm,
                                     na_k_norm, na_rpb, pool_w, pool_scale, w_out, norm_xa, norm_mem, w_xq, w_xkv,
                                     xq_norm, xk_norm, w_xo, norm_ffn, w_router_group, b_router_group,
                                     w_router_expert, b_router_expert, w_e_gate, w_e_up, w_e_down)))
    depth = w_in.shape[0]
    layers = [_prep_layer({k: v[l] for k, v in stacked.items()}) for l in range(depth)]

    bias_cache = {}

    def trunk(x, mem):
        t = x.shape[1]
        for l, lp in enumerate(layers):
            if (l, t) not in bias_cache:
                bias_cache[(l, t)] = _na_bias(lp["na_rpb"], t)
            x = _layer(x, mem, lp, bias_cache[(l, t)])
        return x

    return trunk(x_prompt, mem_prompt), trunk(x_sample, mem_sample)
```

```python
import functools

import jax
import jax.numpy as jnp
import numpy as np
from jax import lax
from jax.experimental import pallas as pl
from jax.experimental.pallas import tpu as pltpu

F32 = jnp.float32
BF16 = jnp.bfloat16
U32 = jnp.uint32
HIGHEST = lax.Precision.HIGHEST

D_MODEL = 1024
GRID_W = 64
EPS = 1e-6
SSD_HEAD_DIM = 64
D_INNER = D_MODEL // 2
SSD_HEADS = D_INNER // SSD_HEAD_DIM
SSD_GROUPS = 2
SSD_STATE = 64
SSD_CHUNK = 128
CONV_W = 4
CONV_CH = D_INNER + 2 * SSD_GROUPS * SSD_STATE
NA_HEADS = 4
NA_HEAD_DIM = D_MODEL // 16
NA_WIDTH = NA_HEADS * NA_HEAD_DIM
NA_MAX_KH = 8
NA_KW = 16
POOL_WINDOWS = (2, 4, 8, 16)
POOL_GROUPS = 4
POOL_WIDTH = D_MODEL - D_INNER - NA_WIDTH
POOL_GROUP_DIM = POOL_WIDTH // POOL_GROUPS
XA_HEADS = 4
XA_HEAD_DIM = D_MODEL // 8
XA_WIDTH = XA_HEADS * XA_HEAD_DIM
N_EXPERT_GROUPS = 4
EXPERTS_PER_GROUP = 8
N_EXPERTS = N_EXPERT_GROUPS * EXPERTS_PER_GROUP
D_EXPERT = D_MODEL // 4
PAIRS_PER_GROUP = EXPERTS_PER_GROUP * (EXPERTS_PER_GROUP - 1) // 2
N_COMBOS = N_EXPERT_GROUPS * PAIRS_PER_GROUP

LANES = 128
BF16_SUBLANES = 16
VMEM_LIMIT_BYTES = 56 * 1024 * 1024

TOKEN_TILE = 512
NA_QUERY_ROWS = 8
NA_SUB_ROWS = 8
NA_SUBS = NA_QUERY_ROWS // NA_SUB_ROWS
NA_KEY_ROWS = NA_SUB_ROWS + NA_MAX_KH
EXPERT_BLOCK_MIN = 128
HALF = D_MODEL // 2
ROW_WORDS = HALF + LANES
SSD_STEP_CHUNKS = 4
DT_PAD = LANES
DMA_UNROLL = 8
NEG_BIG = -1e30


def _cparams(sem):
    return pltpu.CompilerParams(dimension_semantics=sem, vmem_limit_bytes=VMEM_LIMIT_BYTES)


def _sigmoid(x):
    return 1.0 / (1.0 + jnp.exp(-x))


def _silu(x):
    return x * _sigmoid(x)


def _softplus(x):
    return jnp.maximum(x, 0.0) + jnp.log(1.0 + jnp.exp(-jnp.abs(x)))


def _pack_bf16_pairs(v):
    k = v.shape[1] // 2
    bits = pltpu.bitcast(v, U32)
    return (bits[:, :k] >> 16) | (bits[:, k:] & jnp.uint32(0xFFFF0000))


def _unpack_bf16_pairs(w):
    lo = pltpu.bitcast(w << 16, F32)
    hi = pltpu.bitcast(w & jnp.uint32(0xFFFF0000), F32)
    return jnp.concatenate([lo, hi], axis=1)


def _full(shape):
    n = len(shape)
    return pl.BlockSpec(shape, lambda *_: (0,) * n)


def _inproj_kernel(x_ref, g_ref, w_ref, z_ref, xbc_ref, qkv_ref, u_ref, dt_ref):
    x = x_ref[...]
    ms = jnp.mean(x * x, axis=-1, keepdims=True)
    h = (x * lax.rsqrt(ms + EPS) * g_ref[...]).astype(BF16)
    o = 0
    for ref in (z_ref, xbc_ref, qkv_ref, u_ref, dt_ref):
        w = ref.shape[-1]
        ref[...] = jnp.dot(h, w_ref[:, o:o + w], preferred_element_type=F32).astype(ref.dtype)
        o += w


def _inproj(x2d, gain, w_cat):
    n = x2d.shape[0]
    tm = TOKEN_TILE
    widths = (D_INNER, CONV_CH, 3 * NA_WIDTH, POOL_WIDTH, DT_PAD)
    dtypes = (BF16, BF16, BF16, BF16, F32)
    return pl.pallas_call(
        _inproj_kernel,
        grid=(n // tm,),
        in_specs=[pl.BlockSpec((tm, D_MODEL), lambda i: (i, 0)),
                  _full((1, D_MODEL)),
                  _full(w_cat.shape)],
        out_specs=[pl.BlockSpec((tm, w), lambda i: (i, 0)) for w in widths],
        out_shape=[jax.ShapeDtypeStruct((n, w), d) for w, d in zip(widths, dtypes)],
        compiler_params=_cparams(("arbitrary",)),
    )(x2d, gain.reshape(1, D_MODEL), w_cat)


def _split3(a):
    hi = a.astype(BF16)
    r = a - hi.astype(F32)
    mid = r.astype(BF16)
    lo = (r - mid.astype(F32)).astype(BF16)
    return hi, mid, lo


def _ssd_kernel(xc_ref, xp_ref, xn_ref, dt_ref, z_ref, cw_ref, cb_ref, dtb_ref, alog_ref, dsk_ref, nrm_ref, emat_ref,
                y_ref, state_ref, yf_ref, xs_c, bc_c, cbm_c, dt_c, *, nblocks):
    L = SSD_CHUNK
    LB = SSD_STEP_CHUNKS * L
    P = SSD_HEAD_DIM
    NS = SSD_STATE
    HG = SSD_HEADS // SSD_GROUPS
    gn = SSD_GROUPS * NS
    j = pl.program_id(1)
    c = jnp.where(j < nblocks, j, 2 * nblocks - 1 - j)
    row0 = pl.multiple_of(c * LB, LB)
    rows = pl.ds(row0, LB)
    lane1 = lax.broadcasted_iota(jnp.int32, (1, LANES), 1)
    lo_half = lane1 < P
    ti = lax.broadcasted_iota(jnp.int32, (L, L), 0)
    si = lax.broadcasted_iota(jnp.int32, (L, L), 1)

    def masked_c(bc):
        return [jnp.where(lane1 // NS == g, bc[:, gn:], 0.0).astype(BF16) for g in range(SSD_GROUPS)]

    def prepare():
        cur = xc_ref[0].astype(F32)
        prev = xp_ref[0].astype(F32)
        nxt = xn_ref[0].astype(F32)
        has_prev = (c > 0).astype(F32)
        has_next = (c < nblocks - 1).astype(F32)
        p_last = prev[BF16_SUBLANES - 1:BF16_SUBLANES, :] * has_prev
        n0 = nxt[0:1, :] * has_next
        n1 = nxt[1:2, :] * has_next
        row = lax.broadcasted_iota(jnp.int32, (LB, 1), 0)
        um1 = jnp.where(row == 0, p_last, pltpu.roll(cur, 1, 0))
        up1 = jnp.where(row == LB - 1, n0, pltpu.roll(cur, LB - 1, 0))
        up2 = jnp.where(row == LB - 2, n0, jnp.where(row == LB - 1, n1, pltpu.roll(cur, LB - 2, 0)))
        cw = cw_ref[...]
        acc = cb_ref[...] + um1 * cw[0:1, :] + cur * cw[1:2, :] + up1 * cw[2:3, :] + up2 * cw[3:4, :]
        xbc = _silu(acc)
        xs = xbc[:, :D_INNER]
        bc = xbc[:, D_INNER:D_INNER + 2 * gn].astype(BF16)
        dt = _softplus(dt_ref[0] + dtb_ref[...])
        xs_c[rows, :] = xs
        bc_c[rows, :] = bc
        dt_c[rows, :] = dt
        ops = []
        for sub in range(SSD_STEP_CHUNKS):
            sl = slice(sub * L, (sub + 1) * L)
            cg = masked_c(bc[sl])
            cb_mat = [lax.dot_general(cg[g], bc[sl, :gn], (((1,), (1,)), ((), ())), preferred_element_type=F32)
                      for g in range(SSD_GROUPS)]
            cbm_c[pl.ds(row0 + sub * L, L), :] = jnp.concatenate(cb_mat, axis=1)
            ops.append((xs[sl], bc[sl, :gn], cg, cb_mat, dt[sl]))
        return ops

    def recall():
        ops = []
        for sub in range(SSD_STEP_CHUNKS):
            r = pl.ds(row0 + sub * L, L)
            bc = bc_c[r, :]
            cbm = cbm_c[r, :]
            ops.append((xs_c[r, :], bc[:, :gn], masked_c(bc), [cbm[:, g * L:(g + 1) * L] for g in range(SSD_GROUPS)],
                        dt_c[r, :]))
        return ops

    def scan_chunk(direction, xs, bfull, cg, cb_mat, dt):
        if direction == 0:
            mask = ti >= si
            edge = L - 1
        else:
            mask = si >= ti
            edge = 0
        la = dt * (-jnp.exp(alog_ref[...]))
        tri = mask.astype(BF16)
        csum = sum(jnp.dot(tri, part, preferred_element_type=F32) for part in _split3(la))
        csum_t = csum.T
        emat = emat_ref[direction]
        colb = sum(jnp.dot(part, emat, preferred_element_type=F32) for part in _split3(csum))
        tot = csum[edge:edge + 1, :]
        e_tot = jnp.exp(tot)
        e_in = jnp.exp(csum)
        e_out = jnp.exp(tot - csum)
        ys = []
        for g in range(SSD_GROUPS):
            s_old = state_ref[g]
            y_off = lax.dot_general(cg[g], s_old.astype(BF16), (((1,), (1,)), ((), ())),
                                    preferred_element_type=F32)
            xw = []
            for pr in range(HG // 2):
                h0 = g * HG + 2 * pr
                l0 = direction * SSD_HEADS + h0

                def col(a, l0=l0):
                    return jnp.where(lo_half, a[:, l0:l0 + 1], a[:, l0 + 1:l0 + 2])

                xdt = xs[:, h0 * P:(h0 + 2) * P] * col(dt)
                y_pair = y_off[:, 2 * pr * P:(2 * pr + 2) * P] * col(e_in)
                for hh, half in ((h0, lo_half), (h0 + 1, jnp.logical_not(lo_half))):
                    ll = direction * SSD_HEADS + hh
                    seg = colb[:, hh * L:(hh + 1) * L] - csum_t[ll:ll + 1, :]
                    dec = jnp.exp(jnp.where(mask, seg, NEG_BIG))
                    m = (cb_mat[g] * dec).astype(BF16)
                    y_pair += jnp.dot(m, jnp.where(half, xdt, 0.0).astype(BF16), preferred_element_type=F32)
                ys.append(y_pair)
                xw.append(xdt * col(e_out))
            xw = jnp.concatenate(xw, axis=1).astype(BF16)
            s_new = lax.dot_general(xw, bfull, (((0,), (0,)), ((), ())), preferred_element_type=F32)
            s_scaled = []
            for hl in range(HG):
                lane = direction * SSD_HEADS + g * HG + hl
                s_scaled.append(s_old[hl * P:(hl + 1) * P, :] * e_tot[:, lane:lane + 1])
            state_ref[g] = jnp.concatenate(s_scaled, axis=0) + s_new
        return jnp.concatenate(ys, axis=1)

    @pl.when(jnp.logical_or(j == 0, j == nblocks))
    def _():
        state_ref[...] = jnp.zeros_like(state_ref)

    @pl.when(j < nblocks)
    def _():
        ops = prepare()
        for sub in range(SSD_STEP_CHUNKS):
            yf_ref[pl.ds(row0 + sub * L, L), :] = scan_chunk(0, *ops[sub])

    @pl.when(j >= nblocks)
    def _():
        ops = recall()
        for sub in reversed(range(SSD_STEP_CHUNKS)):
            sl = slice(sub * L, (sub + 1) * L)
            y = yf_ref[pl.ds(row0 + sub * L, L), :] + scan_chunk(1, *ops[sub]) + dsk_ref[...] * ops[sub][0]
            y = y * _silu(z_ref[0, sl, :].astype(F32))
            gw = D_INNER // SSD_GROUPS
            outs = []
            for g in range(SSD_GROUPS):
                yg = y[:, g * gw:(g + 1) * gw]
                outs.append(yg * lax.rsqrt(jnp.mean(yg * yg, axis=-1, keepdims=True) + EPS))
            y_ref[0, sl, :] = (jnp.concatenate(outs, axis=1) * nrm_ref[...]).astype(y_ref.dtype)


def _ssd(z, xbc, dt, conv_w, conv_b, dt_bias, a_log, d_skip, ssd_norm):
    b, t, _ = z.shape
    L = SSD_CHUNK
    lb = SSD_STEP_CHUNKS * L
    nb = t // lb
    hb = lb // BF16_SUBLANES
    nhalo = t // BF16_SUBLANES

    def blk(j):
        return jnp.where(j < nb, j, 2 * nb - 1 - j)

    pad = DT_PAD - 2 * SSD_HEADS
    dtb = jnp.pad(dt_bias.reshape(1, -1).astype(F32), ((0, 0), (0, pad)))
    alog = jnp.pad(a_log.reshape(1, -1).astype(F32), ((0, 0), (0, pad)))
    dsk = jnp.repeat(d_skip.astype(F32), SSD_HEAD_DIM).reshape(1, D_INNER)
    sel = np.arange(DT_PAD)[None, :, None] == (np.arange(2)[:, None, None] * SSD_HEADS
                                                + np.arange(SSD_HEADS)[None, None, :])
    emat = jnp.asarray(np.repeat(sel, L, axis=2), BF16)
    return pl.pallas_call(
        functools.partial(_ssd_kernel, nblocks=nb),
        grid=(b, 2 * nb),
        in_specs=[
            pl.BlockSpec((1, lb, CONV_CH), lambda i, j: (i, blk(j), 0)),
            pl.BlockSpec((1, BF16_SUBLANES, CONV_CH), lambda i, j: (i, jnp.maximum(blk(j) * hb - 1, 0), 0)),
            pl.BlockSpec((1, BF16_SUBLANES, CONV_CH),
                         lambda i, j: (i, jnp.minimum((blk(j) + 1) * hb, nhalo - 1), 0)),
            pl.BlockSpec((1, lb, DT_PAD), lambda i, j: (i, blk(j), 0)),
            pl.BlockSpec((1, lb, D_INNER), lambda i, j: (i, blk(j), 0)),
            _full((CONV_W, CONV_CH)), _full((1, CONV_CH)), _full((1, DT_PAD)), _full((1, DT_PAD)),
            _full((1, D_INNER)), _full((1, D_INNER)), _full((2, DT_PAD, SSD_HEADS * L)),
        ],
        out_specs=pl.BlockSpec((1, lb, D_INNER), lambda i, j: (i, jnp.where(j < nb, nb - 1, 2 * nb - 1 - j), 0)),
        out_shape=jax.ShapeDtypeStruct((b, t, D_INNER), BF16),
        scratch_shapes=[pltpu.VMEM((SSD_GROUPS, (SSD_HEADS // SSD_GROUPS) * SSD_HEAD_DIM, LANES), F32),
                        pltpu.VMEM((t, D_INNER), F32),
                        pltpu.VMEM((t, D_INNER), F32),
                        pltpu.VMEM((t, 2 * SSD_GROUPS * SSD_STATE), BF16),
                        pltpu.VMEM((t, SSD_GROUPS * L), F32),
                        pltpu.VMEM((t, DT_PAD), F32)],
        compiler_params=_cparams(("arbitrary", "arbitrary")),
    )(xbc, xbc, xbc, dt, z, conv_w.astype(F32), conv_b.reshape(1, CONV_CH).astype(F32), dtb, alog, dsk,
      ssd_norm.reshape(1, D_INNER).astype(F32), emat)


def _na_bias(rpb, t):
    r = t // GRID_W
    kh = min(NA_MAX_KH, r)
    nsb = r // NA_SUB_ROWS
    rows = np.arange(r)
    row_start = np.clip(rows - NA_MAX_KH // 2, 0, r - kh)
    r0 = np.arange(nsb) * NA_SUB_ROWS
    kr0 = np.clip(r0 - NA_MAX_KH // 2, 0, r - NA_KEY_ROWS)
    qrow = r0[:, None] + np.arange(NA_SUB_ROWS)[None, :]
    krow = kr0[:, None] + np.arange(NA_KEY_ROWS)[None, :]
    rs = row_start[qrow]
    row_ok = (krow[:, None, :] >= rs[:, :, None]) & (krow[:, None, :] < rs[:, :, None] + kh)
    dr = np.clip(krow[:, None, :] - qrow[:, :, None] + (NA_MAX_KH - 1), 0, 2 * NA_MAX_KH - 2)
    cols = np.arange(GRID_W)
    col_start = np.clip(cols - NA_KW // 2, 0, GRID_W - NA_KW)
    col_ok = (cols[None, :] >= col_start[:, None]) & (cols[None, :] < col_start[:, None] + NA_KW)
    dc = np.clip(cols[None, :] - cols[:, None] + (NA_KW - 1), 0, 2 * NA_KW - 2)
    sel_c = (dc[..., None] == np.arange(2 * NA_KW - 1)) & col_ok[..., None]
    nr = 2 * NA_MAX_KH - 1
    tile_r = jnp.einsum("hrc,xyc->hrxy", rpb.astype(F32), jnp.asarray(sel_c, F32), precision=HIGHEST)
    tile_r = jnp.where(jnp.asarray(col_ok)[None, None], tile_r, NEG_BIG)
    tile_r = jnp.concatenate([tile_r, jnp.full((NA_HEADS, 1, GRID_W, GRID_W), NEG_BIG, F32)], axis=1)
    kp = NA_KEY_ROWS // 2
    code = np.where(row_ok, dr, nr).reshape(nsb * NA_SUB_ROWS * kp, 2)
    pairs, inv = np.unique(code, axis=0, return_inverse=True)
    blocks = jnp.concatenate([tile_r[:, pairs[:, 0]], tile_r[:, pairs[:, 1]]], axis=-1)
    blocks = jnp.moveaxis(blocks, 0, 1).reshape(len(pairs), NA_HEADS * GRID_W * 2 * GRID_W)
    onehot = jnp.asarray(inv.reshape(-1, 1) == np.arange(len(pairs))[None, :], F32)
    bias = jnp.dot(onehot, blocks, precision=HIGHEST)
    return bias.reshape(nsb // NA_SUBS, NA_SUBS, NA_SUB_ROWS, kp, NA_HEADS, GRID_W, 2 * GRID_W).astype(BF16)


def _natten_kernel(qkv_ref, bias_ref, qg_ref, kg_ref, seg_ref, o_ref, *, grid_rows):
    nq = NA_SUB_ROWS * GRID_W
    nk = NA_KEY_ROWS * GRID_W
    rb = pl.program_id(0)
    seg = seg_ref[...]
    lane_h = lax.broadcasted_iota(jnp.int32, (1, NA_WIDTH), 1) // NA_HEAD_DIM
    for sub in range(NA_SUBS):
        r0 = rb * NA_QUERY_ROWS + sub * NA_SUB_ROWS
        kr0 = jnp.clip(r0 - NA_MAX_KH // 2, 0, grid_rows - NA_KEY_ROWS)
        q0 = pl.multiple_of(r0 * GRID_W, nq)
        k0 = pl.multiple_of(kr0 * GRID_W, NA_MAX_KH // 2 * GRID_W)
        q = qkv_ref[0, pl.ds(q0, nq), 0:NA_WIDTH].astype(F32)
        k = qkv_ref[0, pl.ds(k0, nk), NA_WIDTH:2 * NA_WIDTH].astype(F32)
        v = qkv_ref[0, pl.ds(k0, nk), 2 * NA_WIDTH:3 * NA_WIDTH]
        qms = jnp.dot(q * q, seg, precision=HIGHEST, preferred_element_type=F32)
        kms = jnp.dot(k * k, seg, precision=HIGHEST, preferred_element_type=F32)
        qn = q * lax.rsqrt(qms + EPS) * (qg_ref[...] * NA_HEAD_DIM ** -0.5)
        kn = (k * lax.rsqrt(kms + EPS) * kg_ref[...]).astype(BF16)
        acc = jnp.zeros((nq, NA_WIDTH), F32)
        for h in range(NA_HEADS):
            hm = lane_h == h
            s = lax.dot_general(jnp.where(hm, qn, 0.0).astype(BF16), kn, (((1,), (1,)), ((), ())),
                                preferred_element_type=F32)
            bias = jnp.concatenate(
                [jnp.concatenate([bias_ref[0, sub, qr, kc, h] for kc in range(NA_KEY_ROWS // 2)], axis=1)
                 for qr in range(NA_SUB_ROWS)], axis=0)
            s = s + bias.astype(F32)
            p = jnp.exp(s - jnp.max(s, axis=-1, keepdims=True))
            l = jnp.sum(p, axis=-1, keepdims=True)
            o = jnp.dot(p.astype(BF16), v, preferred_element_type=F32)
            acc += jnp.where(hm, o / l, 0.0)
        o_ref[0, sub * nq:(sub + 1) * nq, :] = acc.astype(o_ref.dtype)


def _natten(qkv, bias, q_norm, k_norm):
    b, t, _ = qkv.shape
    r = t // GRID_W
    nrb = r // NA_QUERY_ROWS
    nq = NA_QUERY_ROWS * GRID_W
    head = jnp.arange(NA_WIDTH) // NA_HEAD_DIM
    seg = (head[:, None] == head[None, :]).astype(F32) / NA_HEAD_DIM
    return pl.pallas_call(
        functools.partial(_natten_kernel, grid_rows=r),
        grid=(nrb, b),
        in_specs=[pl.BlockSpec((1, t, 3 * NA_WIDTH), lambda i, j: (j, 0, 0)),
                  pl.BlockSpec((1,) + bias.shape[1:], lambda i, j: (i,) + (0,) * (bias.ndim - 1)),
                  _full((1, NA_WIDTH)), _full((1, NA_WIDTH)), _full((NA_WIDTH, NA_WIDTH))],
        out_specs=pl.BlockSpec((1, nq, NA_WIDTH), lambda i, j: (j, i, 0)),
        out_shape=jax.ShapeDtypeStruct((b, t, NA_WIDTH), BF16),
        compiler_params=_cparams(("arbitrary", "arbitrary")),
    )(qkv, bias, jnp.tile(q_norm.astype(F32), NA_HEADS).reshape(1, NA_WIDTH),
      jnp.tile(k_norm.astype(F32), NA_HEADS).reshape(1, NA_WIDTH), seg)


def _memkv_kernel(m_ref, g_ref, w_ref, kg_ref, k_ref, v_ref):
    x = m_ref[0]
    ms = jnp.mean(x * x, axis=-1, keepdims=True)
    h = (x * lax.rsqrt(ms + EPS) * g_ref[...]).astype(BF16)
    kv = jnp.dot(h, w_ref[...], preferred_element_type=F32)
    ks = []
    for hd in range(XA_HEADS):
        kh = kv[:, hd * XA_HEAD_DIM:(hd + 1) * XA_HEAD_DIM]
        ks.append(kh * lax.rsqrt(jnp.mean(kh * kh, axis=-1, keepdims=True) + EPS))
    k_ref[0] = (jnp.concatenate(ks, axis=1) * kg_ref[...]).astype(k_ref.dtype)
    v_ref[0] = kv[:, XA_WIDTH:].astype(v_ref.dtype)


def _memkv(mem, norm_mem, w_xkv, xk_norm):
    b, m, _ = mem.shape
    return pl.pallas_call(
        _memkv_kernel,
        grid=(b,),
        in_specs=[pl.BlockSpec((1, m, D_MODEL), lambda i: (i, 0, 0)),
                  _full((1, D_MODEL)), _full((D_MODEL, 2 * XA_WIDTH)), _full((1, XA_WIDTH))],
        out_specs=[pl.BlockSpec((1, m, XA_WIDTH), lambda i: (i, 0, 0))] * 2,
        out_shape=[jax.ShapeDtypeStruct((b, m, XA_WIDTH), BF16)] * 2,
        compiler_params=_cparams(("arbitrary",)),
    )(mem, norm_mem.reshape(1, D_MODEL).astype(F32), w_xkv,
      jnp.tile(xk_norm.astype(F32), XA_HEADS).reshape(1, XA_WIDTH))


def _mixer_kernel(x_ref, ys_ref, yn_ref, u_ref, up_ref, un_ref, k_ref, v_ref,
                  pw_ref, psc_ref, wo_ref, gxa_ref, wq_ref, qg_ref, wxo_ref, gff_ref,
                  wrb_ref, br_ref, ltri_ref,
                  x2_ref, hf_ref, meta_ref, cnt_ref, carry_ref, *, seq):
    tm = x_ref.shape[1]
    halo = BF16_SUBLANES
    bi = pl.program_id(0)
    i = pl.program_id(1)
    nt = pl.num_programs(1)

    @pl.when(jnp.logical_and(bi == 0, i == 0))
    def _():
        carry_ref[...] = jnp.zeros_like(carry_ref)

    u = u_ref[0].astype(F32)
    up = up_ref[0].astype(F32) * (i > 0).astype(F32)
    un = un_ref[0].astype(F32) * (i < nt - 1).astype(F32)
    cat = jnp.concatenate([up, u, un], axis=0)
    n = tm + 2 * halo

    def sh(a, k):
        return pltpu.roll(a, (-k) % n, 0)

    a2 = cat + sh(cat, -1)
    a4 = sh(a2, 1) + sh(a2, -1)
    a8 = sh(a4, 2) + sh(a4, -2)
    a16 = sh(a8, 4) + sh(a8, -4)
    lane_g = lax.broadcasted_iota(jnp.int32, (1, POOL_WIDTH), 1) // POOL_GROUP_DIM
    wsum = jnp.where(lane_g == 0, a2, jnp.where(lane_g == 1, a4, jnp.where(lane_g == 2, a8, a16)))
    wsum = wsum[halo:halo + tm, :]
    half = jnp.where(lane_g == 0, POOL_WINDOWS[0] // 2,
                     jnp.where(lane_g == 1, POOL_WINDOWS[1] // 2,
                               jnp.where(lane_g == 2, POOL_WINDOWS[2] // 2, POOL_WINDOWS[3] // 2)))
    tpos = i * tm + lax.broadcasted_iota(jnp.int32, (tm, 1), 0)
    cnt = (jnp.minimum(tpos + half, seq) - jnp.maximum(tpos - half, 0)).astype(F32)
    d = wsum / cnt - u
    ypool = jnp.dot(d.astype(BF16), pw_ref[...], preferred_element_type=F32) * psc_ref[...]

    mix = jnp.dot(ys_ref[0], wo_ref[0:D_INNER, :], preferred_element_type=F32)
    mix += jnp.dot(yn_ref[0], wo_ref[D_INNER:D_INNER + NA_WIDTH, :], preferred_element_type=F32)
    mix += jnp.dot(ypool.astype(BF16), wo_ref[D_INNER + NA_WIDTH:, :], preferred_element_type=F32)
    x1 = x_ref[0] + mix

    hn = (x1 * lax.rsqrt(jnp.mean(x1 * x1, axis=-1, keepdims=True) + EPS) * gxa_ref[...]).astype(BF16)
    q = jnp.dot(hn, wq_ref[...], preferred_element_type=F32)
    kk = k_ref[0]
    vv = v_ref[0]
    scale = XA_HEAD_DIM ** -0.5
    outs = []
    for hd in range(XA_HEADS):
        sl = slice(hd * XA_HEAD_DIM, (hd + 1) * XA_HEAD_DIM)
        qh = q[:, sl]
        qh = (qh * lax.rsqrt(jnp.mean(qh * qh, axis=-1, keepdims=True) + EPS) * qg_ref[:, sl]).astype(BF16)
        s = lax.dot_general(qh, kk[:, sl], (((1,), (1,)), ((), ())), preferred_element_type=F32) * scale
        p = jnp.exp(s - jnp.max(s, axis=-1, keepdims=True))
        l = jnp.sum(p, axis=-1, keepdims=True)
        outs.append(jnp.dot(p.astype(BF16), vv[:, sl], preferred_element_type=F32) / l)
    att = jnp.concatenate(outs, axis=1).astype(BF16)
    x2 = x1 + jnp.dot(att, wxo_ref[...], preferred_element_type=F32)
    x2_ref[0] = x2

    hf = x2 * lax.rsqrt(jnp.mean(x2 * x2, axis=-1, keepdims=True) + EPS) * gff_ref[...]
    h_hi = hf.astype(BF16)
    hf_ref[0, :, :HALF] = _pack_bf16_pairs(h_hi.astype(F32))
    h_lo = (hf - h_hi.astype(F32)).astype(BF16)
    both = jnp.dot(h_hi, wrb_ref[...], preferred_element_type=F32)
    logits = (both[:, :LANES] + both[:, LANES:]
              + jnp.dot(h_lo, wrb_ref[:, :LANES], preferred_element_type=F32)) + br_ref[...]
    lane = lax.broadcasted_iota(jnp.int32, (1, LANES), 1)
    lane_f = lane.astype(F32)
    lane_grp = (lane // EXPERTS_PER_GROUP).astype(F32)
    is_g = jnp.logical_and(lane >= N_EXPERTS, lane < N_EXPERTS + N_EXPERT_GROUPS)
    gl = jnp.where(is_g, logits, NEG_BIG)
    gmax = jnp.max(gl, axis=-1, keepdims=True)
    g_sel = jnp.min(jnp.where(gl == gmax, lane_f, float(LANES)), axis=-1, keepdims=True) - N_EXPERTS
    g_gate = 1.0 / jnp.sum(jnp.where(is_g, jnp.exp(gl - gmax), 0.0), axis=-1, keepdims=True)
    in_grp = jnp.logical_and(lane < N_EXPERTS, lane_grp == g_sel)
    el = jnp.where(in_grp, logits, NEG_BIG)
    v1 = jnp.max(el, axis=-1, keepdims=True)
    e0 = jnp.min(jnp.where(el == v1, lane_f, float(LANES)), axis=-1, keepdims=True)
    el2 = jnp.where(lane_f == e0, NEG_BIG, el)
    v2 = jnp.max(el2, axis=-1, keepdims=True)
    e1 = jnp.min(jnp.where(el2 == v2, lane_f, float(LANES)), axis=-1, keepdims=True)
    w1 = jnp.exp(v2 - v1)
    gate0 = g_gate / (1.0 + w1)
    gate1 = g_gate * w1 / (1.0 + w1)

    base = g_sel * EXPERTS_PER_GROUP
    ea = jnp.minimum(e0, e1) - base
    eb = jnp.maximum(e0, e1) - base
    combo = g_sel * PAIRS_PER_GROUP + ea * EXPERTS_PER_GROUP - ea * (ea + 1.0) * 0.5 + (eb - ea - 1.0)
    gate_a = jnp.where(e0 < e1, gate0, gate1)
    gate_b = jnp.where(e0 < e1, gate1, gate0)
    hf_ref[0, :, HALF:] = pltpu.bitcast(jnp.where(lane == 0, gate_a, 0.0) + jnp.where(lane == 1, gate_b, 0.0), U32)

    oh = lane_f == combo
    cnt_tok = oh.astype(F32)
    before = jnp.dot(ltri_ref[...], cnt_tok.astype(BF16), preferred_element_type=F32) + carry_ref[0:1, :]
    rank = jnp.sum(jnp.where(oh, before, 0.0), axis=-1, keepdims=True)
    new_carry = carry_ref[0:1, :] + jnp.sum(cnt_tok, axis=0, keepdims=True)
    carry_ref[...] = jnp.broadcast_to(new_carry, carry_ref.shape)
    cnt_ref[...] = jnp.broadcast_to(new_carry, cnt_ref.shape)

    slab = jnp.where(lane == 0, combo, 0.0) + jnp.where(lane == 1, rank, 0.0)
    meta_ref[0] = slab.T[0:8, :]


def _mixer(x, y_ssd, y_na, u, kmem, vmem, p):
    b, t, _ = x.shape
    tm = TOKEN_TILE
    nt = t // tm
    hb = tm // BF16_SUBLANES
    nhalo = t // BF16_SUBLANES
    ltri = (jnp.arange(tm)[:, None] > jnp.arange(tm)[None, :]).astype(BF16)
    tok = lambda w: pl.BlockSpec((1, tm, w), lambda i, j: (i, j, 0))
    mem = pl.BlockSpec((1, kmem.shape[1], XA_WIDTH), lambda i, j: (i, 0, 0))
    weights = (p["pool_bd"], p["pool_scale"], p["w_out"], p["norm_xa"], p["w_xq"], p["xq_norm"], p["w_xo"],
               p["norm_ffn"], p["w_r_both"], p["b_r"], ltri)
    return pl.pallas_call(
        functools.partial(_mixer_kernel, seq=t),
        grid=(b, nt),
        in_specs=[tok(D_MODEL), tok(D_INNER), tok(NA_WIDTH), tok(POOL_WIDTH),
                  pl.BlockSpec((1, BF16_SUBLANES, POOL_WIDTH), lambda i, j: (i, jnp.maximum(j * hb - 1, 0), 0)),
                  pl.BlockSpec((1, BF16_SUBLANES, POOL_WIDTH),
                               lambda i, j: (i, jnp.minimum((j + 1) * hb, nhalo - 1), 0)),
                  mem, mem] + [_full(w.shape) for w in weights],
        out_specs=[tok(D_MODEL), tok(ROW_WORDS),
                   pl.BlockSpec((1, 8, tm), lambda i, j: (i * nt + j, 0, 0)),
                   pl.BlockSpec((8, LANES), lambda i, j: (0, 0))],
        out_shape=[jax.ShapeDtypeStruct((b, t, D_MODEL), F32),
                   jax.ShapeDtypeStruct((b, t, ROW_WORDS), U32),
                   jax.ShapeDtypeStruct((b * nt, 8, tm), F32),
                   jax.ShapeDtypeStruct((8, LANES), F32)],
        scratch_shapes=[pltpu.VMEM((8, LANES), F32)],
        compiler_params=_cparams(("arbitrary", "arbitrary")),
    )(x, y_ssd, y_na, u, u, u, kmem, vmem, *weights)


def _dispatch_kernel(pstart_ref, cnt_ref, cmb_ref, rk_ref, hf_ref, zero_ref, xs_ref, stage, sem, *, bm):
    tm = hf_ref.shape[0]
    i = pl.program_id(0)
    n = pl.num_programs(0)
    slot = i % 2

    def row_copy(src_ref, src_row, dst_row, s):
        return pltpu.make_async_copy(src_ref.at[pl.ds(src_row, 1)], xs_ref.at[pl.ds(dst_row, 1)], s)

    def wait_tile(sl):
        pltpu.make_async_copy(stage.at[sl], xs_ref.at[pl.ds(0, tm)], sem.at[sl]).wait()

    @pl.when(i == 0)
    def _():
        def per_combo(cm, _):
            n_c = cnt_ref[cm]
            npad = (bm - n_c % bm) % bm
            base = pstart_ref[cm] + n_c

            def start(r, _):
                row_copy(zero_ref, 0, base + r, sem.at[0]).start()
                return 0

            def wait(r, _):
                row_copy(zero_ref, 0, base + r, sem.at[0]).wait()
                return 0

            lax.fori_loop(0, npad, start, 0)
            lax.fori_loop(0, npad, wait, 0)
            return 0

        lax.fori_loop(0, N_COMBOS, per_combo, 0)

    stage[slot] = hf_ref[...]

    def start(pair, _):
        for u in range(2):
            tk = 2 * pair + u
            row_copy(stage.at[slot], tk, pstart_ref[cmb_ref[0, 0, tk]] + rk_ref[0, 0, tk],
                     sem.at[slot]).start(priority=u)
        return 0

    lax.fori_loop(0, tm // 2, start, 0, unroll=DMA_UNROLL // 2)

    @pl.when(i > 0)
    def _():
        wait_tile(1 - slot)

    @pl.when(i == n - 1)
    def _():
        wait_tile(slot)


def _dispatch(hf2d, combo, rank, pstart, counts, n_rows, bm):
    n = hf2d.shape[0]
    tm = TOKEN_TILE
    smem_blk = pl.BlockSpec((1, 1, tm), lambda i, *_: (i, 0, 0), memory_space=pltpu.SMEM)
    return pl.pallas_call(
        functools.partial(_dispatch_kernel, bm=bm),
        grid_spec=pltpu.PrefetchScalarGridSpec(
            num_scalar_prefetch=2,
            grid=(n // tm,),
            in_specs=[smem_blk, smem_blk,
                      pl.BlockSpec((tm, ROW_WORDS), lambda i, *_: (i, 0)),
                      pl.BlockSpec((8, ROW_WORDS), lambda i, *_: (0, 0))],
            out_specs=pl.BlockSpec(memory_space=pl.ANY),
            scratch_shapes=[pltpu.VMEM((2, tm, ROW_WORDS), U32), pltpu.SemaphoreType.DMA((2,))]),
        out_shape=jax.ShapeDtypeStruct((n_rows, ROW_WORDS), U32),
        compiler_params=_cparams(("arbitrary",)),
    )(pstart, counts, combo, rank, hf2d, jnp.zeros((8, ROW_WORDS), U32))


def _experts_kernel(ea_ref, eb_ref, nused_ref, x_ref, wga_ref, wua_ref, wda_ref, wgb_ref, wub_ref, wdb_ref, y_ref):
    @pl.when(pl.program_id(0) < nused_ref[0])
    def _():
        x = _unpack_bf16_pairs(x_ref[:, :HALF]).astype(BF16)
        gates = pltpu.bitcast(x_ref[:, HALF:], F32)

        def mlp(wg_ref, wu_ref, wd_ref):
            hg = jnp.dot(x, wg_ref[0], preferred_element_type=F32)
            hu = jnp.dot(x, wu_ref[0], preferred_element_type=F32)
            return jnp.dot((_silu(hg) * hu).astype(BF16), wd_ref[0], preferred_element_type=F32)

        y = mlp(wga_ref, wua_ref, wda_ref) * gates[:, 0:1] + mlp(wgb_ref, wub_ref, wdb_ref) * gates[:, 1:2]
        y_ref[...] = _pack_bf16_pairs(y.astype(BF16).astype(F32))


def _experts(xs, blk_ea, blk_eb, nused, w_gate, w_up, w_down, bm):
    nblk = blk_ea.shape[0]
    row = lambda j, ea, eb, nu: (jnp.minimum(j, nu[0] - 1), 0)
    row_out = lambda j, ea, eb, nu: (jnp.where(j < nu[0], j, nblk - 1), 0)
    sel_a = lambda j, ea, eb, nu: (ea[j], 0, 0)
    sel_b = lambda j, ea, eb, nu: (eb[j], 0, 0)
    up = lambda sel: pl.BlockSpec((1, D_MODEL, D_EXPERT), sel)
    down = lambda sel: pl.BlockSpec((1, D_EXPERT, D_MODEL), sel)
    return pl.pallas_call(
        _experts_kernel,
        grid_spec=pltpu.PrefetchScalarGridSpec(
            num_scalar_prefetch=3,
            grid=(nblk,),
            in_specs=[pl.BlockSpec((bm, ROW_WORDS), row),
                      up(sel_a), up(sel_a), down(sel_a), up(sel_b), up(sel_b), down(sel_b)],
            out_specs=pl.BlockSpec((bm, HALF), row_out)),
        out_shape=jax.ShapeDtypeStruct((nblk * bm, HALF), U32),
        compiler_params=_cparams(("arbitrary",)),
    )(blk_ea, blk_eb, nused, xs, w_gate, w_up, w_down, w_gate, w_up, w_down)


def _combine_kernel(pstart_ref, cmb_ref, rk_ref, cmb_nx_ref, rk_nx_ref, x_ref, y_hbm, o_ref, ybuf, sem):
    tm = x_ref.shape[0]
    i = pl.program_id(0)
    n = pl.num_programs(0)
    slot = i % 2

    def gather_tile(c_ref, r_ref, sl):
        def start(pair, _):
            for u in range(2):
                tk = 2 * pair + u
                d = pstart_ref[c_ref[0, 0, tk]] + r_ref[0, 0, tk]
                pltpu.make_async_copy(y_hbm.at[pl.ds(d, 1)], ybuf.at[sl, pl.ds(tk, 1)], sem.at[sl]).start(priority=u)
            return 0

        lax.fori_loop(0, tm // 2, start, 0, unroll=DMA_UNROLL // 2)

    @pl.when(i == 0)
    def _():
        gather_tile(cmb_ref, rk_ref, slot)

    @pl.when(i + 1 < n)
    def _():
        gather_tile(cmb_nx_ref, rk_nx_ref, 1 - slot)

    pltpu.make_async_copy(y_hbm.at[pl.ds(0, tm)], ybuf.at[slot], sem.at[slot]).wait()
    o_ref[...] = x_ref[...] + _unpack_bf16_pairs(ybuf[slot])


def _combine(x2d, y, combo, rank, pstart):
    n = x2d.shape[0]
    tm = TOKEN_TILE
    nt = n // tm
    smem_blk = pl.BlockSpec((1, 1, tm), lambda i, *_: (i, 0, 0), memory_space=pltpu.SMEM)
    smem_nxt = pl.BlockSpec((1, 1, tm), lambda i, *_: (jnp.minimum(i + 1, nt - 1), 0, 0), memory_space=pltpu.SMEM)
    return pl.pallas_call(
        _combine_kernel,
        grid_spec=pltpu.PrefetchScalarGridSpec(
            num_scalar_prefetch=1,
            grid=(nt,),
            in_specs=[smem_blk, smem_blk, smem_nxt, smem_nxt,
                      pl.BlockSpec((tm, D_MODEL), lambda i, *_: (i, 0)),
                      pl.BlockSpec(memory_space=pl.ANY)],
            out_specs=pl.BlockSpec((tm, D_MODEL), lambda i, *_: (i, 0)),
            scratch_shapes=[pltpu.VMEM((2, tm, HALF), U32), pltpu.SemaphoreType.DMA((2,))]),
        out_shape=jax.ShapeDtypeStruct((n, D_MODEL), F32),
        compiler_params=_cparams(("arbitrary",)),
    )(pstart, combo, rank, combo, rank, x2d, y)


_PAIR_A = np.array([a for a in range(EXPERTS_PER_GROUP) for _ in range(a + 1, EXPERTS_PER_GROUP)], np.int32)
_PAIR_B = np.array([b for a in range(EXPERTS_PER_GROUP) for b in range(a + 1, EXPERTS_PER_GROUP)], np.int32)


def _expert_block(n):
    return 2 * EXPERT_BLOCK_MIN if n >= 2 * 2 * EXPERT_BLOCK_MIN * N_COMBOS else EXPERT_BLOCK_MIN


def _moe(x2, hf, meta, counts, w_gate, w_up, w_down):
    b, t, _ = x2.shape
    n = b * t
    bm = _expert_block(n)
    nblk = (n + N_COMBOS * (bm - 1) + bm - 1) // bm
    cnt = counts[0, :N_COMBOS].astype(jnp.int32)
    psz = (cnt + bm - 1) // bm * bm
    pend = jnp.cumsum(psz)
    pstart = (pend - psz).astype(jnp.int32)
    nused = jnp.maximum(pend[-1] // bm, 1).astype(jnp.int32).reshape(1)
    blk = jnp.minimum(jnp.arange(nblk, dtype=jnp.int32), nused[0] - 1)
    blk_c = jnp.minimum(jnp.sum(pend[None, :] <= (blk * bm)[:, None], axis=1), N_COMBOS - 1).astype(jnp.int32)
    grp = blk_c // PAIRS_PER_GROUP
    blk_ea = (grp * EXPERTS_PER_GROUP + jnp.asarray(_PAIR_A)[blk_c % PAIRS_PER_GROUP]).astype(jnp.int32)
    blk_eb = (grp * EXPERTS_PER_GROUP + jnp.asarray(_PAIR_B)[blk_c % PAIRS_PER_GROUP]).astype(jnp.int32)
    ids = meta.astype(jnp.int32)
    combo = ids[:, 0:1, :]
    rank = ids[:, 1:2, :]
    xs = _dispatch(hf.reshape(n, ROW_WORDS), combo, rank, pstart, cnt, nblk * bm, bm)
    y = _experts(xs, blk_ea, blk_eb, nused, w_gate, w_up, w_down, bm)
    out = _combine(x2.reshape(n, D_MODEL), y, combo, rank, pstart)
    return out.reshape(b, t, D_MODEL)


def _prep_layer(lp):
    w_in = lp["w_in"]
    c0 = D_INNER + CONV_CH
    c1 = c0 + 2 * SSD_HEADS
    w_cat = jnp.concatenate([w_in[:, :c0], w_in[:, c1:], w_in[:, c0:c1],
                             jnp.zeros((D_MODEL, DT_PAD - 2 * SSD_HEADS), w_in.dtype)], axis=1).astype(BF16)
    pool_bd = jnp.zeros((POOL_WIDTH, POOL_WIDTH), F32)
    for g in range(POOL_GROUPS):
        sl = slice(g * POOL_GROUP_DIM, (g + 1) * POOL_GROUP_DIM)
        pool_bd = pool_bd.at[sl, sl].set(lp["pool_w"][g].astype(F32))
    w_r = jnp.concatenate([lp["w_router_expert"], lp["w_router_group"],
                           jnp.zeros((D_MODEL, LANES - N_EXPERTS - N_EXPERT_GROUPS), F32)], axis=1).astype(F32)
    w_r_hi = w_r.astype(BF16)
    w_r_lo = (w_r - w_r_hi.astype(F32)).astype(BF16)
    b_r = jnp.concatenate([lp["b_router_expert"], lp["b_router_group"],
                           jnp.zeros((LANES - N_EXPERTS - N_EXPERT_GROUPS,), F32)]).reshape(1, LANES).astype(F32)
    row = lambda a, w: a.reshape(1, w).astype(F32)
    return dict(
        norm_mix=lp["norm_mix"], w_cat=w_cat,
        conv_w=lp["conv_w"], conv_b=lp["conv_b"], dt_bias=lp["dt_bias"], a_log=lp["a_log"],
        d_skip=lp["d_skip"], ssd_norm=lp["ssd_norm"],
        na_q_norm=lp["na_q_norm"], na_k_norm=lp["na_k_norm"], na_rpb=lp["na_rpb"],
        pool_bd=pool_bd.astype(BF16), pool_scale=row(lp["pool_scale"], POOL_WIDTH),
        w_out=lp["w_out"].astype(BF16), norm_xa=row(lp["norm_xa"], D_MODEL),
        norm_mem=lp["norm_mem"], w_xq=lp["w_xq"].astype(BF16), w_xkv=lp["w_xkv"].astype(BF16),
        xq_norm=row(jnp.tile(lp["xq_norm"], XA_HEADS), XA_WIDTH), xk_norm=lp["xk_norm"],
        w_xo=lp["w_xo"].astype(BF16), norm_ffn=row(lp["norm_ffn"], D_MODEL),
        w_r_both=jnp.concatenate([w_r_hi, w_r_lo], axis=1), b_r=b_r,
        w_e_gate=lp["w_e_gate"].astype(BF16), w_e_up=lp["w_e_up"].astype(BF16),
        w_e_down=lp["w_e_down"].astype(BF16),
    )


def _layer(x, mem, p, na_bias):
    b, t, _ = x.shape
    z, xbc, qkv, u, dt = _inproj(x.reshape(b * t, D_MODEL), p["norm_mix"], p["w_cat"])
    r3 = lambda a: a.reshape(b, t, a.shape[-1])
    y_ssd = _ssd(r3(z), r3(xbc), r3(dt), p["conv_w"], p["conv_b"], p["dt_bias"], p["a_log"], p["d_skip"],
                 p["ssd_norm"])
    y_na = _natten(r3(qkv), na_bias, p["na_q_norm"], p["na_k_norm"])
    kmem, vmem = _memkv(mem, p["norm_mem"], p["w_xkv"], p["xk_norm"])
    x2, hf, meta, counts = _mixer(x, y_ssd, y_na, r3(u), kmem, vmem, p)
    return _moe(x2, hf, meta, counts, p["w_e_gate"], p["w_e_up"], p["w_e_down"])


_LAYER_KEYS = ("norm_mix", "w_in", "conv_w", "conv_b", "dt_bias", "a_log", "d_skip", "ssd_norm", "na_q_norm",
               "na_k_norm", "na_rpb", "pool_w", "pool_scale", "w_out", "norm_xa", "norm_mem", "w_xq", "w_xkv",
               "xq_norm", "xk_norm", "w_xo", "norm_ffn", "w_router_group", "b_router_group", "w_router_expert",
               "b_router_expert", "w_e_gate", "w_e_up", "w_e_down")


def kernel(x_prompt, x_sample, mem_prompt, mem_sample, norm_mix, w_in, conv_w, conv_b, dt_bias, a_log, d_skip, ssd_norm, na_q_norm, na_k_norm, na_rpb, pool_w, pool_scale, w_out, norm_xa, norm_mem, w_xq, w_xkv, xq_norm, xk_norm, w_xo, norm_ffn, w_router_group, b_router_group, w_router_expert, b_router_expert, w_e_gate, w_e_up, w_e_down):
    stacked = dict(zip(_LAYER_KEYS, (norm_mix, w_in, conv_w, conv_b, dt_bias, a_log, d_skip, ssd_norm, na_q_norm,
                                     na_k_norm, na_rpb, pool_w, pool_scale, w_out, norm_xa, norm_mem, w_xq, w_xkv,
                                     xq_norm, xk_norm, w_xo, norm_ffn, w_router_group, b_router_group,
                                     w_router_expert, b_router_expert, w_e_gate, w_e_up, w_e_down)))
    depth = w_in.shape[0]
    layers = [_prep_layer({k: v[l] for k, v in stacked.items()}) for l in range(depth)]

    bias_cache = {}

    def trunk(x, mem):
        t = x.shape[1]
        for l, lp in enumerate(layers):
            if (l, t) not in bias_cache:
                bias_cache[(l, t)] = _na_bias(lp["na_rpb"], t)
            x = _layer(x, mem, lp, bias_cache[(l, t)])
        return x

    return trunk(x_prompt, mem_prompt), trunk(x_sample, mem_sample)
```

```python
import functools

import jax
import jax.numpy as jnp
import numpy as np
from jax import lax
from jax.experimental import pallas as pl
from jax.experimental.pallas import tpu as pltpu

F32 = jnp.float32
BF16 = jnp.bfloat16
U32 = jnp.uint32
HIGHEST = lax.Precision.HIGHEST

D_MODEL = 1024
GRID_W = 64
EPS = 1e-6
SSD_HEAD_DIM = 64
D_INNER = D_MODEL // 2
SSD_HEADS = D_INNER // SSD_HEAD_DIM
SSD_GROUPS = 2
SSD_STATE = 64
SSD_CHUNK = 128
CONV_W = 4
CONV_CH = D_INNER + 2 * SSD_GROUPS * SSD_STATE
NA_HEADS = 4
NA_HEAD_DIM = D_MODEL // 16
NA_WIDTH = NA_HEADS * NA_HEAD_DIM
NA_MAX_KH = 8
NA_KW = 16
POOL_WINDOWS = (2, 4, 8, 16)
POOL_GROUPS = 4
POOL_WIDTH = D_MODEL - D_INNER - NA_WIDTH
POOL_GROUP_DIM = POOL_WIDTH // POOL_GROUPS
XA_HEADS = 4
XA_HEAD_DIM = D_MODEL // 8
XA_WIDTH = XA_HEADS * XA_HEAD_DIM
N_EXPERT_GROUPS = 4
EXPERTS_PER_GROUP = 8
N_EXPERTS = N_EXPERT_GROUPS * EXPERTS_PER_GROUP
D_EXPERT = D_MODEL // 4
PAIRS_PER_GROUP = EXPERTS_PER_GROUP * (EXPERTS_PER_GROUP - 1) // 2
N_COMBOS = N_EXPERT_GROUPS * PAIRS_PER_GROUP

LANES = 128
BF16_SUBLANES = 16
VMEM_LIMIT_BYTES = 56 * 1024 * 1024

TOKEN_TILE = 512
NA_QUERY_ROWS = 8
NA_SUB_ROWS = 8
NA_SUBS = NA_QUERY_ROWS // NA_SUB_ROWS
NA_KEY_ROWS = NA_SUB_ROWS + NA_MAX_KH
EXPERT_BLOCK_MIN = 128
HALF = D_MODEL // 2
ROW_WORDS = HALF + LANES
SSD_STEP_CHUNKS = 4
DT_PAD = LANES
DMA_UNROLL = 8
NEG_BIG = -1e30


def _cparams(sem):
    return pltpu.CompilerParams(dimension_semantics=sem, vmem_limit_bytes=VMEM_LIMIT_BYTES)


def _sigmoid(x):
    return 1.0 / (1.0 + jnp.exp(-x))


def _silu(x):
    return x * _sigmoid(x)


def _softplus(x):
    return jnp.maximum(x, 0.0) + jnp.log(1.0 + jnp.exp(-jnp.abs(x)))


def _pack_bf16_pairs(v):
    k = v.shape[1] // 2
    bits = pltpu.bitcast(v, U32)
    return (bits[:, :k] >> 16) | (bits[:, k:] & jnp.uint32(0xFFFF0000))


def _unpack_bf16_pairs(w):
    lo = pltpu.bitcast(w << 16, F32)
    hi = pltpu.bitcast(w & jnp.uint32(0xFFFF0000), F32)
    return jnp.concatenate([lo, hi], axis=1)


def _full(shape):
    n = len(shape)
    return pl.BlockSpec(shape, lambda *_: (0,) * n)


def _inproj_kernel(x_ref, g_ref, w_ref, z_ref, xbc_ref, qkv_ref, u_ref, dt_ref):
    x = x_ref[...]
    ms = jnp.mean(x * x, axis=-1, keepdims=True)
    h = (x * lax.rsqrt(ms + EPS) * g_ref[...]).astype(BF16)
    o = 0
    for ref in (z_ref, xbc_ref, qkv_ref, u_ref, dt_ref):
        w = ref.shape[-1]
        ref[...] = jnp.dot(h, w_ref[:, o:o + w], preferred_element_type=F32).astype(ref.dtype)
        o += w


_INPROJ_WIDTHS = (D_INNER, CONV_CH, 3 * NA_WIDTH, POOL_WIDTH, DT_PAD)
_INPROJ_DTYPES = (BF16, BF16, BF16, BF16, F32)


def _inproj(x2d, gain, w_cat):
    n = x2d.shape[0]
    tm = TOKEN_TILE
    widths, dtypes = _INPROJ_WIDTHS, _INPROJ_DTYPES
    return pl.pallas_call(
        _inproj_kernel,
        grid=(n // tm,),
        in_specs=[pl.BlockSpec((tm, D_MODEL), lambda i: (i, 0)),
                  _full((1, D_MODEL)),
                  _full(w_cat.shape)],
        out_specs=[pl.BlockSpec((tm, w), lambda i: (i, 0)) for w in widths],
        out_shape=[jax.ShapeDtypeStruct((n, w), d) for w, d in zip(widths, dtypes)],
        compiler_params=_cparams(("arbitrary",)),
    )(x2d, gain.reshape(1, D_MODEL), w_cat)


def _split3(a):
    hi = a.astype(BF16)
    r = a - hi.astype(F32)
    mid = r.astype(BF16)
    lo = (r - mid.astype(F32)).astype(BF16)
    return hi, mid, lo


def _ssd_kernel(xc_ref, xp_ref, xn_ref, dt_ref, z_ref, cw_ref, cb_ref, dtb_ref, alog_ref, dsk_ref, nrm_ref, emat_ref,
                y_ref, state_ref, yf_ref, xs_c, bc_c, cbm_c, dt_c, *, nblocks):
    L = SSD_CHUNK
    LB = SSD_STEP_CHUNKS * L
    P = SSD_HEAD_DIM
    NS = SSD_STATE
    HG = SSD_HEADS // SSD_GROUPS
    gn = SSD_GROUPS * NS
    j = pl.program_id(1)
    c = jnp.where(j < nblocks, j, 2 * nblocks - 1 - j)
    row0 = pl.multiple_of(c * LB, LB)
    rows = pl.ds(row0, LB)
    lane1 = lax.broadcasted_iota(jnp.int32, (1, LANES), 1)
    lo_half = lane1 < P
    ti = lax.broadcasted_iota(jnp.int32, (L, L), 0)
    si = lax.broadcasted_iota(jnp.int32, (L, L), 1)

    def masked_c(bc):
        return [jnp.where(lane1 // NS == g, bc[:, gn:], 0.0).astype(BF16) for g in range(SSD_GROUPS)]

    def prepare():
        cur = xc_ref[0].astype(F32)
        prev = xp_ref[0].astype(F32)
        nxt = xn_ref[0].astype(F32)
        has_prev = (c > 0).astype(F32)
        has_next = (c < nblocks - 1).astype(F32)
        p_last = prev[BF16_SUBLANES - 1:BF16_SUBLANES, :] * has_prev
        n0 = nxt[0:1, :] * has_next
        n1 = nxt[1:2, :] * has_next
        row = lax.broadcasted_iota(jnp.int32, (LB, 1), 0)
        um1 = jnp.where(row == 0, p_last, pltpu.roll(cur, 1, 0))
        up1 = jnp.where(row == LB - 1, n0, pltpu.roll(cur, LB - 1, 0))
        up2 = jnp.where(row == LB - 2, n0, jnp.where(row == LB - 1, n1, pltpu.roll(cur, LB - 2, 0)))
        cw = cw_ref[...]
        acc = cb_ref[...] + um1 * cw[0:1, :] + cur * cw[1:2, :] + up1 * cw[2:3, :] + up2 * cw[3:4, :]
        xbc = _silu(acc)
        xs = xbc[:, :D_INNER]
        bc = xbc[:, D_INNER:D_INNER + 2 * gn].astype(BF16)
        dt = _softplus(dt_ref[0] + dtb_ref[...])
        xs_c[rows, :] = xs
        bc_c[rows, :] = bc
        dt_c[rows, :] = dt
        ops = []
        for sub in range(SSD_STEP_CHUNKS):
            sl = slice(sub * L, (sub + 1) * L)
            cg = masked_c(bc[sl])
            cb_mat = [lax.dot_general(cg[g], bc[sl, :gn], (((1,), (1,)), ((), ())), preferred_element_type=F32)
                      for g in range(SSD_GROUPS)]
            cbm_c[pl.ds(row0 + sub * L, L), :] = jnp.concatenate(cb_mat, axis=1)
            ops.append((xs[sl], bc[sl, :gn], cg, cb_mat, dt[sl]))
        return ops

    def recall():
        ops = []
        for sub in range(SSD_STEP_CHUNKS):
            r = pl.ds(row0 + sub * L, L)
            bc = bc_c[r, :]
            cbm = cbm_c[r, :]
            ops.append((xs_c[r, :], bc[:, :gn], masked_c(bc), [cbm[:, g * L:(g + 1) * L] for g in range(SSD_GROUPS)],
                        dt_c[r, :]))
        return ops

    def scan_chunk(direction, xs, bfull, cg, cb_mat, dt):
        if direction == 0:
            mask = ti >= si
            edge = L - 1
        else:
            mask = si >= ti
            edge = 0
        la = dt * (-jnp.exp(alog_ref[...]))
        tri = mask.astype(BF16)
        csum = sum(jnp.dot(tri, part, preferred_element_type=F32) for part in _split3(la))
        csum_t = csum.T
        emat = emat_ref[direction]
        colb = sum(jnp.dot(part, emat, preferred_element_type=F32) for part in _split3(csum))
        tot = csum[edge:edge + 1, :]
        e_tot = jnp.exp(tot)
        e_in = jnp.exp(csum)
        e_out = jnp.exp(tot - csum)
        ys = []
        for g in range(SSD_GROUPS):
            s_old = state_ref[g]
            y_off = lax.dot_general(cg[g], s_old.astype(BF16), (((1,), (1,)), ((), ())),
                                    preferred_element_type=F32)
            xw = []
            for pr in range(HG // 2):
                h0 = g * HG + 2 * pr
                l0 = direction * SSD_HEADS + h0

                def col(a, l0=l0):
                    return jnp.where(lo_half, a[:, l0:l0 + 1], a[:, l0 + 1:l0 + 2])

                xdt = xs[:, h0 * P:(h0 + 2) * P] * col(dt)
                y_pair = y_off[:, 2 * pr * P:(2 * pr + 2) * P] * col(e_in)
                for hh, half in ((h0, lo_half), (h0 + 1, jnp.logical_not(lo_half))):
                    ll = direction * SSD_HEADS + hh
                    seg = colb[:, hh * L:(hh + 1) * L] - csum_t[ll:ll + 1, :]
                    dec = jnp.exp(jnp.where(mask, seg, NEG_BIG))
                    m = (cb_mat[g] * dec).astype(BF16)
                    y_pair += jnp.dot(m, jnp.where(half, xdt, 0.0).astype(BF16), preferred_element_type=F32)
                ys.append(y_pair)
                xw.append(xdt * col(e_out))
            xw = jnp.concatenate(xw, axis=1).astype(BF16)
            s_new = lax.dot_general(xw, bfull, (((0,), (0,)), ((), ())), preferred_element_type=F32)
            s_scaled = []
            for hl in range(HG):
                lane = direction * SSD_HEADS + g * HG + hl
                s_scaled.append(s_old[hl * P:(hl + 1) * P, :] * e_tot[:, lane:lane + 1])
            state_ref[g] = jnp.concatenate(s_scaled, axis=0) + s_new
        return jnp.concatenate(ys, axis=1)

    @pl.when(jnp.logical_or(j == 0, j == nblocks))
    def _():
        state_ref[...] = jnp.zeros_like(state_ref)

    @pl.when(j < nblocks)
    def _():
        ops = prepare()
        for sub in range(SSD_STEP_CHUNKS):
            yf_ref[pl.ds(row0 + sub * L, L), :] = scan_chunk(0, *ops[sub])

    @pl.when(j >= nblocks)
    def _():
        ops = recall()
        for sub in reversed(range(SSD_STEP_CHUNKS)):
            sl = slice(sub * L, (sub + 1) * L)
            y = yf_ref[pl.ds(row0 + sub * L, L), :] + scan_chunk(1, *ops[sub]) + dsk_ref[...] * ops[sub][0]
            y = y * _silu(z_ref[0, sl, :].astype(F32))
            gw = D_INNER // SSD_GROUPS
            outs = []
            for g in range(SSD_GROUPS):
                yg = y[:, g * gw:(g + 1) * gw]
                outs.append(yg * lax.rsqrt(jnp.mean(yg * yg, axis=-1, keepdims=True) + EPS))
            y_ref[0, sl, :] = (jnp.concatenate(outs, axis=1) * nrm_ref[...]).astype(y_ref.dtype)


def _ssd(z, xbc, dt, conv_w, conv_b, dt_bias, a_log, d_skip, ssd_norm):
    b, t, _ = z.shape
    L = SSD_CHUNK
    lb = SSD_STEP_CHUNKS * L
    nb = t // lb
    hb = lb // BF16_SUBLANES
    nhalo = t // BF16_SUBLANES

    def blk(j):
        return jnp.where(j < nb, j, 2 * nb - 1 - j)

    pad = DT_PAD - 2 * SSD_HEADS
    dtb = jnp.pad(dt_bias.reshape(1, -1).astype(F32), ((0, 0), (0, pad)))
    alog = jnp.pad(a_log.reshape(1, -1).astype(F32), ((0, 0), (0, pad)))
    dsk = jnp.repeat(d_skip.astype(F32), SSD_HEAD_DIM).reshape(1, D_INNER)
    sel = np.arange(DT_PAD)[None, :, None] == (np.arange(2)[:, None, None] * SSD_HEADS
                                                + np.arange(SSD_HEADS)[None, None, :])
    emat = jnp.asarray(np.repeat(sel, L, axis=2), BF16)
    return pl.pallas_call(
        functools.partial(_ssd_kernel, nblocks=nb),
        grid=(b, 2 * nb),
        in_specs=[
            pl.BlockSpec((1, lb, CONV_CH), lambda i, j: (i, blk(j), 0)),
            pl.BlockSpec((1, BF16_SUBLANES, CONV_CH), lambda i, j: (i, jnp.maximum(blk(j) * hb - 1, 0), 0)),
            pl.BlockSpec((1, BF16_SUBLANES, CONV_CH),
                         lambda i, j: (i, jnp.minimum((blk(j) + 1) * hb, nhalo - 1), 0)),
            pl.BlockSpec((1, lb, DT_PAD), lambda i, j: (i, blk(j), 0)),
            pl.BlockSpec((1, lb, D_INNER), lambda i, j: (i, blk(j), 0)),
            _full((CONV_W, CONV_CH)), _full((1, CONV_CH)), _full((1, DT_PAD)), _full((1, DT_PAD)),
            _full((1, D_INNER)), _full((1, D_INNER)), _full((2, DT_PAD, SSD_HEADS * L)),
        ],
        out_specs=pl.BlockSpec((1, lb, D_INNER), lambda i, j: (i, jnp.where(j < nb, nb - 1, 2 * nb - 1 - j), 0)),
        out_shape=jax.ShapeDtypeStruct((b, t, D_INNER), BF16),
        scratch_shapes=[pltpu.VMEM((SSD_GROUPS, (SSD_HEADS // SSD_GROUPS) * SSD_HEAD_DIM, LANES), F32),
                        pltpu.VMEM((t, D_INNER), F32),
                        pltpu.VMEM((t, D_INNER), F32),
                        pltpu.VMEM((t, 2 * SSD_GROUPS * SSD_STATE), BF16),
                        pltpu.VMEM((t, SSD_GROUPS * L), F32),
                        pltpu.VMEM((t, DT_PAD), F32)],
        compiler_params=_cparams(("arbitrary", "arbitrary")),
    )(xbc, xbc, xbc, dt, z, conv_w.astype(F32), conv_b.reshape(1, CONV_CH).astype(F32), dtb, alog, dsk,
      ssd_norm.reshape(1, D_INNER).astype(F32), emat)


def _na_bias(rpb, t):
    r = t // GRID_W
    kh = min(NA_MAX_KH, r)
    nsb = r // NA_SUB_ROWS
    rows = np.arange(r)
    row_start = np.clip(rows - NA_MAX_KH // 2, 0, r - kh)
    r0 = np.arange(nsb) * NA_SUB_ROWS
    kr0 = np.clip(r0 - NA_MAX_KH // 2, 0, r - NA_KEY_ROWS)
    qrow = r0[:, None] + np.arange(NA_SUB_ROWS)[None, :]
    krow = kr0[:, None] + np.arange(NA_KEY_ROWS)[None, :]
    rs = row_start[qrow]
    row_ok = (krow[:, None, :] >= rs[:, :, None]) & (krow[:, None, :] < rs[:, :, None] + kh)
    dr = np.clip(krow[:, None, :] - qrow[:, :, None] + (NA_MAX_KH - 1), 0, 2 * NA_MAX_KH - 2)
    cols = np.arange(GRID_W)
    col_start = np.clip(cols - NA_KW // 2, 0, GRID_W - NA_KW)
    col_ok = (cols[None, :] >= col_start[:, None]) & (cols[None, :] < col_start[:, None] + NA_KW)
    dc = np.clip(cols[None, :] - cols[:, None] + (NA_KW - 1), 0, 2 * NA_KW - 2)
    sel_c = (dc[..., None] == np.arange(2 * NA_KW - 1)) & col_ok[..., None]
    nr = 2 * NA_MAX_KH - 1
    tile_r = jnp.einsum("hrc,xyc->hrxy", rpb.astype(F32), jnp.asarray(sel_c, F32), precision=HIGHEST)
    tile_r = jnp.where(jnp.asarray(col_ok)[None, None], tile_r, NEG_BIG)
    tile_r = jnp.concatenate([tile_r, jnp.full((NA_HEADS, 1, GRID_W, GRID_W), NEG_BIG, F32)], axis=1)
    kp = NA_KEY_ROWS // 2
    code = np.where(row_ok, dr, nr).reshape(nsb * NA_SUB_ROWS * kp, 2)
    pairs, inv = np.unique(code, axis=0, return_inverse=True)
    blocks = jnp.concatenate([tile_r[:, pairs[:, 0]], tile_r[:, pairs[:, 1]]], axis=-1)
    blocks = jnp.moveaxis(blocks, 0, 1).reshape(len(pairs), NA_HEADS * GRID_W * 2 * GRID_W)
    onehot = jnp.asarray(inv.reshape(-1, 1) == np.arange(len(pairs))[None, :], F32)
    bias = jnp.dot(onehot, blocks, precision=HIGHEST)
    return bias.reshape(nsb // NA_SUBS, NA_SUBS, NA_SUB_ROWS, kp, NA_HEADS, GRID_W, 2 * GRID_W).astype(BF16)


def _natten_kernel(qkv_ref, bias_ref, qg_ref, kg_ref, seg_ref, o_ref, *, grid_rows):
    nq = NA_SUB_ROWS * GRID_W
    nk = NA_KEY_ROWS * GRID_W
    rb = pl.program_id(0)
    seg = seg_ref[...]
    lane_h = lax.broadcasted_iota(jnp.int32, (1, NA_WIDTH), 1) // NA_HEAD_DIM
    for sub in range(NA_SUBS):
        r0 = rb * NA_QUERY_ROWS + sub * NA_SUB_ROWS
        kr0 = jnp.clip(r0 - NA_MAX_KH // 2, 0, grid_rows - NA_KEY_ROWS)
        q0 = pl.multiple_of(r0 * GRID_W, nq)
        k0 = pl.multiple_of(kr0 * GRID_W, NA_MAX_KH // 2 * GRID_W)
        q = qkv_ref[0, pl.ds(q0, nq), 0:NA_WIDTH].astype(F32)
        k = qkv_ref[0, pl.ds(k0, nk), NA_WIDTH:2 * NA_WIDTH].astype(F32)
        v = qkv_ref[0, pl.ds(k0, nk), 2 * NA_WIDTH:3 * NA_WIDTH]
        qms = jnp.dot(q * q, seg, precision=HIGHEST, preferred_element_type=F32)
        kms = jnp.dot(k * k, seg, precision=HIGHEST, preferred_element_type=F32)
        qn = q * lax.rsqrt(qms + EPS) * (qg_ref[...] * NA_HEAD_DIM ** -0.5)
        kn = (k * lax.rsqrt(kms + EPS) * kg_ref[...]).astype(BF16)
        acc = jnp.zeros((nq, NA_WIDTH), F32)
        for h in range(NA_HEADS):
            hm = lane_h == h
            s = lax.dot_general(jnp.where(hm, qn, 0.0).astype(BF16), kn, (((1,), (1,)), ((), ())),
                                preferred_element_type=F32)
            bias = jnp.concatenate(
                [jnp.concatenate([bias_ref[0, sub, qr, kc, h] for kc in range(NA_KEY_ROWS // 2)], axis=1)
                 for qr in range(NA_SUB_ROWS)], axis=0)
            s = s + bias.astype(F32)
            p = jnp.exp(s - jnp.max(s, axis=-1, keepdims=True))
            l = jnp.sum(p, axis=-1, keepdims=True)
            o = jnp.dot(p.astype(BF16), v, preferred_element_type=F32)
            acc += jnp.where(hm, o / l, 0.0)
        o_ref[0, sub * nq:(sub + 1) * nq, :] = acc.astype(o_ref.dtype)


def _natten(qkv, bias, q_norm, k_norm):
    b, t, _ = qkv.shape
    r = t // GRID_W
    nrb = r // NA_QUERY_ROWS
    nq = NA_QUERY_ROWS * GRID_W
    head = jnp.arange(NA_WIDTH) // NA_HEAD_DIM
    seg = (head[:, None] == head[None, :]).astype(F32) / NA_HEAD_DIM
    return pl.pallas_call(
        functools.partial(_natten_kernel, grid_rows=r),
        grid=(nrb, b),
        in_specs=[pl.BlockSpec((1, t, 3 * NA_WIDTH), lambda i, j: (j, 0, 0)),
                  pl.BlockSpec((1,) + bias.shape[1:], lambda i, j: (i,) + (0,) * (bias.ndim - 1)),
                  _full((1, NA_WIDTH)), _full((1, NA_WIDTH)), _full((NA_WIDTH, NA_WIDTH))],
        out_specs=pl.BlockSpec((1, nq, NA_WIDTH), lambda i, j: (j, i, 0)),
        out_shape=jax.ShapeDtypeStruct((b, t, NA_WIDTH), BF16),
        compiler_params=_cparams(("arbitrary", "arbitrary")),
    )(qkv, bias, jnp.tile(q_norm.astype(F32), NA_HEADS).reshape(1, NA_WIDTH),
      jnp.tile(k_norm.astype(F32), NA_HEADS).reshape(1, NA_WIDTH), seg)


def _memkv_kernel(m_ref, g_ref, w_ref, kg_ref, k_ref, v_ref):
    x = m_ref[0]
    ms = jnp.mean(x * x, axis=-1, keepdims=True)
    h = (x * lax.rsqrt(ms + EPS) * g_ref[...]).astype(BF16)
    kv = jnp.dot(h, w_ref[...], preferred_element_type=F32)
    ks = []
    for hd in range(XA_HEADS):
        kh = kv[:, hd * XA_HEAD_DIM:(hd + 1) * XA_HEAD_DIM]
        ks.append(kh * lax.rsqrt(jnp.mean(kh * kh, axis=-1, keepdims=True) + EPS))
    k_ref[0] = (jnp.concatenate(ks, axis=1) * kg_ref[...]).astype(k_ref.dtype)
    v_ref[0] = kv[:, XA_WIDTH:].astype(v_ref.dtype)


def _memkv(mem, norm_mem, w_xkv, xk_norm):
    b, m, _ = mem.shape
    return pl.pallas_call(
        _memkv_kernel,
        grid=(b,),
        in_specs=[pl.BlockSpec((1, m, D_MODEL), lambda i: (i, 0, 0)),
                  _full((1, D_MODEL)), _full((D_MODEL, 2 * XA_WIDTH)), _full((1, XA_WIDTH))],
        out_specs=[pl.BlockSpec((1, m, XA_WIDTH), lambda i: (i, 0, 0))] * 2,
        out_shape=[jax.ShapeDtypeStruct((b, m, XA_WIDTH), BF16)] * 2,
        compiler_params=_cparams(("arbitrary",)),
    )(mem, norm_mem.reshape(1, D_MODEL).astype(F32), w_xkv,
      jnp.tile(xk_norm.astype(F32), XA_HEADS).reshape(1, XA_WIDTH))


def _mixer_kernel(x_ref, ys_ref, yn_ref, u_ref, up_ref, un_ref, k_ref, v_ref,
                  pw_ref, psc_ref, wo_ref, gxa_ref, wq_ref, qg_ref, wxo_ref, gff_ref,
                  wrb_ref, br_ref, ltri_ref,
                  x2_ref, hf_ref, meta_ref, cnt_ref, carry_ref, *, seq):
    tm = x_ref.shape[1]
    halo = BF16_SUBLANES
    bi = pl.program_id(0)
    i = pl.program_id(1)
    nt = pl.num_programs(1)

    @pl.when(jnp.logical_and(bi == 0, i == 0))
    def _():
        carry_ref[...] = jnp.zeros_like(carry_ref)

    u = u_ref[0].astype(F32)
    up = up_ref[0].astype(F32) * (i > 0).astype(F32)
    un = un_ref[0].astype(F32) * (i < nt - 1).astype(F32)
    cat = jnp.concatenate([up, u, un], axis=0)
    n = tm + 2 * halo

    def sh(a, k):
        return pltpu.roll(a, (-k) % n, 0)

    a2 = cat + sh(cat, -1)
    a4 = sh(a2, 1) + sh(a2, -1)
    a8 = sh(a4, 2) + sh(a4, -2)
    a16 = sh(a8, 4) + sh(a8, -4)
    lane_g = lax.broadcasted_iota(jnp.int32, (1, POOL_WIDTH), 1) // POOL_GROUP_DIM
    wsum = jnp.where(lane_g == 0, a2, jnp.where(lane_g == 1, a4, jnp.where(lane_g == 2, a8, a16)))
    wsum = wsum[halo:halo + tm, :]
    half = jnp.where(lane_g == 0, POOL_WINDOWS[0] // 2,
                     jnp.where(lane_g == 1, POOL_WINDOWS[1] // 2,
                               jnp.where(lane_g == 2, POOL_WINDOWS[2] // 2, POOL_WINDOWS[3] // 2)))
    tpos = i * tm + lax.broadcasted_iota(jnp.int32, (tm, 1), 0)
    cnt = (jnp.minimum(tpos + half, seq) - jnp.maximum(tpos - half, 0)).astype(F32)
    d = wsum / cnt - u
    ypool = jnp.dot(d.astype(BF16), pw_ref[...], preferred_element_type=F32) * psc_ref[...]

    mix = jnp.dot(ys_ref[0], wo_ref[0:D_INNER, :], preferred_element_type=F32)
    mix += jnp.dot(yn_ref[0], wo_ref[D_INNER:D_INNER + NA_WIDTH, :], preferred_element_type=F32)
    mix += jnp.dot(ypool.astype(BF16), wo_ref[D_INNER + NA_WIDTH:, :], preferred_element_type=F32)
    x1 = x_ref[0] + mix

    hn = (x1 * lax.rsqrt(jnp.mean(x1 * x1, axis=-1, keepdims=True) + EPS) * gxa_ref[...]).astype(BF16)
    q = jnp.dot(hn, wq_ref[...], preferred_element_type=F32)
    kk = k_ref[0]
    vv = v_ref[0]
    scale = XA_HEAD_DIM ** -0.5
    outs = []
    for hd in range(XA_HEADS):
        sl = slice(hd * XA_HEAD_DIM, (hd + 1) * XA_HEAD_DIM)
        qh = q[:, sl]
        qh = (qh * lax.rsqrt(jnp.mean(qh * qh, axis=-1, keepdims=True) + EPS) * qg_ref[:, sl]).astype(BF16)
        s = lax.dot_general(qh, kk[:, sl], (((1,), (1,)), ((), ())), preferred_element_type=F32) * scale
        p = jnp.exp(s - jnp.max(s, axis=-1, keepdims=True))
        l = jnp.sum(p, axis=-1, keepdims=True)
        outs.append(jnp.dot(p.astype(BF16), vv[:, sl], preferred_element_type=F32) / l)
    att = jnp.concatenate(outs, axis=1).astype(BF16)
    x2 = x1 + jnp.dot(att, wxo_ref[...], preferred_element_type=F32)
    x2_ref[0] = x2

    hf = x2 * lax.rsqrt(jnp.mean(x2 * x2, axis=-1, keepdims=True) + EPS) * gff_ref[...]
    h_hi = hf.astype(BF16)
    hf_ref[0, :, :HALF] = _pack_bf16_pairs(h_hi.astype(F32))
    h_lo = (hf - h_hi.astype(F32)).astype(BF16)
    both = jnp.dot(h_hi, wrb_ref[...], preferred_element_type=F32)
    logits = (both[:, :LANES] + both[:, LANES:]
              + jnp.dot(h_lo, wrb_ref[:, :LANES], preferred_element_type=F32)) + br_ref[...]
    lane = lax.broadcasted_iota(jnp.int32, (1, LANES), 1)
    lane_f = lane.astype(F32)
    lane_grp = (lane // EXPERTS_PER_GROUP).astype(F32)
    is_g = jnp.logical_and(lane >= N_EXPERTS, lane < N_EXPERTS + N_EXPERT_GROUPS)
    gl = jnp.where(is_g, logits, NEG_BIG)
    gmax = jnp.max(gl, axis=-1, keepdims=True)
    g_sel = jnp.min(jnp.where(gl == gmax, lane_f, float(LANES)), axis=-1, keepdims=True) - N_EXPERTS
    g_gate = 1.0 / jnp.sum(jnp.where(is_g, jnp.exp(gl - gmax), 0.0), axis=-1, keepdims=True)
    in_grp = jnp.logical_and(lane < N_EXPERTS, lane_grp == g_sel)
    el = jnp.where(in_grp, logits, NEG_BIG)
    v1 = jnp.max(el, axis=-1, keepdims=True)
    e0 = jnp.min(jnp.where(el == v1, lane_f, float(LANES)), axis=-1, keepdims=True)
    el2 = jnp.where(lane_f == e0, NEG_BIG, el)
    v2 = jnp.max(el2, axis=-1, keepdims=True)
    e1 = jnp.min(jnp.where(el2 == v2, lane_f, float(LANES)), axis=-1, keepdims=True)
    w1 = jnp.exp(v2 - v1)
    gate0 = g_gate / (1.0 + w1)
    gate1 = g_gate * w1 / (1.0 + w1)

    base = g_sel * EXPERTS_PER_GROUP
    ea = jnp.minimum(e0, e1) - base
    eb = jnp.maximum(e0, e1) - base
    combo = g_sel * PAIRS_PER_GROUP + ea * EXPERTS_PER_GROUP - ea * (ea + 1.0) * 0.5 + (eb - ea - 1.0)
    gate_a = jnp.where(e0 < e1, gate0, gate1)
    gate_b = jnp.where(e0 < e1, gate1, gate0)
    hf_ref[0, :, HALF:] = pltpu.bitcast(jnp.where(lane == 0, gate_a, 0.0) + jnp.where(lane == 1, gate_b, 0.0), U32)

    oh = lane_f == combo
    cnt_tok = oh.astype(F32)
    before = jnp.dot(ltri_ref[...], cnt_tok.astype(BF16), preferred_element_type=F32) + carry_ref[0:1, :]
    rank = jnp.sum(jnp.where(oh, before, 0.0), axis=-1, keepdims=True)
    new_carry = carry_ref[0:1, :] + jnp.sum(cnt_tok, axis=0, keepdims=True)
    carry_ref[...] = jnp.broadcast_to(new_carry, carry_ref.shape)
    cnt_ref[...] = jnp.broadcast_to(new_carry, cnt_ref.shape)

    slab = jnp.where(lane == 0, combo, 0.0) + jnp.where(lane == 1, rank, 0.0)
    meta_ref[0] = slab.T[0:8, :]


def _mixer(x, y_ssd, y_na, u, kmem, vmem, p):
    b, t, _ = x.shape
    tm = TOKEN_TILE
    nt = t // tm
    hb = tm // BF16_SUBLANES
    nhalo = t // BF16_SUBLANES
    ltri = (jnp.arange(tm)[:, None] > jnp.arange(tm)[None, :]).astype(BF16)
    tok = lambda w: pl.BlockSpec((1, tm, w), lambda i, j: (i, j, 0))
    mem = pl.BlockSpec((1, kmem.shape[1], XA_WIDTH), lambda i, j: (i, 0, 0))
    weights = (p["pool_bd"], p["pool_scale"], p["w_out"], p["norm_xa"], p["w_xq"], p["xq_norm"], p["w_xo"],
               p["norm_ffn"], p["w_r_both"], p["b_r"], ltri)
    return pl.pallas_call(
        functools.partial(_mixer_kernel, seq=t),
        grid=(b, nt),
        in_specs=[tok(D_MODEL), tok(D_INNER), tok(NA_WIDTH), tok(POOL_WIDTH),
                  pl.BlockSpec((1, BF16_SUBLANES, POOL_WIDTH), lambda i, j: (i, jnp.maximum(j * hb - 1, 0), 0)),
                  pl.BlockSpec((1, BF16_SUBLANES, POOL_WIDTH),
                               lambda i, j: (i, jnp.minimum((j + 1) * hb, nhalo - 1), 0)),
                  mem, mem] + [_full(w.shape) for w in weights],
        out_specs=[tok(D_MODEL), tok(ROW_WORDS),
                   pl.BlockSpec((1, 8, tm), lambda i, j: (i * nt + j, 0, 0)),
                   pl.BlockSpec((8, LANES), lambda i, j: (0, 0))],
        out_shape=[jax.ShapeDtypeStruct((b, t, D_MODEL), F32),
                   jax.ShapeDtypeStruct((b, t, ROW_WORDS), U32),
                   jax.ShapeDtypeStruct((b * nt, 8, tm), F32),
                   jax.ShapeDtypeStruct((8, LANES), F32)],
        scratch_shapes=[pltpu.VMEM((8, LANES), F32)],
        compiler_params=_cparams(("arbitrary", "arbitrary")),
    )(x, y_ssd, y_na, u, u, u, kmem, vmem, *weights)


def _dispatch_kernel(pstart_ref, cnt_ref, cmb_ref, rk_ref, hf_ref, zero_ref, xs_ref, stage, sem, *, bm):
    tm = hf_ref.shape[0]
    i = pl.program_id(0)
    n = pl.num_programs(0)
    slot = i % 2

    def row_copy(src_ref, src_row, dst_row, s):
        return pltpu.make_async_copy(src_ref.at[pl.ds(src_row, 1)], xs_ref.at[pl.ds(dst_row, 1)], s)

    def wait_tile(sl):
        pltpu.make_async_copy(stage.at[sl], xs_ref.at[pl.ds(0, tm)], sem.at[sl]).wait()

    @pl.when(i == 0)
    def _():
        def fill(cm, wait):
            n_c = cnt_ref[cm]
            npad = (bm - n_c % bm) % bm
            head = (8 - n_c % 8) % 8
            base = pstart_ref[cm] + n_c

            def go(cp):
                if wait:
                    cp.wait()
                else:
                    cp.start()

            def one(r, _):
                go(row_copy(zero_ref, 0, base + r, sem.at[0]))
                return 0

            lax.fori_loop(0, head, one, 0)
            rest = npad - head
            off = base + head
            size = bm // 2
            while size >= 8:
                @pl.when(rest & size != 0)
                def _(off=off, size=size):
                    go(pltpu.make_async_copy(zero_ref.at[pl.ds(0, size)],
                                             xs_ref.at[pl.ds(pl.multiple_of(off, 8), size)], sem.at[0]))

                off = off + (rest & size)
                size //= 2
            return 0

        lax.fori_loop(0, N_COMBOS, lambda cm, _: fill(cm, False), 0)
        lax.fori_loop(0, N_COMBOS, lambda cm, _: fill(cm, True), 0)

    stage[slot] = hf_ref[...]

    def start(pair, _):
        for u in range(2):
            tk = 2 * pair + u
            row_copy(stage.at[slot], tk, pstart_ref[cmb_ref[0, 0, tk]] + rk_ref[0, 0, tk],
                     sem.at[slot]).start(priority=u)
        return 0

    lax.fori_loop(0, tm // 2, start, 0, unroll=DMA_UNROLL // 2)

    @pl.when(i > 0)
    def _():
        wait_tile(1 - slot)

    @pl.when(i == n - 1)
    def _():
        wait_tile(slot)


def _dispatch(hf2d, combo, rank, pstart, counts, n_rows, bm):
    n = hf2d.shape[0]
    tm = TOKEN_TILE
    smem_blk = pl.BlockSpec((1, 1, tm), lambda i, *_: (i, 0, 0), memory_space=pltpu.SMEM)
    return pl.pallas_call(
        functools.partial(_dispatch_kernel, bm=bm),
        grid_spec=pltpu.PrefetchScalarGridSpec(
            num_scalar_prefetch=2,
            grid=(n // tm,),
            in_specs=[smem_blk, smem_blk,
                      pl.BlockSpec((tm, ROW_WORDS), lambda i, *_: (i, 0)),
                      pl.BlockSpec((bm // 2, ROW_WORDS), lambda i, *_: (0, 0))],
            out_specs=pl.BlockSpec(memory_space=pl.ANY),
            scratch_shapes=[pltpu.VMEM((2, tm, ROW_WORDS), U32), pltpu.SemaphoreType.DMA((2,))]),
        out_shape=jax.ShapeDtypeStruct((n_rows, ROW_WORDS), U32),
        compiler_params=_cparams(("arbitrary",)),
    )(pstart, counts, combo, rank, hf2d, jnp.zeros((bm // 2, ROW_WORDS), U32))


def _experts_kernel(ea_ref, eb_ref, nused_ref, x_ref, wga_ref, wua_ref, wda_ref, wgb_ref, wub_ref, wdb_ref, y_ref):
    @pl.when(pl.program_id(0) < nused_ref[0])
    def _():
        x = _unpack_bf16_pairs(x_ref[:, :HALF]).astype(BF16)
        gates = pltpu.bitcast(x_ref[:, HALF:], F32)

        def mlp(wg_ref, wu_ref, wd_ref):
            hg = jnp.dot(x, wg_ref[0], preferred_element_type=F32)
            hu = jnp.dot(x, wu_ref[0], preferred_element_type=F32)
            return jnp.dot((_silu(hg) * hu).astype(BF16), wd_ref[0], preferred_element_type=F32)

        y = mlp(wga_ref, wua_ref, wda_ref) * gates[:, 0:1] + mlp(wgb_ref, wub_ref, wdb_ref) * gates[:, 1:2]
        y_ref[...] = _pack_bf16_pairs(y.astype(BF16).astype(F32))


def _experts(xs, blk_ea, blk_eb, nused, w_gate, w_up, w_down, bm):
    nblk = blk_ea.shape[0]
    row = lambda j, ea, eb, nu: (jnp.minimum(j, nu[0] - 1), 0)
    row_out = lambda j, ea, eb, nu: (jnp.where(j < nu[0], j, nblk - 1), 0)
    sel_a = lambda j, ea, eb, nu: (ea[j], 0, 0)
    sel_b = lambda j, ea, eb, nu: (eb[j], 0, 0)
    up = lambda sel: pl.BlockSpec((1, D_MODEL, D_EXPERT), sel)
    down = lambda sel: pl.BlockSpec((1, D_EXPERT, D_MODEL), sel)
    return pl.pallas_call(
        _experts_kernel,
        grid_spec=pltpu.PrefetchScalarGridSpec(
            num_scalar_prefetch=3,
            grid=(nblk,),
            in_specs=[pl.BlockSpec((bm, ROW_WORDS), row),
                      up(sel_a), up(sel_a), down(sel_a), up(sel_b), up(sel_b), down(sel_b)],
            out_specs=pl.BlockSpec((bm, HALF), row_out)),
        out_shape=jax.ShapeDtypeStruct((nblk * bm, HALF), U32),
        compiler_params=_cparams(("arbitrary",)),
    )(blk_ea, blk_eb, nused, xs, w_gate, w_up, w_down, w_gate, w_up, w_down)


def _combine_kernel(pstart_ref, cmb_ref, rk_ref, cmb_nx_ref, rk_nx_ref, x_ref, y_hbm, o_ref, ybuf, sem):
    tm = x_ref.shape[0]
    i = pl.program_id(0)
    n = pl.num_programs(0)
    slot = i % 2

    def gather_tile(c_ref, r_ref, sl):
        def start(pair, _):
            for u in range(2):
                tk = 2 * pair + u
                d = pstart_ref[c_ref[0, 0, tk]] + r_ref[0, 0, tk]
                pltpu.make_async_copy(y_hbm.at[pl.ds(d, 1)], ybuf.at[sl, pl.ds(tk, 1)], sem.at[sl]).start(priority=u)
            return 0

        lax.fori_loop(0, tm // 2, start, 0, unroll=DMA_UNROLL // 2)

    @pl.when(i == 0)
    def _():
        gather_tile(cmb_ref, rk_ref, slot)

    @pl.when(i + 1 < n)
    def _():
        gather_tile(cmb_nx_ref, rk_nx_ref, 1 - slot)

    pltpu.make_async_copy(y_hbm.at[pl.ds(0, tm)], ybuf.at[slot], sem.at[slot]).wait()
    o_ref[...] = x_ref[...] + _unpack_bf16_pairs(ybuf[slot])


def _combine(x2d, y, combo, rank, pstart):
    n = x2d.shape[0]
    tm = TOKEN_TILE
    nt = n // tm
    smem_blk = pl.BlockSpec((1, 1, tm), lambda i, *_: (i, 0, 0), memory_space=pltpu.SMEM)
    smem_nxt = pl.BlockSpec((1, 1, tm), lambda i, *_: (jnp.minimum(i + 1, nt - 1), 0, 0), memory_space=pltpu.SMEM)
    return pl.pallas_call(
        _combine_kernel,
        grid_spec=pltpu.PrefetchScalarGridSpec(
            num_scalar_prefetch=1,
            grid=(nt,),
            in_specs=[smem_blk, smem_blk, smem_nxt, smem_nxt,
                      pl.BlockSpec((tm, D_MODEL), lambda i, *_: (i, 0)),
                      pl.BlockSpec(memory_space=pl.ANY)],
            out_specs=pl.BlockSpec((tm, D_MODEL), lambda i, *_: (i, 0)),
            scratch_shapes=[pltpu.VMEM((2, tm, HALF), U32), pltpu.SemaphoreType.DMA((2,))]),
        out_shape=jax.ShapeDtypeStruct((n, D_MODEL), F32),
        compiler_params=_cparams(("arbitrary",)),
    )(pstart, combo, rank, combo, rank, x2d, y)


def _combine_inproj_kernel(pstart_ref, cmb_ref, rk_ref, cmb_nx_ref, rk_nx_ref, x2_ref, y_hbm, g_ref, w_ref,
                           x_ref, z_ref, xbc_ref, qkv_ref, u_ref, dt_ref, ybuf, sem):
    _combine_kernel(pstart_ref, cmb_ref, rk_ref, cmb_nx_ref, rk_nx_ref, x2_ref, y_hbm, x_ref, ybuf, sem)
    _inproj_kernel(x_ref, g_ref, w_ref, z_ref, xbc_ref, qkv_ref, u_ref, dt_ref)


def _combine_inproj(x2d, y, combo, rank, pstart, gain, w_cat):
    n = x2d.shape[0]
    tm = TOKEN_TILE
    nt = n // tm
    smem_blk = pl.BlockSpec((1, 1, tm), lambda i, *_: (i, 0, 0), memory_space=pltpu.SMEM)
    smem_nxt = pl.BlockSpec((1, 1, tm), lambda i, *_: (jnp.minimum(i + 1, nt - 1), 0, 0), memory_space=pltpu.SMEM)
    widths = (D_MODEL,) + _INPROJ_WIDTHS
    dtypes = (F32,) + _INPROJ_DTYPES
    return pl.pallas_call(
        _combine_inproj_kernel,
        grid_spec=pltpu.PrefetchScalarGridSpec(
            num_scalar_prefetch=1,
            grid=(nt,),
            in_specs=[smem_blk, smem_blk, smem_nxt, smem_nxt,
                      pl.BlockSpec((tm, D_MODEL), lambda i, *_: (i, 0)),
                      pl.BlockSpec(memory_space=pl.ANY),
                      pl.BlockSpec((1, D_MODEL), lambda i, *_: (0, 0)),
                      pl.BlockSpec(w_cat.shape, lambda i, *_: (0, 0))],
            out_specs=[pl.BlockSpec((tm, w), lambda i, *_: (i, 0)) for w in widths],
            scratch_shapes=[pltpu.VMEM((2, tm, HALF), U32), pltpu.SemaphoreType.DMA((2,))]),
        out_shape=[jax.ShapeDtypeStruct((n, w), d) for w, d in zip(widths, dtypes)],
        compiler_params=_cparams(("arbitrary",)),
    )(pstart, combo, rank, combo, rank, x2d, y, gain.reshape(1, D_MODEL), w_cat)


_PAIR_A = np.array([a for a in range(EXPERTS_PER_GROUP) for _ in range(a + 1, EXPERTS_PER_GROUP)], np.int32)
_PAIR_B = np.array([b for a in range(EXPERTS_PER_GROUP) for b in range(a + 1, EXPERTS_PER_GROUP)], np.int32)


def _expert_block(n):
    return 2 * EXPERT_BLOCK_MIN if n >= 2 * 2 * EXPERT_BLOCK_MIN * N_COMBOS else EXPERT_BLOCK_MIN


def _moe(x2, hf, meta, counts, w_gate, w_up, w_down):
    b, t, _ = x2.shape
    n = b * t
    bm = _expert_block(n)
    nblk = (n + N_COMBOS * (bm - 1) + bm - 1) // bm
    cnt = counts[0, :N_COMBOS].astype(jnp.int32)
    psz = (cnt + bm - 1) // bm * bm
    pend = jnp.cumsum(psz)
    pstart = (pend - psz).astype(jnp.int32)
    nused = jnp.maximum(pend[-1] // bm, 1).astype(jnp.int32).reshape(1)
    blk = jnp.minimum(jnp.arange(nblk, dtype=jnp.int32), nused[0] - 1)
    blk_c = jnp.minimum(jnp.sum(pend[None, :] <= (blk * bm)[:, None], axis=1), N_COMBOS - 1).astype(jnp.int32)
    grp = blk_c // PAIRS_PER_GROUP
    blk_ea = (grp * EXPERTS_PER_GROUP + jnp.asarray(_PAIR_A)[blk_c % PAIRS_PER_GROUP]).astype(jnp.int32)
    blk_eb = (grp * EXPERTS_PER_GROUP + jnp.asarray(_PAIR_B)[blk_c % PAIRS_PER_GROUP]).astype(jnp.int32)
    ids = meta.astype(jnp.int32)
    combo = ids[:, 0:1, :]
    rank = ids[:, 1:2, :]
    xs = _dispatch(hf.reshape(n, ROW_WORDS), combo, rank, pstart, cnt, nblk * bm, bm)
    y = _experts(xs, blk_ea, blk_eb, nused, w_gate, w_up, w_down, bm)
    return x2.reshape(n, D_MODEL), y, combo, rank, pstart


def _prep_layer(lp):
    w_in = lp["w_in"]
    c0 = D_INNER + CONV_CH
    c1 = c0 + 2 * SSD_HEADS
    w_cat = jnp.concatenate([w_in[:, :c0], w_in[:, c1:], w_in[:, c0:c1],
                             jnp.zeros((D_MODEL, DT_PAD - 2 * SSD_HEADS), w_in.dtype)], axis=1).astype(BF16)
    pool_bd = jnp.zeros((POOL_WIDTH, POOL_WIDTH), F32)
    for g in range(POOL_GROUPS):
        sl = slice(g * POOL_GROUP_DIM, (g + 1) * POOL_GROUP_DIM)
        pool_bd = pool_bd.at[sl, sl].set(lp["pool_w"][g].astype(F32))
    w_r = jnp.concatenate([lp["w_router_expert"], lp["w_router_group"],
                           jnp.zeros((D_MODEL, LANES - N_EXPERTS - N_EXPERT_GROUPS), F32)], axis=1).astype(F32)
    w_r_hi = w_r.astype(BF16)
    w_r_lo = (w_r - w_r_hi.astype(F32)).astype(BF16)
    b_r = jnp.concatenate([lp["b_router_expert"], lp["b_router_group"],
                           jnp.zeros((LANES - N_EXPERTS - N_EXPERT_GROUPS,), F32)]).reshape(1, LANES).astype(F32)
    row = lambda a, w: a.reshape(1, w).astype(F32)
    return dict(
        norm_mix=lp["norm_mix"], w_cat=w_cat,
        conv_w=lp["conv_w"], conv_b=lp["conv_b"], dt_bias=lp["dt_bias"], a_log=lp["a_log"],
        d_skip=lp["d_skip"], ssd_norm=lp["ssd_norm"],
        na_q_norm=lp["na_q_norm"], na_k_norm=lp["na_k_norm"], na_rpb=lp["na_rpb"],
        pool_bd=pool_bd.astype(BF16), pool_scale=row(lp["pool_scale"], POOL_WIDTH),
        w_out=lp["w_out"].astype(BF16), norm_xa=row(lp["norm_xa"], D_MODEL),
        norm_mem=lp["norm_mem"], w_xq=lp["w_xq"].astype(BF16), w_xkv=lp["w_xkv"].astype(BF16),
        xq_norm=row(jnp.tile(lp["xq_norm"], XA_HEADS), XA_WIDTH), xk_norm=lp["xk_norm"],
        w_xo=lp["w_xo"].astype(BF16), norm_ffn=row(lp["norm_ffn"], D_MODEL),
        w_r_both=jnp.concatenate([w_r_hi, w_r_lo], axis=1), b_r=b_r,
        w_e_gate=lp["w_e_gate"].astype(BF16), w_e_up=lp["w_e_up"].astype(BF16),
        w_e_down=lp["w_e_down"].astype(BF16),
    )


def _layer(x, pending, mem, p, na_bias):
    b, m, _ = mem.shape
    if pending is None:
        t = x.shape[1]
        z, xbc, qkv, u, dt = _inproj(x.reshape(b * t, D_MODEL), p["norm_mix"], p["w_cat"])
    else:
        t = pending[0].shape[0] // b
        x, z, xbc, qkv, u, dt = _combine_inproj(*pending, p["norm_mix"], p["w_cat"])
        x = x.reshape(b, t, D_MODEL)
    r3 = lambda a: a.reshape(b, t, a.shape[-1])
    y_ssd = _ssd(r3(z), r3(xbc), r3(dt), p["conv_w"], p["conv_b"], p["dt_bias"], p["a_log"], p["d_skip"],
                 p["ssd_norm"])
    y_na = _natten(r3(qkv), na_bias, p["na_q_norm"], p["na_k_norm"])
    kmem, vmem = _memkv(mem, p["norm_mem"], p["w_xkv"], p["xk_norm"])
    x2, hf, meta, counts = _mixer(x, y_ssd, y_na, r3(u), kmem, vmem, p)
    return _moe(x2, hf, meta, counts, p["w_e_gate"], p["w_e_up"], p["w_e_down"])


_LAYER_KEYS = ("norm_mix", "w_in", "conv_w", "conv_b", "dt_bias", "a_log", "d_skip", "ssd_norm", "na_q_norm",
               "na_k_norm", "na_rpb", "pool_w", "pool_scale", "w_out", "norm_xa", "norm_mem", "w_xq", "w_xkv",
               "xq_norm", "xk_norm", "w_xo", "norm_ffn", "w_router_group", "b_router_group", "w_router_expert",
               "b_router_expert", "w_e_gate", "w_e_up", "w_e_down")


def kernel(x_prompt, x_sample, mem_prompt, mem_sample, norm_mix, w_in, conv_w, conv_b, dt_bias, a_log, d_skip, ssd_norm, na_q_norm, na_k_norm, na_rpb, pool_w, pool_scale, w_out, norm_xa, norm_mem, w_xq, w_xkv, xq_norm, xk_norm, w_xo, norm_ffn, w_router_group, b_router_group, w_router_expert, b_router_expert, w_e_gate, w_e_up, w_e_down):
    stacked = dict(zip(_LAYER_KEYS, (norm_mix, w_in, conv_w, conv_b, dt_bias, a_log, d_skip, ssd_norm, na_q_norm,
                                     na_k_norm, na_rpb, pool_w, pool_scale, w_out, norm_xa, norm_mem, w_xq, w_xkv,
                                     xq_norm, xk_norm, w_xo, norm_ffn, w_router_group, b_router_group,
                                     w_router_expert, b_router_expert, w_e_gate, w_e_up, w_e_down)))
    depth = w_in.shape[0]
    layers = [_prep_layer({k: v[l] for k, v in stacked.items()}) for l in range(depth)]

    bias_cache = {}

    def trunk(x, mem):
        t = x.shape[1]
        pending = None
        for l, lp in enumerate(layers):
            if (l, t) not in bias_cache:
                bias_cache[(l, t)] = _na_bias(lp["na_rpb"], t)
            pending = _layer(x, pending, mem, lp, bias_cache[(l, t)])
            x = None
        return _combine(*pending).reshape(mem.shape[0], t, D_MODEL)

    return trunk(x_prompt, mem_prompt), trunk(x_sample, mem_sample)
```

```python
import functools

import jax
import jax.numpy as jnp
import numpy as np
from jax import lax
from jax.experimental import pallas as pl
from jax.experimental.pallas import tpu as pltpu
from jax.experimental.pallas import tpu_sc as plsc

F32 = jnp.float32
BF16 = jnp.bfloat16
U32 = jnp.uint32
HIGHEST = lax.Precision.HIGHEST

D_MODEL = 1024
GRID_W = 64
EPS = 1e-6
SSD_HEAD_DIM = 64
D_INNER = D_MODEL // 2
SSD_HEADS = D_INNER // SSD_HEAD_DIM
SSD_GROUPS = 2
SSD_STATE = 64
SSD_CHUNK = 128
CONV_W = 4
CONV_CH = D_INNER + 2 * SSD_GROUPS * SSD_STATE
NA_HEADS = 4
NA_HEAD_DIM = D_MODEL // 16
NA_WIDTH = NA_HEADS * NA_HEAD_DIM
NA_MAX_KH = 8
NA_KW = 16
POOL_WINDOWS = (2, 4, 8, 16)
POOL_GROUPS = 4
POOL_WIDTH = D_MODEL - D_INNER - NA_WIDTH
POOL_GROUP_DIM = POOL_WIDTH // POOL_GROUPS
XA_HEADS = 4
XA_HEAD_DIM = D_MODEL // 8
XA_WIDTH = XA_HEADS * XA_HEAD_DIM
N_EXPERT_GROUPS = 4
EXPERTS_PER_GROUP = 8
N_EXPERTS = N_EXPERT_GROUPS * EXPERTS_PER_GROUP
D_EXPERT = D_MODEL // 4
PAIRS_PER_GROUP = EXPERTS_PER_GROUP * (EXPERTS_PER_GROUP - 1) // 2
N_COMBOS = N_EXPERT_GROUPS * PAIRS_PER_GROUP

LANES = 128
BF16_SUBLANES = 16
VMEM_LIMIT_BYTES = 56 * 1024 * 1024

TOKEN_TILE = 512
NA_QUERY_ROWS = 8
NA_SUB_ROWS = 8
NA_SUBS = NA_QUERY_ROWS // NA_SUB_ROWS
NA_KEY_ROWS = NA_SUB_ROWS + NA_MAX_KH
EXPERT_BLOCK_MIN = 128
HALF = D_MODEL // 2
ROW_WORDS = HALF + LANES
SSD_STEP_CHUNKS = 4
DT_PAD = LANES
SC_GATHER_ROWS = 128
DMA_UNROLL = 8
NEG_BIG = -1e30


def _cparams(sem):
    return pltpu.CompilerParams(dimension_semantics=sem, vmem_limit_bytes=VMEM_LIMIT_BYTES)


def _sigmoid(x):
    return 1.0 / (1.0 + jnp.exp(-x))


def _silu(x):
    return x * _sigmoid(x)


def _softplus(x):
    return jnp.maximum(x, 0.0) + jnp.log(1.0 + jnp.exp(-jnp.abs(x)))


def _pack_bf16_pairs(v):
    k = v.shape[1] // 2
    bits = pltpu.bitcast(v, U32)
    return (bits[:, :k] >> 16) | (bits[:, k:] & jnp.uint32(0xFFFF0000))


def _unpack_bf16_pairs(w):
    lo = pltpu.bitcast(w << 16, F32)
    hi = pltpu.bitcast(w & jnp.uint32(0xFFFF0000), F32)
    return jnp.concatenate([lo, hi], axis=1)


def _full(shape):
    n = len(shape)
    return pl.BlockSpec(shape, lambda *_: (0,) * n)


def _inproj_kernel(x_ref, g_ref, w_ref, z_ref, xbc_ref, qkv_ref, u_ref, dt_ref):
    x = x_ref[...]
    ms = jnp.mean(x * x, axis=-1, keepdims=True)
    h = (x * lax.rsqrt(ms + EPS) * g_ref[...]).astype(BF16)
    o = 0
    for ref in (z_ref, xbc_ref, qkv_ref, u_ref, dt_ref):
        w = ref.shape[-1]
        ref[...] = jnp.dot(h, w_ref[:, o:o + w], preferred_element_type=F32).astype(ref.dtype)
        o += w


_INPROJ_WIDTHS = (D_INNER, CONV_CH, 3 * NA_WIDTH, POOL_WIDTH, DT_PAD)
_INPROJ_DTYPES = (BF16, BF16, BF16, BF16, F32)


def _inproj(x2d, gain, w_cat):
    n = x2d.shape[0]
    tm = TOKEN_TILE
    widths, dtypes = _INPROJ_WIDTHS, _INPROJ_DTYPES
    return pl.pallas_call(
        _inproj_kernel,
        grid=(n // tm,),
        in_specs=[pl.BlockSpec((tm, D_MODEL), lambda i: (i, 0)),
                  _full((1, D_MODEL)),
                  _full(w_cat.shape)],
        out_specs=[pl.BlockSpec((tm, w), lambda i: (i, 0)) for w in widths],
        out_shape=[jax.ShapeDtypeStruct((n, w), d) for w, d in zip(widths, dtypes)],
        compiler_params=_cparams(("arbitrary",)),
    )(x2d, gain.reshape(1, D_MODEL), w_cat)


def _split3(a):
    hi = a.astype(BF16)
    r = a - hi.astype(F32)
    mid = r.astype(BF16)
    lo = (r - mid.astype(F32)).astype(BF16)
    return hi, mid, lo


def _ssd_kernel(xc_ref, xp_ref, xn_ref, dt_ref, z_ref, cw_ref, cb_ref, dtb_ref, alog_ref, dsk_ref, nrm_ref, emat_ref,
                y_ref, state_ref, yf_ref, xs_c, bc_c, cbm_c, dt_c, *, nblocks):
    L = SSD_CHUNK
    LB = SSD_STEP_CHUNKS * L
    P = SSD_HEAD_DIM
    NS = SSD_STATE
    HG = SSD_HEADS // SSD_GROUPS
    gn = SSD_GROUPS * NS
    j = pl.program_id(1)
    c = jnp.where(j < nblocks, j, 2 * nblocks - 1 - j)
    row0 = pl.multiple_of(c * LB, LB)
    rows = pl.ds(row0, LB)
    lane1 = lax.broadcasted_iota(jnp.int32, (1, LANES), 1)
    lo_half = lane1 < P
    ti = lax.broadcasted_iota(jnp.int32, (L, L), 0)
    si = lax.broadcasted_iota(jnp.int32, (L, L), 1)

    def masked_c(bc):
        return [jnp.where(lane1 // NS == g, bc[:, gn:], 0.0).astype(BF16) for g in range(SSD_GROUPS)]

    def prepare():
        cur = xc_ref[0].astype(F32)
        prev = xp_ref[0].astype(F32)
        nxt = xn_ref[0].astype(F32)
        has_prev = (c > 0).astype(F32)
        has_next = (c < nblocks - 1).astype(F32)
        p_last = prev[BF16_SUBLANES - 1:BF16_SUBLANES, :] * has_prev
        n0 = nxt[0:1, :] * has_next
        n1 = nxt[1:2, :] * has_next
        row = lax.broadcasted_iota(jnp.int32, (LB, 1), 0)
        um1 = jnp.where(row == 0, p_last, pltpu.roll(cur, 1, 0))
        up1 = jnp.where(row == LB - 1, n0, pltpu.roll(cur, LB - 1, 0))
        up2 = jnp.where(row == LB - 2, n0, jnp.where(row == LB - 1, n1, pltpu.roll(cur, LB - 2, 0)))
        cw = cw_ref[...]
        acc = cb_ref[...] + um1 * cw[0:1, :] + cur * cw[1:2, :] + up1 * cw[2:3, :] + up2 * cw[3:4, :]
        xbc = _silu(acc)
        xs = xbc[:, :D_INNER]
        bc = xbc[:, D_INNER:D_INNER + 2 * gn].astype(BF16)
        dt = _softplus(dt_ref[0] + dtb_ref[...])
        xs_c[rows, :] = xs
        bc_c[rows, :] = bc
        dt_c[rows, :] = dt
        ops = []
        for sub in range(SSD_STEP_CHUNKS):
            sl = slice(sub * L, (sub + 1) * L)
            cg = masked_c(bc[sl])
            cb_mat = [lax.dot_general(cg[g], bc[sl, :gn], (((1,), (1,)), ((), ())), preferred_element_type=F32)
                      for g in range(SSD_GROUPS)]
            cbm_c[pl.ds(row0 + sub * L, L), :] = jnp.concatenate(cb_mat, axis=1)
            ops.append((xs[sl], bc[sl, :gn], cg, cb_mat, dt[sl]))
        return ops

    def recall():
        ops = []
        for sub in range(SSD_STEP_CHUNKS):
            r = pl.ds(row0 + sub * L, L)
            bc = bc_c[r, :]
            cbm = cbm_c[r, :]
            ops.append((xs_c[r, :], bc[:, :gn], masked_c(bc), [cbm[:, g * L:(g + 1) * L] for g in range(SSD_GROUPS)],
                        dt_c[r, :]))
        return ops

    def scan_chunk(direction, xs, bfull, cg, cb_mat, dt):
        if direction == 0:
            mask = ti >= si
            edge = L - 1
        else:
            mask = si >= ti
            edge = 0
        la = dt * (-jnp.exp(alog_ref[...]))
        tri = mask.astype(BF16)
        csum = sum(jnp.dot(tri, part, preferred_element_type=F32) for part in _split3(la))
        csum_t = csum.T
        emat = emat_ref[direction]
        colb = sum(jnp.dot(part, emat, preferred_element_type=F32) for part in _split3(csum))
        tot = csum[edge:edge + 1, :]
        e_tot = jnp.exp(tot)
        e_in = jnp.exp(csum)
        e_out = jnp.exp(tot - csum)
        ys = []
        for g in range(SSD_GROUPS):
            s_old = state_ref[g]
            y_off = lax.dot_general(cg[g], s_old.astype(BF16), (((1,), (1,)), ((), ())),
                                    preferred_element_type=F32)
            xw = []
            for pr in range(HG // 2):
                h0 = g * HG + 2 * pr
                l0 = direction * SSD_HEADS + h0

                def col(a, l0=l0):
                    return jnp.where(lo_half, a[:, l0:l0 + 1], a[:, l0 + 1:l0 + 2])

                xdt = xs[:, h0 * P:(h0 + 2) * P] * col(dt)
                y_pair = y_off[:, 2 * pr * P:(2 * pr + 2) * P] * col(e_in)
                for hh, half in ((h0, lo_half), (h0 + 1, jnp.logical_not(lo_half))):
                    ll = direction * SSD_HEADS + hh
                    seg = colb[:, hh * L:(hh + 1) * L] - csum_t[ll:ll + 1, :]
                    dec = jnp.exp(jnp.where(mask, seg, NEG_BIG))
                    m = (cb_mat[g] * dec).astype(BF16)
                    y_pair += jnp.dot(m, jnp.where(half, xdt, 0.0).astype(BF16), preferred_element_type=F32)
                ys.append(y_pair)
                xw.append(xdt * col(e_out))
            xw = jnp.concatenate(xw, axis=1).astype(BF16)
            s_new = lax.dot_general(xw, bfull, (((0,), (0,)), ((), ())), preferred_element_type=F32)
            s_scaled = []
            for hl in range(HG):
                lane = direction * SSD_HEADS + g * HG + hl
                s_scaled.append(s_old[hl * P:(hl + 1) * P, :] * e_tot[:, lane:lane + 1])
            state_ref[g] = jnp.concatenate(s_scaled, axis=0) + s_new
        return jnp.concatenate(ys, axis=1)

    @pl.when(jnp.logical_or(j == 0, j == nblocks))
    def _():
        state_ref[...] = jnp.zeros_like(state_ref)

    @pl.when(j < nblocks)
    def _():
        ops = prepare()
        for sub in range(SSD_STEP_CHUNKS):
            yf_ref[pl.ds(row0 + sub * L, L), :] = scan_chunk(0, *ops[sub])

    @pl.when(j >= nblocks)
    def _():
        ops = recall()
        for sub in reversed(range(SSD_STEP_CHUNKS)):
            sl = slice(sub * L, (sub + 1) * L)
            y = yf_ref[pl.ds(row0 + sub * L, L), :] + scan_chunk(1, *ops[sub]) + dsk_ref[...] * ops[sub][0]
            y = y * _silu(z_ref[0, sl, :].astype(F32))
            gw = D_INNER // SSD_GROUPS
            outs = []
            for g in range(SSD_GROUPS):
                yg = y[:, g * gw:(g + 1) * gw]
                outs.append(yg * lax.rsqrt(jnp.mean(yg * yg, axis=-1, keepdims=True) + EPS))
            y_ref[0, sl, :] = (jnp.concatenate(outs, axis=1) * nrm_ref[...]).astype(y_ref.dtype)


def _ssd(z, xbc, dt, conv_w, conv_b, dt_bias, a_log, d_skip, ssd_norm):
    b, t, _ = z.shape
    L = SSD_CHUNK
    lb = SSD_STEP_CHUNKS * L
    nb = t // lb
    hb = lb // BF16_SUBLANES
    nhalo = t // BF16_SUBLANES

    def blk(j):
        return jnp.where(j < nb, j, 2 * nb - 1 - j)

    pad = DT_PAD - 2 * SSD_HEADS
    dtb = jnp.pad(dt_bias.reshape(1, -1).astype(F32), ((0, 0), (0, pad)))
    alog = jnp.pad(a_log.reshape(1, -1).astype(F32), ((0, 0), (0, pad)))
    dsk = jnp.repeat(d_skip.astype(F32), SSD_HEAD_DIM).reshape(1, D_INNER)
    sel = np.arange(DT_PAD)[None, :, None] == (np.arange(2)[:, None, None] * SSD_HEADS
                                                + np.arange(SSD_HEADS)[None, None, :])
    emat = jnp.asarray(np.repeat(sel, L, axis=2), BF16)
    return pl.pallas_call(
        functools.partial(_ssd_kernel, nblocks=nb),
        grid=(b, 2 * nb),
        in_specs=[
            pl.BlockSpec((1, lb, CONV_CH), lambda i, j: (i, blk(j), 0)),
            pl.BlockSpec((1, BF16_SUBLANES, CONV_CH), lambda i, j: (i, jnp.maximum(blk(j) * hb - 1, 0), 0)),
            pl.BlockSpec((1, BF16_SUBLANES, CONV_CH),
                         lambda i, j: (i, jnp.minimum((blk(j) + 1) * hb, nhalo - 1), 0)),
            pl.BlockSpec((1, lb, DT_PAD), lambda i, j: (i, blk(j), 0)),
            pl.BlockSpec((1, lb, D_INNER), lambda i, j: (i, blk(j), 0)),
            _full((CONV_W, CONV_CH)), _full((1, CONV_CH)), _full((1, DT_PAD)), _full((1, DT_PAD)),
            _full((1, D_INNER)), _full((1, D_INNER)), _full((2, DT_PAD, SSD_HEADS * L)),
        ],
        out_specs=pl.BlockSpec((1, lb, D_INNER), lambda i, j: (i, jnp.where(j < nb, nb - 1, 2 * nb - 1 - j), 0)),
        out_shape=jax.ShapeDtypeStruct((b, t, D_INNER), BF16),
        scratch_shapes=[pltpu.VMEM((SSD_GROUPS, (SSD_HEADS // SSD_GROUPS) * SSD_HEAD_DIM, LANES), F32),
                        pltpu.VMEM((t, D_INNER), F32),
                        pltpu.VMEM((t, D_INNER), F32),
                        pltpu.VMEM((t, 2 * SSD_GROUPS * SSD_STATE), BF16),
                        pltpu.VMEM((t, SSD_GROUPS * L), F32),
                        pltpu.VMEM((t, DT_PAD), F32)],
        compiler_params=_cparams(("arbitrary", "arbitrary")),
    )(xbc, xbc, xbc, dt, z, conv_w.astype(F32), conv_b.reshape(1, CONV_CH).astype(F32), dtb, alog, dsk,
      ssd_norm.reshape(1, D_INNER).astype(F32), emat)


def _na_bias(rpb, t):
    r = t // GRID_W
    kh = min(NA_MAX_KH, r)
    nsb = r // NA_SUB_ROWS
    rows = np.arange(r)
    row_start = np.clip(rows - NA_MAX_KH // 2, 0, r - kh)
    r0 = np.arange(nsb) * NA_SUB_ROWS
    kr0 = np.clip(r0 - NA_MAX_KH // 2, 0, r - NA_KEY_ROWS)
    qrow = r0[:, None] + np.arange(NA_SUB_ROWS)[None, :]
    krow = kr0[:, None] + np.arange(NA_KEY_ROWS)[None, :]
    rs = row_start[qrow]
    row_ok = (krow[:, None, :] >= rs[:, :, None]) & (krow[:, None, :] < rs[:, :, None] + kh)
    dr = np.clip(krow[:, None, :] - qrow[:, :, None] + (NA_MAX_KH - 1), 0, 2 * NA_MAX_KH - 2)
    cols = np.arange(GRID_W)
    col_start = np.clip(cols - NA_KW // 2, 0, GRID_W - NA_KW)
    col_ok = (cols[None, :] >= col_start[:, None]) & (cols[None, :] < col_start[:, None] + NA_KW)
    dc = np.clip(cols[None, :] - cols[:, None] + (NA_KW - 1), 0, 2 * NA_KW - 2)
    sel_c = (dc[..., None] == np.arange(2 * NA_KW - 1)) & col_ok[..., None]
    nr = 2 * NA_MAX_KH - 1
    tile_r = jnp.einsum("hrc,xyc->hrxy", rpb.astype(F32), jnp.asarray(sel_c, F32), precision=HIGHEST)
    tile_r = jnp.where(jnp.asarray(col_ok)[None, None], tile_r, NEG_BIG)
    tile_r = jnp.concatenate([tile_r, jnp.full((NA_HEADS, 1, GRID_W, GRID_W), NEG_BIG, F32)], axis=1)
    kp = NA_KEY_ROWS // 2
    code = np.where(row_ok, dr, nr).reshape(nsb * NA_SUB_ROWS * kp, 2)
    pairs, inv = np.unique(code, axis=0, return_inverse=True)
    blocks = jnp.concatenate([tile_r[:, pairs[:, 0]], tile_r[:, pairs[:, 1]]], axis=-1)
    blocks = jnp.moveaxis(blocks, 0, 1).reshape(len(pairs), NA_HEADS * GRID_W * 2 * GRID_W)
    onehot = jnp.asarray(inv.reshape(-1, 1) == np.arange(len(pairs))[None, :], F32)
    bias = jnp.dot(onehot, blocks, precision=HIGHEST)
    return bias.reshape(nsb // NA_SUBS, NA_SUBS, NA_SUB_ROWS, kp, NA_HEADS, GRID_W, 2 * GRID_W).astype(BF16)


def _natten_kernel(qkv_ref, bias_ref, qg_ref, kg_ref, seg_ref, o_ref, *, grid_rows):
    nq = NA_SUB_ROWS * GRID_W
    nk = NA_KEY_ROWS * GRID_W
    rb = pl.program_id(0)
    seg = seg_ref[...]
    lane_h = lax.broadcasted_iota(jnp.int32, (1, NA_WIDTH), 1) // NA_HEAD_DIM
    for sub in range(NA_SUBS):
        r0 = rb * NA_QUERY_ROWS + sub * NA_SUB_ROWS
        kr0 = jnp.clip(r0 - NA_MAX_KH // 2, 0, grid_rows - NA_KEY_ROWS)
        q0 = pl.multiple_of(r0 * GRID_W, nq)
        k0 = pl.multiple_of(kr0 * GRID_W, NA_MAX_KH // 2 * GRID_W)
        q = qkv_ref[0, pl.ds(q0, nq), 0:NA_WIDTH].astype(F32)
        k = qkv_ref[0, pl.ds(k0, nk), NA_WIDTH:2 * NA_WIDTH].astype(F32)
        v = qkv_ref[0, pl.ds(k0, nk), 2 * NA_WIDTH:3 * NA_WIDTH]
        qms = jnp.dot(q * q, seg, precision=HIGHEST, preferred_element_type=F32)
        kms = jnp.dot(k * k, seg, precision=HIGHEST, preferred_element_type=F32)
        qn = q * lax.rsqrt(qms + EPS) * (qg_ref[...] * NA_HEAD_DIM ** -0.5)
        kn = (k * lax.rsqrt(kms + EPS) * kg_ref[...]).astype(BF16)
        acc = jnp.zeros((nq, NA_WIDTH), F32)
        for h in range(NA_HEADS):
            hm = lane_h == h
            s = lax.dot_general(jnp.where(hm, qn, 0.0).astype(BF16), kn, (((1,), (1,)), ((), ())),
                                preferred_element_type=F32)
            bias = jnp.concatenate(
                [jnp.concatenate([bias_ref[0, sub, qr, kc, h] for kc in range(NA_KEY_ROWS // 2)], axis=1)
                 for qr in range(NA_SUB_ROWS)], axis=0)
            s = s + bias.astype(F32)
            p = jnp.exp(s - jnp.max(s, axis=-1, keepdims=True))
            l = jnp.sum(p, axis=-1, keepdims=True)
            o = jnp.dot(p.astype(BF16), v, preferred_element_type=F32)
            acc += jnp.where(hm, o / l, 0.0)
        o_ref[0, sub * nq:(sub + 1) * nq, :] = acc.astype(o_ref.dtype)


def _natten(qkv, bias, q_norm, k_norm):
    b, t, _ = qkv.shape
    r = t // GRID_W
    nrb = r // NA_QUERY_ROWS
    nq = NA_QUERY_ROWS * GRID_W
    head = jnp.arange(NA_WIDTH) // NA_HEAD_DIM
    seg = (head[:, None] == head[None, :]).astype(F32) / NA_HEAD_DIM
    return pl.pallas_call(
        functools.partial(_natten_kernel, grid_rows=r),
        grid=(nrb, b),
        in_specs=[pl.BlockSpec((1, t, 3 * NA_WIDTH), lambda i, j: (j, 0, 0)),
                  pl.BlockSpec((1,) + bias.shape[1:], lambda i, j: (i,) + (0,) * (bias.ndim - 1)),
                  _full((1, NA_WIDTH)), _full((1, NA_WIDTH)), _full((NA_WIDTH, NA_WIDTH))],
        out_specs=pl.BlockSpec((1, nq, NA_WIDTH), lambda i, j: (j, i, 0)),
        out_shape=jax.ShapeDtypeStruct((b, t, NA_WIDTH), BF16),
        compiler_params=_cparams(("arbitrary", "arbitrary")),
    )(qkv, bias, jnp.tile(q_norm.astype(F32), NA_HEADS).reshape(1, NA_WIDTH),
      jnp.tile(k_norm.astype(F32), NA_HEADS).reshape(1, NA_WIDTH), seg)


def _memkv_kernel(m_ref, g_ref, w_ref, kg_ref, k_ref, v_ref):
    x = m_ref[0]
    ms = jnp.mean(x * x, axis=-1, keepdims=True)
    h = (x * lax.rsqrt(ms + EPS) * g_ref[...]).astype(BF16)
    kv = jnp.dot(h, w_ref[...], preferred_element_type=F32)
    ks = []
    for hd in range(XA_HEADS):
        kh = kv[:, hd * XA_HEAD_DIM:(hd + 1) * XA_HEAD_DIM]
        ks.append(kh * lax.rsqrt(jnp.mean(kh * kh, axis=-1, keepdims=True) + EPS))
    k_ref[0] = (jnp.concatenate(ks, axis=1) * kg_ref[...]).astype(k_ref.dtype)
    v_ref[0] = kv[:, XA_WIDTH:].astype(v_ref.dtype)


def _memkv(mem, norm_mem, w_xkv, xk_norm):
    b, m, _ = mem.shape
    return pl.pallas_call(
        _memkv_kernel,
        grid=(b,),
        in_specs=[pl.BlockSpec((1, m, D_MODEL), lambda i: (i, 0, 0)),
                  _full((1, D_MODEL)), _full((D_MODEL, 2 * XA_WIDTH)), _full((1, XA_WIDTH))],
        out_specs=[pl.BlockSpec((1, m, XA_WIDTH), lambda i: (i, 0, 0))] * 2,
        out_shape=[jax.ShapeDtypeStruct((b, m, XA_WIDTH), BF16)] * 2,
        compiler_params=_cparams(("arbitrary",)),
    )(mem, norm_mem.reshape(1, D_MODEL).astype(F32), w_xkv,
      jnp.tile(xk_norm.astype(F32), XA_HEADS).reshape(1, XA_WIDTH))


def _mixer_kernel(x_ref, ys_ref, yn_ref, u_ref, up_ref, un_ref, k_ref, v_ref,
                  pw_ref, psc_ref, wo_ref, gxa_ref, wq_ref, qg_ref, wxo_ref, gff_ref,
                  wrb_ref, br_ref, ltri_ref,
                  x2_ref, hf_ref, meta_ref, cnt_ref, carry_ref, *, seq):
    tm = x_ref.shape[1]
    halo = BF16_SUBLANES
    bi = pl.program_id(0)
    i = pl.program_id(1)
    nt = pl.num_programs(1)

    @pl.when(jnp.logical_and(bi == 0, i == 0))
    def _():
        carry_ref[...] = jnp.zeros_like(carry_ref)

    u = u_ref[0].astype(F32)
    up = up_ref[0].astype(F32) * (i > 0).astype(F32)
    un = un_ref[0].astype(F32) * (i < nt - 1).astype(F32)
    cat = jnp.concatenate([up, u, un], axis=0)
    n = tm + 2 * halo

    def sh(a, k):
        return pltpu.roll(a, (-k) % n, 0)

    a2 = cat + sh(cat, -1)
    a4 = sh(a2, 1) + sh(a2, -1)
    a8 = sh(a4, 2) + sh(a4, -2)
    a16 = sh(a8, 4) + sh(a8, -4)
    lane_g = lax.broadcasted_iota(jnp.int32, (1, POOL_WIDTH), 1) // POOL_GROUP_DIM
    wsum = jnp.where(lane_g == 0, a2, jnp.where(lane_g == 1, a4, jnp.where(lane_g == 2, a8, a16)))
    wsum = wsum[halo:halo + tm, :]
    half = jnp.where(lane_g == 0, POOL_WINDOWS[0] // 2,
                     jnp.where(lane_g == 1, POOL_WINDOWS[1] // 2,
                               jnp.where(lane_g == 2, POOL_WINDOWS[2] // 2, POOL_WINDOWS[3] // 2)))
    tpos = i * tm + lax.broadcasted_iota(jnp.int32, (tm, 1), 0)
    cnt = (jnp.minimum(tpos + half, seq) - jnp.maximum(tpos - half, 0)).astype(F32)
    d = wsum / cnt - u
    ypool = jnp.dot(d.astype(BF16), pw_ref[...], preferred_element_type=F32) * psc_ref[...]

    mix = jnp.dot(ys_ref[0], wo_ref[0:D_INNER, :], preferred_element_type=F32)
    mix += jnp.dot(yn_ref[0], wo_ref[D_INNER:D_INNER + NA_WIDTH, :], preferred_element_type=F32)
    mix += jnp.dot(ypool.astype(BF16), wo_ref[D_INNER + NA_WIDTH:, :], preferred_element_type=F32)
    x1 = x_ref[0] + mix

    hn = (x1 * lax.rsqrt(jnp.mean(x1 * x1, axis=-1, keepdims=True) + EPS) * gxa_ref[...]).astype(BF16)
    q = jnp.dot(hn, wq_ref[...], preferred_element_type=F32)
    kk = k_ref[0]
    vv = v_ref[0]
    scale = XA_HEAD_DIM ** -0.5
    outs = []
    for hd in range(XA_HEADS):
        sl = slice(hd * XA_HEAD_DIM, (hd + 1) * XA_HEAD_DIM)
        qh = q[:, sl]
        qh = (qh * lax.rsqrt(jnp.mean(qh * qh, axis=-1, keepdims=True) + EPS) * qg_ref[:, sl]).astype(BF16)
        s = lax.dot_general(qh, kk[:, sl], (((1,), (1,)), ((), ())), preferred_element_type=F32) * scale
        p = jnp.exp(s - jnp.max(s, axis=-1, keepdims=True))
        l = jnp.sum(p, axis=-1, keepdims=True)
        outs.append(jnp.dot(p.astype(BF16), vv[:, sl], preferred_element_type=F32) / l)
    att = jnp.concatenate(outs, axis=1).astype(BF16)
    x2 = x1 + jnp.dot(att, wxo_ref[...], preferred_element_type=F32)
    x2_ref[0] = x2

    hf = x2 * lax.rsqrt(jnp.mean(x2 * x2, axis=-1, keepdims=True) + EPS) * gff_ref[...]
    h_hi = hf.astype(BF16)
    hf_ref[0, :, :HALF] = _pack_bf16_pairs(h_hi.astype(F32))
    h_lo = (hf - h_hi.astype(F32)).astype(BF16)
    both = jnp.dot(h_hi, wrb_ref[...], preferred_element_type=F32)
    logits = (both[:, :LANES] + both[:, LANES:]
              + jnp.dot(h_lo, wrb_ref[:, :LANES], preferred_element_type=F32)) + br_ref[...]
    lane = lax.broadcasted_iota(jnp.int32, (1, LANES), 1)
    lane_f = lane.astype(F32)
    lane_grp = (lane // EXPERTS_PER_GROUP).astype(F32)
    is_g = jnp.logical_and(lane >= N_EXPERTS, lane < N_EXPERTS + N_EXPERT_GROUPS)
    gl = jnp.where(is_g, logits, NEG_BIG)
    gmax = jnp.max(gl, axis=-1, keepdims=True)
    g_sel = jnp.min(jnp.where(gl == gmax, lane_f, float(LANES)), axis=-1, keepdims=True) - N_EXPERTS
    g_gate = 1.0 / jnp.sum(jnp.where(is_g, jnp.exp(gl - gmax), 0.0), axis=-1, keepdims=True)
    in_grp = jnp.logical_and(lane < N_EXPERTS, lane_grp == g_sel)
    el = jnp.where(in_grp, logits, NEG_BIG)
    v1 = jnp.max(el, axis=-1, keepdims=True)
    e0 = jnp.min(jnp.where(el == v1, lane_f, float(LANES)), axis=-1, keepdims=True)
    el2 = jnp.where(lane_f == e0, NEG_BIG, el)
    v2 = jnp.max(el2, axis=-1, keepdims=True)
    e1 = jnp.min(jnp.where(el2 == v2, lane_f, float(LANES)), axis=-1, keepdims=True)
    w1 = jnp.exp(v2 - v1)
    gate0 = g_gate / (1.0 + w1)
    gate1 = g_gate * w1 / (1.0 + w1)

    base = g_sel * EXPERTS_PER_GROUP
    ea = jnp.minimum(e0, e1) - base
    eb = jnp.maximum(e0, e1) - base
    combo = g_sel * PAIRS_PER_GROUP + ea * EXPERTS_PER_GROUP - ea * (ea + 1.0) * 0.5 + (eb - ea - 1.0)
    gate_a = jnp.where(e0 < e1, gate0, gate1)
    gate_b = jnp.where(e0 < e1, gate1, gate0)
    hf_ref[0, :, HALF:] = pltpu.bitcast(jnp.where(lane == 0, gate_a, 0.0) + jnp.where(lane == 1, gate_b, 0.0), U32)

    oh = lane_f == combo
    cnt_tok = oh.astype(F32)
    before = jnp.dot(ltri_ref[...], cnt_tok.astype(BF16), preferred_element_type=F32) + carry_ref[0:1, :]
    rank = jnp.sum(jnp.where(oh, before, 0.0), axis=-1, keepdims=True)
    new_carry = carry_ref[0:1, :] + jnp.sum(cnt_tok, axis=0, keepdims=True)
    carry_ref[...] = jnp.broadcast_to(new_carry, carry_ref.shape)
    cnt_ref[...] = jnp.broadcast_to(new_carry, cnt_ref.shape)

    slab = jnp.where(lane == 0, combo, 0.0) + jnp.where(lane == 1, rank, 0.0)
    meta_ref[0] = slab.T[0:8, :]


def _mixer(x, y_ssd, y_na, u, kmem, vmem, p):
    b, t, _ = x.shape
    tm = TOKEN_TILE
    nt = t // tm
    hb = tm // BF16_SUBLANES
    nhalo = t // BF16_SUBLANES
    ltri = (jnp.arange(tm)[:, None] > jnp.arange(tm)[None, :]).astype(BF16)
    tok = lambda w: pl.BlockSpec((1, tm, w), lambda i, j: (i, j, 0))
    mem = pl.BlockSpec((1, kmem.shape[1], XA_WIDTH), lambda i, j: (i, 0, 0))
    weights = (p["pool_bd"], p["pool_scale"], p["w_out"], p["norm_xa"], p["w_xq"], p["xq_norm"], p["w_xo"],
               p["norm_ffn"], p["w_r_both"], p["b_r"], ltri)
    return pl.pallas_call(
        functools.partial(_mixer_kernel, seq=t),
        grid=(b, nt),
        in_specs=[tok(D_MODEL), tok(D_INNER), tok(NA_WIDTH), tok(POOL_WIDTH),
                  pl.BlockSpec((1, BF16_SUBLANES, POOL_WIDTH), lambda i, j: (i, jnp.maximum(j * hb - 1, 0), 0)),
                  pl.BlockSpec((1, BF16_SUBLANES, POOL_WIDTH),
                               lambda i, j: (i, jnp.minimum((j + 1) * hb, nhalo - 1), 0)),
                  mem, mem] + [_full(w.shape) for w in weights],
        out_specs=[tok(D_MODEL), tok(ROW_WORDS),
                   pl.BlockSpec((1, 8, tm), lambda i, j: (i * nt + j, 0, 0)),
                   pl.BlockSpec((8, LANES), lambda i, j: (0, 0))],
        out_shape=[jax.ShapeDtypeStruct((b, t, D_MODEL), F32),
                   jax.ShapeDtypeStruct((b, t, ROW_WORDS), U32),
                   jax.ShapeDtypeStruct((b * nt, 8, tm), F32),
                   jax.ShapeDtypeStruct((8, LANES), F32)],
        scratch_shapes=[pltpu.VMEM((8, LANES), F32)],
        compiler_params=_cparams(("arbitrary", "arbitrary")),
    )(x, y_ssd, y_na, u, u, u, kmem, vmem, *weights)


def _dispatch_kernel(pstart_ref, cnt_ref, cmb_ref, rk_ref, hf_ref, zero_ref, xs_ref, stage, sem, *, bm):
    tm = hf_ref.shape[0]
    i = pl.program_id(0)
    n = pl.num_programs(0)
    slot = i % 2

    def row_copy(src_ref, src_row, dst_row, s):
        return pltpu.make_async_copy(src_ref.at[pl.ds(src_row, 1)], xs_ref.at[pl.ds(dst_row, 1)], s)

    def wait_tile(sl):
        pltpu.make_async_copy(stage.at[sl], xs_ref.at[pl.ds(0, tm)], sem.at[sl]).wait()

    @pl.when(i == 0)
    def _():
        def fill(cm, wait):
            n_c = cnt_ref[cm]
            npad = (bm - n_c % bm) % bm
            head = (8 - n_c % 8) % 8
            base = pstart_ref[cm] + n_c

            def go(cp):
                if wait:
                    cp.wait()
                else:
                    cp.start()

            def one(r, _):
                go(row_copy(zero_ref, 0, base + r, sem.at[0]))
                return 0

            lax.fori_loop(0, head, one, 0)
            rest = npad - head
            off = base + head
            size = bm // 2
            while size >= 8:
                @pl.when(rest & size != 0)
                def _(off=off, size=size):
                    go(pltpu.make_async_copy(zero_ref.at[pl.ds(0, size)],
                                             xs_ref.at[pl.ds(pl.multiple_of(off, 8), size)], sem.at[0]))

                off = off + (rest & size)
                size //= 2
            return 0

        lax.fori_loop(0, N_COMBOS, lambda cm, _: fill(cm, False), 0)
        lax.fori_loop(0, N_COMBOS, lambda cm, _: fill(cm, True), 0)

    stage[slot] = hf_ref[...]

    def start(pair, _):
        for u in range(2):
            tk = 2 * pair + u
            row_copy(stage.at[slot], tk, pstart_ref[cmb_ref[0, 0, tk]] + rk_ref[0, 0, tk],
                     sem.at[slot]).start(priority=u)
        return 0

    lax.fori_loop(0, tm // 2, start, 0, unroll=DMA_UNROLL // 2)

    @pl.when(i > 0)
    def _():
        wait_tile(1 - slot)

    @pl.when(i == n - 1)
    def _():
        wait_tile(slot)


def _dispatch(hf2d, combo, rank, pstart, counts, n_rows, bm):
    n = hf2d.shape[0]
    tm = TOKEN_TILE
    smem_blk = pl.BlockSpec((1, 1, tm), lambda i, *_: (i, 0, 0), memory_space=pltpu.SMEM)
    return pl.pallas_call(
        functools.partial(_dispatch_kernel, bm=bm),
        grid_spec=pltpu.PrefetchScalarGridSpec(
            num_scalar_prefetch=2,
            grid=(n // tm,),
            in_specs=[smem_blk, smem_blk,
                      pl.BlockSpec((tm, ROW_WORDS), lambda i, *_: (i, 0)),
                      pl.BlockSpec((bm // 2, ROW_WORDS), lambda i, *_: (0, 0))],
            out_specs=pl.BlockSpec(memory_space=pl.ANY),
            scratch_shapes=[pltpu.VMEM((2, tm, ROW_WORDS), U32), pltpu.SemaphoreType.DMA((2,))]),
        out_shape=jax.ShapeDtypeStruct((n_rows, ROW_WORDS), U32),
        compiler_params=_cparams(("arbitrary",)),
    )(pstart, counts, combo, rank, hf2d, jnp.zeros((bm // 2, ROW_WORDS), U32))


def _experts_kernel(ea_ref, eb_ref, nused_ref, x_ref, wga_ref, wua_ref, wda_ref, wgb_ref, wub_ref, wdb_ref, y_ref):
    @pl.when(pl.program_id(0) < nused_ref[0])
    def _():
        x = _unpack_bf16_pairs(x_ref[:, :HALF]).astype(BF16)
        gates = pltpu.bitcast(x_ref[:, HALF:], F32)

        def mlp(wg_ref, wu_ref, wd_ref):
            hg = jnp.dot(x, wg_ref[0], preferred_element_type=F32)
            hu = jnp.dot(x, wu_ref[0], preferred_element_type=F32)
            return jnp.dot((_silu(hg) * hu).astype(BF16), wd_ref[0], preferred_element_type=F32)

        y = mlp(wga_ref, wua_ref, wda_ref) * gates[:, 0:1] + mlp(wgb_ref, wub_ref, wdb_ref) * gates[:, 1:2]
        y_ref[...] = _pack_bf16_pairs(y.astype(BF16).astype(F32))


def _experts(xs, blk_ea, blk_eb, nused, w_gate, w_up, w_down, bm):
    nblk = blk_ea.shape[0]
    row = lambda j, ea, eb, nu: (jnp.minimum(j, nu[0] - 1), 0)
    row_out = lambda j, ea, eb, nu: (jnp.where(j < nu[0], j, nblk - 1), 0)
    sel_a = lambda j, ea, eb, nu: (ea[j], 0, 0)
    sel_b = lambda j, ea, eb, nu: (eb[j], 0, 0)
    up = lambda sel: pl.BlockSpec((1, D_MODEL, D_EXPERT), sel)
    down = lambda sel: pl.BlockSpec((1, D_EXPERT, D_MODEL), sel)
    return pl.pallas_call(
        _experts_kernel,
        grid_spec=pltpu.PrefetchScalarGridSpec(
            num_scalar_prefetch=3,
            grid=(nblk,),
            in_specs=[pl.BlockSpec((bm, ROW_WORDS), row),
                      up(sel_a), up(sel_a), down(sel_a), up(sel_b), up(sel_b), down(sel_b)],
            out_specs=pl.BlockSpec((bm, HALF), row_out)),
        out_shape=jax.ShapeDtypeStruct((nblk * bm, HALF), U32),
        compiler_params=_cparams(("arbitrary",)),
    )(blk_ea, blk_eb, nused, xs, w_gate, w_up, w_down, w_gate, w_up, w_down)


def _gather_rows(table, idx):
    n = idx.shape[0]
    info = plsc.get_sparse_core_info()
    workers = info.num_cores * info.num_subcores
    per_worker = n // workers
    assert per_worker * workers == n and per_worker % SC_GATHER_ROWS == 0, (n, workers)
    mesh = plsc.VectorSubcoreMesh(core_axis_name="core", subcore_axis_name="subcore")

    def body(table_hbm, idx_hbm, out_hbm, idx_v, rows_v, sem):
        base = (lax.axis_index("subcore") * info.num_cores + lax.axis_index("core")) * per_worker

        @pl.loop(0, per_worker // SC_GATHER_ROWS)
        def _(j):
            off = pl.multiple_of(base + j * SC_GATHER_ROWS, SC_GATHER_ROWS)
            pltpu.sync_copy(idx_hbm.at[pl.ds(off, SC_GATHER_ROWS)], idx_v)
            pltpu.async_copy(table_hbm.at[idx_v], rows_v, sem).wait()
            pltpu.sync_copy(rows_v, out_hbm.at[pl.ds(off, SC_GATHER_ROWS)])

    return pl.kernel(
        body, mesh=mesh, out_type=jax.ShapeDtypeStruct((n, table.shape[1]), table.dtype),
        scratch_types=[pltpu.VMEM((SC_GATHER_ROWS,), jnp.int32), pltpu.VMEM((SC_GATHER_ROWS, table.shape[1]), table.dtype),
                       pltpu.SemaphoreType.DMA])(table, idx)


def _add_rows_kernel(x_ref, yg_ref, o_ref):
    o_ref[...] = x_ref[...] + _unpack_bf16_pairs(pltpu.bitcast(yg_ref[...], U32))


def _add_inproj_kernel(x2_ref, yg_ref, g_ref, w_ref, x_ref, z_ref, xbc_ref, qkv_ref, u_ref, dt_ref):
    _add_rows_kernel(x2_ref, yg_ref, x_ref)
    _inproj_kernel(x_ref, g_ref, w_ref, z_ref, xbc_ref, qkv_ref, u_ref, dt_ref)


def _combine(x2d, yg, gain=None, w_cat=None):
    n = x2d.shape[0]
    tm = TOKEN_TILE
    tile = lambda w: pl.BlockSpec((tm, w), lambda i: (i, 0))
    if w_cat is None:
        return pl.pallas_call(
            _add_rows_kernel, grid=(n // tm,), in_specs=[tile(D_MODEL), tile(HALF)], out_specs=tile(D_MODEL),
            out_shape=jax.ShapeDtypeStruct((n, D_MODEL), F32), compiler_params=_cparams(("arbitrary",)))(x2d, yg)
    widths = (D_MODEL,) + _INPROJ_WIDTHS
    dtypes = (F32,) + _INPROJ_DTYPES
    return pl.pallas_call(
        _add_inproj_kernel, grid=(n // tm,),
        in_specs=[tile(D_MODEL), tile(HALF), _full((1, D_MODEL)), _full(w_cat.shape)],
        out_specs=[tile(w) for w in widths],
        out_shape=[jax.ShapeDtypeStruct((n, w), d) for w, d in zip(widths, dtypes)],
        compiler_params=_cparams(("arbitrary",)),
    )(x2d, yg, gain.reshape(1, D_MODEL), w_cat)


_PAIR_A = np.array([a for a in range(EXPERTS_PER_GROUP) for _ in range(a + 1, EXPERTS_PER_GROUP)], np.int32)
_PAIR_B = np.array([b for a in range(EXPERTS_PER_GROUP) for b in range(a + 1, EXPERTS_PER_GROUP)], np.int32)


def _expert_block(n):
    return 2 * EXPERT_BLOCK_MIN if n >= 2 * 2 * EXPERT_BLOCK_MIN * N_COMBOS else EXPERT_BLOCK_MIN


def _moe(x2, hf, meta, counts, w_gate, w_up, w_down):
    b, t, _ = x2.shape
    n = b * t
    bm = _expert_block(n)
    nblk = (n + N_COMBOS * (bm - 1) + bm - 1) // bm
    cnt = counts[0, :N_COMBOS].astype(jnp.int32)
    psz = (cnt + bm - 1) // bm * bm
    pend = jnp.cumsum(psz)
    pstart = (pend - psz).astype(jnp.int32)
    nused = jnp.maximum(pend[-1] // bm, 1).astype(jnp.int32).reshape(1)
    blk = jnp.minimum(jnp.arange(nblk, dtype=jnp.int32), nused[0] - 1)
    blk_c = jnp.minimum(jnp.sum(pend[None, :] <= (blk * bm)[:, None], axis=1), N_COMBOS - 1).astype(jnp.int32)
    grp = blk_c // PAIRS_PER_GROUP
    blk_ea = (grp * EXPERTS_PER_GROUP + jnp.asarray(_PAIR_A)[blk_c % PAIRS_PER_GROUP]).astype(jnp.int32)
    blk_eb = (grp * EXPERTS_PER_GROUP + jnp.asarray(_PAIR_B)[blk_c % PAIRS_PER_GROUP]).astype(jnp.int32)
    ids = meta.astype(jnp.int32)
    combo = ids[:, 0:1, :]
    rank = ids[:, 1:2, :]
    xs = _dispatch(hf.reshape(n, ROW_WORDS), combo, rank, pstart, cnt, nblk * bm, bm)
    y = _experts(xs, blk_ea, blk_eb, nused, w_gate, w_up, w_down, bm)
    dest = rank.reshape(n) + jnp.sum(jnp.where(combo.reshape(n, 1) == jnp.arange(N_COMBOS)[None, :], pstart[None, :], 0),
                                     axis=1)
    yg = _gather_rows(lax.bitcast_convert_type(y, jnp.int32), dest.astype(jnp.int32))
    return x2.reshape(n, D_MODEL), yg


def _prep_layer(lp):
    w_in = lp["w_in"]
    c0 = D_INNER + CONV_CH
    c1 = c0 + 2 * SSD_HEADS
    w_cat = jnp.concatenate([w_in[:, :c0], w_in[:, c1:], w_in[:, c0:c1],
                             jnp.zeros((D_MODEL, DT_PAD - 2 * SSD_HEADS), w_in.dtype)], axis=1).astype(BF16)
    pool_bd = jnp.zeros((POOL_WIDTH, POOL_WIDTH), F32)
    for g in range(POOL_GROUPS):
        sl = slice(g * POOL_GROUP_DIM, (g + 1) * POOL_GROUP_DIM)
        pool_bd = pool_bd.at[sl, sl].set(lp["pool_w"][g].astype(F32))
    w_r = jnp.concatenate([lp["w_router_expert"], lp["w_router_group"],
                           jnp.zeros((D_MODEL, LANES - N_EXPERTS - N_EXPERT_GROUPS), F32)], axis=1).astype(F32)
    w_r_hi = w_r.astype(BF16)
    w_r_lo = (w_r - w_r_hi.astype(F32)).astype(BF16)
    b_r = jnp.concatenate([lp["b_router_expert"], lp["b_router_group"],
                           jnp.zeros((LANES - N_EXPERTS - N_EXPERT_GROUPS,), F32)]).reshape(1, LANES).astype(F32)
    row = lambda a, w: a.reshape(1, w).astype(F32)
    return dict(
        norm_mix=lp["norm_mix"], w_cat=w_cat,
        conv_w=lp["conv_w"], conv_b=lp["conv_b"], dt_bias=lp["dt_bias"], a_log=lp["a_log"],
        d_skip=lp["d_skip"], ssd_norm=lp["ssd_norm"],
        na_q_norm=lp["na_q_norm"], na_k_norm=lp["na_k_norm"], na_rpb=lp["na_rpb"],
        pool_bd=pool_bd.astype(BF16), pool_scale=row(lp["pool_scale"], POOL_WIDTH),
        w_out=lp["w_out"].astype(BF16), norm_xa=row(lp["norm_xa"], D_MODEL),
        norm_mem=lp["norm_mem"], w_xq=lp["w_xq"].astype(BF16), w_xkv=lp["w_xkv"].astype(BF16),
        xq_norm=row(jnp.tile(lp["xq_norm"], XA_HEADS), XA_WIDTH), xk_norm=lp["xk_norm"],
        w_xo=lp["w_xo"].astype(BF16), norm_ffn=row(lp["norm_ffn"], D_MODEL),
        w_r_both=jnp.concatenate([w_r_hi, w_r_lo], axis=1), b_r=b_r,
        w_e_gate=lp["w_e_gate"].astype(BF16), w_e_up=lp["w_e_up"].astype(BF16),
        w_e_down=lp["w_e_down"].astype(BF16),
    )


def _layer(x, pending, mem, p, na_bias):
    b, m, _ = mem.shape
    if pending is None:
        t = x.shape[1]
        z, xbc, qkv, u, dt = _inproj(x.reshape(b * t, D_MODEL), p["norm_mix"], p["w_cat"])
    else:
        t = pending[0].shape[0] // b
        x, z, xbc, qkv, u, dt = _combine(*pending, p["norm_mix"], p["w_cat"])
        x = x.reshape(b, t, D_MODEL)
    r3 = lambda a: a.reshape(b, t, a.shape[-1])
    y_ssd = _ssd(r3(z), r3(xbc), r3(dt), p["conv_w"], p["conv_b"], p["dt_bias"], p["a_log"], p["d_skip"],
                 p["ssd_norm"])
    y_na = _natten(r3(qkv), na_bias, p["na_q_norm"], p["na_k_norm"])
    kmem, vmem = _memkv(mem, p["norm_mem"], p["w_xkv"], p["xk_norm"])
    x2, hf, meta, counts = _mixer(x, y_ssd, y_na, r3(u), kmem, vmem, p)
    return _moe(x2, hf, meta, counts, p["w_e_gate"], p["w_e_up"], p["w_e_down"])


_LAYER_KEYS = ("norm_mix", "w_in", "conv_w", "conv_b", "dt_bias", "a_log", "d_skip", "ssd_norm", "na_q_norm",
               "na_k_norm", "na_rpb", "pool_w", "pool_scale", "w_out", "norm_xa", "norm_mem", "w_xq", "w_xkv",
               "xq_norm", "xk_norm", "w_xo", "norm_ffn", "w_router_group", "b_router_group", "w_router_expert",
               "b_router_expert", "w_e_gate", "w_e_up", "w_e_down")


def kernel(x_prompt, x_sample, mem_prompt, mem_sample, norm_mix, w_in, conv_w, conv_b, dt_bias, a_log, d_skip, ssd_norm, na_q_norm, na_k_norm, na_rpb, pool_w, pool_scale, w_out, norm_xa, norm_mem, w_xq, w_xkv, xq_norm, xk_norm, w_xo, norm_ffn, w_router_group, b_router_group, w_router_expert, b_router_expert, w_e_gate, w_e_up, w_e_down):
    stacked = dict(zip(_LAYER_KEYS, (norm_mix, w_in, conv_w, conv_b, dt_bias, a_log, d_skip, ssd_norm, na_q_norm,
                                     na_k_norm, na_rpb, pool_w, pool_scale, w_out, norm_xa, norm_mem, w_xq, w_xkv,
                                     xq_norm, xk_norm, w_xo, norm_ffn, w_router_group, b_router_group,
                                     w_router_expert, b_router_expert, w_e_gate, w_e_up, w_e_down)))
    depth = w_in.shape[0]
    layers = [_prep_layer({k: v[l] for k, v in stacked.items()}) for l in range(depth)]

    bias_cache = {}

    def trunk(x, mem):
        t = x.shape[1]
        pending = None
        for l, lp in enumerate(layers):
            if (l, t) not in bias_cache:
                bias_cache[(l, t)] = _na_bias(lp["na_rpb"], t)
            pending = _layer(x, pending, mem, lp, bias_cache[(l, t)])
            x = None
        return _combine(*pending).reshape(mem.shape[0], t, D_MODEL)

    return trunk(x_prompt, mem_prompt), trunk(x_sample, mem_sample)
```

```python
import functools

import jax
import jax.numpy as jnp
import numpy as np
from jax import lax
from jax.experimental import pallas as pl
from jax.experimental.pallas import tpu as pltpu
from jax.experimental.pallas import tpu_sc as plsc

F32 = jnp.float32
BF16 = jnp.bfloat16
U32 = jnp.uint32
HIGHEST = lax.Precision.HIGHEST

D_MODEL = 1024
GRID_W = 64
EPS = 1e-6
SSD_HEAD_DIM = 64
D_INNER = D_MODEL // 2
SSD_HEADS = D_INNER // SSD_HEAD_DIM
SSD_GROUPS = 2
SSD_STATE = 64
SSD_CHUNK = 128
CONV_W = 4
CONV_CH = D_INNER + 2 * SSD_GROUPS * SSD_STATE
NA_HEADS = 4
NA_HEAD_DIM = D_MODEL // 16
NA_WIDTH = NA_HEADS * NA_HEAD_DIM
NA_MAX_KH = 8
NA_KW = 16
POOL_WINDOWS = (2, 4, 8, 16)
POOL_GROUPS = 4
POOL_WIDTH = D_MODEL - D_INNER - NA_WIDTH
POOL_GROUP_DIM = POOL_WIDTH // POOL_GROUPS
XA_HEADS = 4
XA_HEAD_DIM = D_MODEL // 8
XA_WIDTH = XA_HEADS * XA_HEAD_DIM
N_EXPERT_GROUPS = 4
EXPERTS_PER_GROUP = 8
N_EXPERTS = N_EXPERT_GROUPS * EXPERTS_PER_GROUP
D_EXPERT = D_MODEL // 4
PAIRS_PER_GROUP = EXPERTS_PER_GROUP * (EXPERTS_PER_GROUP - 1) // 2
N_COMBOS = N_EXPERT_GROUPS * PAIRS_PER_GROUP

LANES = 128
BF16_SUBLANES = 16
VMEM_LIMIT_BYTES = 56 * 1024 * 1024

TOKEN_TILE = 512
NA_QUERY_ROWS = 8
NA_SUB_ROWS = 8
NA_SUBS = NA_QUERY_ROWS // NA_SUB_ROWS
NA_KEY_ROWS = NA_SUB_ROWS + NA_MAX_KH
EXPERT_BLOCK_MIN = 128
HALF = D_MODEL // 2
ROW_WORDS = HALF + LANES
SSD_STEP_CHUNKS = 4
DT_PAD = LANES
SC_GATHER_ROWS = 128
NEG_BIG = -1e30


def _cparams(sem):
    return pltpu.CompilerParams(dimension_semantics=sem, vmem_limit_bytes=VMEM_LIMIT_BYTES)


def _sigmoid(x):
    return 1.0 / (1.0 + jnp.exp(-x))


def _silu(x):
    return x * _sigmoid(x)


def _softplus(x):
    return jnp.maximum(x, 0.0) + jnp.log(1.0 + jnp.exp(-jnp.abs(x)))


def _pack_bf16_pairs(v):
    k = v.shape[1] // 2
    bits = pltpu.bitcast(v, U32)
    return (bits[:, :k] >> 16) | (bits[:, k:] & jnp.uint32(0xFFFF0000))


def _unpack_bf16_pairs(w):
    lo = pltpu.bitcast(w << 16, F32)
    hi = pltpu.bitcast(w & jnp.uint32(0xFFFF0000), F32)
    return jnp.concatenate([lo, hi], axis=1)


def _full(shape):
    n = len(shape)
    return pl.BlockSpec(shape, lambda *_: (0,) * n)


def _inproj_kernel(x_ref, g_ref, w_ref, z_ref, xbc_ref, qkv_ref, u_ref, dt_ref):
    x = x_ref[...]
    ms = jnp.mean(x * x, axis=-1, keepdims=True)
    h = (x * lax.rsqrt(ms + EPS) * g_ref[...]).astype(BF16)
    o = 0
    for ref in (z_ref, xbc_ref, qkv_ref, u_ref, dt_ref):
        w = ref.shape[-1]
        ref[...] = jnp.dot(h, w_ref[:, o:o + w], preferred_element_type=F32).astype(ref.dtype)
        o += w


_INPROJ_WIDTHS = (D_INNER, CONV_CH, 3 * NA_WIDTH, POOL_WIDTH, DT_PAD)
_INPROJ_DTYPES = (BF16, BF16, BF16, BF16, F32)


def _inproj(x2d, gain, w_cat):
    n = x2d.shape[0]
    tm = TOKEN_TILE
    widths, dtypes = _INPROJ_WIDTHS, _INPROJ_DTYPES
    return pl.pallas_call(
        _inproj_kernel,
        grid=(n // tm,),
        in_specs=[pl.BlockSpec((tm, D_MODEL), lambda i: (i, 0)),
                  _full((1, D_MODEL)),
                  _full(w_cat.shape)],
        out_specs=[pl.BlockSpec((tm, w), lambda i: (i, 0)) for w in widths],
        out_shape=[jax.ShapeDtypeStruct((n, w), d) for w, d in zip(widths, dtypes)],
        compiler_params=_cparams(("arbitrary",)),
    )(x2d, gain.reshape(1, D_MODEL), w_cat)


def _split3(a):
    hi = a.astype(BF16)
    r = a - hi.astype(F32)
    mid = r.astype(BF16)
    lo = (r - mid.astype(F32)).astype(BF16)
    return hi, mid, lo


def _ssd_kernel(xc_ref, xp_ref, xn_ref, dt_ref, z_ref, cw_ref, cb_ref, dtb_ref, alog_ref, dsk_ref, nrm_ref, emat_ref,
                y_ref, state_ref, yf_ref, xs_c, bc_c, cbm_c, dt_c, *, nblocks):
    L = SSD_CHUNK
    LB = SSD_STEP_CHUNKS * L
    P = SSD_HEAD_DIM
    NS = SSD_STATE
    HG = SSD_HEADS // SSD_GROUPS
    gn = SSD_GROUPS * NS
    j = pl.program_id(1)
    c = jnp.where(j < nblocks, j, 2 * nblocks - 1 - j)
    row0 = pl.multiple_of(c * LB, LB)
    rows = pl.ds(row0, LB)
    lane1 = lax.broadcasted_iota(jnp.int32, (1, LANES), 1)
    lo_half = lane1 < P
    ti = lax.broadcasted_iota(jnp.int32, (L, L), 0)
    si = lax.broadcasted_iota(jnp.int32, (L, L), 1)

    def masked_c(bc):
        return [jnp.where(lane1 // NS == g, bc[:, gn:], 0.0).astype(BF16) for g in range(SSD_GROUPS)]

    def prepare():
        cur = xc_ref[0].astype(F32)
        prev = xp_ref[0].astype(F32)
        nxt = xn_ref[0].astype(F32)
        has_prev = (c > 0).astype(F32)
        has_next = (c < nblocks - 1).astype(F32)
        p_last = prev[BF16_SUBLANES - 1:BF16_SUBLANES, :] * has_prev
        n0 = nxt[0:1, :] * has_next
        n1 = nxt[1:2, :] * has_next
        row = lax.broadcasted_iota(jnp.int32, (LB, 1), 0)
        um1 = jnp.where(row == 0, p_last, pltpu.roll(cur, 1, 0))
        up1 = jnp.where(row == LB - 1, n0, pltpu.roll(cur, LB - 1, 0))
        up2 = jnp.where(row == LB - 2, n0, jnp.where(row == LB - 1, n1, pltpu.roll(cur, LB - 2, 0)))
        cw = cw_ref[...]
        acc = cb_ref[...] + um1 * cw[0:1, :] + cur * cw[1:2, :] + up1 * cw[2:3, :] + up2 * cw[3:4, :]
        xbc = _silu(acc)
        xs = xbc[:, :D_INNER]
        bc = xbc[:, D_INNER:D_INNER + 2 * gn].astype(BF16)
        dt = _softplus(dt_ref[0] + dtb_ref[...])
        xs_c[rows, :] = xs
        bc_c[rows, :] = bc
        dt_c[rows, :] = dt
        ops = []
        for sub in range(SSD_STEP_CHUNKS):
            sl = slice(sub * L, (sub + 1) * L)
            cg = masked_c(bc[sl])
            cb_mat = [lax.dot_general(cg[g], bc[sl, :gn], (((1,), (1,)), ((), ())), preferred_element_type=F32)
                      for g in range(SSD_GROUPS)]
            cbm_c[pl.ds(row0 + sub * L, L), :] = jnp.concatenate(cb_mat, axis=1)
            ops.append((xs[sl], bc[sl, :gn], cg, cb_mat, dt[sl]))
        return ops

    def recall():
        ops = []
        for sub in range(SSD_STEP_CHUNKS):
            r = pl.ds(row0 + sub * L, L)
            bc = bc_c[r, :]
            cbm = cbm_c[r, :]
            ops.append((xs_c[r, :], bc[:, :gn], masked_c(bc), [cbm[:, g * L:(g + 1) * L] for g in range(SSD_GROUPS)],
                        dt_c[r, :]))
        return ops

    def scan_chunk(direction, xs, bfull, cg, cb_mat, dt):
        if direction == 0:
            mask = ti >= si
            edge = L - 1
        else:
            mask = si >= ti
            edge = 0
        la = dt * (-jnp.exp(alog_ref[...]))
        tri = mask.astype(BF16)
        csum = sum(jnp.dot(tri, part, preferred_element_type=F32) for part in _split3(la))
        csum_t = csum.T
        emat = emat_ref[direction]
        colb = sum(jnp.dot(part, emat, preferred_element_type=F32) for part in _split3(csum))
        tot = csum[edge:edge + 1, :]
        e_tot = jnp.exp(tot)
        e_in = jnp.exp(csum)
        e_out = jnp.exp(tot - csum)
        ys = []
        for g in range(SSD_GROUPS):
            s_old = state_ref[g]
            y_off = lax.dot_general(cg[g], s_old.astype(BF16), (((1,), (1,)), ((), ())),
                                    preferred_element_type=F32)
            xw = []
            for pr in range(HG // 2):
                h0 = g * HG + 2 * pr
                l0 = direction * SSD_HEADS + h0

                def col(a, l0=l0):
                    return jnp.where(lo_half, a[:, l0:l0 + 1], a[:, l0 + 1:l0 + 2])

                xdt = xs[:, h0 * P:(h0 + 2) * P] * col(dt)
                y_pair = y_off[:, 2 * pr * P:(2 * pr + 2) * P] * col(e_in)
                for hh, half in ((h0, lo_half), (h0 + 1, jnp.logical_not(lo_half))):
                    ll = direction * SSD_HEADS + hh
                    seg = colb[:, hh * L:(hh + 1) * L] - csum_t[ll:ll + 1, :]
                    dec = jnp.exp(jnp.where(mask, seg, NEG_BIG))
                    m = (cb_mat[g] * dec).astype(BF16)
                    y_pair += jnp.dot(m, jnp.where(half, xdt, 0.0).astype(BF16), preferred_element_type=F32)
                ys.append(y_pair)
                xw.append(xdt * col(e_out))
            xw = jnp.concatenate(xw, axis=1).astype(BF16)
            s_new = lax.dot_general(xw, bfull, (((0,), (0,)), ((), ())), preferred_element_type=F32)
            s_scaled = []
            for hl in range(HG):
                lane = direction * SSD_HEADS + g * HG + hl
                s_scaled.append(s_old[hl * P:(hl + 1) * P, :] * e_tot[:, lane:lane + 1])
            state_ref[g] = jnp.concatenate(s_scaled, axis=0) + s_new
        return jnp.concatenate(ys, axis=1)

    @pl.when(jnp.logical_or(j == 0, j == nblocks))
    def _():
        state_ref[...] = jnp.zeros_like(state_ref)

    @pl.when(j < nblocks)
    def _():
        ops = prepare()
        for sub in range(SSD_STEP_CHUNKS):
            yf_ref[pl.ds(row0 + sub * L, L), :] = scan_chunk(0, *ops[sub])

    @pl.when(j >= nblocks)
    def _():
        ops = recall()
        for sub in reversed(range(SSD_STEP_CHUNKS)):
            sl = slice(sub * L, (sub + 1) * L)
            y = yf_ref[pl.ds(row0 + sub * L, L), :] + scan_chunk(1, *ops[sub]) + dsk_ref[...] * ops[sub][0]
            y = y * _silu(z_ref[0, sl, :].astype(F32))
            gw = D_INNER // SSD_GROUPS
            outs = []
            for g in range(SSD_GROUPS):
                yg = y[:, g * gw:(g + 1) * gw]
                outs.append(yg * lax.rsqrt(jnp.mean(yg * yg, axis=-1, keepdims=True) + EPS))
            y_ref[0, sl, :] = (jnp.concatenate(outs, axis=1) * nrm_ref[...]).astype(y_ref.dtype)


def _ssd(z, xbc, dt, conv_w, conv_b, dt_bias, a_log, d_skip, ssd_norm):
    b, t, _ = z.shape
    L = SSD_CHUNK
    lb = SSD_STEP_CHUNKS * L
    nb = t // lb
    hb = lb // BF16_SUBLANES
    nhalo = t // BF16_SUBLANES

    def blk(j):
        return jnp.where(j < nb, j, 2 * nb - 1 - j)

    pad = DT_PAD - 2 * SSD_HEADS
    dtb = jnp.pad(dt_bias.reshape(1, -1).astype(F32), ((0, 0), (0, pad)))
    alog = jnp.pad(a_log.reshape(1, -1).astype(F32), ((0, 0), (0, pad)))
    dsk = jnp.repeat(d_skip.astype(F32), SSD_HEAD_DIM).reshape(1, D_INNER)
    sel = np.arange(DT_PAD)[None, :, None] == (np.arange(2)[:, None, None] * SSD_HEADS
                                                + np.arange(SSD_HEADS)[None, None, :])
    emat = jnp.asarray(np.repeat(sel, L, axis=2), BF16)
    return pl.pallas_call(
        functools.partial(_ssd_kernel, nblocks=nb),
        grid=(b, 2 * nb),
        in_specs=[
            pl.BlockSpec((1, lb, CONV_CH), lambda i, j: (i, blk(j), 0)),
            pl.BlockSpec((1, BF16_SUBLANES, CONV_CH), lambda i, j: (i, jnp.maximum(blk(j) * hb - 1, 0), 0)),
            pl.BlockSpec((1, BF16_SUBLANES, CONV_CH),
                         lambda i, j: (i, jnp.minimum((blk(j) + 1) * hb, nhalo - 1), 0)),
            pl.BlockSpec((1, lb, DT_PAD), lambda i, j: (i, blk(j), 0)),
            pl.BlockSpec((1, lb, D_INNER), lambda i, j: (i, blk(j), 0)),
            _full((CONV_W, CONV_CH)), _full((1, CONV_CH)), _full((1, DT_PAD)), _full((1, DT_PAD)),
            _full((1, D_INNER)), _full((1, D_INNER)), _full((2, DT_PAD, SSD_HEADS * L)),
        ],
        out_specs=pl.BlockSpec((1, lb, D_INNER), lambda i, j: (i, jnp.where(j < nb, nb - 1, 2 * nb - 1 - j), 0)),
        out_shape=jax.ShapeDtypeStruct((b, t, D_INNER), BF16),
        scratch_shapes=[pltpu.VMEM((SSD_GROUPS, (SSD_HEADS // SSD_GROUPS) * SSD_HEAD_DIM, LANES), F32),
                        pltpu.VMEM((t, D_INNER), F32),
                        pltpu.VMEM((t, D_INNER), F32),
                        pltpu.VMEM((t, 2 * SSD_GROUPS * SSD_STATE), BF16),
                        pltpu.VMEM((t, SSD_GROUPS * L), F32),
                        pltpu.VMEM((t, DT_PAD), F32)],
        compiler_params=_cparams(("arbitrary", "arbitrary")),
    )(xbc, xbc, xbc, dt, z, conv_w.astype(F32), conv_b.reshape(1, CONV_CH).astype(F32), dtb, alog, dsk,
      ssd_norm.reshape(1, D_INNER).astype(F32), emat)


def _na_bias(rpb, t):
    r = t // GRID_W
    kh = min(NA_MAX_KH, r)
    nsb = r // NA_SUB_ROWS
    rows = np.arange(r)
    row_start = np.clip(rows - NA_MAX_KH // 2, 0, r - kh)
    r0 = np.arange(nsb) * NA_SUB_ROWS
    kr0 = np.clip(r0 - NA_MAX_KH // 2, 0, r - NA_KEY_ROWS)
    qrow = r0[:, None] + np.arange(NA_SUB_ROWS)[None, :]
    krow = kr0[:, None] + np.arange(NA_KEY_ROWS)[None, :]
    rs = row_start[qrow]
    row_ok = (krow[:, None, :] >= rs[:, :, None]) & (krow[:, None, :] < rs[:, :, None] + kh)
    dr = np.clip(krow[:, None, :] - qrow[:, :, None] + (NA_MAX_KH - 1), 0, 2 * NA_MAX_KH - 2)
    cols = np.arange(GRID_W)
    col_start = np.clip(cols - NA_KW // 2, 0, GRID_W - NA_KW)
    col_ok = (cols[None, :] >= col_start[:, None]) & (cols[None, :] < col_start[:, None] + NA_KW)
    dc = np.clip(cols[None, :] - cols[:, None] + (NA_KW - 1), 0, 2 * NA_KW - 2)
    sel_c = (dc[..., None] == np.arange(2 * NA_KW - 1)) & col_ok[..., None]
    nr = 2 * NA_MAX_KH - 1
    tile_r = jnp.einsum("hrc,xyc->hrxy", rpb.astype(F32), jnp.asarray(sel_c, F32), precision=HIGHEST)
    tile_r = jnp.where(jnp.asarray(col_ok)[None, None], tile_r, NEG_BIG)
    tile_r = jnp.concatenate([tile_r, jnp.full((NA_HEADS, 1, GRID_W, GRID_W), NEG_BIG, F32)], axis=1)
    kp = NA_KEY_ROWS // 2
    code = np.where(row_ok, dr, nr).reshape(nsb * NA_SUB_ROWS * kp, 2)
    pairs, inv = np.unique(code, axis=0, return_inverse=True)
    blocks = jnp.concatenate([tile_r[:, pairs[:, 0]], tile_r[:, pairs[:, 1]]], axis=-1)
    blocks = jnp.moveaxis(blocks, 0, 1).reshape(len(pairs), NA_HEADS * GRID_W * 2 * GRID_W)
    onehot = jnp.asarray(inv.reshape(-1, 1) == np.arange(len(pairs))[None, :], F32)
    bias = jnp.dot(onehot, blocks, precision=HIGHEST)
    return bias.reshape(nsb // NA_SUBS, NA_SUBS, NA_SUB_ROWS, kp, NA_HEADS, GRID_W, 2 * GRID_W).astype(BF16)


def _natten_kernel(qkv_ref, bias_ref, qg_ref, kg_ref, seg_ref, o_ref, *, grid_rows):
    nq = NA_SUB_ROWS * GRID_W
    nk = NA_KEY_ROWS * GRID_W
    rb = pl.program_id(0)
    seg = seg_ref[...]
    lane_h = lax.broadcasted_iota(jnp.int32, (1, NA_WIDTH), 1) // NA_HEAD_DIM
    for sub in range(NA_SUBS):
        r0 = rb * NA_QUERY_ROWS + sub * NA_SUB_ROWS
        kr0 = jnp.clip(r0 - NA_MAX_KH // 2, 0, grid_rows - NA_KEY_ROWS)
        q0 = pl.multiple_of(r0 * GRID_W, nq)
        k0 = pl.multiple_of(kr0 * GRID_W, NA_MAX_KH // 2 * GRID_W)
        q = qkv_ref[0, pl.ds(q0, nq), 0:NA_WIDTH].astype(F32)
        k = qkv_ref[0, pl.ds(k0, nk), NA_WIDTH:2 * NA_WIDTH].astype(F32)
        v = qkv_ref[0, pl.ds(k0, nk), 2 * NA_WIDTH:3 * NA_WIDTH]
        qms = jnp.dot(q * q, seg, precision=HIGHEST, preferred_element_type=F32)
        kms = jnp.dot(k * k, seg, precision=HIGHEST, preferred_element_type=F32)
        qn = q * lax.rsqrt(qms + EPS) * (qg_ref[...] * NA_HEAD_DIM ** -0.5)
        kn = (k * lax.rsqrt(kms + EPS) * kg_ref[...]).astype(BF16)
        acc = jnp.zeros((nq, NA_WIDTH), F32)
        for h in range(NA_HEADS):
            hm = lane_h == h
            s = lax.dot_general(jnp.where(hm, qn, 0.0).astype(BF16), kn, (((1,), (1,)), ((), ())),
                                preferred_element_type=F32)
            bias = jnp.concatenate(
                [jnp.concatenate([bias_ref[0, sub, qr, kc, h] for kc in range(NA_KEY_ROWS // 2)], axis=1)
                 for qr in range(NA_SUB_ROWS)], axis=0)
            s = s + bias.astype(F32)
            p = jnp.exp(s - jnp.max(s, axis=-1, keepdims=True))
            l = jnp.sum(p, axis=-1, keepdims=True)
            o = jnp.dot(p.astype(BF16), v, preferred_element_type=F32)
            acc += jnp.where(hm, o / l, 0.0)
        o_ref[0, sub * nq:(sub + 1) * nq, :] = acc.astype(o_ref.dtype)


def _natten(qkv, bias, q_norm, k_norm):
    b, t, _ = qkv.shape
    r = t // GRID_W
    nrb = r // NA_QUERY_ROWS
    nq = NA_QUERY_ROWS * GRID_W
    head = jnp.arange(NA_WIDTH) // NA_HEAD_DIM
    seg = (head[:, None] == head[None, :]).astype(F32) / NA_HEAD_DIM
    return pl.pallas_call(
        functools.partial(_natten_kernel, grid_rows=r),
        grid=(nrb, b),
        in_specs=[pl.BlockSpec((1, t, 3 * NA_WIDTH), lambda i, j: (j, 0, 0)),
                  pl.BlockSpec((1,) + bias.shape[1:], lambda i, j: (i,) + (0,) * (bias.ndim - 1)),
                  _full((1, NA_WIDTH)), _full((1, NA_WIDTH)), _full((NA_WIDTH, NA_WIDTH))],
        out_specs=pl.BlockSpec((1, nq, NA_WIDTH), lambda i, j: (j, i, 0)),
        out_shape=jax.ShapeDtypeStruct((b, t, NA_WIDTH), BF16),
        compiler_params=_cparams(("arbitrary", "arbitrary")),
    )(qkv, bias, jnp.tile(q_norm.astype(F32), NA_HEADS).reshape(1, NA_WIDTH),
      jnp.tile(k_norm.astype(F32), NA_HEADS).reshape(1, NA_WIDTH), seg)


def _memkv_kernel(m_ref, g_ref, w_ref, kg_ref, k_ref, v_ref):
    x = m_ref[0]
    ms = jnp.mean(x * x, axis=-1, keepdims=True)
    h = (x * lax.rsqrt(ms + EPS) * g_ref[...]).astype(BF16)
    kv = jnp.dot(h, w_ref[...], preferred_element_type=F32)
    ks = []
    for hd in range(XA_HEADS):
        kh = kv[:, hd * XA_HEAD_DIM:(hd + 1) * XA_HEAD_DIM]
        ks.append(kh * lax.rsqrt(jnp.mean(kh * kh, axis=-1, keepdims=True) + EPS))
    k_ref[0] = (jnp.concatenate(ks, axis=1) * kg_ref[...]).astype(k_ref.dtype)
    v_ref[0] = kv[:, XA_WIDTH:].astype(v_ref.dtype)


def _memkv(mem, norm_mem, w_xkv, xk_norm):
    b, m, _ = mem.shape
    return pl.pallas_call(
        _memkv_kernel,
        grid=(b,),
        in_specs=[pl.BlockSpec((1, m, D_MODEL), lambda i: (i, 0, 0)),
                  _full((1, D_MODEL)), _full((D_MODEL, 2 * XA_WIDTH)), _full((1, XA_WIDTH))],
        out_specs=[pl.BlockSpec((1, m, XA_WIDTH), lambda i: (i, 0, 0))] * 2,
        out_shape=[jax.ShapeDtypeStruct((b, m, XA_WIDTH), BF16)] * 2,
        compiler_params=_cparams(("arbitrary",)),
    )(mem, norm_mem.reshape(1, D_MODEL).astype(F32), w_xkv,
      jnp.tile(xk_norm.astype(F32), XA_HEADS).reshape(1, XA_WIDTH))


def _mixer_kernel(x_ref, ys_ref, yn_ref, u_ref, up_ref, un_ref, k_ref, v_ref,
                  pw_ref, psc_ref, wo_ref, gxa_ref, wq_ref, qg_ref, wxo_ref, gff_ref,
                  wrb_ref, br_ref, ltri_ref,
                  x2_ref, hf_ref, meta_ref, cnt_ref, carry_ref, *, seq):
    tm = x_ref.shape[1]
    halo = BF16_SUBLANES
    bi = pl.program_id(0)
    i = pl.program_id(1)
    nt = pl.num_programs(1)

    @pl.when(jnp.logical_and(bi == 0, i == 0))
    def _():
        carry_ref[...] = jnp.zeros_like(carry_ref)

    u = u_ref[0].astype(F32)
    up = up_ref[0].astype(F32) * (i > 0).astype(F32)
    un = un_ref[0].astype(F32) * (i < nt - 1).astype(F32)
    cat = jnp.concatenate([up, u, un], axis=0)
    n = tm + 2 * halo

    def sh(a, k):
        return pltpu.roll(a, (-k) % n, 0)

    a2 = cat + sh(cat, -1)
    a4 = sh(a2, 1) + sh(a2, -1)
    a8 = sh(a4, 2) + sh(a4, -2)
    a16 = sh(a8, 4) + sh(a8, -4)
    lane_g = lax.broadcasted_iota(jnp.int32, (1, POOL_WIDTH), 1) // POOL_GROUP_DIM
    wsum = jnp.where(lane_g == 0, a2, jnp.where(lane_g == 1, a4, jnp.where(lane_g == 2, a8, a16)))
    wsum = wsum[halo:halo + tm, :]
    half = jnp.where(lane_g == 0, POOL_WINDOWS[0] // 2,
                     jnp.where(lane_g == 1, POOL_WINDOWS[1] // 2,
                               jnp.where(lane_g == 2, POOL_WINDOWS[2] // 2, POOL_WINDOWS[3] // 2)))
    tpos = i * tm + lax.broadcasted_iota(jnp.int32, (tm, 1), 0)
    cnt = (jnp.minimum(tpos + half, seq) - jnp.maximum(tpos - half, 0)).astype(F32)
    d = wsum / cnt - u
    ypool = jnp.dot(d.astype(BF16), pw_ref[...], preferred_element_type=F32) * psc_ref[...]

    mix = jnp.dot(ys_ref[0], wo_ref[0:D_INNER, :], preferred_element_type=F32)
    mix += jnp.dot(yn_ref[0], wo_ref[D_INNER:D_INNER + NA_WIDTH, :], preferred_element_type=F32)
    mix += jnp.dot(ypool.astype(BF16), wo_ref[D_INNER + NA_WIDTH:, :], preferred_element_type=F32)
    x1 = x_ref[0] + mix

    hn = (x1 * lax.rsqrt(jnp.mean(x1 * x1, axis=-1, keepdims=True) + EPS) * gxa_ref[...]).astype(BF16)
    q = jnp.dot(hn, wq_ref[...], preferred_element_type=F32)
    kk = k_ref[0]
    vv = v_ref[0]
    scale = XA_HEAD_DIM ** -0.5
    outs = []
    for hd in range(XA_HEADS):
        sl = slice(hd * XA_HEAD_DIM, (hd + 1) * XA_HEAD_DIM)
        qh = q[:, sl]
        qh = (qh * lax.rsqrt(jnp.mean(qh * qh, axis=-1, keepdims=True) + EPS) * qg_ref[:, sl]).astype(BF16)
        s = lax.dot_general(qh, kk[:, sl], (((1,), (1,)), ((), ())), preferred_element_type=F32) * scale
        p = jnp.exp(s - jnp.max(s, axis=-1, keepdims=True))
        l = jnp.sum(p, axis=-1, keepdims=True)
        outs.append(jnp.dot(p.astype(BF16), vv[:, sl], preferred_element_type=F32) / l)
    att = jnp.concatenate(outs, axis=1).astype(BF16)
    x2 = x1 + jnp.dot(att, wxo_ref[...], preferred_element_type=F32)
    x2_ref[0] = x2

    hf = x2 * lax.rsqrt(jnp.mean(x2 * x2, axis=-1, keepdims=True) + EPS) * gff_ref[...]
    h_hi = hf.astype(BF16)
    hf_ref[0, :, :HALF] = pltpu.bitcast(_pack_bf16_pairs(h_hi.astype(F32)), jnp.int32)
    h_lo = (hf - h_hi.astype(F32)).astype(BF16)
    both = jnp.dot(h_hi, wrb_ref[...], preferred_element_type=F32)
    logits = (both[:, :LANES] + both[:, LANES:]
              + jnp.dot(h_lo, wrb_ref[:, :LANES], preferred_element_type=F32)) + br_ref[...]
    lane = lax.broadcasted_iota(jnp.int32, (1, LANES), 1)
    lane_f = lane.astype(F32)
    lane_grp = (lane // EXPERTS_PER_GROUP).astype(F32)
    is_g = jnp.logical_and(lane >= N_EXPERTS, lane < N_EXPERTS + N_EXPERT_GROUPS)
    gl = jnp.where(is_g, logits, NEG_BIG)
    gmax = jnp.max(gl, axis=-1, keepdims=True)
    g_sel = jnp.min(jnp.where(gl == gmax, lane_f, float(LANES)), axis=-1, keepdims=True) - N_EXPERTS
    g_gate = 1.0 / jnp.sum(jnp.where(is_g, jnp.exp(gl - gmax), 0.0), axis=-1, keepdims=True)
    in_grp = jnp.logical_and(lane < N_EXPERTS, lane_grp == g_sel)
    el = jnp.where(in_grp, logits, NEG_BIG)
    v1 = jnp.max(el, axis=-1, keepdims=True)
    e0 = jnp.min(jnp.where(el == v1, lane_f, float(LANES)), axis=-1, keepdims=True)
    el2 = jnp.where(lane_f == e0, NEG_BIG, el)
    v2 = jnp.max(el2, axis=-1, keepdims=True)
    e1 = jnp.min(jnp.where(el2 == v2, lane_f, float(LANES)), axis=-1, keepdims=True)
    w1 = jnp.exp(v2 - v1)
    gate0 = g_gate / (1.0 + w1)
    gate1 = g_gate * w1 / (1.0 + w1)

    base = g_sel * EXPERTS_PER_GROUP
    ea = jnp.minimum(e0, e1) - base
    eb = jnp.maximum(e0, e1) - base
    combo = g_sel * PAIRS_PER_GROUP + ea * EXPERTS_PER_GROUP - ea * (ea + 1.0) * 0.5 + (eb - ea - 1.0)
    gate_a = jnp.where(e0 < e1, gate0, gate1)
    gate_b = jnp.where(e0 < e1, gate1, gate0)
    hf_ref[0, :, HALF:] = pltpu.bitcast(jnp.where(lane == 0, gate_a, 0.0) + jnp.where(lane == 1, gate_b, 0.0), jnp.int32)

    oh = lane_f == combo
    cnt_tok = oh.astype(F32)
    before = jnp.dot(ltri_ref[...], cnt_tok.astype(BF16), preferred_element_type=F32) + carry_ref[0:1, :]
    rank = jnp.sum(jnp.where(oh, before, 0.0), axis=-1, keepdims=True)
    new_carry = carry_ref[0:1, :] + jnp.sum(cnt_tok, axis=0, keepdims=True)
    carry_ref[...] = jnp.broadcast_to(new_carry, carry_ref.shape)
    cnt_ref[...] = jnp.broadcast_to(new_carry, cnt_ref.shape)

    slab = jnp.where(lane == 0, combo, 0.0) + jnp.where(lane == 1, rank, 0.0)
    meta_ref[0] = slab.T[0:8, :]


def _mixer(x, y_ssd, y_na, u, kmem, vmem, p):
    b, t, _ = x.shape
    tm = TOKEN_TILE
    nt = t // tm
    hb = tm // BF16_SUBLANES
    nhalo = t // BF16_SUBLANES
    ltri = (jnp.arange(tm)[:, None] > jnp.arange(tm)[None, :]).astype(BF16)
    tok = lambda w: pl.BlockSpec((1, tm, w), lambda i, j: (i, j, 0))
    mem = pl.BlockSpec((1, kmem.shape[1], XA_WIDTH), lambda i, j: (i, 0, 0))
    weights = (p["pool_bd"], p["pool_scale"], p["w_out"], p["norm_xa"], p["w_xq"], p["xq_norm"], p["w_xo"],
               p["norm_ffn"], p["w_r_both"], p["b_r"], ltri)
    return pl.pallas_call(
        functools.partial(_mixer_kernel, seq=t),
        grid=(b, nt),
        in_specs=[tok(D_MODEL), tok(D_INNER), tok(NA_WIDTH), tok(POOL_WIDTH),
                  pl.BlockSpec((1, BF16_SUBLANES, POOL_WIDTH), lambda i, j: (i, jnp.maximum(j * hb - 1, 0), 0)),
                  pl.BlockSpec((1, BF16_SUBLANES, POOL_WIDTH),
                               lambda i, j: (i, jnp.minimum((j + 1) * hb, nhalo - 1), 0)),
                  mem, mem] + [_full(w.shape) for w in weights],
        out_specs=[tok(D_MODEL), tok(ROW_WORDS),
                   pl.BlockSpec((1, 8, tm), lambda i, j: (i * nt + j, 0, 0)),
                   pl.BlockSpec((8, LANES), lambda i, j: (0, 0))],
        out_shape=[jax.ShapeDtypeStruct((b, t, D_MODEL), F32),
                   jax.ShapeDtypeStruct((b, t, ROW_WORDS), jnp.int32),
                   jax.ShapeDtypeStruct((b * nt, 8, tm), F32),
                   jax.ShapeDtypeStruct((8, LANES), F32)],
        scratch_shapes=[pltpu.VMEM((8, LANES), F32)],
        compiler_params=_cparams(("arbitrary", "arbitrary")),
    )(x, y_ssd, y_na, u, u, u, kmem, vmem, *weights)


def _zero_pad_kernel(pstart_ref, cnt_ref, xs_in, zero_ref, xs_ref, sem, *, bm):
    del xs_in

    def fill(cm, wait):
        n_c = cnt_ref[cm]
        npad = (bm - n_c % bm) % bm
        head = (8 - n_c % 8) % 8
        base = pstart_ref[cm] + n_c

        def go(cp):
            if wait:
                cp.wait()
            else:
                cp.start()

        def one(r, _):
            go(pltpu.make_async_copy(zero_ref.at[pl.ds(0, 1)], xs_ref.at[pl.ds(base + r, 1)], sem))
            return 0

        lax.fori_loop(0, head, one, 0)
        rest = npad - head
        off = base + head
        size = bm // 2
        while size >= 8:
            @pl.when(rest & size != 0)
            def _(off=off, size=size):
                go(pltpu.make_async_copy(zero_ref.at[pl.ds(0, size)],
                                         xs_ref.at[pl.ds(pl.multiple_of(off, 8), size)], sem))

            off = off + (rest & size)
            size //= 2
        return 0

    lax.fori_loop(0, N_COMBOS, lambda cm, _: fill(cm, False), 0)
    lax.fori_loop(0, N_COMBOS, lambda cm, _: fill(cm, True), 0)


def _zero_pad_rows(xs, pstart, counts, bm):
    return pl.pallas_call(
        functools.partial(_zero_pad_kernel, bm=bm),
        grid_spec=pltpu.PrefetchScalarGridSpec(
            num_scalar_prefetch=2,
            grid=(1,),
            in_specs=[pl.BlockSpec(memory_space=pl.ANY),
                      pl.BlockSpec((bm // 2, ROW_WORDS), lambda i, *_: (0, 0))],
            out_specs=pl.BlockSpec(memory_space=pl.ANY),
            scratch_shapes=[pltpu.SemaphoreType.DMA(())]),
        out_shape=jax.ShapeDtypeStruct(xs.shape, xs.dtype),
        input_output_aliases={2: 0},
        compiler_params=_cparams(("arbitrary",)),
    )(pstart, counts, xs, jnp.zeros((bm // 2, ROW_WORDS), xs.dtype))


def _sc_rows(kind, src, idx, n_out):
    n = idx.shape[0]
    width = src.shape[1]
    info = plsc.get_sparse_core_info()
    workers = info.num_cores * info.num_subcores
    per_worker = n // workers
    assert per_worker * workers == n and per_worker % SC_GATHER_ROWS == 0, (n, workers)
    mesh = plsc.VectorSubcoreMesh(core_axis_name="core", subcore_axis_name="subcore")

    def body(src_hbm, idx_hbm, out_hbm, idx_v, rows_v, sem):
        base = (lax.axis_index("subcore") * info.num_cores + lax.axis_index("core")) * per_worker

        @pl.loop(0, per_worker // SC_GATHER_ROWS)
        def _(j):
            off = pl.multiple_of(base + j * SC_GATHER_ROWS, SC_GATHER_ROWS)
            pltpu.sync_copy(idx_hbm.at[pl.ds(off, SC_GATHER_ROWS)], idx_v)
            if kind == "gather":
                pltpu.async_copy(src_hbm.at[idx_v], rows_v, sem).wait()
                pltpu.sync_copy(rows_v, out_hbm.at[pl.ds(off, SC_GATHER_ROWS)])
            else:
                pltpu.sync_copy(src_hbm.at[pl.ds(off, SC_GATHER_ROWS)], rows_v)
                pltpu.async_copy(rows_v, out_hbm.at[idx_v], sem).wait()

    return pl.kernel(
        body, mesh=mesh, out_type=jax.ShapeDtypeStruct((n_out, width), src.dtype),
        scratch_types=[pltpu.VMEM((SC_GATHER_ROWS,), jnp.int32), pltpu.VMEM((SC_GATHER_ROWS, width), src.dtype),
                       pltpu.SemaphoreType.DMA])(src, idx)


def _experts_kernel(ea_ref, eb_ref, nused_ref, x_ref, wga_ref, wua_ref, wda_ref, wgb_ref, wub_ref, wdb_ref, y_ref):
    @pl.when(pl.program_id(0) < nused_ref[0])
    def _():
        x = _unpack_bf16_pairs(pltpu.bitcast(x_ref[:, :HALF], U32)).astype(BF16)
        gates = pltpu.bitcast(x_ref[:, HALF:], F32)

        def mlp(wg_ref, wu_ref, wd_ref):
            hg = jnp.dot(x, wg_ref[0], preferred_element_type=F32)
            hu = jnp.dot(x, wu_ref[0], preferred_element_type=F32)
            return jnp.dot((_silu(hg) * hu).astype(BF16), wd_ref[0], preferred_element_type=F32)

        y = mlp(wga_ref, wua_ref, wda_ref) * gates[:, 0:1] + mlp(wgb_ref, wub_ref, wdb_ref) * gates[:, 1:2]
        y_ref[...] = pltpu.bitcast(_pack_bf16_pairs(y.astype(BF16).astype(F32)), jnp.int32)


def _experts(xs, blk_ea, blk_eb, nused, w_gate, w_up, w_down, bm):
    nblk = blk_ea.shape[0]
    row = lambda j, ea, eb, nu: (jnp.minimum(j, nu[0] - 1), 0)
    row_out = lambda j, ea, eb, nu: (jnp.where(j < nu[0], j, nblk - 1), 0)
    sel_a = lambda j, ea, eb, nu: (ea[j], 0, 0)
    sel_b = lambda j, ea, eb, nu: (eb[j], 0, 0)
    up = lambda sel: pl.BlockSpec((1, D_MODEL, D_EXPERT), sel)
    down = lambda sel: pl.BlockSpec((1, D_EXPERT, D_MODEL), sel)
    return pl.pallas_call(
        _experts_kernel,
        grid_spec=pltpu.PrefetchScalarGridSpec(
            num_scalar_prefetch=3,
            grid=(nblk,),
            in_specs=[pl.BlockSpec((bm, ROW_WORDS), row),
                      up(sel_a), up(sel_a), down(sel_a), up(sel_b), up(sel_b), down(sel_b)],
            out_specs=pl.BlockSpec((bm, HALF), row_out)),
        out_shape=jax.ShapeDtypeStruct((nblk * bm, HALF), jnp.int32),
        compiler_params=_cparams(("arbitrary",)),
    )(blk_ea, blk_eb, nused, xs, w_gate, w_up, w_down, w_gate, w_up, w_down)


def _add_rows_kernel(x_ref, yg_ref, o_ref):
    o_ref[...] = x_ref[...] + _unpack_bf16_pairs(pltpu.bitcast(yg_ref[...], U32))


def _add_inproj_kernel(x2_ref, yg_ref, g_ref, w_ref, x_ref, z_ref, xbc_ref, qkv_ref, u_ref, dt_ref):
    _add_rows_kernel(x2_ref, yg_ref, x_ref)
    _inproj_kernel(x_ref, g_ref, w_ref, z_ref, xbc_ref, qkv_ref, u_ref, dt_ref)


def _combine(x2d, yg, gain=None, w_cat=None):
    n = x2d.shape[0]
    tm = TOKEN_TILE
    tile = lambda w: pl.BlockSpec((tm, w), lambda i: (i, 0))
    if w_cat is None:
        return pl.pallas_call(
            _add_rows_kernel, grid=(n // tm,), in_specs=[tile(D_MODEL), tile(HALF)], out_specs=tile(D_MODEL),
            out_shape=jax.ShapeDtypeStruct((n, D_MODEL), F32), compiler_params=_cparams(("arbitrary",)))(x2d, yg)
    widths = (D_MODEL,) + _INPROJ_WIDTHS
    dtypes = (F32,) + _INPROJ_DTYPES
    return pl.pallas_call(
        _add_inproj_kernel, grid=(n // tm,),
        in_specs=[tile(D_MODEL), tile(HALF), _full((1, D_MODEL)), _full(w_cat.shape)],
        out_specs=[tile(w) for w in widths],
        out_shape=[jax.ShapeDtypeStruct((n, w), d) for w, d in zip(widths, dtypes)],
        compiler_params=_cparams(("arbitrary",)),
    )(x2d, yg, gain.reshape(1, D_MODEL), w_cat)


_PAIR_A = np.array([a for a in range(EXPERTS_PER_GROUP) for _ in range(a + 1, EXPERTS_PER_GROUP)], np.int32)
_PAIR_B = np.array([b for a in range(EXPERTS_PER_GROUP) for b in range(a + 1, EXPERTS_PER_GROUP)], np.int32)


def _expert_block(n):
    return 2 * EXPERT_BLOCK_MIN if n >= 2 * 2 * EXPERT_BLOCK_MIN * N_COMBOS else EXPERT_BLOCK_MIN


def _moe(x2, hf, meta, counts, w_gate, w_up, w_down):
    b, t, _ = x2.shape
    n = b * t
    bm = _expert_block(n)
    nblk = (n + N_COMBOS * (bm - 1) + bm - 1) // bm
    cnt = counts[0, :N_COMBOS].astype(jnp.int32)
    psz = (cnt + bm - 1) // bm * bm
    pend = jnp.cumsum(psz)
    pstart = (pend - psz).astype(jnp.int32)
    nused = jnp.maximum(pend[-1] // bm, 1).astype(jnp.int32).reshape(1)
    blk = jnp.minimum(jnp.arange(nblk, dtype=jnp.int32), nused[0] - 1)
    blk_c = jnp.minimum(jnp.sum(pend[None, :] <= (blk * bm)[:, None], axis=1), N_COMBOS - 1).astype(jnp.int32)
    grp = blk_c // PAIRS_PER_GROUP
    blk_ea = (grp * EXPERTS_PER_GROUP + jnp.asarray(_PAIR_A)[blk_c % PAIRS_PER_GROUP]).astype(jnp.int32)
    blk_eb = (grp * EXPERTS_PER_GROUP + jnp.asarray(_PAIR_B)[blk_c % PAIRS_PER_GROUP]).astype(jnp.int32)
    ids = meta.astype(jnp.int32)
    combo = ids[:, 0, :].reshape(n)
    dest = ids[:, 1, :].reshape(n) + jnp.sum(jnp.where(combo[:, None] == jnp.arange(N_COMBOS)[None, :], pstart[None, :], 0),
                                             axis=1)
    xs = _sc_rows("scatter", hf.reshape(n, ROW_WORDS), dest, nblk * bm)
    xs = _zero_pad_rows(xs, pstart, cnt, bm)
    y = _experts(xs, blk_ea, blk_eb, nused, w_gate, w_up, w_down, bm)
    yg = _sc_rows("gather", y, dest, n)
    return x2.reshape(n, D_MODEL), yg


def _prep_layer(lp):
    w_in = lp["w_in"]
    c0 = D_INNER + CONV_CH
    c1 = c0 + 2 * SSD_HEADS
    w_cat = jnp.concatenate([w_in[:, :c0], w_in[:, c1:], w_in[:, c0:c1],
                             jnp.zeros((D_MODEL, DT_PAD - 2 * SSD_HEADS), w_in.dtype)], axis=1).astype(BF16)
    pool_bd = jnp.zeros((POOL_WIDTH, POOL_WIDTH), F32)
    for g in range(POOL_GROUPS):
        sl = slice(g * POOL_GROUP_DIM, (g + 1) * POOL_GROUP_DIM)
        pool_bd = pool_bd.at[sl, sl].set(lp["pool_w"][g].astype(F32))
    w_r = jnp.concatenate([lp["w_router_expert"], lp["w_router_group"],
                           jnp.zeros((D_MODEL, LANES - N_EXPERTS - N_EXPERT_GROUPS), F32)], axis=1).astype(F32)
    w_r_hi = w_r.astype(BF16)
    w_r_lo = (w_r - w_r_hi.astype(F32)).astype(BF16)
    b_r = jnp.concatenate([lp["b_router_expert"], lp["b_router_group"],
                           jnp.zeros((LANES - N_EXPERTS - N_EXPERT_GROUPS,), F32)]).reshape(1, LANES).astype(F32)
    row = lambda a, w: a.reshape(1, w).astype(F32)
    return dict(
        norm_mix=lp["norm_mix"], w_cat=w_cat,
        conv_w=lp["conv_w"], conv_b=lp["conv_b"], dt_bias=lp["dt_bias"], a_log=lp["a_log"],
        d_skip=lp["d_skip"], ssd_norm=lp["ssd_norm"],
        na_q_norm=lp["na_q_norm"], na_k_norm=lp["na_k_norm"], na_rpb=lp["na_rpb"],
        pool_bd=pool_bd.astype(BF16), pool_scale=row(lp["pool_scale"], POOL_WIDTH),
        w_out=lp["w_out"].astype(BF16), norm_xa=row(lp["norm_xa"], D_MODEL),
        norm_mem=lp["norm_mem"], w_xq=lp["w_xq"].astype(BF16), w_xkv=lp["w_xkv"].astype(BF16),
        xq_norm=row(jnp.tile(lp["xq_norm"], XA_HEADS), XA_WIDTH), xk_norm=lp["xk_norm"],
        w_xo=lp["w_xo"].astype(BF16), norm_ffn=row(lp["norm_ffn"], D_MODEL),
        w_r_both=jnp.concatenate([w_r_hi, w_r_lo], axis=1), b_r=b_r,
        w_e_gate=lp["w_e_gate"].astype(BF16), w_e_up=lp["w_e_up"].astype(BF16),
        w_e_down=lp["w_e_down"].astype(BF16),
    )


def _layer(x, pending, mem, p, na_bias):
    b, m, _ = mem.shape
    if pending is None:
        t = x.shape[1]
        z, xbc, qkv, u, dt = _inproj(x.reshape(b * t, D_MODEL), p["norm_mix"], p["w_cat"])
    else:
        t = pending[0].shape[0] // b
        x, z, xbc, qkv, u, dt = _combine(*pending, p["norm_mix"], p["w_cat"])
        x = x.reshape(b, t, D_MODEL)
    r3 = lambda a: a.reshape(b, t, a.shape[-1])
    y_ssd = _ssd(r3(z), r3(xbc), r3(dt), p["conv_w"], p["conv_b"], p["dt_bias"], p["a_log"], p["d_skip"],
                 p["ssd_norm"])
    y_na = _natten(r3(qkv), na_bias, p["na_q_norm"], p["na_k_norm"])
    kmem, vmem = _memkv(mem, p["norm_mem"], p["w_xkv"], p["xk_norm"])
    x2, hf, meta, counts = _mixer(x, y_ssd, y_na, r3(u), kmem, vmem, p)
    return _moe(x2, hf, meta, counts, p["w_e_gate"], p["w_e_up"], p["w_e_down"])


_LAYER_KEYS = ("norm_mix", "w_in", "conv_w", "conv_b", "dt_bias", "a_log", "d_skip", "ssd_norm", "na_q_norm",
               "na_k_norm", "na_rpb", "pool_w", "pool_scale", "w_out", "norm_xa", "norm_mem", "w_xq", "w_xkv",
               "xq_norm", "xk_norm", "w_xo", "norm_ffn", "w_router_group", "b_router_group", "w_router_expert",
               "b_router_expert", "w_e_gate", "w_e_up", "w_e_down")


def kernel(x_prompt, x_sample, mem_prompt, mem_sample, norm_mix, w_in, conv_w, conv_b, dt_bias, a_log, d_skip, ssd_norm, na_q_norm, na_k_norm, na_rpb, pool_w, pool_scale, w_out, norm_xa, norm_mem, w_xq, w_xkv, xq_norm, xk_norm, w_xo, norm_ffn, w_router_group, b_router_group, w_router_expert, b_router_expert, w_e_gate, w_e_up, w_e_down):
    stacked = dict(zip(_LAYER_KEYS, (norm_mix, w_in, conv_w, conv_b, dt_bias, a_log, d_skip, ssd_norm, na_q_norm,
                                     na_k_norm, na_rpb, pool_w, pool_scale, w_out, norm_xa, norm_mem, w_xq, w_xkv,
                                     xq_norm, xk_norm, w_xo, norm_ffn, w_router_group, b_router_group,
                                     w_router_expert, b_router_expert, w_e_gate, w_e_up, w_e_down)))
    depth = w_in.shape[0]
    layers = [_prep_layer({k: v[l] for k, v in stacked.items()}) for l in range(depth)]

    bias_cache = {}

    def trunk(x, mem):
        t = x.shape[1]
        pending = None
        for l, lp in enumerate(layers):
            if (l, t) not in bias_cache:
                bias_cache[(l, t)] = _na_bias(lp["na_rpb"], t)
            pending = _layer(x, pending, mem, lp, bias_cache[(l, t)])
            x = None
        return _combine(*pending).reshape(mem.shape[0], t, D_MODEL)

    return trunk(x_prompt, mem_prompt), trunk(x_sample, mem_sample)
```

```python
import functools

import jax
import jax.numpy as jnp
import numpy as np
from jax import lax
from jax.experimental import pallas as pl
from jax.experimental.pallas import tpu as pltpu
from jax.experimental.pallas import tpu_sc as plsc

F32 = jnp.float32
BF16 = jnp.bfloat16
U32 = jnp.uint32
HIGHEST = lax.Precision.HIGHEST

D_MODEL = 1024
GRID_W = 64
EPS = 1e-6
SSD_HEAD_DIM = 64
D_INNER = D_MODEL // 2
SSD_HEADS = D_INNER // SSD_HEAD_DIM
SSD_GROUPS = 2
SSD_STATE = 64
SSD_CHUNK = 128
CONV_W = 4
CONV_CH = D_INNER + 2 * SSD_GROUPS * SSD_STATE
NA_HEADS = 4
NA_HEAD_DIM = D_MODEL // 16
NA_WIDTH = NA_HEADS * NA_HEAD_DIM
NA_MAX_KH = 8
NA_KW = 16
POOL_WINDOWS = (2, 4, 8, 16)
POOL_GROUPS = 4
POOL_WIDTH = D_MODEL - D_INNER - NA_WIDTH
POOL_GROUP_DIM = POOL_WIDTH // POOL_GROUPS
XA_HEADS = 4
XA_HEAD_DIM = D_MODEL // 8
XA_WIDTH = XA_HEADS * XA_HEAD_DIM
N_EXPERT_GROUPS = 4
EXPERTS_PER_GROUP = 8
N_EXPERTS = N_EXPERT_GROUPS * EXPERTS_PER_GROUP
D_EXPERT = D_MODEL // 4
PAIRS_PER_GROUP = EXPERTS_PER_GROUP * (EXPERTS_PER_GROUP - 1) // 2
N_COMBOS = N_EXPERT_GROUPS * PAIRS_PER_GROUP

LANES = 128
BF16_SUBLANES = 16
VMEM_LIMIT_BYTES = 56 * 1024 * 1024

TOKEN_TILE = 1024
NA_QUERY_ROWS = 8
NA_SUB_ROWS = 8
NA_SUBS = NA_QUERY_ROWS // NA_SUB_ROWS
NA_KEY_ROWS = NA_SUB_ROWS + NA_MAX_KH
EXPERT_BLOCK_MIN = 128
HALF = D_MODEL // 2
ROW_WORDS = HALF + LANES
SSD_STEP_CHUNKS = 4
DT_PAD = LANES
SC_GATHER_ROWS = 128
NEG_BIG = -1e30


def _cparams(sem):
    return pltpu.CompilerParams(dimension_semantics=sem, vmem_limit_bytes=VMEM_LIMIT_BYTES)


def _sigmoid(x):
    return 1.0 / (1.0 + jnp.exp(-x))


def _silu(x):
    return x * _sigmoid(x)


def _softplus(x):
    return jnp.maximum(x, 0.0) + jnp.log(1.0 + jnp.exp(-jnp.abs(x)))


def _pack_bf16_pairs(v):
    k = v.shape[1] // 2
    bits = pltpu.bitcast(v, U32)
    return (bits[:, :k] >> 16) | (bits[:, k:] & jnp.uint32(0xFFFF0000))


def _unpack_bf16_pairs(w):
    lo = pltpu.bitcast(w << 16, F32)
    hi = pltpu.bitcast(w & jnp.uint32(0xFFFF0000), F32)
    return jnp.concatenate([lo, hi], axis=1)


def _full(shape):
    n = len(shape)
    return pl.BlockSpec(shape, lambda *_: (0,) * n)


def _inproj_kernel(x_ref, g_ref, w_ref, z_ref, xbc_ref, qkv_ref, u_ref, dt_ref):
    x = x_ref[...]
    ms = jnp.mean(x * x, axis=-1, keepdims=True)
    h = (x * lax.rsqrt(ms + EPS) * g_ref[...]).astype(BF16)
    o = 0
    for ref in (z_ref, xbc_ref, qkv_ref, u_ref, dt_ref):
        w = ref.shape[-1]
        ref[...] = jnp.dot(h, w_ref[:, o:o + w], preferred_element_type=F32).astype(ref.dtype)
        o += w


_INPROJ_WIDTHS = (D_INNER, CONV_CH, 3 * NA_WIDTH, POOL_WIDTH, DT_PAD)
_INPROJ_DTYPES = (BF16, BF16, BF16, BF16, F32)


def _inproj(x2d, gain, w_cat):
    n = x2d.shape[0]
    tm = TOKEN_TILE
    widths, dtypes = _INPROJ_WIDTHS, _INPROJ_DTYPES
    return pl.pallas_call(
        _inproj_kernel,
        grid=(n // tm,),
        in_specs=[pl.BlockSpec((tm, D_MODEL), lambda i: (i, 0)),
                  _full((1, D_MODEL)),
                  _full(w_cat.shape)],
        out_specs=[pl.BlockSpec((tm, w), lambda i: (i, 0)) for w in widths],
        out_shape=[jax.ShapeDtypeStruct((n, w), d) for w, d in zip(widths, dtypes)],
        compiler_params=_cparams(("arbitrary",)),
    )(x2d, gain.reshape(1, D_MODEL), w_cat)


def _split3(a):
    hi = a.astype(BF16)
    r = a - hi.astype(F32)
    mid = r.astype(BF16)
    lo = (r - mid.astype(F32)).astype(BF16)
    return hi, mid, lo


def _ssd_kernel(xc_ref, xp_ref, xn_ref, dt_ref, z_ref, cw_ref, cb_ref, dtb_ref, alog_ref, dsk_ref, nrm_ref, emat_ref,
                y_ref, state_ref, yf_ref, xs_c, bc_c, cbm_c, dt_c, *, nblocks):
    L = SSD_CHUNK
    LB = SSD_STEP_CHUNKS * L
    P = SSD_HEAD_DIM
    NS = SSD_STATE
    HG = SSD_HEADS // SSD_GROUPS
    gn = SSD_GROUPS * NS
    j = pl.program_id(1)
    c = jnp.where(j < nblocks, j, 2 * nblocks - 1 - j)
    row0 = pl.multiple_of(c * LB, LB)
    rows = pl.ds(row0, LB)
    lane1 = lax.broadcasted_iota(jnp.int32, (1, LANES), 1)
    lo_half = lane1 < P
    ti = lax.broadcasted_iota(jnp.int32, (L, L), 0)
    si = lax.broadcasted_iota(jnp.int32, (L, L), 1)

    def masked_c(bc):
        return [jnp.where(lane1 // NS == g, bc[:, gn:], 0.0).astype(BF16) for g in range(SSD_GROUPS)]

    def prepare():
        cur = xc_ref[0].astype(F32)
        prev = xp_ref[0].astype(F32)
        nxt = xn_ref[0].astype(F32)
        has_prev = (c > 0).astype(F32)
        has_next = (c < nblocks - 1).astype(F32)
        p_last = prev[BF16_SUBLANES - 1:BF16_SUBLANES, :] * has_prev
        n0 = nxt[0:1, :] * has_next
        n1 = nxt[1:2, :] * has_next
        row = lax.broadcasted_iota(jnp.int32, (LB, 1), 0)
        um1 = jnp.where(row == 0, p_last, pltpu.roll(cur, 1, 0))
        up1 = jnp.where(row == LB - 1, n0, pltpu.roll(cur, LB - 1, 0))
        up2 = jnp.where(row == LB - 2, n0, jnp.where(row == LB - 1, n1, pltpu.roll(cur, LB - 2, 0)))
        cw = cw_ref[...]
        acc = cb_ref[...] + um1 * cw[0:1, :] + cur * cw[1:2, :] + up1 * cw[2:3, :] + up2 * cw[3:4, :]
        xbc = _silu(acc)
        xs = xbc[:, :D_INNER]
        bc = xbc[:, D_INNER:D_INNER + 2 * gn].astype(BF16)
        dt = _softplus(dt_ref[0] + dtb_ref[...])
        xs_c[rows, :] = xs
        bc_c[rows, :] = bc
        dt_c[rows, :] = dt
        ops = []
        for sub in range(SSD_STEP_CHUNKS):
            sl = slice(sub * L, (sub + 1) * L)
            cg = masked_c(bc[sl])
            cb_mat = [lax.dot_general(cg[g], bc[sl, :gn], (((1,), (1,)), ((), ())), preferred_element_type=F32)
                      for g in range(SSD_GROUPS)]
            cbm_c[pl.ds(row0 + sub * L, L), :] = jnp.concatenate(cb_mat, axis=1)
            ops.append((xs[sl], bc[sl, :gn], cg, cb_mat, dt[sl]))
        return ops

    def recall():
        ops = []
        for sub in range(SSD_STEP_CHUNKS):
            r = pl.ds(row0 + sub * L, L)
            bc = bc_c[r, :]
            cbm = cbm_c[r, :]
            ops.append((xs_c[r, :], bc[:, :gn], masked_c(bc), [cbm[:, g * L:(g + 1) * L] for g in range(SSD_GROUPS)],
                        dt_c[r, :]))
        return ops

    def scan_chunk(direction, xs, bfull, cg, cb_mat, dt):
        if direction == 0:
            mask = ti >= si
            edge = L - 1
        else:
            mask = si >= ti
            edge = 0
        la = dt * (-jnp.exp(alog_ref[...]))
        tri = mask.astype(BF16)
        csum = sum(jnp.dot(tri, part, preferred_element_type=F32) for part in _split3(la))
        csum_t = csum.T
        emat = emat_ref[direction]
        colb = sum(jnp.dot(part, emat, preferred_element_type=F32) for part in _split3(csum))
        tot = csum[edge:edge + 1, :]
        e_tot = jnp.exp(tot)
        e_in = jnp.exp(csum)
        e_out = jnp.exp(tot - csum)
        ys = []
        for g in range(SSD_GROUPS):
            s_old = state_ref[g]
            y_off = lax.dot_general(cg[g], s_old.astype(BF16), (((1,), (1,)), ((), ())),
                                    preferred_element_type=F32)
            xw = []
            for pr in range(HG // 2):
                h0 = g * HG + 2 * pr
                l0 = direction * SSD_HEADS + h0

                def col(a, l0=l0):
                    return jnp.where(lo_half, a[:, l0:l0 + 1], a[:, l0 + 1:l0 + 2])

                xdt = xs[:, h0 * P:(h0 + 2) * P] * col(dt)
                y_pair = y_off[:, 2 * pr * P:(2 * pr + 2) * P] * col(e_in)
                for hh, half in ((h0, lo_half), (h0 + 1, jnp.logical_not(lo_half))):
                    ll = direction * SSD_HEADS + hh
                    seg = colb[:, hh * L:(hh + 1) * L] - csum_t[ll:ll + 1, :]
                    dec = jnp.exp(jnp.where(mask, seg, NEG_BIG))
                    m = (cb_mat[g] * dec).astype(BF16)
                    y_pair += jnp.dot(m, jnp.where(half, xdt, 0.0).astype(BF16), preferred_element_type=F32)
                ys.append(y_pair)
                xw.append(xdt * col(e_out))
            xw = jnp.concatenate(xw, axis=1).astype(BF16)
            s_new = lax.dot_general(xw, bfull, (((0,), (0,)), ((), ())), preferred_element_type=F32)
            s_scaled = []
            for hl in range(HG):
                lane = direction * SSD_HEADS + g * HG + hl
                s_scaled.append(s_old[hl * P:(hl + 1) * P, :] * e_tot[:, lane:lane + 1])
            state_ref[g] = jnp.concatenate(s_scaled, axis=0) + s_new
        return jnp.concatenate(ys, axis=1)

    @pl.when(jnp.logical_or(j == 0, j == nblocks))
    def _():
        state_ref[...] = jnp.zeros_like(state_ref)

    @pl.when(j < nblocks)
    def _():
        ops = prepare()
        for sub in range(SSD_STEP_CHUNKS):
            yf_ref[pl.ds(row0 + sub * L, L), :] = scan_chunk(0, *ops[sub])

    @pl.when(j >= nblocks)
    def _():
        ops = recall()
        for sub in reversed(range(SSD_STEP_CHUNKS)):
            sl = slice(sub * L, (sub + 1) * L)
            y = yf_ref[pl.ds(row0 + sub * L, L), :] + scan_chunk(1, *ops[sub]) + dsk_ref[...] * ops[sub][0]
            y = y * _silu(z_ref[0, sl, :].astype(F32))
            gw = D_INNER // SSD_GROUPS
            outs = []
            for g in range(SSD_GROUPS):
                yg = y[:, g * gw:(g + 1) * gw]
                outs.append(yg * lax.rsqrt(jnp.mean(yg * yg, axis=-1, keepdims=True) + EPS))
            y_ref[0, sl, :] = (jnp.concatenate(outs, axis=1) * nrm_ref[...]).astype(y_ref.dtype)


def _ssd(z, xbc, dt, conv_w, conv_b, dt_bias, a_log, d_skip, ssd_norm):
    b, t, _ = z.shape
    L = SSD_CHUNK
    lb = SSD_STEP_CHUNKS * L
    nb = t // lb
    hb = lb // BF16_SUBLANES
    nhalo = t // BF16_SUBLANES

    def blk(j):
        return jnp.where(j < nb, j, 2 * nb - 1 - j)

    pad = DT_PAD - 2 * SSD_HEADS
    dtb = jnp.pad(dt_bias.reshape(1, -1).astype(F32), ((0, 0), (0, pad)))
    alog = jnp.pad(a_log.reshape(1, -1).astype(F32), ((0, 0), (0, pad)))
    dsk = jnp.repeat(d_skip.astype(F32), SSD_HEAD_DIM).reshape(1, D_INNER)
    sel = np.arange(DT_PAD)[None, :, None] == (np.arange(2)[:, None, None] * SSD_HEADS
                                                + np.arange(SSD_HEADS)[None, None, :])
    emat = jnp.asarray(np.repeat(sel, L, axis=2), BF16)
    return pl.pallas_call(
        functools.partial(_ssd_kernel, nblocks=nb),
        grid=(b, 2 * nb),
        in_specs=[
            pl.BlockSpec((1, lb, CONV_CH), lambda i, j: (i, blk(j), 0)),
            pl.BlockSpec((1, BF16_SUBLANES, CONV_CH), lambda i, j: (i, jnp.maximum(blk(j) * hb - 1, 0), 0)),
            pl.BlockSpec((1, BF16_SUBLANES, CONV_CH),
                         lambda i, j: (i, jnp.minimum((blk(j) + 1) * hb, nhalo - 1), 0)),
            pl.BlockSpec((1, lb, DT_PAD), lambda i, j: (i, blk(j), 0)),
            pl.BlockSpec((1, lb, D_INNER), lambda i, j: (i, blk(j), 0)),
            _full((CONV_W, CONV_CH)), _full((1, CONV_CH)), _full((1, DT_PAD)), _full((1, DT_PAD)),
            _full((1, D_INNER)), _full((1, D_INNER)), _full((2, DT_PAD, SSD_HEADS * L)),
        ],
        out_specs=pl.BlockSpec((1, lb, D_INNER), lambda i, j: (i, jnp.where(j < nb, nb - 1, 2 * nb - 1 - j), 0)),
        out_shape=jax.ShapeDtypeStruct((b, t, D_INNER), BF16),
        scratch_shapes=[pltpu.VMEM((SSD_GROUPS, (SSD_HEADS // SSD_GROUPS) * SSD_HEAD_DIM, LANES), F32),
                        pltpu.VMEM((t, D_INNER), F32),
                        pltpu.VMEM((t, D_INNER), F32),
                        pltpu.VMEM((t, 2 * SSD_GROUPS * SSD_STATE), BF16),
                        pltpu.VMEM((t, SSD_GROUPS * L), F32),
                        pltpu.VMEM((t, DT_PAD), F32)],
        compiler_params=_cparams(("arbitrary", "arbitrary")),
    )(xbc, xbc, xbc, dt, z, conv_w.astype(F32), conv_b.reshape(1, CONV_CH).astype(F32), dtb, alog, dsk,
      ssd_norm.reshape(1, D_INNER).astype(F32), emat)


def _na_bias(rpb, t):
    r = t // GRID_W
    kh = min(NA_MAX_KH, r)
    nsb = r // NA_SUB_ROWS
    rows = np.arange(r)
    row_start = np.clip(rows - NA_MAX_KH // 2, 0, r - kh)
    r0 = np.arange(nsb) * NA_SUB_ROWS
    kr0 = np.clip(r0 - NA_MAX_KH // 2, 0, r - NA_KEY_ROWS)
    qrow = r0[:, None] + np.arange(NA_SUB_ROWS)[None, :]
    krow = kr0[:, None] + np.arange(NA_KEY_ROWS)[None, :]
    rs = row_start[qrow]
    row_ok = (krow[:, None, :] >= rs[:, :, None]) & (krow[:, None, :] < rs[:, :, None] + kh)
    dr = np.clip(krow[:, None, :] - qrow[:, :, None] + (NA_MAX_KH - 1), 0, 2 * NA_MAX_KH - 2)
    cols = np.arange(GRID_W)
    col_start = np.clip(cols - NA_KW // 2, 0, GRID_W - NA_KW)
    col_ok = (cols[None, :] >= col_start[:, None]) & (cols[None, :] < col_start[:, None] + NA_KW)
    dc = np.clip(cols[None, :] - cols[:, None] + (NA_KW - 1), 0, 2 * NA_KW - 2)
    sel_c = (dc[..., None] == np.arange(2 * NA_KW - 1)) & col_ok[..., None]
    nr = 2 * NA_MAX_KH - 1
    tile_r = jnp.einsum("hrc,xyc->hrxy", rpb.astype(F32), jnp.asarray(sel_c, F32), precision=HIGHEST)
    tile_r = jnp.where(jnp.asarray(col_ok)[None, None], tile_r, NEG_BIG)
    tile_r = jnp.concatenate([tile_r, jnp.full((NA_HEADS, 1, GRID_W, GRID_W), NEG_BIG, F32)], axis=1)
    kp = NA_KEY_ROWS // 2
    code = np.where(row_ok, dr, nr).reshape(nsb * NA_SUB_ROWS * kp, 2)
    pairs, inv = np.unique(code, axis=0, return_inverse=True)
    blocks = jnp.concatenate([tile_r[:, pairs[:, 0]], tile_r[:, pairs[:, 1]]], axis=-1)
    blocks = jnp.moveaxis(blocks, 0, 1).reshape(len(pairs), NA_HEADS * GRID_W * 2 * GRID_W)
    onehot = jnp.asarray(inv.reshape(-1, 1) == np.arange(len(pairs))[None, :], F32)
    bias = jnp.dot(onehot, blocks, precision=HIGHEST)
    return bias.reshape(nsb // NA_SUBS, NA_SUBS, NA_SUB_ROWS, kp, NA_HEADS, GRID_W, 2 * GRID_W).astype(BF16)


def _natten_kernel(qkv_ref, bias_ref, qg_ref, kg_ref, seg_ref, o_ref, *, grid_rows):
    nq = NA_SUB_ROWS * GRID_W
    nk = NA_KEY_ROWS * GRID_W
    rb = pl.program_id(0)
    seg = seg_ref[...]
    lane_h = lax.broadcasted_iota(jnp.int32, (1, NA_WIDTH), 1) // NA_HEAD_DIM
    for sub in range(NA_SUBS):
        r0 = rb * NA_QUERY_ROWS + sub * NA_SUB_ROWS
        kr0 = jnp.clip(r0 - NA_MAX_KH // 2, 0, grid_rows - NA_KEY_ROWS)
        q0 = pl.multiple_of(r0 * GRID_W, nq)
        k0 = pl.multiple_of(kr0 * GRID_W, NA_MAX_KH // 2 * GRID_W)
        q = qkv_ref[0, pl.ds(q0, nq), 0:NA_WIDTH].astype(F32)
        k = qkv_ref[0, pl.ds(k0, nk), NA_WIDTH:2 * NA_WIDTH].astype(F32)
        v = qkv_ref[0, pl.ds(k0, nk), 2 * NA_WIDTH:3 * NA_WIDTH]
        qms = jnp.dot(q * q, seg, precision=HIGHEST, preferred_element_type=F32)
        kms = jnp.dot(k * k, seg, precision=HIGHEST, preferred_element_type=F32)
        qn = q * lax.rsqrt(qms + EPS) * (qg_ref[...] * NA_HEAD_DIM ** -0.5)
        kn = (k * lax.rsqrt(kms + EPS) * kg_ref[...]).astype(BF16)
        acc = jnp.zeros((nq, NA_WIDTH), F32)
        for h in range(NA_HEADS):
            hm = lane_h == h
            s = lax.dot_general(jnp.where(hm, qn, 0.0).astype(BF16), kn, (((1,), (1,)), ((), ())),
                                preferred_element_type=F32)
            bias = jnp.concatenate(
                [jnp.concatenate([bias_ref[0, sub, qr, kc, h] for kc in range(NA_KEY_ROWS // 2)], axis=1)
                 for qr in range(NA_SUB_ROWS)], axis=0)
            s = s + bias.astype(F32)
            p = jnp.exp(s - jnp.max(s, axis=-1, keepdims=True))
            l = jnp.sum(p, axis=-1, keepdims=True)
            o = jnp.dot(p.astype(BF16), v, preferred_element_type=F32)
            acc += jnp.where(hm, o / l, 0.0)
        o_ref[0, sub * nq:(sub + 1) * nq, :] = acc.astype(o_ref.dtype)


def _natten(qkv, bias, q_norm, k_norm):
    b, t, _ = qkv.shape
    r = t // GRID_W
    nrb = r // NA_QUERY_ROWS
    nq = NA_QUERY_ROWS * GRID_W
    head = jnp.arange(NA_WIDTH) // NA_HEAD_DIM
    seg = (head[:, None] == head[None, :]).astype(F32) / NA_HEAD_DIM
    return pl.pallas_call(
        functools.partial(_natten_kernel, grid_rows=r),
        grid=(nrb, b),
        in_specs=[pl.BlockSpec((1, t, 3 * NA_WIDTH), lambda i, j: (j, 0, 0)),
                  pl.BlockSpec((1,) + bias.shape[1:], lambda i, j: (i,) + (0,) * (bias.ndim - 1)),
                  _full((1, NA_WIDTH)), _full((1, NA_WIDTH)), _full((NA_WIDTH, NA_WIDTH))],
        out_specs=pl.BlockSpec((1, nq, NA_WIDTH), lambda i, j: (j, i, 0)),
        out_shape=jax.ShapeDtypeStruct((b, t, NA_WIDTH), BF16),
        compiler_params=_cparams(("arbitrary", "arbitrary")),
    )(qkv, bias, jnp.tile(q_norm.astype(F32), NA_HEADS).reshape(1, NA_WIDTH),
      jnp.tile(k_norm.astype(F32), NA_HEADS).reshape(1, NA_WIDTH), seg)


def _memkv_kernel(m_ref, g_ref, w_ref, kg_ref, k_ref, v_ref):
    x = m_ref[0]
    ms = jnp.mean(x * x, axis=-1, keepdims=True)
    h = (x * lax.rsqrt(ms + EPS) * g_ref[...]).astype(BF16)
    kv = jnp.dot(h, w_ref[...], preferred_element_type=F32)
    ks = []
    for hd in range(XA_HEADS):
        kh = kv[:, hd * XA_HEAD_DIM:(hd + 1) * XA_HEAD_DIM]
        ks.append(kh * lax.rsqrt(jnp.mean(kh * kh, axis=-1, keepdims=True) + EPS))
    k_ref[0] = (jnp.concatenate(ks, axis=1) * kg_ref[...]).astype(k_ref.dtype)
    v_ref[0] = kv[:, XA_WIDTH:].astype(v_ref.dtype)


def _memkv(mem, norm_mem, w_xkv, xk_norm):
    b, m, _ = mem.shape
    return pl.pallas_call(
        _memkv_kernel,
        grid=(b,),
        in_specs=[pl.BlockSpec((1, m, D_MODEL), lambda i: (i, 0, 0)),
                  _full((1, D_MODEL)), _full((D_MODEL, 2 * XA_WIDTH)), _full((1, XA_WIDTH))],
        out_specs=[pl.BlockSpec((1, m, XA_WIDTH), lambda i: (i, 0, 0))] * 2,
        out_shape=[jax.ShapeDtypeStruct((b, m, XA_WIDTH), BF16)] * 2,
        compiler_params=_cparams(("arbitrary",)),
    )(mem, norm_mem.reshape(1, D_MODEL).astype(F32), w_xkv,
      jnp.tile(xk_norm.astype(F32), XA_HEADS).reshape(1, XA_WIDTH))


def _mixer_kernel(x_ref, ys_ref, yn_ref, u_ref, up_ref, un_ref, k_ref, v_ref,
                  pw_ref, psc_ref, wo_ref, gxa_ref, wq_ref, qg_ref, wxo_ref, gff_ref,
                  wrb_ref, br_ref, ltri_ref,
                  x2_ref, hf_ref, meta_ref, cnt_ref, carry_ref, *, seq):
    tm = x_ref.shape[1]
    halo = BF16_SUBLANES
    bi = pl.program_id(0)
    i = pl.program_id(1)
    nt = pl.num_programs(1)

    @pl.when(jnp.logical_and(bi == 0, i == 0))
    def _():
        carry_ref[...] = jnp.zeros_like(carry_ref)

    u = u_ref[0].astype(F32)
    up = up_ref[0].astype(F32) * (i > 0).astype(F32)
    un = un_ref[0].astype(F32) * (i < nt - 1).astype(F32)
    cat = jnp.concatenate([up, u, un], axis=0)
    n = tm + 2 * halo

    def sh(a, k):
        return pltpu.roll(a, (-k) % n, 0)

    a2 = cat + sh(cat, -1)
    a4 = sh(a2, 1) + sh(a2, -1)
    a8 = sh(a4, 2) + sh(a4, -2)
    a16 = sh(a8, 4) + sh(a8, -4)
    lane_g = lax.broadcasted_iota(jnp.int32, (1, POOL_WIDTH), 1) // POOL_GROUP_DIM
    wsum = jnp.where(lane_g == 0, a2, jnp.where(lane_g == 1, a4, jnp.where(lane_g == 2, a8, a16)))
    wsum = wsum[halo:halo + tm, :]
    half = jnp.where(lane_g == 0, POOL_WINDOWS[0] // 2,
                     jnp.where(lane_g == 1, POOL_WINDOWS[1] // 2,
                               jnp.where(lane_g == 2, POOL_WINDOWS[2] // 2, POOL_WINDOWS[3] // 2)))
    tpos = i * tm + lax.broadcasted_iota(jnp.int32, (tm, 1), 0)
    cnt = (jnp.minimum(tpos + half, seq) - jnp.maximum(tpos - half, 0)).astype(F32)
    d = wsum / cnt - u
    ypool = jnp.dot(d.astype(BF16), pw_ref[...], preferred_element_type=F32) * psc_ref[...]

    mix = jnp.dot(ys_ref[0], wo_ref[0:D_INNER, :], preferred_element_type=F32)
    mix += jnp.dot(yn_ref[0], wo_ref[D_INNER:D_INNER + NA_WIDTH, :], preferred_element_type=F32)
    mix += jnp.dot(ypool.astype(BF16), wo_ref[D_INNER + NA_WIDTH:, :], preferred_element_type=F32)
    x1 = x_ref[0] + mix

    hn = (x1 * lax.rsqrt(jnp.mean(x1 * x1, axis=-1, keepdims=True) + EPS) * gxa_ref[...]).astype(BF16)
    q = jnp.dot(hn, wq_ref[...], preferred_element_type=F32)
    kk = k_ref[0]
    vv = v_ref[0]
    scale = XA_HEAD_DIM ** -0.5
    outs = []
    for hd in range(XA_HEADS):
        sl = slice(hd * XA_HEAD_DIM, (hd + 1) * XA_HEAD_DIM)
        qh = q[:, sl]
        qh = (qh * lax.rsqrt(jnp.mean(qh * qh, axis=-1, keepdims=True) + EPS) * qg_ref[:, sl]).astype(BF16)
        s = lax.dot_general(qh, kk[:, sl], (((1,), (1,)), ((), ())), preferred_element_type=F32) * scale
        p = jnp.exp(s - jnp.max(s, axis=-1, keepdims=True))
        l = jnp.sum(p, axis=-1, keepdims=True)
        outs.append(jnp.dot(p.astype(BF16), vv[:, sl], preferred_element_type=F32) / l)
    att = jnp.concatenate(outs, axis=1).astype(BF16)
    x2 = x1 + jnp.dot(att, wxo_ref[...], preferred_element_type=F32)
    x2_ref[0] = x2

    hf = x2 * lax.rsqrt(jnp.mean(x2 * x2, axis=-1, keepdims=True) + EPS) * gff_ref[...]
    h_hi = hf.astype(BF16)
    hf_ref[0, :, :HALF] = pltpu.bitcast(_pack_bf16_pairs(h_hi.astype(F32)), jnp.int32)
    h_lo = (hf - h_hi.astype(F32)).astype(BF16)
    both = jnp.dot(h_hi, wrb_ref[...], preferred_element_type=F32)
    logits = (both[:, :LANES] + both[:, LANES:]
              + jnp.dot(h_lo, wrb_ref[:, :LANES], preferred_element_type=F32)) + br_ref[...]
    lane = lax.broadcasted_iota(jnp.int32, (1, LANES), 1)
    lane_f = lane.astype(F32)
    lane_grp = (lane // EXPERTS_PER_GROUP).astype(F32)
    is_g = jnp.logical_and(lane >= N_EXPERTS, lane < N_EXPERTS + N_EXPERT_GROUPS)
    gl = jnp.where(is_g, logits, NEG_BIG)
    gmax = jnp.max(gl, axis=-1, keepdims=True)
    g_sel = jnp.min(jnp.where(gl == gmax, lane_f, float(LANES)), axis=-1, keepdims=True) - N_EXPERTS
    g_gate = 1.0 / jnp.sum(jnp.where(is_g, jnp.exp(gl - gmax), 0.0), axis=-1, keepdims=True)
    in_grp = jnp.logical_and(lane < N_EXPERTS, lane_grp == g_sel)
    el = jnp.where(in_grp, logits, NEG_BIG)
    v1 = jnp.max(el, axis=-1, keepdims=True)
    e0 = jnp.min(jnp.where(el == v1, lane_f, float(LANES)), axis=-1, keepdims=True)
    el2 = jnp.where(lane_f == e0, NEG_BIG, el)
    v2 = jnp.max(el2, axis=-1, keepdims=True)
    e1 = jnp.min(jnp.where(el2 == v2, lane_f, float(LANES)), axis=-1, keepdims=True)
    w1 = jnp.exp(v2 - v1)
    gate0 = g_gate / (1.0 + w1)
    gate1 = g_gate * w1 / (1.0 + w1)

    base = g_sel * EXPERTS_PER_GROUP
    ea = jnp.minimum(e0, e1) - base
    eb = jnp.maximum(e0, e1) - base
    combo = g_sel * PAIRS_PER_GROUP + ea * EXPERTS_PER_GROUP - ea * (ea + 1.0) * 0.5 + (eb - ea - 1.0)
    gate_a = jnp.where(e0 < e1, gate0, gate1)
    gate_b = jnp.where(e0 < e1, gate1, gate0)
    hf_ref[0, :, HALF:] = pltpu.bitcast(jnp.where(lane == 0, gate_a, 0.0) + jnp.where(lane == 1, gate_b, 0.0), jnp.int32)

    oh = lane_f == combo
    cnt_tok = oh.astype(F32)
    before = jnp.dot(ltri_ref[...], cnt_tok.astype(BF16), preferred_element_type=F32) + carry_ref[0:1, :]
    rank = jnp.sum(jnp.where(oh, before, 0.0), axis=-1, keepdims=True)
    new_carry = carry_ref[0:1, :] + jnp.sum(cnt_tok, axis=0, keepdims=True)
    carry_ref[...] = jnp.broadcast_to(new_carry, carry_ref.shape)
    cnt_ref[...] = jnp.broadcast_to(new_carry, cnt_ref.shape)

    slab = jnp.where(lane == 0, combo, 0.0) + jnp.where(lane == 1, rank, 0.0)
    meta_ref[0] = slab.T[0:8, :]


def _mixer(x, y_ssd, y_na, u, kmem, vmem, p):
    b, t, _ = x.shape
    tm = TOKEN_TILE
    nt = t // tm
    hb = tm // BF16_SUBLANES
    nhalo = t // BF16_SUBLANES
    ltri = (jnp.arange(tm)[:, None] > jnp.arange(tm)[None, :]).astype(BF16)
    tok = lambda w: pl.BlockSpec((1, tm, w), lambda i, j: (i, j, 0))
    mem = pl.BlockSpec((1, kmem.shape[1], XA_WIDTH), lambda i, j: (i, 0, 0))
    weights = (p["pool_bd"], p["pool_scale"], p["w_out"], p["norm_xa"], p["w_xq"], p["xq_norm"], p["w_xo"],
               p["norm_ffn"], p["w_r_both"], p["b_r"], ltri)
    return pl.pallas_call(
        functools.partial(_mixer_kernel, seq=t),
        grid=(b, nt),
        in_specs=[tok(D_MODEL), tok(D_INNER), tok(NA_WIDTH), tok(POOL_WIDTH),
                  pl.BlockSpec((1, BF16_SUBLANES, POOL_WIDTH), lambda i, j: (i, jnp.maximum(j * hb - 1, 0), 0)),
                  pl.BlockSpec((1, BF16_SUBLANES, POOL_WIDTH),
                               lambda i, j: (i, jnp.minimum((j + 1) * hb, nhalo - 1), 0)),
                  mem, mem] + [_full(w.shape) for w in weights],
        out_specs=[tok(D_MODEL), tok(ROW_WORDS),
                   pl.BlockSpec((1, 8, tm), lambda i, j: (i * nt + j, 0, 0)),
                   pl.BlockSpec((8, LANES), lambda i, j: (0, 0))],
        out_shape=[jax.ShapeDtypeStruct((b, t, D_MODEL), F32),
                   jax.ShapeDtypeStruct((b, t, ROW_WORDS), jnp.int32),
                   jax.ShapeDtypeStruct((b * nt, 8, tm), F32),
                   jax.ShapeDtypeStruct((8, LANES), F32)],
        scratch_shapes=[pltpu.VMEM((8, LANES), F32)],
        compiler_params=_cparams(("arbitrary", "arbitrary")),
    )(x, y_ssd, y_na, u, u, u, kmem, vmem, *weights)


def _zero_pad_kernel(pstart_ref, cnt_ref, xs_in, zero_ref, xs_ref, sem, *, bm):
    del xs_in

    def fill(cm, wait):
        n_c = cnt_ref[cm]
        npad = (bm - n_c % bm) % bm
        head = (8 - n_c % 8) % 8
        base = pstart_ref[cm] + n_c

        def go(cp):
            if wait:
                cp.wait()
            else:
                cp.start()

        def one(r, _):
            go(pltpu.make_async_copy(zero_ref.at[pl.ds(0, 1)], xs_ref.at[pl.ds(base + r, 1)], sem))
            return 0

        lax.fori_loop(0, head, one, 0)
        rest = npad - head
        off = base + head
        size = bm // 2
        while size >= 8:
            @pl.when(rest & size != 0)
            def _(off=off, size=size):
                go(pltpu.make_async_copy(zero_ref.at[pl.ds(0, size)],
                                         xs_ref.at[pl.ds(pl.multiple_of(off, 8), size)], sem))

            off = off + (rest & size)
            size //= 2
        return 0

    lax.fori_loop(0, N_COMBOS, lambda cm, _: fill(cm, False), 0)
    lax.fori_loop(0, N_COMBOS, lambda cm, _: fill(cm, True), 0)


def _zero_pad_rows(xs, pstart, counts, bm):
    return pl.pallas_call(
        functools.partial(_zero_pad_kernel, bm=bm),
        grid_spec=pltpu.PrefetchScalarGridSpec(
            num_scalar_prefetch=2,
            grid=(1,),
            in_specs=[pl.BlockSpec(memory_space=pl.ANY),
                      pl.BlockSpec((bm // 2, ROW_WORDS), lambda i, *_: (0, 0))],
            out_specs=pl.BlockSpec(memory_space=pl.ANY),
            scratch_shapes=[pltpu.SemaphoreType.DMA(())]),
        out_shape=jax.ShapeDtypeStruct(xs.shape, xs.dtype),
        input_output_aliases={2: 0},
        compiler_params=_cparams(("arbitrary",)),
    )(pstart, counts, xs, jnp.zeros((bm // 2, ROW_WORDS), xs.dtype))


def _sc_rows(kind, src, idx, n_out):
    n = idx.shape[0]
    width = src.shape[1]
    info = plsc.get_sparse_core_info()
    workers = info.num_cores * info.num_subcores
    per_worker = n // workers
    assert per_worker * workers == n and per_worker % SC_GATHER_ROWS == 0, (n, workers)
    mesh = plsc.VectorSubcoreMesh(core_axis_name="core", subcore_axis_name="subcore")

    def body(src_hbm, idx_hbm, out_hbm, idx_v, rows_v, sem):
        base = (lax.axis_index("subcore") * info.num_cores + lax.axis_index("core")) * per_worker

        @pl.loop(0, per_worker // SC_GATHER_ROWS)
        def _(j):
            off = pl.multiple_of(base + j * SC_GATHER_ROWS, SC_GATHER_ROWS)
            pltpu.sync_copy(idx_hbm.at[pl.ds(off, SC_GATHER_ROWS)], idx_v)
            if kind == "gather":
                pltpu.async_copy(src_hbm.at[idx_v], rows_v, sem).wait()
                pltpu.sync_copy(rows_v, out_hbm.at[pl.ds(off, SC_GATHER_ROWS)])
            else:
                pltpu.sync_copy(src_hbm.at[pl.ds(off, SC_GATHER_ROWS)], rows_v)
                pltpu.async_copy(rows_v, out_hbm.at[idx_v], sem).wait()

    return pl.kernel(
        body, mesh=mesh, out_type=jax.ShapeDtypeStruct((n_out, width), src.dtype),
        scratch_types=[pltpu.VMEM((SC_GATHER_ROWS,), jnp.int32), pltpu.VMEM((SC_GATHER_ROWS, width), src.dtype),
                       pltpu.SemaphoreType.DMA])(src, idx)


def _experts_kernel(ea_ref, eb_ref, nused_ref, x_ref, wga_ref, wua_ref, wda_ref, wgb_ref, wub_ref, wdb_ref, y_ref):
    @pl.when(pl.program_id(0) < nused_ref[0])
    def _():
        x = _unpack_bf16_pairs(pltpu.bitcast(x_ref[:, :HALF], U32)).astype(BF16)
        gates = pltpu.bitcast(x_ref[:, HALF:], F32)

        def mlp(wg_ref, wu_ref, wd_ref):
            hg = jnp.dot(x, wg_ref[0], preferred_element_type=F32)
            hu = jnp.dot(x, wu_ref[0], preferred_element_type=F32)
            return jnp.dot((_silu(hg) * hu).astype(BF16), wd_ref[0], preferred_element_type=F32)

        y = mlp(wga_ref, wua_ref, wda_ref) * gates[:, 0:1] + mlp(wgb_ref, wub_ref, wdb_ref) * gates[:, 1:2]
        y_ref[...] = pltpu.bitcast(_pack_bf16_pairs(y.astype(BF16).astype(F32)), jnp.int32)


def _experts(xs, blk_ea, blk_eb, nused, w_gate, w_up, w_down, bm):
    nblk = blk_ea.shape[0]
    row = lambda j, ea, eb, nu: (jnp.minimum(j, nu[0] - 1), 0)
    row_out = lambda j, ea, eb, nu: (jnp.where(j < nu[0], j, nblk - 1), 0)
    sel_a = lambda j, ea, eb, nu: (ea[j], 0, 0)
    sel_b = lambda j, ea, eb, nu: (eb[j], 0, 0)
    up = lambda sel: pl.BlockSpec((1, D_MODEL, D_EXPERT), sel)
    down = lambda sel: pl.BlockSpec((1, D_EXPERT, D_MODEL), sel)
    return pl.pallas_call(
        _experts_kernel,
        grid_spec=pltpu.PrefetchScalarGridSpec(
            num_scalar_prefetch=3,
            grid=(nblk,),
            in_specs=[pl.BlockSpec((bm, ROW_WORDS), row),
                      up(sel_a), up(sel_a), down(sel_a), up(sel_b), up(sel_b), down(sel_b)],
            out_specs=pl.BlockSpec((bm, HALF), row_out)),
        out_shape=jax.ShapeDtypeStruct((nblk * bm, HALF), jnp.int32),
        compiler_params=_cparams(("arbitrary",)),
    )(blk_ea, blk_eb, nused, xs, w_gate, w_up, w_down, w_gate, w_up, w_down)


def _add_rows_kernel(x_ref, yg_ref, o_ref):
    o_ref[...] = x_ref[...] + _unpack_bf16_pairs(pltpu.bitcast(yg_ref[...], U32))


def _add_inproj_kernel(x2_ref, yg_ref, g_ref, w_ref, x_ref, z_ref, xbc_ref, qkv_ref, u_ref, dt_ref):
    _add_rows_kernel(x2_ref, yg_ref, x_ref)
    _inproj_kernel(x_ref, g_ref, w_ref, z_ref, xbc_ref, qkv_ref, u_ref, dt_ref)


def _combine(x2d, yg, gain=None, w_cat=None):
    n = x2d.shape[0]
    tm = TOKEN_TILE
    tile = lambda w: pl.BlockSpec((tm, w), lambda i: (i, 0))
    if w_cat is None:
        return pl.pallas_call(
            _add_rows_kernel, grid=(n // tm,), in_specs=[tile(D_MODEL), tile(HALF)], out_specs=tile(D_MODEL),
            out_shape=jax.ShapeDtypeStruct((n, D_MODEL), F32), compiler_params=_cparams(("arbitrary",)))(x2d, yg)
    widths = (D_MODEL,) + _INPROJ_WIDTHS
    dtypes = (F32,) + _INPROJ_DTYPES
    return pl.pallas_call(
        _add_inproj_kernel, grid=(n // tm,),
        in_specs=[tile(D_MODEL), tile(HALF), _full((1, D_MODEL)), _full(w_cat.shape)],
        out_specs=[tile(w) for w in widths],
        out_shape=[jax.ShapeDtypeStruct((n, w), d) for w, d in zip(widths, dtypes)],
        compiler_params=_cparams(("arbitrary",)),
    )(x2d, yg, gain.reshape(1, D_MODEL), w_cat)


_PAIR_A = np.array([a for a in range(EXPERTS_PER_GROUP) for _ in range(a + 1, EXPERTS_PER_GROUP)], np.int32)
_PAIR_B = np.array([b for a in range(EXPERTS_PER_GROUP) for b in range(a + 1, EXPERTS_PER_GROUP)], np.int32)


def _expert_block(n):
    return 2 * EXPERT_BLOCK_MIN if n >= 2 * 2 * EXPERT_BLOCK_MIN * N_COMBOS else EXPERT_BLOCK_MIN


def _moe(x2, hf, meta, counts, w_gate, w_up, w_down):
    b, t, _ = x2.shape
    n = b * t
    bm = _expert_block(n)
    nblk = (n + N_COMBOS * (bm - 1) + bm - 1) // bm
    cnt = counts[0, :N_COMBOS].astype(jnp.int32)
    psz = (cnt + bm - 1) // bm * bm
    pend = jnp.cumsum(psz)
    pstart = (pend - psz).astype(jnp.int32)
    nused = jnp.maximum(pend[-1] // bm, 1).astype(jnp.int32).reshape(1)
    blk = jnp.minimum(jnp.arange(nblk, dtype=jnp.int32), nused[0] - 1)
    blk_c = jnp.minimum(jnp.sum(pend[None, :] <= (blk * bm)[:, None], axis=1), N_COMBOS - 1).astype(jnp.int32)
    grp = blk_c // PAIRS_PER_GROUP
    blk_ea = (grp * EXPERTS_PER_GROUP + jnp.asarray(_PAIR_A)[blk_c % PAIRS_PER_GROUP]).astype(jnp.int32)
    blk_eb = (grp * EXPERTS_PER_GROUP + jnp.asarray(_PAIR_B)[blk_c % PAIRS_PER_GROUP]).astype(jnp.int32)
    ids = meta.astype(jnp.int32)
    combo = ids[:, 0, :].reshape(n)
    dest = ids[:, 1, :].reshape(n) + jnp.sum(jnp.where(combo[:, None] == jnp.arange(N_COMBOS)[None, :], pstart[None, :], 0),
                                             axis=1)
    xs = _sc_rows("scatter", hf.reshape(n, ROW_WORDS), dest, nblk * bm)
    xs = _zero_pad_rows(xs, pstart, cnt, bm)
    y = _experts(xs, blk_ea, blk_eb, nused, w_gate, w_up, w_down, bm)
    yg = _sc_rows("gather", y, dest, n)
    return x2.reshape(n, D_MODEL), yg


def _prep_layer(lp):
    w_in = lp["w_in"]
    c0 = D_INNER + CONV_CH
    c1 = c0 + 2 * SSD_HEADS
    w_cat = jnp.concatenate([w_in[:, :c0], w_in[:, c1:], w_in[:, c0:c1],
                             jnp.zeros((D_MODEL, DT_PAD - 2 * SSD_HEADS), w_in.dtype)], axis=1).astype(BF16)
    pool_bd = jnp.zeros((POOL_WIDTH, POOL_WIDTH), F32)
    for g in range(POOL_GROUPS):
        sl = slice(g * POOL_GROUP_DIM, (g + 1) * POOL_GROUP_DIM)
        pool_bd = pool_bd.at[sl, sl].set(lp["pool_w"][g].astype(F32))
    w_r = jnp.concatenate([lp["w_router_expert"], lp["w_router_group"],
                           jnp.zeros((D_MODEL, LANES - N_EXPERTS - N_EXPERT_GROUPS), F32)], axis=1).astype(F32)
    w_r_hi = w_r.astype(BF16)
    w_r_lo = (w_r - w_r_hi.astype(F32)).astype(BF16)
    b_r = jnp.concatenate([lp["b_router_expert"], lp["b_router_group"],
                           jnp.zeros((LANES - N_EXPERTS - N_EXPERT_GROUPS,), F32)]).reshape(1, LANES).astype(F32)
    row = lambda a, w: a.reshape(1, w).astype(F32)
    return dict(
        norm_mix=lp["norm_mix"], w_cat=w_cat,
        conv_w=lp["conv_w"], conv_b=lp["conv_b"], dt_bias=lp["dt_bias"], a_log=lp["a_log"],
        d_skip=lp["d_skip"], ssd_norm=lp["ssd_norm"],
        na_q_norm=lp["na_q_norm"], na_k_norm=lp["na_k_norm"], na_rpb=lp["na_rpb"],
        pool_bd=pool_bd.astype(BF16), pool_scale=row(lp["pool_scale"], POOL_WIDTH),
        w_out=lp["w_out"].astype(BF16), norm_xa=row(lp["norm_xa"], D_MODEL),
        norm_mem=lp["norm_mem"], w_xq=lp["w_xq"].astype(BF16), w_xkv=lp["w_xkv"].astype(BF16),
        xq_norm=row(jnp.tile(lp["xq_norm"], XA_HEADS), XA_WIDTH), xk_norm=lp["xk_norm"],
        w_xo=lp["w_xo"].astype(BF16), norm_ffn=row(lp["norm_ffn"], D_MODEL),
        w_r_both=jnp.concatenate([w_r_hi, w_r_lo], axis=1), b_r=b_r,
        w_e_gate=lp["w_e_gate"].astype(BF16), w_e_up=lp["w_e_up"].astype(BF16),
        w_e_down=lp["w_e_down"].astype(BF16),
    )


def _layer(x, pending, mem, p, na_bias):
    b, m, _ = mem.shape
    if pending is None:
        t = x.shape[1]
        z, xbc, qkv, u, dt = _inproj(x.reshape(b * t, D_MODEL), p["norm_mix"], p["w_cat"])
    else:
        t = pending[0].shape[0] // b
        x, z, xbc, qkv, u, dt = _combine(*pending, p["norm_mix"], p["w_cat"])
        x = x.reshape(b, t, D_MODEL)
    r3 = lambda a: a.reshape(b, t, a.shape[-1])
    y_ssd = _ssd(r3(z), r3(xbc), r3(dt), p["conv_w"], p["conv_b"], p["dt_bias"], p["a_log"], p["d_skip"],
                 p["ssd_norm"])
    y_na = _natten(r3(qkv), na_bias, p["na_q_norm"], p["na_k_norm"])
    kmem, vmem = _memkv(mem, p["norm_mem"], p["w_xkv"], p["xk_norm"])
    x2, hf, meta, counts = _mixer(x, y_ssd, y_na, r3(u), kmem, vmem, p)
    return _moe(x2, hf, meta, counts, p["w_e_gate"], p["w_e_up"], p["w_e_down"])


_LAYER_KEYS = ("norm_mix", "w_in", "conv_w", "conv_b", "dt_bias", "a_log", "d_skip", "ssd_norm", "na_q_norm",
               "na_k_norm", "na_rpb", "pool_w", "pool_scale", "w_out", "norm_xa", "norm_mem", "w_xq", "w_xkv",
               "xq_norm", "xk_norm", "w_xo", "norm_ffn", "w_router_group", "b_router_group", "w_router_expert",
               "b_router_expert", "w_e_gate", "w_e_up", "w_e_down")


def kernel(x_prompt, x_sample, mem_prompt, mem_sample, norm_mix, w_in, conv_w, conv_b, dt_bias, a_log, d_skip, ssd_norm, na_q_norm, na_k_norm, na_rpb, pool_w, pool_scale, w_out, norm_xa, norm_mem, w_xq, w_xkv, xq_norm, xk_norm, w_xo, norm_ffn, w_router_group, b_router_group, w_router_expert, b_router_expert, w_e_gate, w_e_up, w_e_down):
    stacked = dict(zip(_LAYER_KEYS, (norm_mix, w_in, conv_w, conv_b, dt_bias, a_log, d_skip, ssd_norm, na_q_norm,
                                     na_k_norm, na_rpb, pool_w, pool_scale, w_out, norm_xa, norm_mem, w_xq, w_xkv,
                                     xq_norm, xk_norm, w_xo, norm_ffn, w_router_group, b_router_group,
                                     w_router_expert, b_router_expert, w_e_gate, w_e_up, w_e_down)))
    depth = w_in.shape[0]
    layers = [_prep_layer({k: v[l] for k, v in stacked.items()}) for l in range(depth)]

    bias_cache = {}

    def trunk(x, mem):
        t = x.shape[1]
        pending = None
        for l, lp in enumerate(layers):
            if (l, t) not in bias_cache:
                bias_cache[(l, t)] = _na_bias(lp["na_rpb"], t)
            pending = _layer(x, pending, mem, lp, bias_cache[(l, t)])
            x = None
        return _combine(*pending).reshape(mem.shape[0], t, D_MODEL)

    return trunk(x_prompt, mem_prompt), trunk(x_sample, mem_sample)
```

```python
import functools

import jax
import jax.numpy as jnp
import numpy as np
from jax import lax
from jax.experimental import pallas as pl
from jax.experimental.pallas import tpu as pltpu
from jax.experimental.pallas import tpu_sc as plsc

F32 = jnp.float32
BF16 = jnp.bfloat16
U32 = jnp.uint32
HIGHEST = lax.Precision.HIGHEST

D_MODEL = 1024
GRID_W = 64
EPS = 1e-6
SSD_HEAD_DIM = 64
D_INNER = D_MODEL // 2
SSD_HEADS = D_INNER // SSD_HEAD_DIM
SSD_GROUPS = 2
SSD_STATE = 64
SSD_CHUNK = 128
CONV_W = 4
CONV_CH = D_INNER + 2 * SSD_GROUPS * SSD_STATE
NA_HEADS = 4
NA_HEAD_DIM = D_MODEL // 16
NA_WIDTH = NA_HEADS * NA_HEAD_DIM
NA_MAX_KH = 8
NA_KW = 16
POOL_WINDOWS = (2, 4, 8, 16)
POOL_GROUPS = 4
POOL_WIDTH = D_MODEL - D_INNER - NA_WIDTH
POOL_GROUP_DIM = POOL_WIDTH // POOL_GROUPS
XA_HEADS = 4
XA_HEAD_DIM = D_MODEL // 8
XA_WIDTH = XA_HEADS * XA_HEAD_DIM
N_EXPERT_GROUPS = 4
EXPERTS_PER_GROUP = 8
N_EXPERTS = N_EXPERT_GROUPS * EXPERTS_PER_GROUP
D_EXPERT = D_MODEL // 4
PAIRS_PER_GROUP = EXPERTS_PER_GROUP * (EXPERTS_PER_GROUP - 1) // 2
N_COMBOS = N_EXPERT_GROUPS * PAIRS_PER_GROUP

LANES = 128
BF16_SUBLANES = 16
VMEM_LIMIT_BYTES = 56 * 1024 * 1024

TOKEN_TILE = 1024
NA_QUERY_ROWS = 8
NA_SUB_ROWS = 8
NA_SUBS = NA_QUERY_ROWS // NA_SUB_ROWS
NA_KEY_ROWS = NA_SUB_ROWS + NA_MAX_KH
EXPERT_BLOCK_MIN = 128
HALF = D_MODEL // 2
ROW_WORDS = HALF + LANES
SSD_STEP_CHUNKS = 4
DT_PAD = LANES
SC_GATHER_ROWS = 128
NEG_BIG = -1e30


def _cparams(sem):
    return pltpu.CompilerParams(dimension_semantics=sem, vmem_limit_bytes=VMEM_LIMIT_BYTES)


def _sigmoid(x):
    return 1.0 / (1.0 + jnp.exp(-x))


def _silu(x):
    return x * _sigmoid(x)


def _softplus(x):
    return jnp.maximum(x, 0.0) + jnp.log(1.0 + jnp.exp(-jnp.abs(x)))


def _pack_bf16_pairs(v):
    k = v.shape[1] // 2
    bits = pltpu.bitcast(v, U32)
    return (bits[:, :k] >> 16) | (bits[:, k:] & jnp.uint32(0xFFFF0000))


def _unpack_bf16_pairs(w):
    lo = pltpu.bitcast(w << 16, F32)
    hi = pltpu.bitcast(w & jnp.uint32(0xFFFF0000), F32)
    return jnp.concatenate([lo, hi], axis=1)


def _full(shape):
    n = len(shape)
    return pl.BlockSpec(shape, lambda *_: (0,) * n)


def _inproj_kernel(x_ref, g_ref, w_ref, z_ref, xbc_ref, qkv_ref, u_ref, dt_ref):
    x = x_ref[...]
    ms = jnp.mean(x * x, axis=-1, keepdims=True)
    h = (x * lax.rsqrt(ms + EPS) * g_ref[...]).astype(BF16)
    o = 0
    for ref in (z_ref, xbc_ref, qkv_ref, u_ref, dt_ref):
        w = ref.shape[-1]
        ref[...] = jnp.dot(h, w_ref[:, o:o + w], preferred_element_type=F32).astype(ref.dtype)
        o += w


_INPROJ_WIDTHS = (D_INNER, CONV_CH, 3 * NA_WIDTH, POOL_WIDTH, DT_PAD)
_INPROJ_DTYPES = (BF16, BF16, BF16, BF16, F32)


def _inproj(x2d, gain, w_cat):
    n = x2d.shape[0]
    tm = TOKEN_TILE
    widths, dtypes = _INPROJ_WIDTHS, _INPROJ_DTYPES
    return pl.pallas_call(
        _inproj_kernel,
        grid=(n // tm,),
        in_specs=[pl.BlockSpec((tm, D_MODEL), lambda i: (i, 0)),
                  _full((1, D_MODEL)),
                  _full(w_cat.shape)],
        out_specs=[pl.BlockSpec((tm, w), lambda i: (i, 0)) for w in widths],
        out_shape=[jax.ShapeDtypeStruct((n, w), d) for w, d in zip(widths, dtypes)],
        compiler_params=_cparams(("arbitrary",)),
    )(x2d, gain.reshape(1, D_MODEL), w_cat)


def _split3(a):
    hi = a.astype(BF16)
    r = a - hi.astype(F32)
    mid = r.astype(BF16)
    lo = (r - mid.astype(F32)).astype(BF16)
    return hi, mid, lo


def _ssd_kernel(xc_ref, xp_ref, xn_ref, dt_ref, z_ref, cw_ref, cb_ref, dtb_ref, alog_ref, dsk_ref, nrm_ref, emat_ref,
                y_ref, state_ref, yf_ref, xs_c, bc_c, cbm_c, dt_c, *, nblocks):
    L = SSD_CHUNK
    LB = SSD_STEP_CHUNKS * L
    P = SSD_HEAD_DIM
    NS = SSD_STATE
    HG = SSD_HEADS // SSD_GROUPS
    gn = SSD_GROUPS * NS
    j = pl.program_id(1)
    c = jnp.where(j < nblocks, j, 2 * nblocks - 1 - j)
    row0 = pl.multiple_of(c * LB, LB)
    rows = pl.ds(row0, LB)
    lane1 = lax.broadcasted_iota(jnp.int32, (1, LANES), 1)
    lo_half = lane1 < P
    ti = lax.broadcasted_iota(jnp.int32, (L, L), 0)
    si = lax.broadcasted_iota(jnp.int32, (L, L), 1)

    def masked_c(bc):
        return [jnp.where(lane1 // NS == g, bc[:, gn:], 0.0).astype(BF16) for g in range(SSD_GROUPS)]

    def prepare():
        cur = xc_ref[0].astype(F32)
        prev = xp_ref[0].astype(F32)
        nxt = xn_ref[0].astype(F32)
        has_prev = (c > 0).astype(F32)
        has_next = (c < nblocks - 1).astype(F32)
        p_last = prev[BF16_SUBLANES - 1:BF16_SUBLANES, :] * has_prev
        n0 = nxt[0:1, :] * has_next
        n1 = nxt[1:2, :] * has_next
        row = lax.broadcasted_iota(jnp.int32, (LB, 1), 0)
        um1 = jnp.where(row == 0, p_last, pltpu.roll(cur, 1, 0))
        up1 = jnp.where(row == LB - 1, n0, pltpu.roll(cur, LB - 1, 0))
        up2 = jnp.where(row == LB - 2, n0, jnp.where(row == LB - 1, n1, pltpu.roll(cur, LB - 2, 0)))
        cw = cw_ref[...]
        acc = cb_ref[...] + um1 * cw[0:1, :] + cur * cw[1:2, :] + up1 * cw[2:3, :] + up2 * cw[3:4, :]
        xbc = _silu(acc)
        xs = xbc[:, :D_INNER]
        bc = xbc[:, D_INNER:D_INNER + 2 * gn].astype(BF16)
        dt = _softplus(dt_ref[0] + dtb_ref[...])
        xs_c[rows, :] = xs
        bc_c[rows, :] = bc
        dt_c[rows, :] = dt
        ops = []
        for sub in range(SSD_STEP_CHUNKS):
            sl = slice(sub * L, (sub + 1) * L)
            cg = masked_c(bc[sl])
            cb_mat = [lax.dot_general(cg[g], bc[sl, :gn], (((1,), (1,)), ((), ())), preferred_element_type=F32)
                      for g in range(SSD_GROUPS)]
            cbm_c[pl.ds(row0 + sub * L, L), :] = jnp.concatenate(cb_mat, axis=1)
            ops.append((xs[sl], bc[sl, :gn], cg, cb_mat, dt[sl]))
        return ops

    def recall():
        ops = []
        for sub in range(SSD_STEP_CHUNKS):
            r = pl.ds(row0 + sub * L, L)
            bc = bc_c[r, :]
            cbm = cbm_c[r, :]
            ops.append((xs_c[r, :], bc[:, :gn], masked_c(bc), [cbm[:, g * L:(g + 1) * L] for g in range(SSD_GROUPS)],
                        dt_c[r, :]))
        return ops

    def scan_chunk(direction, xs, bfull, cg, cb_mat, dt):
        if direction == 0:
            mask = ti >= si
            edge = L - 1
        else:
            mask = si >= ti
            edge = 0
        la = dt * (-jnp.exp(alog_ref[...]))
        tri = mask.astype(BF16)
        csum = sum(jnp.dot(tri, part, preferred_element_type=F32) for part in _split3(la))
        csum_t = csum.T
        emat = emat_ref[direction]
        colb = sum(jnp.dot(part, emat, preferred_element_type=F32) for part in _split3(csum))
        tot = csum[edge:edge + 1, :]
        e_tot = jnp.exp(tot)
        e_in = jnp.exp(csum)
        e_out = jnp.exp(tot - csum)
        ys = []
        for g in range(SSD_GROUPS):
            s_old = state_ref[g]
            y_off = lax.dot_general(cg[g], s_old.astype(BF16), (((1,), (1,)), ((), ())),
                                    preferred_element_type=F32)
            xw = []
            for pr in range(HG // 2):
                h0 = g * HG + 2 * pr
                l0 = direction * SSD_HEADS + h0

                def col(a, l0=l0):
                    return jnp.where(lo_half, a[:, l0:l0 + 1], a[:, l0 + 1:l0 + 2])

                xdt = xs[:, h0 * P:(h0 + 2) * P] * col(dt)
                y_pair = y_off[:, 2 * pr * P:(2 * pr + 2) * P] * col(e_in)
                for hh, half in ((h0, lo_half), (h0 + 1, jnp.logical_not(lo_half))):
                    ll = direction * SSD_HEADS + hh
                    seg = colb[:, hh * L:(hh + 1) * L] - csum_t[ll:ll + 1, :]
                    dec = jnp.exp(jnp.where(mask, seg, NEG_BIG))
                    m = (cb_mat[g] * dec).astype(BF16)
                    y_pair += jnp.dot(m, jnp.where(half, xdt, 0.0).astype(BF16), preferred_element_type=F32)
                ys.append(y_pair)
                xw.append(xdt * col(e_out))
            xw = jnp.concatenate(xw, axis=1).astype(BF16)
            s_new = lax.dot_general(xw, bfull, (((0,), (0,)), ((), ())), preferred_element_type=F32)
            s_scaled = []
            for hl in range(HG):
                lane = direction * SSD_HEADS + g * HG + hl
                s_scaled.append(s_old[hl * P:(hl + 1) * P, :] * e_tot[:, lane:lane + 1])
            state_ref[g] = jnp.concatenate(s_scaled, axis=0) + s_new
        return jnp.concatenate(ys, axis=1)

    @pl.when(jnp.logical_or(j == 0, j == nblocks))
    def _():
        state_ref[...] = jnp.zeros_like(state_ref)

    @pl.when(j < nblocks)
    def _():
        ops = prepare()
        for sub in range(SSD_STEP_CHUNKS):
            yf_ref[pl.ds(row0 + sub * L, L), :] = scan_chunk(0, *ops[sub])

    @pl.when(j >= nblocks)
    def _():
        ops = recall()
        for sub in reversed(range(SSD_STEP_CHUNKS)):
            sl = slice(sub * L, (sub + 1) * L)
            y = yf_ref[pl.ds(row0 + sub * L, L), :] + scan_chunk(1, *ops[sub]) + dsk_ref[...] * ops[sub][0]
            y = y * _silu(z_ref[0, sl, :].astype(F32))
            gw = D_INNER // SSD_GROUPS
            outs = []
            for g in range(SSD_GROUPS):
                yg = y[:, g * gw:(g + 1) * gw]
                outs.append(yg * lax.rsqrt(jnp.mean(yg * yg, axis=-1, keepdims=True) + EPS))
            y_ref[0, sl, :] = (jnp.concatenate(outs, axis=1) * nrm_ref[...]).astype(y_ref.dtype)


def _ssd(z, xbc, dt, conv_w, conv_b, dt_bias, a_log, d_skip, ssd_norm):
    b, t, _ = z.shape
    L = SSD_CHUNK
    lb = SSD_STEP_CHUNKS * L
    nb = t // lb
    hb = lb // BF16_SUBLANES
    nhalo = t // BF16_SUBLANES

    def blk(j):
        return jnp.where(j < nb, j, 2 * nb - 1 - j)

    pad = DT_PAD - 2 * SSD_HEADS
    dtb = jnp.pad(dt_bias.reshape(1, -1).astype(F32), ((0, 0), (0, pad)))
    alog = jnp.pad(a_log.reshape(1, -1).astype(F32), ((0, 0), (0, pad)))
    dsk = jnp.repeat(d_skip.astype(F32), SSD_HEAD_DIM).reshape(1, D_INNER)
    sel = np.arange(DT_PAD)[None, :, None] == (np.arange(2)[:, None, None] * SSD_HEADS
                                                + np.arange(SSD_HEADS)[None, None, :])
    emat = jnp.asarray(np.repeat(sel, L, axis=2), BF16)
    return pl.pallas_call(
        functools.partial(_ssd_kernel, nblocks=nb),
        grid=(b, 2 * nb),
        in_specs=[
            pl.BlockSpec((1, lb, CONV_CH), lambda i, j: (i, blk(j), 0)),
            pl.BlockSpec((1, BF16_SUBLANES, CONV_CH), lambda i, j: (i, jnp.maximum(blk(j) * hb - 1, 0), 0)),
            pl.BlockSpec((1, BF16_SUBLANES, CONV_CH),
                         lambda i, j: (i, jnp.minimum((blk(j) + 1) * hb, nhalo - 1), 0)),
            pl.BlockSpec((1, lb, DT_PAD), lambda i, j: (i, blk(j), 0)),
            pl.BlockSpec((1, lb, D_INNER), lambda i, j: (i, blk(j), 0)),
            _full((CONV_W, CONV_CH)), _full((1, CONV_CH)), _full((1, DT_PAD)), _full((1, DT_PAD)),
            _full((1, D_INNER)), _full((1, D_INNER)), _full((2, DT_PAD, SSD_HEADS * L)),
        ],
        out_specs=pl.BlockSpec((1, lb, D_INNER), lambda i, j: (i, jnp.where(j < nb, nb - 1, 2 * nb - 1 - j), 0)),
        out_shape=jax.ShapeDtypeStruct((b, t, D_INNER), BF16),
        scratch_shapes=[pltpu.VMEM((SSD_GROUPS, (SSD_HEADS // SSD_GROUPS) * SSD_HEAD_DIM, LANES), F32),
                        pltpu.VMEM((t, D_INNER), F32),
                        pltpu.VMEM((t, D_INNER), F32),
                        pltpu.VMEM((t, 2 * SSD_GROUPS * SSD_STATE), BF16),
                        pltpu.VMEM((t, SSD_GROUPS * L), F32),
                        pltpu.VMEM((t, DT_PAD), F32)],
        compiler_params=_cparams(("arbitrary", "arbitrary")),
    )(xbc, xbc, xbc, dt, z, conv_w.astype(F32), conv_b.reshape(1, CONV_CH).astype(F32), dtb, alog, dsk,
      ssd_norm.reshape(1, D_INNER).astype(F32), emat)


def _na_bias(rpb, t):
    r = t // GRID_W
    kh = min(NA_MAX_KH, r)
    nsb = r // NA_SUB_ROWS
    rows = np.arange(r)
    row_start = np.clip(rows - NA_MAX_KH // 2, 0, r - kh)
    r0 = np.arange(nsb) * NA_SUB_ROWS
    kr0 = np.clip(r0 - NA_MAX_KH // 2, 0, r - NA_KEY_ROWS)
    qrow = r0[:, None] + np.arange(NA_SUB_ROWS)[None, :]
    krow = kr0[:, None] + np.arange(NA_KEY_ROWS)[None, :]
    rs = row_start[qrow]
    row_ok = (krow[:, None, :] >= rs[:, :, None]) & (krow[:, None, :] < rs[:, :, None] + kh)
    dr = np.clip(krow[:, None, :] - qrow[:, :, None] + (NA_MAX_KH - 1), 0, 2 * NA_MAX_KH - 2)
    cols = np.arange(GRID_W)
    col_start = np.clip(cols - NA_KW // 2, 0, GRID_W - NA_KW)
    col_ok = (cols[None, :] >= col_start[:, None]) & (cols[None, :] < col_start[:, None] + NA_KW)
    dc = np.clip(cols[None, :] - cols[:, None] + (NA_KW - 1), 0, 2 * NA_KW - 2)
    sel_c = (dc[..., None] == np.arange(2 * NA_KW - 1)) & col_ok[..., None]
    nr = 2 * NA_MAX_KH - 1
    tile_r = jnp.einsum("hrc,xyc->hrxy", rpb.astype(F32), jnp.asarray(sel_c, F32), precision=HIGHEST)
    tile_r = jnp.where(jnp.asarray(col_ok)[None, None], tile_r, NEG_BIG)
    tile_r = jnp.concatenate([tile_r, jnp.full((NA_HEADS, 1, GRID_W, GRID_W), NEG_BIG, F32)], axis=1)
    kp = NA_KEY_ROWS // 2
    code = np.where(row_ok, dr, nr).reshape(nsb * NA_SUB_ROWS * kp, 2)
    pairs, inv = np.unique(code, axis=0, return_inverse=True)
    blocks = jnp.concatenate([tile_r[:, pairs[:, 0]], tile_r[:, pairs[:, 1]]], axis=-1)
    blocks = jnp.moveaxis(blocks, 0, 1).reshape(len(pairs), NA_HEADS * GRID_W * 2 * GRID_W)
    onehot = jnp.asarray(inv.reshape(-1, 1) == np.arange(len(pairs))[None, :], F32)
    bias = jnp.dot(onehot, blocks, precision=HIGHEST)
    return bias.reshape(nsb // NA_SUBS, NA_SUBS, NA_SUB_ROWS, kp, NA_HEADS, GRID_W, 2 * GRID_W).astype(BF16)


def _natten_kernel(qkv_ref, bias_ref, qg_ref, kg_ref, seg_ref, o_ref, *, grid_rows):
    nq = NA_SUB_ROWS * GRID_W
    nk = NA_KEY_ROWS * GRID_W
    rb = pl.program_id(0)
    seg = seg_ref[...]
    lane_h = lax.broadcasted_iota(jnp.int32, (1, NA_WIDTH), 1) // NA_HEAD_DIM
    for sub in range(NA_SUBS):
        r0 = rb * NA_QUERY_ROWS + sub * NA_SUB_ROWS
        kr0 = jnp.clip(r0 - NA_MAX_KH // 2, 0, grid_rows - NA_KEY_ROWS)
        q0 = pl.multiple_of(r0 * GRID_W, nq)
        k0 = pl.multiple_of(kr0 * GRID_W, NA_MAX_KH // 2 * GRID_W)
        q = qkv_ref[0, pl.ds(q0, nq), 0:NA_WIDTH].astype(F32)
        k = qkv_ref[0, pl.ds(k0, nk), NA_WIDTH:2 * NA_WIDTH].astype(F32)
        v = qkv_ref[0, pl.ds(k0, nk), 2 * NA_WIDTH:3 * NA_WIDTH]
        qms = sum(jnp.dot(part, seg, preferred_element_type=F32) for part in _split3(q * q)[:2])
        kms = sum(jnp.dot(part, seg, preferred_element_type=F32) for part in _split3(k * k)[:2])
        qn = q * lax.rsqrt(qms + EPS) * (qg_ref[...] * NA_HEAD_DIM ** -0.5)
        kn = (k * lax.rsqrt(kms + EPS) * kg_ref[...]).astype(BF16)
        acc = jnp.zeros((nq, NA_WIDTH), F32)
        for h in range(NA_HEADS):
            hm = lane_h == h
            s = lax.dot_general(jnp.where(hm, qn, 0.0).astype(BF16), kn, (((1,), (1,)), ((), ())),
                                preferred_element_type=F32)
            bias = jnp.concatenate(
                [jnp.concatenate([bias_ref[0, sub, qr, kc, h] for kc in range(NA_KEY_ROWS // 2)], axis=1)
                 for qr in range(NA_SUB_ROWS)], axis=0)
            s = s + bias.astype(F32)
            p = jnp.exp(s - jnp.max(s, axis=-1, keepdims=True))
            l = jnp.sum(p, axis=-1, keepdims=True)
            o = jnp.dot(p.astype(BF16), v, preferred_element_type=F32)
            acc += jnp.where(hm, o / l, 0.0)
        o_ref[0, sub * nq:(sub + 1) * nq, :] = acc.astype(o_ref.dtype)


def _natten(qkv, bias, q_norm, k_norm):
    b, t, _ = qkv.shape
    r = t // GRID_W
    nrb = r // NA_QUERY_ROWS
    nq = NA_QUERY_ROWS * GRID_W
    head = jnp.arange(NA_WIDTH) // NA_HEAD_DIM
    seg = ((head[:, None] == head[None, :]).astype(F32) / NA_HEAD_DIM).astype(BF16)
    return pl.pallas_call(
        functools.partial(_natten_kernel, grid_rows=r),
        grid=(nrb, b),
        in_specs=[pl.BlockSpec((1, t, 3 * NA_WIDTH), lambda i, j: (j, 0, 0)),
                  pl.BlockSpec((1,) + bias.shape[1:], lambda i, j: (i,) + (0,) * (bias.ndim - 1)),
                  _full((1, NA_WIDTH)), _full((1, NA_WIDTH)), _full((NA_WIDTH, NA_WIDTH))],
        out_specs=pl.BlockSpec((1, nq, NA_WIDTH), lambda i, j: (j, i, 0)),
        out_shape=jax.ShapeDtypeStruct((b, t, NA_WIDTH), BF16),
        compiler_params=_cparams(("arbitrary", "arbitrary")),
    )(qkv, bias, jnp.tile(q_norm.astype(F32), NA_HEADS).reshape(1, NA_WIDTH),
      jnp.tile(k_norm.astype(F32), NA_HEADS).reshape(1, NA_WIDTH), seg)


def _memkv_kernel(m_ref, g_ref, w_ref, kg_ref, k_ref, v_ref):
    x = m_ref[0]
    ms = jnp.mean(x * x, axis=-1, keepdims=True)
    h = (x * lax.rsqrt(ms + EPS) * g_ref[...]).astype(BF16)
    kv = jnp.dot(h, w_ref[...], preferred_element_type=F32)
    ks = []
    for hd in range(XA_HEADS):
        kh = kv[:, hd * XA_HEAD_DIM:(hd + 1) * XA_HEAD_DIM]
        ks.append(kh * lax.rsqrt(jnp.mean(kh * kh, axis=-1, keepdims=True) + EPS))
    k_ref[0] = (jnp.concatenate(ks, axis=1) * kg_ref[...]).astype(k_ref.dtype)
    v_ref[0] = kv[:, XA_WIDTH:].astype(v_ref.dtype)


def _memkv(mem, norm_mem, w_xkv, xk_norm):
    b, m, _ = mem.shape
    return pl.pallas_call(
        _memkv_kernel,
        grid=(b,),
        in_specs=[pl.BlockSpec((1, m, D_MODEL), lambda i: (i, 0, 0)),
                  _full((1, D_MODEL)), _full((D_MODEL, 2 * XA_WIDTH)), _full((1, XA_WIDTH))],
        out_specs=[pl.BlockSpec((1, m, XA_WIDTH), lambda i: (i, 0, 0))] * 2,
        out_shape=[jax.ShapeDtypeStruct((b, m, XA_WIDTH), BF16)] * 2,
        compiler_params=_cparams(("arbitrary",)),
    )(mem, norm_mem.reshape(1, D_MODEL).astype(F32), w_xkv,
      jnp.tile(xk_norm.astype(F32), XA_HEADS).reshape(1, XA_WIDTH))


def _mixer_kernel(x_ref, ys_ref, yn_ref, u_ref, up_ref, un_ref, k_ref, v_ref,
                  pw_ref, psc_ref, wo_ref, gxa_ref, wq_ref, qg_ref, wxo_ref, gff_ref,
                  wrb_ref, br_ref, ltri_ref,
                  x2_ref, hf_ref, meta_ref, cnt_ref, carry_ref, *, seq):
    tm = x_ref.shape[1]
    halo = BF16_SUBLANES
    bi = pl.program_id(0)
    i = pl.program_id(1)
    nt = pl.num_programs(1)

    @pl.when(jnp.logical_and(bi == 0, i == 0))
    def _():
        carry_ref[...] = jnp.zeros_like(carry_ref)

    u = u_ref[0].astype(F32)
    up = up_ref[0].astype(F32) * (i > 0).astype(F32)
    un = un_ref[0].astype(F32) * (i < nt - 1).astype(F32)
    cat = jnp.concatenate([up, u, un], axis=0)
    n = tm + 2 * halo

    def sh(a, k):
        return pltpu.roll(a, (-k) % n, 0)

    a2 = cat + sh(cat, -1)
    a4 = sh(a2, 1) + sh(a2, -1)
    a8 = sh(a4, 2) + sh(a4, -2)
    a16 = sh(a8, 4) + sh(a8, -4)
    lane_g = lax.broadcasted_iota(jnp.int32, (1, POOL_WIDTH), 1) // POOL_GROUP_DIM
    wsum = jnp.where(lane_g == 0, a2, jnp.where(lane_g == 1, a4, jnp.where(lane_g == 2, a8, a16)))
    wsum = wsum[halo:halo + tm, :]
    half = jnp.where(lane_g == 0, POOL_WINDOWS[0] // 2,
                     jnp.where(lane_g == 1, POOL_WINDOWS[1] // 2,
                               jnp.where(lane_g == 2, POOL_WINDOWS[2] // 2, POOL_WINDOWS[3] // 2)))
    tpos = i * tm + lax.broadcasted_iota(jnp.int32, (tm, 1), 0)
    cnt = (jnp.minimum(tpos + half, seq) - jnp.maximum(tpos - half, 0)).astype(F32)
    d = wsum / cnt - u
    ypool = jnp.dot(d.astype(BF16), pw_ref[...], preferred_element_type=F32) * psc_ref[...]

    mix = jnp.dot(ys_ref[0], wo_ref[0:D_INNER, :], preferred_element_type=F32)
    mix += jnp.dot(yn_ref[0], wo_ref[D_INNER:D_INNER + NA_WIDTH, :], preferred_element_type=F32)
    mix += jnp.dot(ypool.astype(BF16), wo_ref[D_INNER + NA_WIDTH:, :], preferred_element_type=F32)
    x1 = x_ref[0] + mix

    hn = (x1 * lax.rsqrt(jnp.mean(x1 * x1, axis=-1, keepdims=True) + EPS) * gxa_ref[...]).astype(BF16)
    q = jnp.dot(hn, wq_ref[...], preferred_element_type=F32)
    kk = k_ref[0]
    vv = v_ref[0]
    scale = XA_HEAD_DIM ** -0.5
    outs = []
    for hd in range(XA_HEADS):
        sl = slice(hd * XA_HEAD_DIM, (hd + 1) * XA_HEAD_DIM)
        qh = q[:, sl]
        qh = (qh * lax.rsqrt(jnp.mean(qh * qh, axis=-1, keepdims=True) + EPS) * qg_ref[:, sl]).astype(BF16)
        s = lax.dot_general(qh, kk[:, sl], (((1,), (1,)), ((), ())), preferred_element_type=F32) * scale
        p = jnp.exp(s - jnp.max(s, axis=-1, keepdims=True))
        l = jnp.sum(p, axis=-1, keepdims=True)
        outs.append(jnp.dot(p.astype(BF16), vv[:, sl], preferred_element_type=F32) / l)
    att = jnp.concatenate(outs, axis=1).astype(BF16)
    x2 = x1 + jnp.dot(att, wxo_ref[...], preferred_element_type=F32)
    x2_ref[0] = x2

    hf = x2 * lax.rsqrt(jnp.mean(x2 * x2, axis=-1, keepdims=True) + EPS) * gff_ref[...]
    h_hi = hf.astype(BF16)
    hf_ref[0, :, :HALF] = pltpu.bitcast(_pack_bf16_pairs(h_hi.astype(F32)), jnp.int32)
    h_lo = (hf - h_hi.astype(F32)).astype(BF16)
    both = jnp.dot(h_hi, wrb_ref[...], preferred_element_type=F32)
    logits = (both[:, :LANES] + both[:, LANES:]
              + jnp.dot(h_lo, wrb_ref[:, :LANES], preferred_element_type=F32)) + br_ref[...]
    lane = lax.broadcasted_iota(jnp.int32, (1, LANES), 1)
    lane_f = lane.astype(F32)
    lane_grp = (lane // EXPERTS_PER_GROUP).astype(F32)
    is_g = jnp.logical_and(lane >= N_EXPERTS, lane < N_EXPERTS + N_EXPERT_GROUPS)
    gl = jnp.where(is_g, logits, NEG_BIG)
    gmax = jnp.max(gl, axis=-1, keepdims=True)
    g_sel = jnp.min(jnp.where(gl == gmax, lane_f, float(LANES)), axis=-1, keepdims=True) - N_EXPERTS
    g_gate = 1.0 / jnp.sum(jnp.where(is_g, jnp.exp(gl - gmax), 0.0), axis=-1, keepdims=True)
    in_grp = jnp.logical_and(lane < N_EXPERTS, lane_grp == g_sel)
    el = jnp.where(in_grp, logits, NEG_BIG)
    v1 = jnp.max(el, axis=-1, keepdims=True)
    e0 = jnp.min(jnp.where(el == v1, lane_f, float(LANES)), axis=-1, keepdims=True)
    el2 = jnp.where(lane_f == e0, NEG_BIG, el)
    v2 = jnp.max(el2, axis=-1, keepdims=True)
    e1 = jnp.min(jnp.where(el2 == v2, lane_f, float(LANES)), axis=-1, keepdims=True)
    w1 = jnp.exp(v2 - v1)
    gate0 = g_gate / (1.0 + w1)
    gate1 = g_gate * w1 / (1.0 + w1)

    base = g_sel * EXPERTS_PER_GROUP
    ea = jnp.minimum(e0, e1) - base
    eb = jnp.maximum(e0, e1) - base
    combo = g_sel * PAIRS_PER_GROUP + ea * EXPERTS_PER_GROUP - ea * (ea + 1.0) * 0.5 + (eb - ea - 1.0)
    gate_a = jnp.where(e0 < e1, gate0, gate1)
    gate_b = jnp.where(e0 < e1, gate1, gate0)
    hf_ref[0, :, HALF:] = pltpu.bitcast(jnp.where(lane == 0, gate_a, 0.0) + jnp.where(lane == 1, gate_b, 0.0), jnp.int32)

    oh = lane_f == combo
    cnt_tok = oh.astype(F32)
    before = jnp.dot(ltri_ref[...], cnt_tok.astype(BF16), preferred_element_type=F32) + carry_ref[0:1, :]
    rank = jnp.sum(jnp.where(oh, before, 0.0), axis=-1, keepdims=True)
    new_carry = carry_ref[0:1, :] + jnp.sum(cnt_tok, axis=0, keepdims=True)
    carry_ref[...] = jnp.broadcast_to(new_carry, carry_ref.shape)
    cnt_ref[...] = jnp.broadcast_to(new_carry, cnt_ref.shape)

    slab = jnp.where(lane == 0, combo, 0.0) + jnp.where(lane == 1, rank, 0.0)
    meta_ref[0] = slab.T[0:8, :]


def _mixer(x, y_ssd, y_na, u, kmem, vmem, p):
    b, t, _ = x.shape
    tm = TOKEN_TILE
    nt = t // tm
    hb = tm // BF16_SUBLANES
    nhalo = t // BF16_SUBLANES
    ltri = (jnp.arange(tm)[:, None] > jnp.arange(tm)[None, :]).astype(BF16)
    tok = lambda w: pl.BlockSpec((1, tm, w), lambda i, j: (i, j, 0))
    mem = pl.BlockSpec((1, kmem.shape[1], XA_WIDTH), lambda i, j: (i, 0, 0))
    weights = (p["pool_bd"], p["pool_scale"], p["w_out"], p["norm_xa"], p["w_xq"], p["xq_norm"], p["w_xo"],
               p["norm_ffn"], p["w_r_both"], p["b_r"], ltri)
    return pl.pallas_call(
        functools.partial(_mixer_kernel, seq=t),
        grid=(b, nt),
        in_specs=[tok(D_MODEL), tok(D_INNER), tok(NA_WIDTH), tok(POOL_WIDTH),
                  pl.BlockSpec((1, BF16_SUBLANES, POOL_WIDTH), lambda i, j: (i, jnp.maximum(j * hb - 1, 0), 0)),
                  pl.BlockSpec((1, BF16_SUBLANES, POOL_WIDTH),
                               lambda i, j: (i, jnp.minimum((j + 1) * hb, nhalo - 1), 0)),
                  mem, mem] + [_full(w.shape) for w in weights],
        out_specs=[tok(D_MODEL), tok(ROW_WORDS),
                   pl.BlockSpec((1, 8, tm), lambda i, j: (i * nt + j, 0, 0)),
                   pl.BlockSpec((8, LANES), lambda i, j: (0, 0))],
        out_shape=[jax.ShapeDtypeStruct((b, t, D_MODEL), F32),
                   jax.ShapeDtypeStruct((b, t, ROW_WORDS), jnp.int32),
                   jax.ShapeDtypeStruct((b * nt, 8, tm), F32),
                   jax.ShapeDtypeStruct((8, LANES), F32)],
        scratch_shapes=[pltpu.VMEM((8, LANES), F32)],
        compiler_params=_cparams(("arbitrary", "arbitrary")),
    )(x, y_ssd, y_na, u, u, u, kmem, vmem, *weights)


def _zero_pad_kernel(pstart_ref, cnt_ref, xs_in, zero_ref, xs_ref, sem, *, bm):
    del xs_in

    def fill(cm, wait):
        n_c = cnt_ref[cm]
        npad = (bm - n_c % bm) % bm
        head = (8 - n_c % 8) % 8
        base = pstart_ref[cm] + n_c

        def go(cp):
            if wait:
                cp.wait()
            else:
                cp.start()

        def one(r, _):
            go(pltpu.make_async_copy(zero_ref.at[pl.ds(0, 1)], xs_ref.at[pl.ds(base + r, 1)], sem))
            return 0

        lax.fori_loop(0, head, one, 0)
        rest = npad - head
        off = base + head
        size = bm // 2
        while size >= 8:
            @pl.when(rest & size != 0)
            def _(off=off, size=size):
                go(pltpu.make_async_copy(zero_ref.at[pl.ds(0, size)],
                                         xs_ref.at[pl.ds(pl.multiple_of(off, 8), size)], sem))

            off = off + (rest & size)
            size //= 2
        return 0

    lax.fori_loop(0, N_COMBOS, lambda cm, _: fill(cm, False), 0)
    lax.fori_loop(0, N_COMBOS, lambda cm, _: fill(cm, True), 0)


def _zero_pad_rows(xs, pstart, counts, bm):
    return pl.pallas_call(
        functools.partial(_zero_pad_kernel, bm=bm),
        grid_spec=pltpu.PrefetchScalarGridSpec(
            num_scalar_prefetch=2,
            grid=(1,),
            in_specs=[pl.BlockSpec(memory_space=pl.ANY),
                      pl.BlockSpec((bm // 2, ROW_WORDS), lambda i, *_: (0, 0))],
            out_specs=pl.BlockSpec(memory_space=pl.ANY),
            scratch_shapes=[pltpu.SemaphoreType.DMA(())]),
        out_shape=jax.ShapeDtypeStruct(xs.shape, xs.dtype),
        input_output_aliases={2: 0},
        compiler_params=_cparams(("arbitrary",)),
    )(pstart, counts, xs, jnp.zeros((bm // 2, ROW_WORDS), xs.dtype))


def _sc_rows(kind, src, idx, n_out):
    n = idx.shape[0]
    width = src.shape[1]
    info = plsc.get_sparse_core_info()
    workers = info.num_cores * info.num_subcores
    per_worker = n // workers
    assert per_worker * workers == n and per_worker % SC_GATHER_ROWS == 0, (n, workers)
    mesh = plsc.VectorSubcoreMesh(core_axis_name="core", subcore_axis_name="subcore")

    def body(src_hbm, idx_hbm, out_hbm, idx_v, rows_v, sem):
        base = (lax.axis_index("subcore") * info.num_cores + lax.axis_index("core")) * per_worker

        @pl.loop(0, per_worker // SC_GATHER_ROWS)
        def _(j):
            off = pl.multiple_of(base + j * SC_GATHER_ROWS, SC_GATHER_ROWS)
            pltpu.sync_copy(idx_hbm.at[pl.ds(off, SC_GATHER_ROWS)], idx_v)
            if kind == "gather":
                pltpu.async_copy(src_hbm.at[idx_v], rows_v, sem).wait()
                pltpu.sync_copy(rows_v, out_hbm.at[pl.ds(off, SC_GATHER_ROWS)])
            else:
                pltpu.sync_copy(src_hbm.at[pl.ds(off, SC_GATHER_ROWS)], rows_v)
                pltpu.async_copy(rows_v, out_hbm.at[idx_v], sem).wait()

    return pl.kernel(
        body, mesh=mesh, out_type=jax.ShapeDtypeStruct((n_out, width), src.dtype),
        scratch_types=[pltpu.VMEM((SC_GATHER_ROWS,), jnp.int32), pltpu.VMEM((SC_GATHER_ROWS, width), src.dtype),
                       pltpu.SemaphoreType.DMA])(src, idx)


def _experts_kernel(ea_ref, eb_ref, nused_ref, x_ref, wga_ref, wua_ref, wda_ref, wgb_ref, wub_ref, wdb_ref, y_ref):
    @pl.when(pl.program_id(0) < nused_ref[0])
    def _():
        x = _unpack_bf16_pairs(pltpu.bitcast(x_ref[:, :HALF], U32)).astype(BF16)
        gates = pltpu.bitcast(x_ref[:, HALF:], F32)

        def mlp(wg_ref, wu_ref, wd_ref):
            hg = jnp.dot(x, wg_ref[0], preferred_element_type=F32)
            hu = jnp.dot(x, wu_ref[0], preferred_element_type=F32)
            return jnp.dot((_silu(hg) * hu).astype(BF16), wd_ref[0], preferred_element_type=F32)

        y = mlp(wga_ref, wua_ref, wda_ref) * gates[:, 0:1] + mlp(wgb_ref, wub_ref, wdb_ref) * gates[:, 1:2]
        y_ref[...] = pltpu.bitcast(_pack_bf16_pairs(y.astype(BF16).astype(F32)), jnp.int32)


def _experts(xs, blk_ea, blk_eb, nused, w_gate, w_up, w_down, bm):
    nblk = blk_ea.shape[0]
    row = lambda j, ea, eb, nu: (jnp.minimum(j, nu[0] - 1), 0)
    row_out = lambda j, ea, eb, nu: (jnp.where(j < nu[0], j, nblk - 1), 0)
    sel_a = lambda j, ea, eb, nu: (ea[j], 0, 0)
    sel_b = lambda j, ea, eb, nu: (eb[j], 0, 0)
    up = lambda sel: pl.BlockSpec((1, D_MODEL, D_EXPERT), sel)
    down = lambda sel: pl.BlockSpec((1, D_EXPERT, D_MODEL), sel)
    return pl.pallas_call(
        _experts_kernel,
        grid_spec=pltpu.PrefetchScalarGridSpec(
            num_scalar_prefetch=3,
            grid=(nblk,),
            in_specs=[pl.BlockSpec((bm, ROW_WORDS), row),
                      up(sel_a), up(sel_a), down(sel_a), up(sel_b), up(sel_b), down(sel_b)],
            out_specs=pl.BlockSpec((bm, HALF), row_out)),
        out_shape=jax.ShapeDtypeStruct((nblk * bm, HALF), jnp.int32),
        compiler_params=_cparams(("arbitrary",)),
    )(blk_ea, blk_eb, nused, xs, w_gate, w_up, w_down, w_gate, w_up, w_down)


def _add_rows_kernel(x_ref, yg_ref, o_ref):
    o_ref[...] = x_ref[...] + _unpack_bf16_pairs(pltpu.bitcast(yg_ref[...], U32))


def _add_inproj_kernel(x2_ref, yg_ref, g_ref, w_ref, x_ref, z_ref, xbc_ref, qkv_ref, u_ref, dt_ref):
    _add_rows_kernel(x2_ref, yg_ref, x_ref)
    _inproj_kernel(x_ref, g_ref, w_ref, z_ref, xbc_ref, qkv_ref, u_ref, dt_ref)


def _combine(x2d, yg, gain=None, w_cat=None):
    n = x2d.shape[0]
    tm = TOKEN_TILE
    tile = lambda w: pl.BlockSpec((tm, w), lambda i: (i, 0))
    if w_cat is None:
        return pl.pallas_call(
            _add_rows_kernel, grid=(n // tm,), in_specs=[tile(D_MODEL), tile(HALF)], out_specs=tile(D_MODEL),
            out_shape=jax.ShapeDtypeStruct((n, D_MODEL), F32), compiler_params=_cparams(("arbitrary",)))(x2d, yg)
    widths = (D_MODEL,) + _INPROJ_WIDTHS
    dtypes = (F32,) + _INPROJ_DTYPES
    return pl.pallas_call(
        _add_inproj_kernel, grid=(n // tm,),
        in_specs=[tile(D_MODEL), tile(HALF), _full((1, D_MODEL)), _full(w_cat.shape)],
        out_specs=[tile(w) for w in widths],
        out_shape=[jax.ShapeDtypeStruct((n, w), d) for w, d in zip(widths, dtypes)],
        compiler_params=_cparams(("arbitrary",)),
    )(x2d, yg, gain.reshape(1, D_MODEL), w_cat)


_PAIR_A = np.array([a for a in range(EXPERTS_PER_GROUP) for _ in range(a + 1, EXPERTS_PER_GROUP)], np.int32)
_PAIR_B = np.array([b for a in range(EXPERTS_PER_GROUP) for b in range(a + 1, EXPERTS_PER_GROUP)], np.int32)


def _expert_block(n):
    return 2 * EXPERT_BLOCK_MIN if n >= 2 * 2 * EXPERT_BLOCK_MIN * N_COMBOS else EXPERT_BLOCK_MIN


def _moe(x2, hf, meta, counts, w_gate, w_up, w_down):
    b, t, _ = x2.shape
    n = b * t
    bm = _expert_block(n)
    nblk = (n + N_COMBOS * (bm - 1) + bm - 1) // bm
    cnt = counts[0, :N_COMBOS].astype(jnp.int32)
    psz = (cnt + bm - 1) // bm * bm
    pend = jnp.cumsum(psz)
    pstart = (pend - psz).astype(jnp.int32)
    nused = jnp.maximum(pend[-1] // bm, 1).astype(jnp.int32).reshape(1)
    blk = jnp.minimum(jnp.arange(nblk, dtype=jnp.int32), nused[0] - 1)
    blk_c = jnp.minimum(jnp.sum(pend[None, :] <= (blk * bm)[:, None], axis=1), N_COMBOS - 1).astype(jnp.int32)
    grp = blk_c // PAIRS_PER_GROUP
    blk_ea = (grp * EXPERTS_PER_GROUP + jnp.asarray(_PAIR_A)[blk_c % PAIRS_PER_GROUP]).astype(jnp.int32)
    blk_eb = (grp * EXPERTS_PER_GROUP + jnp.asarray(_PAIR_B)[blk_c % PAIRS_PER_GROUP]).astype(jnp.int32)
    ids = meta.astype(jnp.int32)
    combo = ids[:, 0, :].reshape(n)
    dest = ids[:, 1, :].reshape(n) + jnp.sum(jnp.where(combo[:, None] == jnp.arange(N_COMBOS)[None, :], pstart[None, :], 0),
                                             axis=1)
    xs = _sc_rows("scatter", hf.reshape(n, ROW_WORDS), dest, nblk * bm)
    xs = _zero_pad_rows(xs, pstart, cnt, bm)
    y = _experts(xs, blk_ea, blk_eb, nused, w_gate, w_up, w_down, bm)
    yg = _sc_rows("gather", y, dest, n)
    return x2.reshape(n, D_MODEL), yg


def _prep_layer(lp):
    w_in = lp["w_in"]
    c0 = D_INNER + CONV_CH
    c1 = c0 + 2 * SSD_HEADS
    w_cat = jnp.concatenate([w_in[:, :c0], w_in[:, c1:], w_in[:, c0:c1],
                             jnp.zeros((D_MODEL, DT_PAD - 2 * SSD_HEADS), w_in.dtype)], axis=1).astype(BF16)
    pool_bd = jnp.zeros((POOL_WIDTH, POOL_WIDTH), F32)
    for g in range(POOL_GROUPS):
        sl = slice(g * POOL_GROUP_DIM, (g + 1) * POOL_GROUP_DIM)
        pool_bd = pool_bd.at[sl, sl].set(lp["pool_w"][g].astype(F32))
    w_r = jnp.concatenate([lp["w_router_expert"], lp["w_router_group"],
                           jnp.zeros((D_MODEL, LANES - N_EXPERTS - N_EXPERT_GROUPS), F32)], axis=1).astype(F32)
    w_r_hi = w_r.astype(BF16)
    w_r_lo = (w_r - w_r_hi.astype(F32)).astype(BF16)
    b_r = jnp.concatenate([lp["b_router_expert"], lp["b_router_group"],
                           jnp.zeros((LANES - N_EXPERTS - N_EXPERT_GROUPS,), F32)]).reshape(1, LANES).astype(F32)
    row = lambda a, w: a.reshape(1, w).astype(F32)
    return dict(
        norm_mix=lp["norm_mix"], w_cat=w_cat,
        conv_w=lp["conv_w"], conv_b=lp["conv_b"], dt_bias=lp["dt_bias"], a_log=lp["a_log"],
        d_skip=lp["d_skip"], ssd_norm=lp["ssd_norm"],
        na_q_norm=lp["na_q_norm"], na_k_norm=lp["na_k_norm"], na_rpb=lp["na_rpb"],
        pool_bd=pool_bd.astype(BF16), pool_scale=row(lp["pool_scale"], POOL_WIDTH),
        w_out=lp["w_out"].astype(BF16), norm_xa=row(lp["norm_xa"], D_MODEL),
        norm_mem=lp["norm_mem"], w_xq=lp["w_xq"].astype(BF16), w_xkv=lp["w_xkv"].astype(BF16),
        xq_norm=row(jnp.tile(lp["xq_norm"], XA_HEADS), XA_WIDTH), xk_norm=lp["xk_norm"],
        w_xo=lp["w_xo"].astype(BF16), norm_ffn=row(lp["norm_ffn"], D_MODEL),
        w_r_both=jnp.concatenate([w_r_hi, w_r_lo], axis=1), b_r=b_r,
        w_e_gate=lp["w_e_gate"].astype(BF16), w_e_up=lp["w_e_up"].astype(BF16),
        w_e_down=lp["w_e_down"].astype(BF16),
    )


def _layer(x, pending, mem, p, na_bias):
    b, m, _ = mem.shape
    if pending is None:
        t = x.shape[1]
        z, xbc, qkv, u, dt = _inproj(x.reshape(b * t, D_MODEL), p["norm_mix"], p["w_cat"])
    else:
        t = pending[0].shape[0] // b
        x, z, xbc, qkv, u, dt = _combine(*pending, p["norm_mix"], p["w_cat"])
        x = x.reshape(b, t, D_MODEL)
    r3 = lambda a: a.reshape(b, t, a.shape[-1])
    y_ssd = _ssd(r3(z), r3(xbc), r3(dt), p["conv_w"], p["conv_b"], p["dt_bias"], p["a_log"], p["d_skip"],
                 p["ssd_norm"])
    y_na = _natten(r3(qkv), na_bias, p["na_q_norm"], p["na_k_norm"])
    kmem, vmem = _memkv(mem, p["norm_mem"], p["w_xkv"], p["xk_norm"])
    x2, hf, meta, counts = _mixer(x, y_ssd, y_na, r3(u), kmem, vmem, p)
    return _moe(x2, hf, meta, counts, p["w_e_gate"], p["w_e_up"], p["w_e_down"])


_LAYER_KEYS = ("norm_mix", "w_in", "conv_w", "conv_b", "dt_bias", "a_log", "d_skip", "ssd_norm", "na_q_norm",
               "na_k_norm", "na_rpb", "pool_w", "pool_scale", "w_out", "norm_xa", "norm_mem", "w_xq", "w_xkv",
               "xq_norm", "xk_norm", "w_xo", "norm_ffn", "w_router_group", "b_router_group", "w_router_expert",
               "b_router_expert", "w_e_gate", "w_e_up", "w_e_down")


def kernel(x_prompt, x_sample, mem_prompt, mem_sample, norm_mix, w_in, conv_w, conv_b, dt_bias, a_log, d_skip, ssd_norm, na_q_norm, na_k_norm, na_rpb, pool_w, pool_scale, w_out, norm_xa, norm_mem, w_xq, w_xkv, xq_norm, xk_norm, w_xo, norm_ffn, w_router_group, b_router_group, w_router_expert, b_router_expert, w_e_gate, w_e_up, w_e_down):
    stacked = dict(zip(_LAYER_KEYS, (norm_mix, w_in, conv_w, conv_b, dt_bias, a_log, d_skip, ssd_norm, na_q_norm,
                                     na_k_norm, na_rpb, pool_w, pool_scale, w_out, norm_xa, norm_mem, w_xq, w_xkv,
                                     xq_norm, xk_norm, w_xo, norm_ffn, w_router_group, b_router_group,
                                     w_router_expert, b_router_expert, w_e_gate, w_e_up, w_e_down)))
    depth = w_in.shape[0]
    layers = [_prep_layer({k: v[l] for k, v in stacked.items()}) for l in range(depth)]

    bias_cache = {}

    def trunk(x, mem):
        t = x.shape[1]
        pending = None
        for l, lp in enumerate(layers):
            if (l, t) not in bias_cache:
                bias_cache[(l, t)] = _na_bias(lp["na_rpb"], t)
            pending = _layer(x, pending, mem, lp, bias_cache[(l, t)])
            x = None
        return _combine(*pending).reshape(mem.shape[0], t, D_MODEL)

    return trunk(x_prompt, mem_prompt), trunk(x_sample, mem_sample)
```

```python
import functools

import jax
import jax.numpy as jnp
import numpy as np
from jax import lax
from jax.experimental import pallas as pl
from jax.experimental.pallas import tpu as pltpu
from jax.experimental.pallas import tpu_sc as plsc

F32 = jnp.float32
BF16 = jnp.bfloat16
U32 = jnp.uint32
HIGHEST = lax.Precision.HIGHEST

D_MODEL = 1024
GRID_W = 64
EPS = 1e-6
SSD_HEAD_DIM = 64
D_INNER = D_MODEL // 2
SSD_HEADS = D_INNER // SSD_HEAD_DIM
SSD_GROUPS = 2
SSD_STATE = 64
SSD_CHUNK = 128
CONV_W = 4
CONV_CH = D_INNER + 2 * SSD_GROUPS * SSD_STATE
NA_HEADS = 4
NA_HEAD_DIM = D_MODEL // 16
NA_WIDTH = NA_HEADS * NA_HEAD_DIM
NA_MAX_KH = 8
NA_KW = 16
POOL_WINDOWS = (2, 4, 8, 16)
POOL_GROUPS = 4
POOL_WIDTH = D_MODEL - D_INNER - NA_WIDTH
POOL_GROUP_DIM = POOL_WIDTH // POOL_GROUPS
XA_HEADS = 4
XA_HEAD_DIM = D_MODEL // 8
XA_WIDTH = XA_HEADS * XA_HEAD_DIM
N_EXPERT_GROUPS = 4
EXPERTS_PER_GROUP = 8
N_EXPERTS = N_EXPERT_GROUPS * EXPERTS_PER_GROUP
D_EXPERT = D_MODEL // 4
PAIRS_PER_GROUP = EXPERTS_PER_GROUP * (EXPERTS_PER_GROUP - 1) // 2
N_COMBOS = N_EXPERT_GROUPS * PAIRS_PER_GROUP

LANES = 128
BF16_SUBLANES = 16
VMEM_LIMIT_BYTES = 56 * 1024 * 1024

TOKEN_TILE = 1024
NA_QUERY_ROWS = 16
NA_SUB_ROWS = 8
NA_SUBS = NA_QUERY_ROWS // NA_SUB_ROWS
NA_KEY_ROWS = NA_SUB_ROWS + NA_MAX_KH
EXPERT_BLOCK_MIN = 128
HALF = D_MODEL // 2
ROW_WORDS = HALF + LANES
SSD_STEP_CHUNKS = 8
DT_PAD = LANES
SC_GATHER_ROWS = 128
NEG_BIG = -1e30


def _cparams(sem):
    return pltpu.CompilerParams(dimension_semantics=sem, vmem_limit_bytes=VMEM_LIMIT_BYTES)


def _sigmoid(x):
    return 1.0 / (1.0 + jnp.exp(-x))


def _silu(x):
    return x * _sigmoid(x)


def _softplus(x):
    return jnp.maximum(x, 0.0) + jnp.log(1.0 + jnp.exp(-jnp.abs(x)))


def _pack_bf16_pairs(v):
    k = v.shape[1] // 2
    bits = pltpu.bitcast(v, U32)
    return (bits[:, :k] >> 16) | (bits[:, k:] & jnp.uint32(0xFFFF0000))


def _unpack_bf16_pairs(w):
    lo = pltpu.bitcast(w << 16, F32)
    hi = pltpu.bitcast(w & jnp.uint32(0xFFFF0000), F32)
    return jnp.concatenate([lo, hi], axis=1)


def _full(shape):
    n = len(shape)
    return pl.BlockSpec(shape, lambda *_: (0,) * n)


def _inproj_kernel(x_ref, g_ref, w_ref, z_ref, xbc_ref, qkv_ref, u_ref, dt_ref):
    x = x_ref[...]
    ms = jnp.mean(x * x, axis=-1, keepdims=True)
    h = (x * lax.rsqrt(ms + EPS) * g_ref[...]).astype(BF16)
    o = 0
    for ref in (z_ref, xbc_ref, qkv_ref, u_ref, dt_ref):
        w = ref.shape[-1]
        ref[...] = jnp.dot(h, w_ref[:, o:o + w], preferred_element_type=F32).astype(ref.dtype)
        o += w


_INPROJ_WIDTHS = (D_INNER, CONV_CH, 3 * NA_WIDTH, POOL_WIDTH, DT_PAD)
_INPROJ_DTYPES = (BF16, BF16, BF16, BF16, F32)


def _inproj(x2d, gain, w_cat):
    n = x2d.shape[0]
    tm = TOKEN_TILE
    widths, dtypes = _INPROJ_WIDTHS, _INPROJ_DTYPES
    return pl.pallas_call(
        _inproj_kernel,
        grid=(n // tm,),
        in_specs=[pl.BlockSpec((tm, D_MODEL), lambda i: (i, 0)),
                  _full((1, D_MODEL)),
                  _full(w_cat.shape)],
        out_specs=[pl.BlockSpec((tm, w), lambda i: (i, 0)) for w in widths],
        out_shape=[jax.ShapeDtypeStruct((n, w), d) for w, d in zip(widths, dtypes)],
        compiler_params=_cparams(("arbitrary",)),
    )(x2d, gain.reshape(1, D_MODEL), w_cat)


def _split3(a):
    hi = a.astype(BF16)
    r = a - hi.astype(F32)
    mid = r.astype(BF16)
    lo = (r - mid.astype(F32)).astype(BF16)
    return hi, mid, lo


def _ssd_kernel(xc_ref, xp_ref, xn_ref, dt_ref, z_ref, cw_ref, cb_ref, dtb_ref, alog_ref, dsk_ref, nrm_ref, emat_ref,
                y_ref, state_ref, yf_ref, xs_c, bc_c, cbm_c, dt_c, *, nblocks):
    L = SSD_CHUNK
    LB = SSD_STEP_CHUNKS * L
    P = SSD_HEAD_DIM
    NS = SSD_STATE
    HG = SSD_HEADS // SSD_GROUPS
    gn = SSD_GROUPS * NS
    j = pl.program_id(1)
    c = jnp.where(j < nblocks, j, 2 * nblocks - 1 - j)
    row0 = pl.multiple_of(c * LB, LB)
    rows = pl.ds(row0, LB)
    lane1 = lax.broadcasted_iota(jnp.int32, (1, LANES), 1)
    lo_half = lane1 < P
    ti = lax.broadcasted_iota(jnp.int32, (L, L), 0)
    si = lax.broadcasted_iota(jnp.int32, (L, L), 1)

    def masked_c(bc):
        return [jnp.where(lane1 // NS == g, bc[:, gn:], 0.0).astype(BF16) for g in range(SSD_GROUPS)]

    def prepare():
        cur = xc_ref[0].astype(F32)
        prev = xp_ref[0].astype(F32)
        nxt = xn_ref[0].astype(F32)
        has_prev = (c > 0).astype(F32)
        has_next = (c < nblocks - 1).astype(F32)
        p_last = prev[BF16_SUBLANES - 1:BF16_SUBLANES, :] * has_prev
        n0 = nxt[0:1, :] * has_next
        n1 = nxt[1:2, :] * has_next
        row = lax.broadcasted_iota(jnp.int32, (LB, 1), 0)
        um1 = jnp.where(row == 0, p_last, pltpu.roll(cur, 1, 0))
        up1 = jnp.where(row == LB - 1, n0, pltpu.roll(cur, LB - 1, 0))
        up2 = jnp.where(row == LB - 2, n0, jnp.where(row == LB - 1, n1, pltpu.roll(cur, LB - 2, 0)))
        cw = cw_ref[...]
        acc = cb_ref[...] + um1 * cw[0:1, :] + cur * cw[1:2, :] + up1 * cw[2:3, :] + up2 * cw[3:4, :]
        xbc = _silu(acc)
        xs = xbc[:, :D_INNER]
        bc = xbc[:, D_INNER:D_INNER + 2 * gn].astype(BF16)
        dt = _softplus(dt_ref[0] + dtb_ref[...])
        xs_c[rows, :] = xs
        bc_c[rows, :] = bc
        dt_c[rows, :] = dt
        ops = []
        for sub in range(SSD_STEP_CHUNKS):
            sl = slice(sub * L, (sub + 1) * L)
            cg = masked_c(bc[sl])
            cb_mat = [lax.dot_general(cg[g], bc[sl, :gn], (((1,), (1,)), ((), ())), preferred_element_type=F32)
                      for g in range(SSD_GROUPS)]
            cbm_c[pl.ds(row0 + sub * L, L), :] = jnp.concatenate(cb_mat, axis=1)
            ops.append((xs[sl], bc[sl, :gn], cg, cb_mat, dt[sl]))
        return ops

    def recall():
        ops = []
        for sub in range(SSD_STEP_CHUNKS):
            r = pl.ds(row0 + sub * L, L)
            bc = bc_c[r, :]
            cbm = cbm_c[r, :]
            ops.append((xs_c[r, :], bc[:, :gn], masked_c(bc), [cbm[:, g * L:(g + 1) * L] for g in range(SSD_GROUPS)],
                        dt_c[r, :]))
        return ops

    def scan_chunk(direction, xs, bfull, cg, cb_mat, dt):
        if direction == 0:
            mask = ti >= si
            edge = L - 1
        else:
            mask = si >= ti
            edge = 0
        la = dt * (-jnp.exp(alog_ref[...]))
        tri = mask.astype(BF16)
        csum = sum(jnp.dot(tri, part, preferred_element_type=F32) for part in _split3(la))
        csum_t = csum.T
        emat = emat_ref[direction]
        colb = sum(jnp.dot(part, emat, preferred_element_type=F32) for part in _split3(csum))
        tot = csum[edge:edge + 1, :]
        e_tot = jnp.exp(tot)
        e_in = jnp.exp(csum)
        e_out = jnp.exp(tot - csum)
        ys = []
        for g in range(SSD_GROUPS):
            s_old = state_ref[g]
            y_off = lax.dot_general(cg[g], s_old.astype(BF16), (((1,), (1,)), ((), ())),
                                    preferred_element_type=F32)
            xw = []
            for pr in range(HG // 2):
                h0 = g * HG + 2 * pr
                l0 = direction * SSD_HEADS + h0

                def col(a, l0=l0):
                    return jnp.where(lo_half, a[:, l0:l0 + 1], a[:, l0 + 1:l0 + 2])

                xdt = xs[:, h0 * P:(h0 + 2) * P] * col(dt)
                y_pair = y_off[:, 2 * pr * P:(2 * pr + 2) * P] * col(e_in)
                for hh, half in ((h0, lo_half), (h0 + 1, jnp.logical_not(lo_half))):
                    ll = direction * SSD_HEADS + hh
                    seg = colb[:, hh * L:(hh + 1) * L] - csum_t[ll:ll + 1, :]
                    dec = jnp.exp(jnp.where(mask, seg, NEG_BIG))
                    m = (cb_mat[g] * dec).astype(BF16)
                    y_pair += jnp.dot(m, jnp.where(half, xdt, 0.0).astype(BF16), preferred_element_type=F32)
                ys.append(y_pair)
                xw.append(xdt * col(e_out))
            xw = jnp.concatenate(xw, axis=1).astype(BF16)
            s_new = lax.dot_general(xw, bfull, (((0,), (0,)), ((), ())), preferred_element_type=F32)
            s_scaled = []
            for hl in range(HG):
                lane = direction * SSD_HEADS + g * HG + hl
                s_scaled.append(s_old[hl * P:(hl + 1) * P, :] * e_tot[:, lane:lane + 1])
            state_ref[g] = jnp.concatenate(s_scaled, axis=0) + s_new
        return jnp.concatenate(ys, axis=1)

    @pl.when(jnp.logical_or(j == 0, j == nblocks))
    def _():
        state_ref[...] = jnp.zeros_like(state_ref)

    @pl.when(j < nblocks)
    def _():
        ops = prepare()
        for sub in range(SSD_STEP_CHUNKS):
            yf_ref[pl.ds(row0 + sub * L, L), :] = scan_chunk(0, *ops[sub])

    @pl.when(j >= nblocks)
    def _():
        ops = recall()
        for sub in reversed(range(SSD_STEP_CHUNKS)):
            sl = slice(sub * L, (sub + 1) * L)
            y = yf_ref[pl.ds(row0 + sub * L, L), :] + scan_chunk(1, *ops[sub]) + dsk_ref[...] * ops[sub][0]
            y = y * _silu(z_ref[0, sl, :].astype(F32))
            gw = D_INNER // SSD_GROUPS
            outs = []
            for g in range(SSD_GROUPS):
                yg = y[:, g * gw:(g + 1) * gw]
                outs.append(yg * lax.rsqrt(jnp.mean(yg * yg, axis=-1, keepdims=True) + EPS))
            y_ref[0, sl, :] = (jnp.concatenate(outs, axis=1) * nrm_ref[...]).astype(y_ref.dtype)


def _ssd(z, xbc, dt, conv_w, conv_b, dt_bias, a_log, d_skip, ssd_norm):
    b, t, _ = z.shape
    L = SSD_CHUNK
    lb = SSD_STEP_CHUNKS * L
    nb = t // lb
    hb = lb // BF16_SUBLANES
    nhalo = t // BF16_SUBLANES

    def blk(j):
        return jnp.where(j < nb, j, 2 * nb - 1 - j)

    pad = DT_PAD - 2 * SSD_HEADS
    dtb = jnp.pad(dt_bias.reshape(1, -1).astype(F32), ((0, 0), (0, pad)))
    alog = jnp.pad(a_log.reshape(1, -1).astype(F32), ((0, 0), (0, pad)))
    dsk = jnp.repeat(d_skip.astype(F32), SSD_HEAD_DIM).reshape(1, D_INNER)
    sel = np.arange(DT_PAD)[None, :, None] == (np.arange(2)[:, None, None] * SSD_HEADS
                                                + np.arange(SSD_HEADS)[None, None, :])
    emat = jnp.asarray(np.repeat(sel, L, axis=2), BF16)
    return pl.pallas_call(
        functools.partial(_ssd_kernel, nblocks=nb),
        grid=(b, 2 * nb),
        in_specs=[
            pl.BlockSpec((1, lb, CONV_CH), lambda i, j: (i, blk(j), 0)),
            pl.BlockSpec((1, BF16_SUBLANES, CONV_CH), lambda i, j: (i, jnp.maximum(blk(j) * hb - 1, 0), 0)),
            pl.BlockSpec((1, BF16_SUBLANES, CONV_CH),
                         lambda i, j: (i, jnp.minimum((blk(j) + 1) * hb, nhalo - 1), 0)),
            pl.BlockSpec((1, lb, DT_PAD), lambda i, j: (i, blk(j), 0)),
            pl.BlockSpec((1, lb, D_INNER), lambda i, j: (i, blk(j), 0)),
            _full((CONV_W, CONV_CH)), _full((1, CONV_CH)), _full((1, DT_PAD)), _full((1, DT_PAD)),
            _full((1, D_INNER)), _full((1, D_INNER)), _full((2, DT_PAD, SSD_HEADS * L)),
        ],
        out_specs=pl.BlockSpec((1, lb, D_INNER), lambda i, j: (i, jnp.where(j < nb, nb - 1, 2 * nb - 1 - j), 0)),
        out_shape=jax.ShapeDtypeStruct((b, t, D_INNER), BF16),
        scratch_shapes=[pltpu.VMEM((SSD_GROUPS, (SSD_HEADS // SSD_GROUPS) * SSD_HEAD_DIM, LANES), F32),
                        pltpu.VMEM((t, D_INNER), F32),
                        pltpu.VMEM((t, D_INNER), F32),
                        pltpu.VMEM((t, 2 * SSD_GROUPS * SSD_STATE), BF16),
                        pltpu.VMEM((t, SSD_GROUPS * L), F32),
                        pltpu.VMEM((t, DT_PAD), F32)],
        compiler_params=_cparams(("arbitrary", "arbitrary")),
    )(xbc, xbc, xbc, dt, z, conv_w.astype(F32), conv_b.reshape(1, CONV_CH).astype(F32), dtb, alog, dsk,
      ssd_norm.reshape(1, D_INNER).astype(F32), emat)


def _na_bias(rpb, t):
    r = t // GRID_W
    kh = min(NA_MAX_KH, r)
    nsb = r // NA_SUB_ROWS
    rows = np.arange(r)
    row_start = np.clip(rows - NA_MAX_KH // 2, 0, r - kh)
    r0 = np.arange(nsb) * NA_SUB_ROWS
    kr0 = np.clip(r0 - NA_MAX_KH // 2, 0, r - NA_KEY_ROWS)
    qrow = r0[:, None] + np.arange(NA_SUB_ROWS)[None, :]
    krow = kr0[:, None] + np.arange(NA_KEY_ROWS)[None, :]
    rs = row_start[qrow]
    row_ok = (krow[:, None, :] >= rs[:, :, None]) & (krow[:, None, :] < rs[:, :, None] + kh)
    dr = np.clip(krow[:, None, :] - qrow[:, :, None] + (NA_MAX_KH - 1), 0, 2 * NA_MAX_KH - 2)
    cols = np.arange(GRID_W)
    col_start = np.clip(cols - NA_KW // 2, 0, GRID_W - NA_KW)
    col_ok = (cols[None, :] >= col_start[:, None]) & (cols[None, :] < col_start[:, None] + NA_KW)
    dc = np.clip(cols[None, :] - cols[:, None] + (NA_KW - 1), 0, 2 * NA_KW - 2)
    sel_c = (dc[..., None] == np.arange(2 * NA_KW - 1)) & col_ok[..., None]
    nr = 2 * NA_MAX_KH - 1
    tile_r = jnp.einsum("hrc,xyc->hrxy", rpb.astype(F32), jnp.asarray(sel_c, F32), precision=HIGHEST)
    tile_r = jnp.where(jnp.asarray(col_ok)[None, None], tile_r, NEG_BIG)
    tile_r = jnp.concatenate([tile_r, jnp.full((NA_HEADS, 1, GRID_W, GRID_W), NEG_BIG, F32)], axis=1)
    kp = NA_KEY_ROWS // 2
    code = np.where(row_ok, dr, nr).reshape(nsb * NA_SUB_ROWS * kp, 2)
    pairs, inv = np.unique(code, axis=0, return_inverse=True)
    blocks = jnp.concatenate([tile_r[:, pairs[:, 0]], tile_r[:, pairs[:, 1]]], axis=-1)
    blocks = jnp.moveaxis(blocks, 0, 1).reshape(len(pairs), NA_HEADS * GRID_W * 2 * GRID_W)
    onehot = jnp.asarray(inv.reshape(-1, 1) == np.arange(len(pairs))[None, :], F32)
    bias = jnp.dot(onehot, blocks, precision=HIGHEST)
    return bias.reshape(nsb // NA_SUBS, NA_SUBS, NA_SUB_ROWS, kp, NA_HEADS, GRID_W, 2 * GRID_W).astype(BF16)


def _natten_kernel(qkv_ref, bias_ref, qg_ref, kg_ref, seg_ref, o_ref, *, grid_rows):
    nq = NA_SUB_ROWS * GRID_W
    nk = NA_KEY_ROWS * GRID_W
    rb = pl.program_id(0)
    seg = seg_ref[...]
    lane_h = lax.broadcasted_iota(jnp.int32, (1, NA_WIDTH), 1) // NA_HEAD_DIM
    for sub in range(NA_SUBS):
        r0 = rb * NA_QUERY_ROWS + sub * NA_SUB_ROWS
        kr0 = jnp.clip(r0 - NA_MAX_KH // 2, 0, grid_rows - NA_KEY_ROWS)
        q0 = pl.multiple_of(r0 * GRID_W, nq)
        k0 = pl.multiple_of(kr0 * GRID_W, NA_MAX_KH // 2 * GRID_W)
        q = qkv_ref[0, pl.ds(q0, nq), 0:NA_WIDTH].astype(F32)
        k = qkv_ref[0, pl.ds(k0, nk), NA_WIDTH:2 * NA_WIDTH].astype(F32)
        v = qkv_ref[0, pl.ds(k0, nk), 2 * NA_WIDTH:3 * NA_WIDTH]
        qms = sum(jnp.dot(part, seg, preferred_element_type=F32) for part in _split3(q * q)[:2])
        kms = sum(jnp.dot(part, seg, preferred_element_type=F32) for part in _split3(k * k)[:2])
        qn = q * lax.rsqrt(qms + EPS) * (qg_ref[...] * NA_HEAD_DIM ** -0.5)
        kn = (k * lax.rsqrt(kms + EPS) * kg_ref[...]).astype(BF16)
        acc = jnp.zeros((nq, NA_WIDTH), F32)
        for h in range(NA_HEADS):
            hm = lane_h == h
            s = lax.dot_general(jnp.where(hm, qn, 0.0).astype(BF16), kn, (((1,), (1,)), ((), ())),
                                preferred_element_type=F32)
            bias = jnp.concatenate(
                [jnp.concatenate([bias_ref[0, sub, qr, kc, h] for kc in range(NA_KEY_ROWS // 2)], axis=1)
                 for qr in range(NA_SUB_ROWS)], axis=0)
            s = s + bias.astype(F32)
            p = jnp.exp(s - jnp.max(s, axis=-1, keepdims=True))
            l = jnp.sum(p, axis=-1, keepdims=True)
            o = jnp.dot(p.astype(BF16), v, preferred_element_type=F32)
            acc += jnp.where(hm, o / l, 0.0)
        o_ref[0, sub * nq:(sub + 1) * nq, :] = acc.astype(o_ref.dtype)


def _natten(qkv, bias, q_norm, k_norm):
    b, t, _ = qkv.shape
    r = t // GRID_W
    nrb = r // NA_QUERY_ROWS
    nq = NA_QUERY_ROWS * GRID_W
    head = jnp.arange(NA_WIDTH) // NA_HEAD_DIM
    seg = ((head[:, None] == head[None, :]).astype(F32) / NA_HEAD_DIM).astype(BF16)
    return pl.pallas_call(
        functools.partial(_natten_kernel, grid_rows=r),
        grid=(nrb, b),
        in_specs=[pl.BlockSpec((1, t, 3 * NA_WIDTH), lambda i, j: (j, 0, 0)),
                  pl.BlockSpec((1,) + bias.shape[1:], lambda i, j: (i,) + (0,) * (bias.ndim - 1)),
                  _full((1, NA_WIDTH)), _full((1, NA_WIDTH)), _full((NA_WIDTH, NA_WIDTH))],
        out_specs=pl.BlockSpec((1, nq, NA_WIDTH), lambda i, j: (j, i, 0)),
        out_shape=jax.ShapeDtypeStruct((b, t, NA_WIDTH), BF16),
        compiler_params=_cparams(("arbitrary", "arbitrary")),
    )(qkv, bias, jnp.tile(q_norm.astype(F32), NA_HEADS).reshape(1, NA_WIDTH),
      jnp.tile(k_norm.astype(F32), NA_HEADS).reshape(1, NA_WIDTH), seg)


def _memkv_kernel(m_ref, g_ref, w_ref, kg_ref, k_ref, v_ref):
    x = m_ref[0]
    ms = jnp.mean(x * x, axis=-1, keepdims=True)
    h = (x * lax.rsqrt(ms + EPS) * g_ref[...]).astype(BF16)
    kv = jnp.dot(h, w_ref[...], preferred_element_type=F32)
    ks = []
    for hd in range(XA_HEADS):
        kh = kv[:, hd * XA_HEAD_DIM:(hd + 1) * XA_HEAD_DIM]
        ks.append(kh * lax.rsqrt(jnp.mean(kh * kh, axis=-1, keepdims=True) + EPS))
    k_ref[0] = (jnp.concatenate(ks, axis=1) * kg_ref[...]).astype(k_ref.dtype)
    v_ref[0] = kv[:, XA_WIDTH:].astype(v_ref.dtype)


def _memkv(mem, norm_mem, w_xkv, xk_norm):
    b, m, _ = mem.shape
    return pl.pallas_call(
        _memkv_kernel,
        grid=(b,),
        in_specs=[pl.BlockSpec((1, m, D_MODEL), lambda i: (i, 0, 0)),
                  _full((1, D_MODEL)), _full((D_MODEL, 2 * XA_WIDTH)), _full((1, XA_WIDTH))],
        out_specs=[pl.BlockSpec((1, m, XA_WIDTH), lambda i: (i, 0, 0))] * 2,
        out_shape=[jax.ShapeDtypeStruct((b, m, XA_WIDTH), BF16)] * 2,
        compiler_params=_cparams(("arbitrary",)),
    )(mem, norm_mem.reshape(1, D_MODEL).astype(F32), w_xkv,
      jnp.tile(xk_norm.astype(F32), XA_HEADS).reshape(1, XA_WIDTH))


def _mixer_kernel(x_ref, ys_ref, yn_ref, u_ref, up_ref, un_ref, k_ref, v_ref,
                  pw_ref, psc_ref, wo_ref, gxa_ref, wq_ref, qg_ref, wxo_ref, gff_ref,
                  wrb_ref, br_ref, ltri_ref,
                  x2_ref, hf_ref, meta_ref, cnt_ref, carry_ref, *, seq):
    tm = x_ref.shape[1]
    halo = BF16_SUBLANES
    bi = pl.program_id(0)
    i = pl.program_id(1)
    nt = pl.num_programs(1)

    @pl.when(jnp.logical_and(bi == 0, i == 0))
    def _():
        carry_ref[...] = jnp.zeros_like(carry_ref)

    u = u_ref[0].astype(F32)
    up = up_ref[0].astype(F32) * (i > 0).astype(F32)
    un = un_ref[0].astype(F32) * (i < nt - 1).astype(F32)
    cat = jnp.concatenate([up, u, un], axis=0)
    n = tm + 2 * halo

    def sh(a, k):
        return pltpu.roll(a, (-k) % n, 0)

    a2 = cat + sh(cat, -1)
    a4 = sh(a2, 1) + sh(a2, -1)
    a8 = sh(a4, 2) + sh(a4, -2)
    a16 = sh(a8, 4) + sh(a8, -4)
    lane_g = lax.broadcasted_iota(jnp.int32, (1, POOL_WIDTH), 1) // POOL_GROUP_DIM
    wsum = jnp.where(lane_g == 0, a2, jnp.where(lane_g == 1, a4, jnp.where(lane_g == 2, a8, a16)))
    wsum = wsum[halo:halo + tm, :]
    half = jnp.where(lane_g == 0, POOL_WINDOWS[0] // 2,
                     jnp.where(lane_g == 1, POOL_WINDOWS[1] // 2,
                               jnp.where(lane_g == 2, POOL_WINDOWS[2] // 2, POOL_WINDOWS[3] // 2)))
    tpos = i * tm + lax.broadcasted_iota(jnp.int32, (tm, 1), 0)
    cnt = (jnp.minimum(tpos + half, seq) - jnp.maximum(tpos - half, 0)).astype(F32)
    d = wsum / cnt - u
    ypool = jnp.dot(d.astype(BF16), pw_ref[...], preferred_element_type=F32) * psc_ref[...]

    mix = jnp.dot(ys_ref[0], wo_ref[0:D_INNER, :], preferred_element_type=F32)
    mix += jnp.dot(yn_ref[0], wo_ref[D_INNER:D_INNER + NA_WIDTH, :], preferred_element_type=F32)
    mix += jnp.dot(ypool.astype(BF16), wo_ref[D_INNER + NA_WIDTH:, :], preferred_element_type=F32)
    x1 = x_ref[0] + mix

    hn = (x1 * lax.rsqrt(jnp.mean(x1 * x1, axis=-1, keepdims=True) + EPS) * gxa_ref[...]).astype(BF16)
    q = jnp.dot(hn, wq_ref[...], preferred_element_type=F32)
    kk = k_ref[0]
    vv = v_ref[0]
    scale = XA_HEAD_DIM ** -0.5
    outs = []
    for hd in range(XA_HEADS):
        sl = slice(hd * XA_HEAD_DIM, (hd + 1) * XA_HEAD_DIM)
        qh = q[:, sl]
        qh = (qh * lax.rsqrt(jnp.mean(qh * qh, axis=-1, keepdims=True) + EPS) * qg_ref[:, sl]).astype(BF16)
        s = lax.dot_general(qh, kk[:, sl], (((1,), (1,)), ((), ())), preferred_element_type=F32) * scale
        p = jnp.exp(s - jnp.max(s, axis=-1, keepdims=True))
        l = jnp.sum(p, axis=-1, keepdims=True)
        outs.append(jnp.dot(p.astype(BF16), vv[:, sl], preferred_element_type=F32) / l)
    att = jnp.concatenate(outs, axis=1).astype(BF16)
    x2 = x1 + jnp.dot(att, wxo_ref[...], preferred_element_type=F32)
    x2_ref[0] = x2

    hf = x2 * lax.rsqrt(jnp.mean(x2 * x2, axis=-1, keepdims=True) + EPS) * gff_ref[...]
    h_hi = hf.astype(BF16)
    hf_ref[0, :, :HALF] = pltpu.bitcast(_pack_bf16_pairs(h_hi.astype(F32)), jnp.int32)
    h_lo = (hf - h_hi.astype(F32)).astype(BF16)
    both = jnp.dot(h_hi, wrb_ref[...], preferred_element_type=F32)
    logits = (both[:, :LANES] + both[:, LANES:]
              + jnp.dot(h_lo, wrb_ref[:, :LANES], preferred_element_type=F32)) + br_ref[...]
    lane = lax.broadcasted_iota(jnp.int32, (1, LANES), 1)
    lane_f = lane.astype(F32)
    lane_grp = (lane // EXPERTS_PER_GROUP).astype(F32)
    is_g = jnp.logical_and(lane >= N_EXPERTS, lane < N_EXPERTS + N_EXPERT_GROUPS)
    gl = jnp.where(is_g, logits, NEG_BIG)
    gmax = jnp.max(gl, axis=-1, keepdims=True)
    g_sel = jnp.min(jnp.where(gl == gmax, lane_f, float(LANES)), axis=-1, keepdims=True) - N_EXPERTS
    g_gate = 1.0 / jnp.sum(jnp.where(is_g, jnp.exp(gl - gmax), 0.0), axis=-1, keepdims=True)
    in_grp = jnp.logical_and(lane < N_EXPERTS, lane_grp == g_sel)
    el = jnp.where(in_grp, logits, NEG_BIG)
    v1 = jnp.max(el, axis=-1, keepdims=True)
    e0 = jnp.min(jnp.where(el == v1, lane_f, float(LANES)), axis=-1, keepdims=True)
    el2 = jnp.where(lane_f == e0, NEG_BIG, el)
    v2 = jnp.max(el2, axis=-1, keepdims=True)
    e1 = jnp.min(jnp.where(el2 == v2, lane_f, float(LANES)), axis=-1, keepdims=True)
    w1 = jnp.exp(v2 - v1)
    gate0 = g_gate / (1.0 + w1)
    gate1 = g_gate * w1 / (1.0 + w1)

    base = g_sel * EXPERTS_PER_GROUP
    ea = jnp.minimum(e0, e1) - base
    eb = jnp.maximum(e0, e1) - base
    combo = g_sel * PAIRS_PER_GROUP + ea * EXPERTS_PER_GROUP - ea * (ea + 1.0) * 0.5 + (eb - ea - 1.0)
    gate_a = jnp.where(e0 < e1, gate0, gate1)
    gate_b = jnp.where(e0 < e1, gate1, gate0)
    hf_ref[0, :, HALF:] = pltpu.bitcast(jnp.where(lane == 0, gate_a, 0.0) + jnp.where(lane == 1, gate_b, 0.0), jnp.int32)

    oh = lane_f == combo
    cnt_tok = oh.astype(F32)
    before = jnp.dot(ltri_ref[...], cnt_tok.astype(BF16), preferred_element_type=F32) + carry_ref[0:1, :]
    rank = jnp.sum(jnp.where(oh, before, 0.0), axis=-1, keepdims=True)
    new_carry = carry_ref[0:1, :] + jnp.sum(cnt_tok, axis=0, keepdims=True)
    carry_ref[...] = jnp.broadcast_to(new_carry, carry_ref.shape)
    cnt_ref[...] = jnp.broadcast_to(new_carry, cnt_ref.shape)

    slab = jnp.where(lane == 0, combo, 0.0) + jnp.where(lane == 1, rank, 0.0)
    meta_ref[0] = slab.T[0:8, :]


def _mixer(x, y_ssd, y_na, u, kmem, vmem, p):
    b, t, _ = x.shape
    tm = TOKEN_TILE
    nt = t // tm
    hb = tm // BF16_SUBLANES
    nhalo = t // BF16_SUBLANES
    ltri = (jnp.arange(tm)[:, None] > jnp.arange(tm)[None, :]).astype(BF16)
    tok = lambda w: pl.BlockSpec((1, tm, w), lambda i, j: (i, j, 0))
    mem = pl.BlockSpec((1, kmem.shape[1], XA_WIDTH), lambda i, j: (i, 0, 0))
    weights = (p["pool_bd"], p["pool_scale"], p["w_out"], p["norm_xa"], p["w_xq"], p["xq_norm"], p["w_xo"],
               p["norm_ffn"], p["w_r_both"], p["b_r"], ltri)
    return pl.pallas_call(
        functools.partial(_mixer_kernel, seq=t),
        grid=(b, nt),
        in_specs=[tok(D_MODEL), tok(D_INNER), tok(NA_WIDTH), tok(POOL_WIDTH),
                  pl.BlockSpec((1, BF16_SUBLANES, POOL_WIDTH), lambda i, j: (i, jnp.maximum(j * hb - 1, 0), 0)),
                  pl.BlockSpec((1, BF16_SUBLANES, POOL_WIDTH),
                               lambda i, j: (i, jnp.minimum((j + 1) * hb, nhalo - 1), 0)),
                  mem, mem] + [_full(w.shape) for w in weights],
        out_specs=[tok(D_MODEL), tok(ROW_WORDS),
                   pl.BlockSpec((1, 8, tm), lambda i, j: (i * nt + j, 0, 0)),
                   pl.BlockSpec((8, LANES), lambda i, j: (0, 0))],
        out_shape=[jax.ShapeDtypeStruct((b, t, D_MODEL), F32),
                   jax.ShapeDtypeStruct((b, t, ROW_WORDS), jnp.int32),
                   jax.ShapeDtypeStruct((b * nt, 8, tm), F32),
                   jax.ShapeDtypeStruct((8, LANES), F32)],
        scratch_shapes=[pltpu.VMEM((8, LANES), F32)],
        compiler_params=_cparams(("arbitrary", "arbitrary")),
    )(x, y_ssd, y_na, u, u, u, kmem, vmem, *weights)


def _zero_pad_kernel(pstart_ref, cnt_ref, xs_in, zero_ref, xs_ref, sem, *, bm):
    del xs_in

    def fill(cm, wait):
        n_c = cnt_ref[cm]
        npad = (bm - n_c % bm) % bm
        head = (8 - n_c % 8) % 8
        base = pstart_ref[cm] + n_c

        def go(cp):
            if wait:
                cp.wait()
            else:
                cp.start()

        def one(r, _):
            go(pltpu.make_async_copy(zero_ref.at[pl.ds(0, 1)], xs_ref.at[pl.ds(base + r, 1)], sem))
            return 0

        lax.fori_loop(0, head, one, 0)
        rest = npad - head
        off = base + head
        size = bm // 2
        while size >= 8:
            @pl.when(rest & size != 0)
            def _(off=off, size=size):
                go(pltpu.make_async_copy(zero_ref.at[pl.ds(0, size)],
                                         xs_ref.at[pl.ds(pl.multiple_of(off, 8), size)], sem))

            off = off + (rest & size)
            size //= 2
        return 0

    lax.fori_loop(0, N_COMBOS, lambda cm, _: fill(cm, False), 0)
    lax.fori_loop(0, N_COMBOS, lambda cm, _: fill(cm, True), 0)


def _zero_pad_rows(xs, pstart, counts, bm):
    return pl.pallas_call(
        functools.partial(_zero_pad_kernel, bm=bm),
        grid_spec=pltpu.PrefetchScalarGridSpec(
            num_scalar_prefetch=2,
            grid=(1,),
            in_specs=[pl.BlockSpec(memory_space=pl.ANY),
                      pl.BlockSpec((bm // 2, ROW_WORDS), lambda i, *_: (0, 0))],
            out_specs=pl.BlockSpec(memory_space=pl.ANY),
            scratch_shapes=[pltpu.SemaphoreType.DMA(())]),
        out_shape=jax.ShapeDtypeStruct(xs.shape, xs.dtype),
        input_output_aliases={2: 0},
        compiler_params=_cparams(("arbitrary",)),
    )(pstart, counts, xs, jnp.zeros((bm // 2, ROW_WORDS), xs.dtype))


def _sc_rows(kind, src, idx, n_out):
    n = idx.shape[0]
    width = src.shape[1]
    info = plsc.get_sparse_core_info()
    workers = info.num_cores * info.num_subcores
    per_worker = n // workers
    assert per_worker * workers == n and per_worker % SC_GATHER_ROWS == 0, (n, workers)
    mesh = plsc.VectorSubcoreMesh(core_axis_name="core", subcore_axis_name="subcore")

    def body(src_hbm, idx_hbm, out_hbm, idx_v, rows_v, sem):
        base = (lax.axis_index("subcore") * info.num_cores + lax.axis_index("core")) * per_worker

        @pl.loop(0, per_worker // SC_GATHER_ROWS)
        def _(j):
            off = pl.multiple_of(base + j * SC_GATHER_ROWS, SC_GATHER_ROWS)
            pltpu.sync_copy(idx_hbm.at[pl.ds(off, SC_GATHER_ROWS)], idx_v)
            if kind == "gather":
                pltpu.async_copy(src_hbm.at[idx_v], rows_v, sem).wait()
                pltpu.sync_copy(rows_v, out_hbm.at[pl.ds(off, SC_GATHER_ROWS)])
            else:
                pltpu.sync_copy(src_hbm.at[pl.ds(off, SC_GATHER_ROWS)], rows_v)
                pltpu.async_copy(rows_v, out_hbm.at[idx_v], sem).wait()

    return pl.kernel(
        body, mesh=mesh, out_type=jax.ShapeDtypeStruct((n_out, width), src.dtype),
        scratch_types=[pltpu.VMEM((SC_GATHER_ROWS,), jnp.int32), pltpu.VMEM((SC_GATHER_ROWS, width), src.dtype),
                       pltpu.SemaphoreType.DMA])(src, idx)


def _experts_kernel(ea_ref, eb_ref, nused_ref, x_ref, wga_ref, wua_ref, wda_ref, wgb_ref, wub_ref, wdb_ref, y_ref):
    @pl.when(pl.program_id(0) < nused_ref[0])
    def _():
        x = _unpack_bf16_pairs(pltpu.bitcast(x_ref[:, :HALF], U32)).astype(BF16)
        gates = pltpu.bitcast(x_ref[:, HALF:], F32)

        def mlp(wg_ref, wu_ref, wd_ref):
            hg = jnp.dot(x, wg_ref[0], preferred_element_type=F32)
            hu = jnp.dot(x, wu_ref[0], preferred_element_type=F32)
            return jnp.dot((_silu(hg) * hu).astype(BF16), wd_ref[0], preferred_element_type=F32)

        y = mlp(wga_ref, wua_ref, wda_ref) * gates[:, 0:1] + mlp(wgb_ref, wub_ref, wdb_ref) * gates[:, 1:2]
        y_ref[...] = pltpu.bitcast(_pack_bf16_pairs(y.astype(BF16).astype(F32)), jnp.int32)


def _experts(xs, blk_ea, blk_eb, nused, w_gate, w_up, w_down, bm):
    nblk = blk_ea.shape[0]
    row = lambda j, ea, eb, nu: (jnp.minimum(j, nu[0] - 1), 0)
    row_out = lambda j, ea, eb, nu: (jnp.where(j < nu[0], j, nblk - 1), 0)
    sel_a = lambda j, ea, eb, nu: (ea[j], 0, 0)
    sel_b = lambda j, ea, eb, nu: (eb[j], 0, 0)
    up = lambda sel: pl.BlockSpec((1, D_MODEL, D_EXPERT), sel)
    down = lambda sel: pl.BlockSpec((1, D_EXPERT, D_MODEL), sel)
    return pl.pallas_call(
        _experts_kernel,
        grid_spec=pltpu.PrefetchScalarGridSpec(
            num_scalar_prefetch=3,
            grid=(nblk,),
            in_specs=[pl.BlockSpec((bm, ROW_WORDS), row),
                      up(sel_a), up(sel_a), down(sel_a), up(sel_b), up(sel_b), down(sel_b)],
            out_specs=pl.BlockSpec((bm, HALF), row_out)),
        out_shape=jax.ShapeDtypeStruct((nblk * bm, HALF), jnp.int32),
        compiler_params=_cparams(("arbitrary",)),
    )(blk_ea, blk_eb, nused, xs, w_gate, w_up, w_down, w_gate, w_up, w_down)


def _add_rows_kernel(x_ref, yg_ref, o_ref):
    o_ref[...] = x_ref[...] + _unpack_bf16_pairs(pltpu.bitcast(yg_ref[...], U32))


def _add_inproj_kernel(x2_ref, yg_ref, g_ref, w_ref, x_ref, z_ref, xbc_ref, qkv_ref, u_ref, dt_ref):
    _add_rows_kernel(x2_ref, yg_ref, x_ref)
    _inproj_kernel(x_ref, g_ref, w_ref, z_ref, xbc_ref, qkv_ref, u_ref, dt_ref)


def _combine(x2d, yg, gain=None, w_cat=None):
    n = x2d.shape[0]
    tm = TOKEN_TILE
    tile = lambda w: pl.BlockSpec((tm, w), lambda i: (i, 0))
    if w_cat is None:
        return pl.pallas_call(
            _add_rows_kernel, grid=(n // tm,), in_specs=[tile(D_MODEL), tile(HALF)], out_specs=tile(D_MODEL),
            out_shape=jax.ShapeDtypeStruct((n, D_MODEL), F32), compiler_params=_cparams(("arbitrary",)))(x2d, yg)
    widths = (D_MODEL,) + _INPROJ_WIDTHS
    dtypes = (F32,) + _INPROJ_DTYPES
    return pl.pallas_call(
        _add_inproj_kernel, grid=(n // tm,),
        in_specs=[tile(D_MODEL), tile(HALF), _full((1, D_MODEL)), _full(w_cat.shape)],
        out_specs=[tile(w) for w in widths],
        out_shape=[jax.ShapeDtypeStruct((n, w), d) for w, d in zip(widths, dtypes)],
        compiler_params=_cparams(("arbitrary",)),
    )(x2d, yg, gain.reshape(1, D_MODEL), w_cat)


_PAIR_A = np.array([a for a in range(EXPERTS_PER_GROUP) for _ in range(a + 1, EXPERTS_PER_GROUP)], np.int32)
_PAIR_B = np.array([b for a in range(EXPERTS_PER_GROUP) for b in range(a + 1, EXPERTS_PER_GROUP)], np.int32)


def _expert_block(n):
    return 2 * EXPERT_BLOCK_MIN if n >= 2 * 2 * EXPERT_BLOCK_MIN * N_COMBOS else EXPERT_BLOCK_MIN


def _moe(x2, hf, meta, counts, w_gate, w_up, w_down):
    b, t, _ = x2.shape
    n = b * t
    bm = _expert_block(n)
    nblk = (n + N_COMBOS * (bm - 1) + bm - 1) // bm
    cnt = counts[0, :N_COMBOS].astype(jnp.int32)
    psz = (cnt + bm - 1) // bm * bm
    pend = jnp.cumsum(psz)
    pstart = (pend - psz).astype(jnp.int32)
    nused = jnp.maximum(pend[-1] // bm, 1).astype(jnp.int32).reshape(1)
    blk = jnp.minimum(jnp.arange(nblk, dtype=jnp.int32), nused[0] - 1)
    blk_c = jnp.minimum(jnp.sum(pend[None, :] <= (blk * bm)[:, None], axis=1), N_COMBOS - 1).astype(jnp.int32)
    grp = blk_c // PAIRS_PER_GROUP
    blk_ea = (grp * EXPERTS_PER_GROUP + jnp.asarray(_PAIR_A)[blk_c % PAIRS_PER_GROUP]).astype(jnp.int32)
    blk_eb = (grp * EXPERTS_PER_GROUP + jnp.asarray(_PAIR_B)[blk_c % PAIRS_PER_GROUP]).astype(jnp.int32)
    ids = meta.astype(jnp.int32)
    combo = ids[:, 0, :].reshape(n)
    dest = ids[:, 1, :].reshape(n) + jnp.sum(jnp.where(combo[:, None] == jnp.arange(N_COMBOS)[None, :], pstart[None, :], 0),
                                             axis=1)
    xs = _sc_rows("scatter", hf.reshape(n, ROW_WORDS), dest, nblk * bm)
    xs = _zero_pad_rows(xs, pstart, cnt, bm)
    y = _experts(xs, blk_ea, blk_eb, nused, w_gate, w_up, w_down, bm)
    yg = _sc_rows("gather", y, dest, n)
    return x2.reshape(n, D_MODEL), yg


def _prep_layer(lp):
    w_in = lp["w_in"]
    c0 = D_INNER + CONV_CH
    c1 = c0 + 2 * SSD_HEADS
    w_cat = jnp.concatenate([w_in[:, :c0], w_in[:, c1:], w_in[:, c0:c1],
                             jnp.zeros((D_MODEL, DT_PAD - 2 * SSD_HEADS), w_in.dtype)], axis=1).astype(BF16)
    pool_bd = jnp.zeros((POOL_WIDTH, POOL_WIDTH), F32)
    for g in range(POOL_GROUPS):
        sl = slice(g * POOL_GROUP_DIM, (g + 1) * POOL_GROUP_DIM)
        pool_bd = pool_bd.at[sl, sl].set(lp["pool_w"][g].astype(F32))
    w_r = jnp.concatenate([lp["w_router_expert"], lp["w_router_group"],
                           jnp.zeros((D_MODEL, LANES - N_EXPERTS - N_EXPERT_GROUPS), F32)], axis=1).astype(F32)
    w_r_hi = w_r.astype(BF16)
    w_r_lo = (w_r - w_r_hi.astype(F32)).astype(BF16)
    b_r = jnp.concatenate([lp["b_router_expert"], lp["b_router_group"],
                           jnp.zeros((LANES - N_EXPERTS - N_EXPERT_GROUPS,), F32)]).reshape(1, LANES).astype(F32)
    row = lambda a, w: a.reshape(1, w).astype(F32)
    return dict(
        norm_mix=lp["norm_mix"], w_cat=w_cat,
        conv_w=lp["conv_w"], conv_b=lp["conv_b"], dt_bias=lp["dt_bias"], a_log=lp["a_log"],
        d_skip=lp["d_skip"], ssd_norm=lp["ssd_norm"],
        na_q_norm=lp["na_q_norm"], na_k_norm=lp["na_k_norm"], na_rpb=lp["na_rpb"],
        pool_bd=pool_bd.astype(BF16), pool_scale=row(lp["pool_scale"], POOL_WIDTH),
        w_out=lp["w_out"].astype(BF16), norm_xa=row(lp["norm_xa"], D_MODEL),
        norm_mem=lp["norm_mem"], w_xq=lp["w_xq"].astype(BF16), w_xkv=lp["w_xkv"].astype(BF16),
        xq_norm=row(jnp.tile(lp["xq_norm"], XA_HEADS), XA_WIDTH), xk_norm=lp["xk_norm"],
        w_xo=lp["w_xo"].astype(BF16), norm_ffn=row(lp["norm_ffn"], D_MODEL),
        w_r_both=jnp.concatenate([w_r_hi, w_r_lo], axis=1), b_r=b_r,
        w_e_gate=lp["w_e_gate"].astype(BF16), w_e_up=lp["w_e_up"].astype(BF16),
        w_e_down=lp["w_e_down"].astype(BF16),
    )


def _layer(x, pending, mem, p, na_bias):
    b, m, _ = mem.shape
    if pending is None:
        t = x.shape[1]
        z, xbc, qkv, u, dt = _inproj(x.reshape(b * t, D_MODEL), p["norm_mix"], p["w_cat"])
    else:
        t = pending[0].shape[0] // b
        x, z, xbc, qkv, u, dt = _combine(*pending, p["norm_mix"], p["w_cat"])
        x = x.reshape(b, t, D_MODEL)
    r3 = lambda a: a.reshape(b, t, a.shape[-1])
    y_ssd = _ssd(r3(z), r3(xbc), r3(dt), p["conv_w"], p["conv_b"], p["dt_bias"], p["a_log"], p["d_skip"],
                 p["ssd_norm"])
    y_na = _natten(r3(qkv), na_bias, p["na_q_norm"], p["na_k_norm"])
    kmem, vmem = _memkv(mem, p["norm_mem"], p["w_xkv"], p["xk_norm"])
    x2, hf, meta, counts = _mixer(x, y_ssd, y_na, r3(u), kmem, vmem, p)
    return _moe(x2, hf, meta, counts, p["w_e_gate"], p["w_e_up"], p["w_e_down"])


_LAYER_KEYS = ("norm_mix", "w_in", "conv_w", "conv_b", "dt_bias", "a_log", "d_skip", "ssd_norm", "na_q_norm",
               "na_k_norm", "na_rpb", "pool_w", "pool_scale", "w_out", "norm_xa", "norm_mem", "w_xq", "w_xkv",
               "xq_norm", "xk_norm", "w_xo", "norm_ffn", "w_router_group", "b_router_group", "w_router_expert",
               "b_router_expert", "w_e_gate", "w_e_up", "w_e_down")


def kernel(x_prompt, x_sample, mem_prompt, mem_sample, norm_mix, w_in, conv_w, conv_b, dt_bias, a_log, d_skip, ssd_norm, na_q_norm, na_k_norm, na_rpb, pool_w, pool_scale, w_out, norm_xa, norm_mem, w_xq, w_xkv, xq_norm, xk_norm, w_xo, norm_ffn, w_router_group, b_router_group, w_router_expert, b_router_expert, w_e_gate, w_e_up, w_e_down):
    stacked = dict(zip(_LAYER_KEYS, (norm_mix, w_in, conv_w, conv_b, dt_bias, a_log, d_skip, ssd_norm, na_q_norm,
                                     na_k_norm, na_rpb, pool_w, pool_scale, w_out, norm_xa, norm_mem, w_xq, w_xkv,
                                     xq_norm, xk_norm, w_xo, norm_ffn, w_router_group, b_router_group,
                                     w_router_expert, b_router_expert, w_e_gate, w_e_up, w_e_down)))
    depth = w_in.shape[0]
    layers = [_prep_layer({k: v[l] for k, v in stacked.items()}) for l in range(depth)]

    bias_cache = {}

    def trunk(x, mem):
        t = x.shape[1]
        pending = None
        for l, lp in enumerate(layers):
            if (l, t) not in bias_cache:
                bias_cache[(l, t)] = _na_bias(lp["na_rpb"], t)
            pending = _layer(x, pending, mem, lp, bias_cache[(l, t)])
            x = None
        return _combine(*pending).reshape(mem.shape[0], t, D_MODEL)

    return trunk(x_prompt, mem_prompt), trunk(x_sample, mem_sample)
```

```python
import functools

import jax
import jax.numpy as jnp
import numpy as np
from jax import lax
from jax.experimental import pallas as pl
from jax.experimental.pallas import tpu as pltpu
from jax.experimental.pallas import tpu_sc as plsc

F32 = jnp.float32
BF16 = jnp.bfloat16
U32 = jnp.uint32
HIGHEST = lax.Precision.HIGHEST

D_MODEL = 1024
GRID_W = 64
EPS = 1e-6
SSD_HEAD_DIM = 64
D_INNER = D_MODEL // 2
SSD_HEADS = D_INNER // SSD_HEAD_DIM
SSD_GROUPS = 2
SSD_STATE = 64
SSD_CHUNK = 128
CONV_W = 4
CONV_CH = D_INNER + 2 * SSD_GROUPS * SSD_STATE
NA_HEADS = 4
NA_HEAD_DIM = D_MODEL // 16
NA_WIDTH = NA_HEADS * NA_HEAD_DIM
NA_MAX_KH = 8
NA_KW = 16
POOL_WINDOWS = (2, 4, 8, 16)
POOL_GROUPS = 4
POOL_WIDTH = D_MODEL - D_INNER - NA_WIDTH
POOL_GROUP_DIM = POOL_WIDTH // POOL_GROUPS
XA_HEADS = 4
XA_HEAD_DIM = D_MODEL // 8
XA_WIDTH = XA_HEADS * XA_HEAD_DIM
N_EXPERT_GROUPS = 4
EXPERTS_PER_GROUP = 8
N_EXPERTS = N_EXPERT_GROUPS * EXPERTS_PER_GROUP
D_EXPERT = D_MODEL // 4
PAIRS_PER_GROUP = EXPERTS_PER_GROUP * (EXPERTS_PER_GROUP - 1) // 2
N_COMBOS = N_EXPERT_GROUPS * PAIRS_PER_GROUP

LANES = 128
BF16_SUBLANES = 16
VMEM_LIMIT_BYTES = 56 * 1024 * 1024

TOKEN_TILE = 1024
NA_QUERY_ROWS = 16
NA_SUB_ROWS = 8
NA_SUBS = NA_QUERY_ROWS // NA_SUB_ROWS
NA_KEY_ROWS = NA_SUB_ROWS + NA_MAX_KH
EXPERT_BLOCK_MIN = 128
HALF = D_MODEL // 2
ROW_WORDS = HALF + LANES
SSD_STEP_CHUNKS = 8
DT_PAD = LANES
SC_GATHER_ROWS = 128
NEG_BIG = -1e30


def _cparams(sem):
    return pltpu.CompilerParams(dimension_semantics=sem, vmem_limit_bytes=VMEM_LIMIT_BYTES)


def _sigmoid(x):
    return 1.0 / (1.0 + jnp.exp(-x))


def _silu(x):
    return x * _sigmoid(x)


def _softplus(x):
    return jnp.maximum(x, 0.0) + jnp.log(1.0 + jnp.exp(-jnp.abs(x)))


def _pack_bf16_pairs(v):
    k = v.shape[1] // 2
    bits = pltpu.bitcast(v, U32)
    return (bits[:, :k] >> 16) | (bits[:, k:] & jnp.uint32(0xFFFF0000))


def _unpack_bf16_pairs(w):
    lo = pltpu.bitcast(w << 16, F32)
    hi = pltpu.bitcast(w & jnp.uint32(0xFFFF0000), F32)
    return jnp.concatenate([lo, hi], axis=1)


def _full(shape):
    n = len(shape)
    return pl.BlockSpec(shape, lambda *_: (0,) * n)


def _inproj_kernel(x_ref, g_ref, w_ref, z_ref, xbc_ref, qkv_ref, u_ref, dt_ref):
    x = x_ref[...]
    ms = jnp.mean(x * x, axis=-1, keepdims=True)
    h = (x * lax.rsqrt(ms + EPS) * g_ref[...]).astype(BF16)
    o = 0
    for ref in (z_ref, xbc_ref, qkv_ref, u_ref, dt_ref):
        w = ref.shape[-1]
        ref[...] = jnp.dot(h, w_ref[:, o:o + w], preferred_element_type=F32).astype(ref.dtype)
        o += w


_INPROJ_WIDTHS = (D_INNER, CONV_CH, 3 * NA_WIDTH, POOL_WIDTH, DT_PAD)
_INPROJ_DTYPES = (BF16, BF16, BF16, BF16, F32)


def _inproj(x2d, gain, w_cat):
    n = x2d.shape[0]
    tm = TOKEN_TILE
    widths, dtypes = _INPROJ_WIDTHS, _INPROJ_DTYPES
    return pl.pallas_call(
        _inproj_kernel,
        grid=(n // tm,),
        in_specs=[pl.BlockSpec((tm, D_MODEL), lambda i: (i, 0)),
                  _full((1, D_MODEL)),
                  _full(w_cat.shape)],
        out_specs=[pl.BlockSpec((tm, w), lambda i: (i, 0)) for w in widths],
        out_shape=[jax.ShapeDtypeStruct((n, w), d) for w, d in zip(widths, dtypes)],
        compiler_params=_cparams(("arbitrary",)),
    )(x2d, gain.reshape(1, D_MODEL), w_cat)


def _split3(a):
    hi = a.astype(BF16)
    r = a - hi.astype(F32)
    mid = r.astype(BF16)
    lo = (r - mid.astype(F32)).astype(BF16)
    return hi, mid, lo


def _ssd_kernel(xc_ref, xp_ref, xn_ref, dt_ref, z_ref, cw_ref, cb_ref, dtb_ref, alog_ref, dsk_ref, nrm_ref, emat_ref,
                y_ref, state_ref, yf_ref, xs_c, bc_c, cbm_c, dt_c, *, nblocks):
    L = SSD_CHUNK
    LB = SSD_STEP_CHUNKS * L
    P = SSD_HEAD_DIM
    NS = SSD_STATE
    HG = SSD_HEADS // SSD_GROUPS
    gn = SSD_GROUPS * NS
    j = pl.program_id(1)
    c = jnp.where(j < nblocks, j, 2 * nblocks - 1 - j)
    row0 = pl.multiple_of(c * LB, LB)
    rows = pl.ds(row0, LB)
    lane1 = lax.broadcasted_iota(jnp.int32, (1, LANES), 1)
    lo_half = lane1 < P
    ti = lax.broadcasted_iota(jnp.int32, (L, L), 0)
    si = lax.broadcasted_iota(jnp.int32, (L, L), 1)

    def masked_c(bc):
        return [jnp.where(lane1 // NS == g, bc[:, gn:], 0.0).astype(BF16) for g in range(SSD_GROUPS)]

    def prepare():
        cur = xc_ref[0].astype(F32)
        prev = xp_ref[0].astype(F32)
        nxt = xn_ref[0].astype(F32)
        has_prev = (c > 0).astype(F32)
        has_next = (c < nblocks - 1).astype(F32)
        p_last = prev[BF16_SUBLANES - 1:BF16_SUBLANES, :] * has_prev
        n0 = nxt[0:1, :] * has_next
        n1 = nxt[1:2, :] * has_next
        row = lax.broadcasted_iota(jnp.int32, (LB, 1), 0)
        um1 = jnp.where(row == 0, p_last, pltpu.roll(cur, 1, 0))
        up1 = jnp.where(row == LB - 1, n0, pltpu.roll(cur, LB - 1, 0))
        up2 = jnp.where(row == LB - 2, n0, jnp.where(row == LB - 1, n1, pltpu.roll(cur, LB - 2, 0)))
        cw = cw_ref[...]
        acc = cb_ref[...] + um1 * cw[0:1, :] + cur * cw[1:2, :] + up1 * cw[2:3, :] + up2 * cw[3:4, :]
        xbc = _silu(acc)
        xs = xbc[:, :D_INNER]
        bc = xbc[:, D_INNER:D_INNER + 2 * gn].astype(BF16)
        dt = _softplus(dt_ref[0] + dtb_ref[...])
        xs_c[rows, :] = xs
        bc_c[rows, :] = bc
        dt_c[rows, :] = dt
        ops = []
        for sub in range(SSD_STEP_CHUNKS):
            sl = slice(sub * L, (sub + 1) * L)
            cg = masked_c(bc[sl])
            cb_mat = [lax.dot_general(cg[g], bc[sl, :gn], (((1,), (1,)), ((), ())), preferred_element_type=F32)
                      for g in range(SSD_GROUPS)]
            cbm_c[pl.ds(row0 + sub * L, L), :] = jnp.concatenate(cb_mat, axis=1)
            ops.append((xs[sl], bc[sl, :gn], cg, cb_mat, dt[sl]))
        return ops

    def recall():
        ops = []
        for sub in range(SSD_STEP_CHUNKS):
            r = pl.ds(row0 + sub * L, L)
            bc = bc_c[r, :]
            cbm = cbm_c[r, :]
            ops.append((xs_c[r, :], bc[:, :gn], masked_c(bc), [cbm[:, g * L:(g + 1) * L] for g in range(SSD_GROUPS)],
                        dt_c[r, :]))
        return ops

    def scan_chunk(direction, xs, bfull, cg, cb_mat, dt):
        if direction == 0:
            mask = ti >= si
            edge = L - 1
        else:
            mask = si >= ti
            edge = 0
        la = dt * (-jnp.exp(alog_ref[...]))
        tri = mask.astype(BF16)
        csum = sum(jnp.dot(tri, part, preferred_element_type=F32) for part in _split3(la))
        csum_t = csum.T
        emat = emat_ref[direction]
        colb = sum(jnp.dot(part, emat, preferred_element_type=F32) for part in _split3(csum))
        tot = csum[edge:edge + 1, :]
        e_tot = jnp.exp(tot)
        e_in = jnp.exp(csum)
        e_out = jnp.exp(tot - csum)
        ys = []
        for g in range(SSD_GROUPS):
            s_old = state_ref[g]
            y_off = lax.dot_general(cg[g], s_old.astype(BF16), (((1,), (1,)), ((), ())),
                                    preferred_element_type=F32)
            xw = []
            for pr in range(HG // 2):
                h0 = g * HG + 2 * pr
                l0 = direction * SSD_HEADS + h0

                def col(a, l0=l0):
                    return jnp.where(lo_half, a[:, l0:l0 + 1], a[:, l0 + 1:l0 + 2])

                xdt = xs[:, h0 * P:(h0 + 2) * P] * col(dt)
                y_pair = y_off[:, 2 * pr * P:(2 * pr + 2) * P] * col(e_in)
                for hh, half in ((h0, lo_half), (h0 + 1, jnp.logical_not(lo_half))):
                    ll = direction * SSD_HEADS + hh
                    seg = colb[:, hh * L:(hh + 1) * L] - csum_t[ll:ll + 1, :]
                    dec = jnp.exp(jnp.where(mask, seg, NEG_BIG))
                    m = (cb_mat[g] * dec).astype(BF16)
                    y_pair += jnp.dot(m, jnp.where(half, xdt, 0.0).astype(BF16), preferred_element_type=F32)
                ys.append(y_pair)
                xw.append(xdt * col(e_out))
            xw = jnp.concatenate(xw, axis=1).astype(BF16)
            s_new = lax.dot_general(xw, bfull, (((0,), (0,)), ((), ())), preferred_element_type=F32)
            s_scaled = []
            for hl in range(HG):
                lane = direction * SSD_HEADS + g * HG + hl
                s_scaled.append(s_old[hl * P:(hl + 1) * P, :] * e_tot[:, lane:lane + 1])
            state_ref[g] = jnp.concatenate(s_scaled, axis=0) + s_new
        return jnp.concatenate(ys, axis=1)

    @pl.when(jnp.logical_or(j == 0, j == nblocks))
    def _():
        state_ref[...] = jnp.zeros_like(state_ref)

    @pl.when(j < nblocks)
    def _():
        ops = prepare()
        for sub in range(SSD_STEP_CHUNKS):
            yf_ref[pl.ds(row0 + sub * L, L), :] = scan_chunk(0, *ops[sub])

    @pl.when(j >= nblocks)
    def _():
        ops = recall()
        for sub in reversed(range(SSD_STEP_CHUNKS)):
            sl = slice(sub * L, (sub + 1) * L)
            y = yf_ref[pl.ds(row0 + sub * L, L), :] + scan_chunk(1, *ops[sub]) + dsk_ref[...] * ops[sub][0]
            y = y * _silu(z_ref[0, sl, :].astype(F32))
            gw = D_INNER // SSD_GROUPS
            outs = []
            for g in range(SSD_GROUPS):
                yg = y[:, g * gw:(g + 1) * gw]
                outs.append(yg * lax.rsqrt(jnp.mean(yg * yg, axis=-1, keepdims=True) + EPS))
            y_ref[0, sl, :] = (jnp.concatenate(outs, axis=1) * nrm_ref[...]).astype(y_ref.dtype)


def _ssd(z, xbc, dt, conv_w, conv_b, dt_bias, a_log, d_skip, ssd_norm):
    b, t, _ = z.shape
    L = SSD_CHUNK
    lb = SSD_STEP_CHUNKS * L
    nb = t // lb
    hb = lb // BF16_SUBLANES
    nhalo = t // BF16_SUBLANES

    def blk(j):
        return jnp.where(j < nb, j, 2 * nb - 1 - j)

    pad = DT_PAD - 2 * SSD_HEADS
    dtb = jnp.pad(dt_bias.reshape(1, -1).astype(F32), ((0, 0), (0, pad)))
    alog = jnp.pad(a_log.reshape(1, -1).astype(F32), ((0, 0), (0, pad)))
    dsk = jnp.repeat(d_skip.astype(F32), SSD_HEAD_DIM).reshape(1, D_INNER)
    sel = np.arange(DT_PAD)[None, :, None] == (np.arange(2)[:, None, None] * SSD_HEADS
                                                + np.arange(SSD_HEADS)[None, None, :])
    emat = jnp.asarray(np.repeat(sel, L, axis=2), BF16)
    return pl.pallas_call(
        functools.partial(_ssd_kernel, nblocks=nb),
        grid=(b, 2 * nb),
        in_specs=[
            pl.BlockSpec((1, lb, CONV_CH), lambda i, j: (i, blk(j), 0)),
            pl.BlockSpec((1, BF16_SUBLANES, CONV_CH), lambda i, j: (i, jnp.maximum(blk(j) * hb - 1, 0), 0)),
            pl.BlockSpec((1, BF16_SUBLANES, CONV_CH),
                         lambda i, j: (i, jnp.minimum((blk(j) + 1) * hb, nhalo - 1), 0)),
            pl.BlockSpec((1, lb, DT_PAD), lambda i, j: (i, blk(j), 0)),
            pl.BlockSpec((1, lb, D_INNER), lambda i, j: (i, blk(j), 0)),
            _full((CONV_W, CONV_CH)), _full((1, CONV_CH)), _full((1, DT_PAD)), _full((1, DT_PAD)),
            _full((1, D_INNER)), _full((1, D_INNER)), _full((2, DT_PAD, SSD_HEADS * L)),
        ],
        out_specs=pl.BlockSpec((1, lb, D_INNER), lambda i, j: (i, jnp.where(j < nb, nb - 1, 2 * nb - 1 - j), 0)),
        out_shape=jax.ShapeDtypeStruct((b, t, D_INNER), BF16),
        scratch_shapes=[pltpu.VMEM((SSD_GROUPS, (SSD_HEADS // SSD_GROUPS) * SSD_HEAD_DIM, LANES), F32),
                        pltpu.VMEM((t, D_INNER), F32),
                        pltpu.VMEM((t, D_INNER), F32),
                        pltpu.VMEM((t, 2 * SSD_GROUPS * SSD_STATE), BF16),
                        pltpu.VMEM((t, SSD_GROUPS * L), F32),
                        pltpu.VMEM((t, DT_PAD), F32)],
        compiler_params=_cparams(("arbitrary", "arbitrary")),
    )(xbc, xbc, xbc, dt, z, conv_w.astype(F32), conv_b.reshape(1, CONV_CH).astype(F32), dtb, alog, dsk,
      ssd_norm.reshape(1, D_INNER).astype(F32), emat)


def _na_bias(rpb, t):
    r = t // GRID_W
    kh = min(NA_MAX_KH, r)
    nsb = r // NA_SUB_ROWS
    rows = np.arange(r)
    row_start = np.clip(rows - NA_MAX_KH // 2, 0, r - kh)
    r0 = np.arange(nsb) * NA_SUB_ROWS
    kr0 = np.clip(r0 - NA_MAX_KH // 2, 0, r - NA_KEY_ROWS)
    qrow = r0[:, None] + np.arange(NA_SUB_ROWS)[None, :]
    krow = kr0[:, None] + np.arange(NA_KEY_ROWS)[None, :]
    rs = row_start[qrow]
    row_ok = (krow[:, None, :] >= rs[:, :, None]) & (krow[:, None, :] < rs[:, :, None] + kh)
    dr = np.clip(krow[:, None, :] - qrow[:, :, None] + (NA_MAX_KH - 1), 0, 2 * NA_MAX_KH - 2)
    cols = np.arange(GRID_W)
    col_start = np.clip(cols - NA_KW // 2, 0, GRID_W - NA_KW)
    col_ok = (cols[None, :] >= col_start[:, None]) & (cols[None, :] < col_start[:, None] + NA_KW)
    dc = np.clip(cols[None, :] - cols[:, None] + (NA_KW - 1), 0, 2 * NA_KW - 2)
    sel_c = (dc[..., None] == np.arange(2 * NA_KW - 1)) & col_ok[..., None]
    nr = 2 * NA_MAX_KH - 1
    tile_r = jnp.einsum("hrc,xyc->hrxy", rpb.astype(F32), jnp.asarray(sel_c, F32), precision=HIGHEST)
    tile_r = jnp.where(jnp.asarray(col_ok)[None, None], tile_r, NEG_BIG)
    tile_r = jnp.concatenate([tile_r, jnp.full((NA_HEADS, 1, GRID_W, GRID_W), NEG_BIG, F32)], axis=1)
    kp = NA_KEY_ROWS // 2
    code = np.where(row_ok, dr, nr).reshape(nsb * NA_SUB_ROWS * kp, 2)
    pairs, inv = np.unique(code, axis=0, return_inverse=True)
    blocks = jnp.concatenate([tile_r[:, pairs[:, 0]], tile_r[:, pairs[:, 1]]], axis=-1)
    blocks = jnp.moveaxis(blocks, 0, 1).reshape(len(pairs), NA_HEADS * GRID_W * 2 * GRID_W)
    onehot = jnp.asarray(inv.reshape(-1, 1) == np.arange(len(pairs))[None, :], F32)
    bias = jnp.dot(onehot, blocks, precision=HIGHEST)
    return bias.reshape(nsb // NA_SUBS, NA_SUBS, NA_SUB_ROWS, kp, NA_HEADS, GRID_W, 2 * GRID_W).astype(BF16)


def _natten_kernel(qkv_ref, bias_ref, qg_ref, kg_ref, seg_ref, o_ref, *, grid_rows):
    nq = NA_SUB_ROWS * GRID_W
    nk = NA_KEY_ROWS * GRID_W
    rb = pl.program_id(0)
    seg = seg_ref[...]
    lane_h = lax.broadcasted_iota(jnp.int32, (1, NA_WIDTH), 1) // NA_HEAD_DIM
    for sub in range(NA_SUBS):
        r0 = rb * NA_QUERY_ROWS + sub * NA_SUB_ROWS
        kr0 = jnp.clip(r0 - NA_MAX_KH // 2, 0, grid_rows - NA_KEY_ROWS)
        q0 = pl.multiple_of(r0 * GRID_W, nq)
        k0 = pl.multiple_of(kr0 * GRID_W, NA_MAX_KH // 2 * GRID_W)
        q = qkv_ref[0, pl.ds(q0, nq), 0:NA_WIDTH].astype(F32)
        k = qkv_ref[0, pl.ds(k0, nk), NA_WIDTH:2 * NA_WIDTH].astype(F32)
        v = qkv_ref[0, pl.ds(k0, nk), 2 * NA_WIDTH:3 * NA_WIDTH]
        qms = sum(jnp.dot(part, seg, preferred_element_type=F32) for part in _split3(q * q)[:2])
        kms = sum(jnp.dot(part, seg, preferred_element_type=F32) for part in _split3(k * k)[:2])
        qn = q * lax.rsqrt(qms + EPS) * (qg_ref[...] * NA_HEAD_DIM ** -0.5)
        kn = (k * lax.rsqrt(kms + EPS) * kg_ref[...]).astype(BF16)
        acc = jnp.zeros((nq, NA_WIDTH), F32)
        for h in range(NA_HEADS):
            hm = lane_h == h
            s = lax.dot_general(jnp.where(hm, qn, 0.0).astype(BF16), kn, (((1,), (1,)), ((), ())),
                                preferred_element_type=F32)
            bias = jnp.concatenate(
                [jnp.concatenate([bias_ref[0, sub, qr, kc, h] for kc in range(NA_KEY_ROWS // 2)], axis=1)
                 for qr in range(NA_SUB_ROWS)], axis=0)
            s = s + bias.astype(F32)
            p = jnp.exp(s - jnp.max(s, axis=-1, keepdims=True))
            l = jnp.sum(p, axis=-1, keepdims=True)
            o = jnp.dot(p.astype(BF16), v, preferred_element_type=F32)
            acc += jnp.where(hm, o / l, 0.0)
        o_ref[0, sub * nq:(sub + 1) * nq, :] = acc.astype(o_ref.dtype)


def _natten(qkv, bias, q_norm, k_norm):
    b, t, _ = qkv.shape
    r = t // GRID_W
    nrb = r // NA_QUERY_ROWS
    nq = NA_QUERY_ROWS * GRID_W
    head = jnp.arange(NA_WIDTH) // NA_HEAD_DIM
    seg = ((head[:, None] == head[None, :]).astype(F32) / NA_HEAD_DIM).astype(BF16)
    return pl.pallas_call(
        functools.partial(_natten_kernel, grid_rows=r),
        grid=(nrb, b),
        in_specs=[pl.BlockSpec((1, t, 3 * NA_WIDTH), lambda i, j: (j, 0, 0)),
                  pl.BlockSpec((1,) + bias.shape[1:], lambda i, j: (i,) + (0,) * (bias.ndim - 1)),
                  _full((1, NA_WIDTH)), _full((1, NA_WIDTH)), _full((NA_WIDTH, NA_WIDTH))],
        out_specs=pl.BlockSpec((1, nq, NA_WIDTH), lambda i, j: (j, i, 0)),
        out_shape=jax.ShapeDtypeStruct((b, t, NA_WIDTH), BF16),
        compiler_params=_cparams(("arbitrary", "arbitrary")),
    )(qkv, bias, jnp.tile(q_norm.astype(F32), NA_HEADS).reshape(1, NA_WIDTH),
      jnp.tile(k_norm.astype(F32), NA_HEADS).reshape(1, NA_WIDTH), seg)


def _memkv_kernel(m_ref, g_ref, w_ref, kg_ref, k_ref, v_ref):
    x = m_ref[0]
    ms = jnp.mean(x * x, axis=-1, keepdims=True)
    h = (x * lax.rsqrt(ms + EPS) * g_ref[...]).astype(BF16)
    kv = jnp.dot(h, w_ref[...], preferred_element_type=F32)
    ks = []
    for hd in range(XA_HEADS):
        kh = kv[:, hd * XA_HEAD_DIM:(hd + 1) * XA_HEAD_DIM]
        ks.append(kh * lax.rsqrt(jnp.mean(kh * kh, axis=-1, keepdims=True) + EPS))
    k_ref[0] = (jnp.concatenate(ks, axis=1) * kg_ref[...]).astype(k_ref.dtype)
    v_ref[0] = kv[:, XA_WIDTH:].astype(v_ref.dtype)


def _memkv(mem, norm_mem, w_xkv, xk_norm):
    b, m, _ = mem.shape
    return pl.pallas_call(
        _memkv_kernel,
        grid=(b,),
        in_specs=[pl.BlockSpec((1, m, D_MODEL), lambda i: (i, 0, 0)),
                  _full((1, D_MODEL)), _full((D_MODEL, 2 * XA_WIDTH)), _full((1, XA_WIDTH))],
        out_specs=[pl.BlockSpec((1, m, XA_WIDTH), lambda i: (i, 0, 0))] * 2,
        out_shape=[jax.ShapeDtypeStruct((b, m, XA_WIDTH), BF16)] * 2,
        compiler_params=_cparams(("arbitrary",)),
    )(mem, norm_mem.reshape(1, D_MODEL).astype(F32), w_xkv,
      jnp.tile(xk_norm.astype(F32), XA_HEADS).reshape(1, XA_WIDTH))


def _mixer_kernel(x_ref, ys_ref, yn_ref, u_ref, up_ref, un_ref, k_ref, v_ref,
                  pw_ref, psc_ref, wo_ref, gxa_ref, wq_ref, qg_ref, wxo_ref, gff_ref,
                  wrb_ref, br_ref, ltri_ref,
                  x2_ref, hf_ref, meta_ref, cnt_ref, carry_ref, *, seq):
    tm = x_ref.shape[1]
    halo = BF16_SUBLANES
    bi = pl.program_id(0)
    i = pl.program_id(1)
    nt = pl.num_programs(1)

    @pl.when(jnp.logical_and(bi == 0, i == 0))
    def _():
        carry_ref[...] = jnp.zeros_like(carry_ref)

    u = u_ref[0].astype(F32)
    up = up_ref[0].astype(F32) * (i > 0).astype(F32)
    un = un_ref[0].astype(F32) * (i < nt - 1).astype(F32)
    cat = jnp.concatenate([up, u, un], axis=0)
    n = tm + 2 * halo

    def sh(a, k):
        return pltpu.roll(a, (-k) % n, 0)

    a2 = cat + sh(cat, -1)
    a4 = sh(a2, 1) + sh(a2, -1)
    a8 = sh(a4, 2) + sh(a4, -2)
    a16 = sh(a8, 4) + sh(a8, -4)
    lane_g = lax.broadcasted_iota(jnp.int32, (1, POOL_WIDTH), 1) // POOL_GROUP_DIM
    wsum = jnp.where(lane_g == 0, a2, jnp.where(lane_g == 1, a4, jnp.where(lane_g == 2, a8, a16)))
    wsum = wsum[halo:halo + tm, :]
    half = jnp.where(lane_g == 0, POOL_WINDOWS[0] // 2,
                     jnp.where(lane_g == 1, POOL_WINDOWS[1] // 2,
                               jnp.where(lane_g == 2, POOL_WINDOWS[2] // 2, POOL_WINDOWS[3] // 2)))
    tpos = i * tm + lax.broadcasted_iota(jnp.int32, (tm, 1), 0)
    cnt = (jnp.minimum(tpos + half, seq) - jnp.maximum(tpos - half, 0)).astype(F32)
    d = wsum / cnt - u
    ypool = jnp.dot(d.astype(BF16), pw_ref[...], preferred_element_type=F32) * psc_ref[...]

    mix = jnp.dot(ys_ref[0], wo_ref[0:D_INNER, :], preferred_element_type=F32)
    mix += jnp.dot(yn_ref[0], wo_ref[D_INNER:D_INNER + NA_WIDTH, :], preferred_element_type=F32)
    mix += jnp.dot(ypool.astype(BF16), wo_ref[D_INNER + NA_WIDTH:, :], preferred_element_type=F32)
    x1 = x_ref[0] + mix

    hn = (x1 * lax.rsqrt(jnp.mean(x1 * x1, axis=-1, keepdims=True) + EPS) * gxa_ref[...]).astype(BF16)
    q = jnp.dot(hn, wq_ref[...], preferred_element_type=F32)
    kk = k_ref[0]
    vv = v_ref[0]
    scale = XA_HEAD_DIM ** -0.5
    outs = []
    for hd in range(XA_HEADS):
        sl = slice(hd * XA_HEAD_DIM, (hd + 1) * XA_HEAD_DIM)
        qh = q[:, sl]
        qh = (qh * lax.rsqrt(jnp.mean(qh * qh, axis=-1, keepdims=True) + EPS) * qg_ref[:, sl]).astype(BF16)
        s = lax.dot_general(qh, kk[:, sl], (((1,), (1,)), ((), ())), preferred_element_type=F32) * scale
        p = jnp.exp(s - jnp.max(s, axis=-1, keepdims=True))
        l = jnp.sum(p, axis=-1, keepdims=True)
        outs.append(jnp.dot(p.astype(BF16), vv[:, sl], preferred_element_type=F32) / l)
    att = jnp.concatenate(outs, axis=1).astype(BF16)
    x2 = x1 + jnp.dot(att, wxo_ref[...], preferred_element_type=F32)
    x2_ref[0] = x2

    hf = x2 * lax.rsqrt(jnp.mean(x2 * x2, axis=-1, keepdims=True) + EPS) * gff_ref[...]
    h_hi = hf.astype(BF16)
    hf_ref[0, :, :HALF] = pltpu.bitcast(_pack_bf16_pairs(h_hi.astype(F32)), jnp.int32)
    h_lo = (hf - h_hi.astype(F32)).astype(BF16)
    both = jnp.dot(h_hi, wrb_ref[...], preferred_element_type=F32)
    logits = (both[:, :LANES] + both[:, LANES:]
              + jnp.dot(h_lo, wrb_ref[:, :LANES], preferred_element_type=F32)) + br_ref[...]
    lane = lax.broadcasted_iota(jnp.int32, (1, LANES), 1)
    lane_f = lane.astype(F32)
    lane_grp = (lane // EXPERTS_PER_GROUP).astype(F32)
    is_g = jnp.logical_and(lane >= N_EXPERTS, lane < N_EXPERTS + N_EXPERT_GROUPS)
    gl = jnp.where(is_g, logits, NEG_BIG)
    gmax = jnp.max(gl, axis=-1, keepdims=True)
    g_sel = jnp.min(jnp.where(gl == gmax, lane_f, float(LANES)), axis=-1, keepdims=True) - N_EXPERTS
    g_gate = 1.0 / jnp.sum(jnp.where(is_g, jnp.exp(gl - gmax), 0.0), axis=-1, keepdims=True)
    in_grp = jnp.logical_and(lane < N_EXPERTS, lane_grp == g_sel)
    el = jnp.where(in_grp, logits, NEG_BIG)
    v1 = jnp.max(el, axis=-1, keepdims=True)
    e0 = jnp.min(jnp.where(el == v1, lane_f, float(LANES)), axis=-1, keepdims=True)
    el2 = jnp.where(lane_f == e0, NEG_BIG, el)
    v2 = jnp.max(el2, axis=-1, keepdims=True)
    e1 = jnp.min(jnp.where(el2 == v2, lane_f, float(LANES)), axis=-1, keepdims=True)
    w1 = jnp.exp(v2 - v1)
    gate0 = g_gate / (1.0 + w1)
    gate1 = g_gate * w1 / (1.0 + w1)

    base = g_sel * EXPERTS_PER_GROUP
    ea = jnp.minimum(e0, e1) - base
    eb = jnp.maximum(e0, e1) - base
    combo = g_sel * PAIRS_PER_GROUP + ea * EXPERTS_PER_GROUP - ea * (ea + 1.0) * 0.5 + (eb - ea - 1.0)
    gate_a = jnp.where(e0 < e1, gate0, gate1)
    gate_b = jnp.where(e0 < e1, gate1, gate0)
    hf_ref[0, :, HALF:] = pltpu.bitcast(jnp.where(lane == 0, gate_a, 0.0) + jnp.where(lane == 1, gate_b, 0.0), jnp.int32)

    oh = lane_f == combo
    cnt_tok = oh.astype(F32)
    before = jnp.dot(ltri_ref[...], cnt_tok.astype(BF16), preferred_element_type=F32) + carry_ref[0:1, :]
    rank = jnp.sum(jnp.where(oh, before, 0.0), axis=-1, keepdims=True)
    new_carry = carry_ref[0:1, :] + jnp.sum(cnt_tok, axis=0, keepdims=True)
    carry_ref[...] = jnp.broadcast_to(new_carry, carry_ref.shape)
    cnt_ref[...] = jnp.broadcast_to(new_carry, cnt_ref.shape)

    slab = jnp.where(lane == 0, combo, 0.0) + jnp.where(lane == 1, rank, 0.0)
    meta_ref[0] = slab.T[0:8, :]


def _mixer(x, y_ssd, y_na, u, kmem, vmem, p):
    b, t, _ = x.shape
    tm = TOKEN_TILE
    nt = t // tm
    hb = tm // BF16_SUBLANES
    nhalo = t // BF16_SUBLANES
    ltri = (jnp.arange(tm)[:, None] > jnp.arange(tm)[None, :]).astype(BF16)
    tok = lambda w: pl.BlockSpec((1, tm, w), lambda i, j: (i, j, 0))
    mem = pl.BlockSpec((1, kmem.shape[1], XA_WIDTH), lambda i, j: (i, 0, 0))
    weights = (p["pool_bd"], p["pool_scale"], p["w_out"], p["norm_xa"], p["w_xq"], p["xq_norm"], p["w_xo"],
               p["norm_ffn"], p["w_r_both"], p["b_r"], ltri)
    return pl.pallas_call(
        functools.partial(_mixer_kernel, seq=t),
        grid=(b, nt),
        in_specs=[tok(D_MODEL), tok(D_INNER), tok(NA_WIDTH), tok(POOL_WIDTH),
                  pl.BlockSpec((1, BF16_SUBLANES, POOL_WIDTH), lambda i, j: (i, jnp.maximum(j * hb - 1, 0), 0)),
                  pl.BlockSpec((1, BF16_SUBLANES, POOL_WIDTH),
                               lambda i, j: (i, jnp.minimum((j + 1) * hb, nhalo - 1), 0)),
                  mem, mem] + [_full(w.shape) for w in weights],
        out_specs=[tok(D_MODEL), tok(ROW_WORDS),
                   pl.BlockSpec((1, 8, tm), lambda i, j: (i * nt + j, 0, 0)),
                   pl.BlockSpec((8, LANES), lambda i, j: (0, 0))],
        out_shape=[jax.ShapeDtypeStruct((b, t, D_MODEL), F32),
                   jax.ShapeDtypeStruct((b, t, ROW_WORDS), jnp.int32),
                   jax.ShapeDtypeStruct((b * nt, 8, tm), F32),
                   jax.ShapeDtypeStruct((8, LANES), F32)],
        scratch_shapes=[pltpu.VMEM((8, LANES), F32)],
        compiler_params=_cparams(("arbitrary", "arbitrary")),
    )(x, y_ssd, y_na, u, u, u, kmem, vmem, *weights)


def _zero_pad_kernel(pstart_ref, cnt_ref, xs_in, zero_ref, xs_ref, sem, *, bm):
    del xs_in

    def fill(cm, wait):
        n_c = cnt_ref[cm]
        npad = (bm - n_c % bm) % bm
        head = (8 - n_c % 8) % 8
        base = pstart_ref[cm] + n_c

        def go(cp):
            if wait:
                cp.wait()
            else:
                cp.start()

        def one(r, _):
            go(pltpu.make_async_copy(zero_ref.at[pl.ds(0, 1)], xs_ref.at[pl.ds(base + r, 1)], sem))
            return 0

        lax.fori_loop(0, head, one, 0)
        rest = npad - head
        off = base + head
        size = bm // 2
        while size >= 8:
            @pl.when(rest & size != 0)
            def _(off=off, size=size):
                go(pltpu.make_async_copy(zero_ref.at[pl.ds(0, size)],
                                         xs_ref.at[pl.ds(pl.multiple_of(off, 8), size)], sem))

            off = off + (rest & size)
            size //= 2
        return 0

    lax.fori_loop(0, N_COMBOS, lambda cm, _: fill(cm, False), 0)
    lax.fori_loop(0, N_COMBOS, lambda cm, _: fill(cm, True), 0)


def _zero_pad_rows(xs, pstart, counts, bm):
    return pl.pallas_call(
        functools.partial(_zero_pad_kernel, bm=bm),
        grid_spec=pltpu.PrefetchScalarGridSpec(
            num_scalar_prefetch=2,
            grid=(1,),
            in_specs=[pl.BlockSpec(memory_space=pl.ANY),
                      pl.BlockSpec((bm // 2, ROW_WORDS), lambda i, *_: (0, 0))],
            out_specs=pl.BlockSpec(memory_space=pl.ANY),
            scratch_shapes=[pltpu.SemaphoreType.DMA(())]),
        out_shape=jax.ShapeDtypeStruct(xs.shape, xs.dtype),
        input_output_aliases={2: 0},
        compiler_params=_cparams(("arbitrary",)),
    )(pstart, counts, xs, jnp.zeros((bm // 2, ROW_WORDS), xs.dtype))


def _sc_rows(kind, src, idx, n_out):
    n = idx.shape[0]
    width = src.shape[1]
    info = plsc.get_sparse_core_info()
    workers = info.num_cores * info.num_subcores
    per_worker = n // workers
    assert per_worker * workers == n and per_worker % SC_GATHER_ROWS == 0, (n, workers)
    mesh = plsc.VectorSubcoreMesh(core_axis_name="core", subcore_axis_name="subcore")

    def body(src_hbm, idx_hbm, out_hbm, idx_v, rows_v, sem):
        base = (lax.axis_index("subcore") * info.num_cores + lax.axis_index("core")) * per_worker

        @pl.loop(0, per_worker // SC_GATHER_ROWS)
        def _(j):
            off = pl.multiple_of(base + j * SC_GATHER_ROWS, SC_GATHER_ROWS)
            pltpu.sync_copy(idx_hbm.at[pl.ds(off, SC_GATHER_ROWS)], idx_v)
            if kind == "gather":
                pltpu.async_copy(src_hbm.at[idx_v], rows_v, sem).wait()
                pltpu.sync_copy(rows_v, out_hbm.at[pl.ds(off, SC_GATHER_ROWS)])
            else:
                pltpu.sync_copy(src_hbm.at[pl.ds(off, SC_GATHER_ROWS)], rows_v)
                pltpu.async_copy(rows_v, out_hbm.at[idx_v], sem).wait()

    return pl.kernel(
        body, mesh=mesh, out_type=jax.ShapeDtypeStruct((n_out, width), src.dtype),
        scratch_types=[pltpu.VMEM((SC_GATHER_ROWS,), jnp.int32), pltpu.VMEM((SC_GATHER_ROWS, width), src.dtype),
                       pltpu.SemaphoreType.DMA])(src, idx)


def _experts_kernel(ea_ref, eb_ref, nused_ref, x_ref, wga_ref, wua_ref, wda_ref, wgb_ref, wub_ref, wdb_ref, y_ref):
    @pl.when(pl.program_id(0) < nused_ref[0])
    def _():
        x = _unpack_bf16_pairs(pltpu.bitcast(x_ref[:, :HALF], U32)).astype(BF16)
        gates = pltpu.bitcast(x_ref[:, HALF:], F32)

        def mlp(wg_ref, wu_ref, wd_ref):
            hg = jnp.dot(x, wg_ref[0], preferred_element_type=F32)
            hu = jnp.dot(x, wu_ref[0], preferred_element_type=F32)
            return jnp.dot((_silu(hg) * hu).astype(BF16), wd_ref[0], preferred_element_type=F32)

        y = mlp(wga_ref, wua_ref, wda_ref) * gates[:, 0:1] + mlp(wgb_ref, wub_ref, wdb_ref) * gates[:, 1:2]
        y_ref[...] = pltpu.bitcast(_pack_bf16_pairs(y.astype(BF16).astype(F32)), jnp.int32)


def _experts(xs, blk_ea, blk_eb, nused, w_gate, w_up, w_down, bm):
    nblk = blk_ea.shape[0]
    row = lambda j, ea, eb, nu: (jnp.minimum(j, nu[0] - 1), 0)
    row_out = lambda j, ea, eb, nu: (jnp.where(j < nu[0], j, nblk - 1), 0)
    sel_a = lambda j, ea, eb, nu: (ea[j], 0, 0)
    sel_b = lambda j, ea, eb, nu: (eb[j], 0, 0)
    up = lambda sel: pl.BlockSpec((1, D_MODEL, D_EXPERT), sel)
    down = lambda sel: pl.BlockSpec((1, D_EXPERT, D_MODEL), sel)
    return pl.pallas_call(
        _experts_kernel,
        grid_spec=pltpu.PrefetchScalarGridSpec(
            num_scalar_prefetch=3,
            grid=(nblk,),
            in_specs=[pl.BlockSpec((bm, ROW_WORDS), row),
                      up(sel_a), up(sel_a), down(sel_a), up(sel_b), up(sel_b), down(sel_b)],
            out_specs=pl.BlockSpec((bm, HALF), row_out)),
        out_shape=jax.ShapeDtypeStruct((nblk * bm, HALF), jnp.int32),
        compiler_params=_cparams(("arbitrary",)),
    )(blk_ea, blk_eb, nused, xs, w_gate, w_up, w_down, w_gate, w_up, w_down)


def _add_rows_kernel(x_ref, yg_ref, o_ref):
    o_ref[...] = x_ref[...] + _unpack_bf16_pairs(pltpu.bitcast(yg_ref[...], U32))


def _add_inproj_kernel(x2_ref, yg_ref, g_ref, w_ref, x_ref, z_ref, xbc_ref, qkv_ref, u_ref, dt_ref):
    _add_rows_kernel(x2_ref, yg_ref, x_ref)
    _inproj_kernel(x_ref, g_ref, w_ref, z_ref, xbc_ref, qkv_ref, u_ref, dt_ref)


def _combine(x2d, yg, gain=None, w_cat=None):
    n = x2d.shape[0]
    tm = TOKEN_TILE
    tile = lambda w: pl.BlockSpec((tm, w), lambda i: (i, 0))
    if w_cat is None:
        return pl.pallas_call(
            _add_rows_kernel, grid=(n // tm,), in_specs=[tile(D_MODEL), tile(HALF)], out_specs=tile(D_MODEL),
            out_shape=jax.ShapeDtypeStruct((n, D_MODEL), F32), compiler_params=_cparams(("arbitrary",)))(x2d, yg)
    widths = (D_MODEL,) + _INPROJ_WIDTHS
    dtypes = (F32,) + _INPROJ_DTYPES
    return pl.pallas_call(
        _add_inproj_kernel, grid=(n // tm,),
        in_specs=[tile(D_MODEL), tile(HALF), _full((1, D_MODEL)), _full(w_cat.shape)],
        out_specs=[tile(w) for w in widths],
        out_shape=[jax.ShapeDtypeStruct((n, w), d) for w, d in zip(widths, dtypes)],
        compiler_params=_cparams(("arbitrary",)),
    )(x2d, yg, gain.reshape(1, D_MODEL), w_cat)


_PAIR_A = np.array([a for a in range(EXPERTS_PER_GROUP) for _ in range(a + 1, EXPERTS_PER_GROUP)], np.int32)
_PAIR_B = np.array([b for a in range(EXPERTS_PER_GROUP) for b in range(a + 1, EXPERTS_PER_GROUP)], np.int32)


def _expert_block(n):
    return 2 * EXPERT_BLOCK_MIN if n >= EXPERT_BLOCK_MIN * N_COMBOS else EXPERT_BLOCK_MIN


def _moe(x2, hf, meta, counts, w_gate, w_up, w_down):
    b, t, _ = x2.shape
    n = b * t
    bm = _expert_block(n)
    nblk = (n + N_COMBOS * (bm - 1) + bm - 1) // bm
    cnt = counts[0, :N_COMBOS].astype(jnp.int32)
    psz = (cnt + bm - 1) // bm * bm
    pend = jnp.cumsum(psz)
    pstart = (pend - psz).astype(jnp.int32)
    nused = jnp.maximum(pend[-1] // bm, 1).astype(jnp.int32).reshape(1)
    blk = jnp.minimum(jnp.arange(nblk, dtype=jnp.int32), nused[0] - 1)
    blk_c = jnp.minimum(jnp.sum(pend[None, :] <= (blk * bm)[:, None], axis=1), N_COMBOS - 1).astype(jnp.int32)
    grp = blk_c // PAIRS_PER_GROUP
    blk_ea = (grp * EXPERTS_PER_GROUP + jnp.asarray(_PAIR_A)[blk_c % PAIRS_PER_GROUP]).astype(jnp.int32)
    blk_eb = (grp * EXPERTS_PER_GROUP + jnp.asarray(_PAIR_B)[blk_c % PAIRS_PER_GROUP]).astype(jnp.int32)
    ids = meta.astype(jnp.int32)
    combo = ids[:, 0, :].reshape(n)
    dest = ids[:, 1, :].reshape(n) + jnp.sum(jnp.where(combo[:, None] == jnp.arange(N_COMBOS)[None, :], pstart[None, :], 0),
                                             axis=1)
    xs = _sc_rows("scatter", hf.reshape(n, ROW_WORDS), dest, nblk * bm)
    xs = _zero_pad_rows(xs, pstart, cnt, bm)
    y = _experts(xs, blk_ea, blk_eb, nused, w_gate, w_up, w_down, bm)
    yg = _sc_rows("gather", y, dest, n)
    return x2.reshape(n, D_MODEL), yg


def _prep_layer(lp):
    w_in = lp["w_in"]
    c0 = D_INNER + CONV_CH
    c1 = c0 + 2 * SSD_HEADS
    w_cat = jnp.concatenate([w_in[:, :c0], w_in[:, c1:], w_in[:, c0:c1],
                             jnp.zeros((D_MODEL, DT_PAD - 2 * SSD_HEADS), w_in.dtype)], axis=1).astype(BF16)
    pool_bd = jnp.zeros((POOL_WIDTH, POOL_WIDTH), F32)
    for g in range(POOL_GROUPS):
        sl = slice(g * POOL_GROUP_DIM, (g + 1) * POOL_GROUP_DIM)
        pool_bd = pool_bd.at[sl, sl].set(lp["pool_w"][g].astype(F32))
    w_r = jnp.concatenate([lp["w_router_expert"], lp["w_router_group"],
                           jnp.zeros((D_MODEL, LANES - N_EXPERTS - N_EXPERT_GROUPS), F32)], axis=1).astype(F32)
    w_r_hi = w_r.astype(BF16)
    w_r_lo = (w_r - w_r_hi.astype(F32)).astype(BF16)
    b_r = jnp.concatenate([lp["b_router_expert"], lp["b_router_group"],
                           jnp.zeros((LANES - N_EXPERTS - N_EXPERT_GROUPS,), F32)]).reshape(1, LANES).astype(F32)
    row = lambda a, w: a.reshape(1, w).astype(F32)
    return dict(
        norm_mix=lp["norm_mix"], w_cat=w_cat,
        conv_w=lp["conv_w"], conv_b=lp["conv_b"], dt_bias=lp["dt_bias"], a_log=lp["a_log"],
        d_skip=lp["d_skip"], ssd_norm=lp["ssd_norm"],
        na_q_norm=lp["na_q_norm"], na_k_norm=lp["na_k_norm"], na_rpb=lp["na_rpb"],
        pool_bd=pool_bd.astype(BF16), pool_scale=row(lp["pool_scale"], POOL_WIDTH),
        w_out=lp["w_out"].astype(BF16), norm_xa=row(lp["norm_xa"], D_MODEL),
        norm_mem=lp["norm_mem"], w_xq=lp["w_xq"].astype(BF16), w_xkv=lp["w_xkv"].astype(BF16),
        xq_norm=row(jnp.tile(lp["xq_norm"], XA_HEADS), XA_WIDTH), xk_norm=lp["xk_norm"],
        w_xo=lp["w_xo"].astype(BF16), norm_ffn=row(lp["norm_ffn"], D_MODEL),
        w_r_both=jnp.concatenate([w_r_hi, w_r_lo], axis=1), b_r=b_r,
        w_e_gate=lp["w_e_gate"].astype(BF16), w_e_up=lp["w_e_up"].astype(BF16),
        w_e_down=lp["w_e_down"].astype(BF16),
    )


def _layer(x, pending, mem, p, na_bias):
    b, m, _ = mem.shape
    if pending is None:
        t = x.shape[1]
        z, xbc, qkv, u, dt = _inproj(x.reshape(b * t, D_MODEL), p["norm_mix"], p["w_cat"])
    else:
        t = pending[0].shape[0] // b
        x, z, xbc, qkv, u, dt = _combine(*pending, p["norm_mix"], p["w_cat"])
        x = x.reshape(b, t, D_MODEL)
    r3 = lambda a: a.reshape(b, t, a.shape[-1])
    y_ssd = _ssd(r3(z), r3(xbc), r3(dt), p["conv_w"], p["conv_b"], p["dt_bias"], p["a_log"], p["d_skip"],
                 p["ssd_norm"])
    y_na = _natten(r3(qkv), na_bias, p["na_q_norm"], p["na_k_norm"])
    kmem, vmem = _memkv(mem, p["norm_mem"], p["w_xkv"], p["xk_norm"])
    x2, hf, meta, counts = _mixer(x, y_ssd, y_na, r3(u), kmem, vmem, p)
    return _moe(x2, hf, meta, counts, p["w_e_gate"], p["w_e_up"], p["w_e_down"])


_LAYER_KEYS = ("norm_mix", "w_in", "conv_w", "conv_b", "dt_bias", "a_log", "d_skip", "ssd_norm", "na_q_norm",
               "na_k_norm", "na_rpb", "pool_w", "pool_scale", "w_out", "norm_xa", "norm_mem", "w_xq", "w_xkv",
               "xq_norm", "xk_norm", "w_xo", "norm_ffn", "w_router_group", "b_router_group", "w_router_expert",
               "b_router_expert", "w_e_gate", "w_e_up", "w_e_down")


def kernel(x_prompt, x_sample, mem_prompt, mem_sample, norm_mix, w_in, conv_w, conv_b, dt_bias, a_log, d_skip, ssd_norm, na_q_norm, na_k_norm, na_rpb, pool_w, pool_scale, w_out, norm_xa, norm_mem, w_xq, w_xkv, xq_norm, xk_norm, w_xo, norm_ffn, w_router_group, b_router_group, w_router_expert, b_router_expert, w_e_gate, w_e_up, w_e_down):
    stacked = dict(zip(_LAYER_KEYS, (norm_mix, w_in, conv_w, conv_b, dt_bias, a_log, d_skip, ssd_norm, na_q_norm,
                                     na_k_norm, na_rpb, pool_w, pool_scale, w_out, norm_xa, norm_mem, w_xq, w_xkv,
                                     xq_norm, xk_norm, w_xo, norm_ffn, w_router_group, b_router_group,
                                     w_router_expert, b_router_expert, w_e_gate, w_e_up, w_e_down)))
    depth = w_in.shape[0]
    layers = [_prep_layer({k: v[l] for k, v in stacked.items()}) for l in range(depth)]

    bias_cache = {}

    def trunk(x, mem):
        t = x.shape[1]
        pending = None
        for l, lp in enumerate(layers):
            if (l, t) not in bias_cache:
                bias_cache[(l, t)] = _na_bias(lp["na_rpb"], t)
            pending = _layer(x, pending, mem, lp, bias_cache[(l, t)])
            x = None
        return _combine(*pending).reshape(mem.shape[0], t, D_MODEL)

    return trunk(x_prompt, mem_prompt), trunk(x_sample, mem_sample)
```

```python
import functools

import jax
import jax.numpy as jnp
import numpy as np
from jax import lax
from jax.experimental import pallas as pl
from jax.experimental.pallas import tpu as pltpu
from jax.experimental.pallas import tpu_sc as plsc

F32 = jnp.float32
BF16 = jnp.bfloat16
U32 = jnp.uint32
HIGHEST = lax.Precision.HIGHEST

D_MODEL = 1024
GRID_W = 64
EPS = 1e-6
SSD_HEAD_DIM = 64
D_INNER = D_MODEL // 2
SSD_HEADS = D_INNER // SSD_HEAD_DIM
SSD_GROUPS = 2
SSD_STATE = 64
SSD_CHUNK = 128
CONV_W = 4
CONV_CH = D_INNER + 2 * SSD_GROUPS * SSD_STATE
NA_HEADS = 4
NA_HEAD_DIM = D_MODEL // 16
NA_WIDTH = NA_HEADS * NA_HEAD_DIM
NA_MAX_KH = 8
NA_KW = 16
POOL_WINDOWS = (2, 4, 8, 16)
POOL_GROUPS = 4
POOL_WIDTH = D_MODEL - D_INNER - NA_WIDTH
POOL_GROUP_DIM = POOL_WIDTH // POOL_GROUPS
XA_HEADS = 4
XA_HEAD_DIM = D_MODEL // 8
XA_WIDTH = XA_HEADS * XA_HEAD_DIM
N_EXPERT_GROUPS = 4
EXPERTS_PER_GROUP = 8
N_EXPERTS = N_EXPERT_GROUPS * EXPERTS_PER_GROUP
D_EXPERT = D_MODEL // 4
PAIRS_PER_GROUP = EXPERTS_PER_GROUP * (EXPERTS_PER_GROUP - 1) // 2
N_COMBOS = N_EXPERT_GROUPS * PAIRS_PER_GROUP

LANES = 128
BF16_SUBLANES = 16
VMEM_LIMIT_BYTES = 56 * 1024 * 1024

TOKEN_TILE = 1024
NA_QUERY_ROWS = 16
NA_SUB_ROWS = 8
NA_SUBS = NA_QUERY_ROWS // NA_SUB_ROWS
NA_KEY_ROWS = NA_SUB_ROWS + NA_MAX_KH
EXPERT_BLOCK_MIN = 128
HALF = D_MODEL // 2
ROW_WORDS = HALF + LANES
SSD_STEP_CHUNKS = 8
DT_PAD = LANES
SC_GATHER_ROWS = 128
NEG_BIG = -1e30


def _cparams(sem):
    return pltpu.CompilerParams(dimension_semantics=sem, vmem_limit_bytes=VMEM_LIMIT_BYTES)


def _sigmoid(x):
    return 1.0 / (1.0 + jnp.exp(-x))


def _silu(x):
    return x * _sigmoid(x)


def _softplus(x):
    return jnp.maximum(x, 0.0) + jnp.log(1.0 + jnp.exp(-jnp.abs(x)))


def _pack_bf16_pairs(v):
    k = v.shape[1] // 2
    bits = pltpu.bitcast(v, U32)
    return (bits[:, :k] >> 16) | (bits[:, k:] & jnp.uint32(0xFFFF0000))


def _unpack_bf16_pairs(w):
    lo = pltpu.bitcast(w << 16, F32)
    hi = pltpu.bitcast(w & jnp.uint32(0xFFFF0000), F32)
    return jnp.concatenate([lo, hi], axis=1)


def _full(shape):
    n = len(shape)
    return pl.BlockSpec(shape, lambda *_: (0,) * n)


def _inproj_kernel(x_ref, g_ref, w_ref, z_ref, xbc_ref, qkv_ref, u_ref, dt_ref):
    x = x_ref[...]
    ms = jnp.mean(x * x, axis=-1, keepdims=True)
    h = (x * lax.rsqrt(ms + EPS) * g_ref[...]).astype(BF16)
    o = 0
    for ref in (z_ref, xbc_ref, qkv_ref, u_ref, dt_ref):
        w = ref.shape[-1]
        ref[...] = jnp.dot(h, w_ref[:, o:o + w], preferred_element_type=F32).astype(ref.dtype)
        o += w


_INPROJ_WIDTHS = (D_INNER, CONV_CH, 3 * NA_WIDTH, POOL_WIDTH, DT_PAD)
_INPROJ_DTYPES = (BF16, BF16, BF16, BF16, F32)


def _inproj(x2d, gain, w_cat):
    n = x2d.shape[0]
    tm = TOKEN_TILE
    widths, dtypes = _INPROJ_WIDTHS, _INPROJ_DTYPES
    return pl.pallas_call(
        _inproj_kernel,
        grid=(n // tm,),
        in_specs=[pl.BlockSpec((tm, D_MODEL), lambda i: (i, 0)),
                  _full((1, D_MODEL)),
                  _full(w_cat.shape)],
        out_specs=[pl.BlockSpec((tm, w), lambda i: (i, 0)) for w in widths],
        out_shape=[jax.ShapeDtypeStruct((n, w), d) for w, d in zip(widths, dtypes)],
        compiler_params=_cparams(("arbitrary",)),
    )(x2d, gain.reshape(1, D_MODEL), w_cat)


def _split3(a):
    hi = a.astype(BF16)
    r = a - hi.astype(F32)
    mid = r.astype(BF16)
    lo = (r - mid.astype(F32)).astype(BF16)
    return hi, mid, lo


def _ssd_kernel(xc_ref, xp_ref, xn_ref, dt_ref, z_ref, cw_ref, cb_ref, dtb_ref, alog_ref, dsk_ref, nrm_ref, emat_ref,
                y_ref, state_ref, yf_ref, xs_c, bc_c, cbm_c, dt_c, *, nblocks):
    L = SSD_CHUNK
    LB = SSD_STEP_CHUNKS * L
    P = SSD_HEAD_DIM
    NS = SSD_STATE
    HG = SSD_HEADS // SSD_GROUPS
    gn = SSD_GROUPS * NS
    j = pl.program_id(1)
    c = jnp.where(j < nblocks, j, 2 * nblocks - 1 - j)
    row0 = pl.multiple_of(c * LB, LB)
    rows = pl.ds(row0, LB)
    lane1 = lax.broadcasted_iota(jnp.int32, (1, LANES), 1)
    lo_half = lane1 < P
    ti = lax.broadcasted_iota(jnp.int32, (L, L), 0)
    si = lax.broadcasted_iota(jnp.int32, (L, L), 1)

    def masked_c(bc):
        return [jnp.where(lane1 // NS == g, bc[:, gn:], 0.0).astype(BF16) for g in range(SSD_GROUPS)]

    def prepare():
        cur = xc_ref[0].astype(F32)
        prev = xp_ref[0].astype(F32)
        nxt = xn_ref[0].astype(F32)
        has_prev = (c > 0).astype(F32)
        has_next = (c < nblocks - 1).astype(F32)
        p_last = prev[BF16_SUBLANES - 1:BF16_SUBLANES, :] * has_prev
        n0 = nxt[0:1, :] * has_next
        n1 = nxt[1:2, :] * has_next
        row = lax.broadcasted_iota(jnp.int32, (LB, 1), 0)
        um1 = jnp.where(row == 0, p_last, pltpu.roll(cur, 1, 0))
        up1 = jnp.where(row == LB - 1, n0, pltpu.roll(cur, LB - 1, 0))
        up2 = jnp.where(row == LB - 2, n0, jnp.where(row == LB - 1, n1, pltpu.roll(cur, LB - 2, 0)))
        cw = cw_ref[...]
        acc = cb_ref[...] + um1 * cw[0:1, :] + cur * cw[1:2, :] + up1 * cw[2:3, :] + up2 * cw[3:4, :]
        xbc = _silu(acc)
        xs = xbc[:, :D_INNER]
        bc = xbc[:, D_INNER:D_INNER + 2 * gn].astype(BF16)
        dt = _softplus(dt_ref[0] + dtb_ref[...])
        xs_c[rows, :] = xs
        bc_c[rows, :] = bc
        dt_c[rows, :] = dt
        ops = []
        for sub in range(SSD_STEP_CHUNKS):
            sl = slice(sub * L, (sub + 1) * L)
            cg = masked_c(bc[sl])
            cb_mat = [lax.dot_general(cg[g], bc[sl, :gn], (((1,), (1,)), ((), ())), preferred_element_type=F32)
                      for g in range(SSD_GROUPS)]
            cbm_c[pl.ds(row0 + sub * L, L), :] = jnp.concatenate(cb_mat, axis=1)
            ops.append((xs[sl], bc[sl, :gn], cg, cb_mat, dt[sl]))
        return ops

    def recall():
        ops = []
        for sub in range(SSD_STEP_CHUNKS):
            r = pl.ds(row0 + sub * L, L)
            bc = bc_c[r, :]
            cbm = cbm_c[r, :]
            ops.append((xs_c[r, :], bc[:, :gn], masked_c(bc), [cbm[:, g * L:(g + 1) * L] for g in range(SSD_GROUPS)],
                        dt_c[r, :]))
        return ops

    def scan_chunk(direction, xs, bfull, cg, cb_mat, dt):
        if direction == 0:
            mask = ti >= si
            edge = L - 1
        else:
            mask = si >= ti
            edge = 0
        la = dt * (-jnp.exp(alog_ref[...]))
        tri = mask.astype(BF16)
        csum = sum(jnp.dot(tri, part, preferred_element_type=F32) for part in _split3(la))
        csum_t = csum.T
        emat = emat_ref[direction]
        colb = sum(jnp.dot(part, emat, preferred_element_type=F32) for part in _split3(csum))
        tot = csum[edge:edge + 1, :]
        e_tot = jnp.exp(tot)
        e_in = jnp.exp(csum)
        e_out = jnp.exp(tot - csum)
        ys = []
        for g in range(SSD_GROUPS):
            s_old = state_ref[g]
            y_off = lax.dot_general(cg[g], s_old.astype(BF16), (((1,), (1,)), ((), ())),
                                    preferred_element_type=F32)
            xw = []
            for pr in range(HG // 2):
                h0 = g * HG + 2 * pr
                l0 = direction * SSD_HEADS + h0

                def col(a, l0=l0):
                    return jnp.where(lo_half, a[:, l0:l0 + 1], a[:, l0 + 1:l0 + 2])

                xdt = xs[:, h0 * P:(h0 + 2) * P] * col(dt)
                y_pair = y_off[:, 2 * pr * P:(2 * pr + 2) * P] * col(e_in)
                for hh, half in ((h0, lo_half), (h0 + 1, jnp.logical_not(lo_half))):
                    ll = direction * SSD_HEADS + hh
                    seg = colb[:, hh * L:(hh + 1) * L] - csum_t[ll:ll + 1, :]
                    dec = jnp.exp(jnp.where(mask, seg, NEG_BIG))
                    m = (cb_mat[g] * dec).astype(BF16)
                    y_pair += jnp.dot(m, jnp.where(half, xdt, 0.0).astype(BF16), preferred_element_type=F32)
                ys.append(y_pair)
                xw.append(xdt * col(e_out))
            xw = jnp.concatenate(xw, axis=1).astype(BF16)
            s_new = lax.dot_general(xw, bfull, (((0,), (0,)), ((), ())), preferred_element_type=F32)
            s_scaled = []
            for hl in range(HG):
                lane = direction * SSD_HEADS + g * HG + hl
                s_scaled.append(s_old[hl * P:(hl + 1) * P, :] * e_tot[:, lane:lane + 1])
            state_ref[g] = jnp.concatenate(s_scaled, axis=0) + s_new
        return jnp.concatenate(ys, axis=1)

    @pl.when(jnp.logical_or(j == 0, j == nblocks))
    def _():
        state_ref[...] = jnp.zeros_like(state_ref)

    @pl.when(j < nblocks)
    def _():
        ops = prepare()
        for sub in range(SSD_STEP_CHUNKS):
            yf_ref[pl.ds(row0 + sub * L, L), :] = scan_chunk(0, *ops[sub])

    @pl.when(j >= nblocks)
    def _():
        ops = recall()
        for sub in reversed(range(SSD_STEP_CHUNKS)):
            sl = slice(sub * L, (sub + 1) * L)
            y = yf_ref[pl.ds(row0 + sub * L, L), :] + scan_chunk(1, *ops[sub]) + dsk_ref[...] * ops[sub][0]
            y = y * _silu(z_ref[0, sl, :].astype(F32))
            gw = D_INNER // SSD_GROUPS
            outs = []
            for g in range(SSD_GROUPS):
                yg = y[:, g * gw:(g + 1) * gw]
                outs.append(yg * lax.rsqrt(jnp.mean(yg * yg, axis=-1, keepdims=True) + EPS))
            y_ref[0, sl, :] = (jnp.concatenate(outs, axis=1) * nrm_ref[...]).astype(y_ref.dtype)


def _ssd(z, xbc, dt, conv_w, conv_b, dt_bias, a_log, d_skip, ssd_norm):
    b, t, _ = z.shape
    L = SSD_CHUNK
    lb = SSD_STEP_CHUNKS * L
    nb = t // lb
    hb = lb // BF16_SUBLANES
    nhalo = t // BF16_SUBLANES

    def blk(j):
        return jnp.where(j < nb, j, 2 * nb - 1 - j)

    pad = DT_PAD - 2 * SSD_HEADS
    dtb = jnp.pad(dt_bias.reshape(1, -1).astype(F32), ((0, 0), (0, pad)))
    alog = jnp.pad(a_log.reshape(1, -1).astype(F32), ((0, 0), (0, pad)))
    dsk = jnp.repeat(d_skip.astype(F32), SSD_HEAD_DIM).reshape(1, D_INNER)
    sel = np.arange(DT_PAD)[None, :, None] == (np.arange(2)[:, None, None] * SSD_HEADS
                                                + np.arange(SSD_HEADS)[None, None, :])
    emat = jnp.asarray(np.repeat(sel, L, axis=2), BF16)
    return pl.pallas_call(
        functools.partial(_ssd_kernel, nblocks=nb),
        grid=(b, 2 * nb),
        in_specs=[
            pl.BlockSpec((1, lb, CONV_CH), lambda i, j: (i, blk(j), 0)),
            pl.BlockSpec((1, BF16_SUBLANES, CONV_CH), lambda i, j: (i, jnp.maximum(blk(j) * hb - 1, 0), 0)),
            pl.BlockSpec((1, BF16_SUBLANES, CONV_CH),
                         lambda i, j: (i, jnp.minimum((blk(j) + 1) * hb, nhalo - 1), 0)),
            pl.BlockSpec((1, lb, DT_PAD), lambda i, j: (i, blk(j), 0)),
            pl.BlockSpec((1, lb, D_INNER), lambda i, j: (i, blk(j), 0)),
            _full((CONV_W, CONV_CH)), _full((1, CONV_CH)), _full((1, DT_PAD)), _full((1, DT_PAD)),
            _full((1, D_INNER)), _full((1, D_INNER)), _full((2, DT_PAD, SSD_HEADS * L)),
        ],
        out_specs=pl.BlockSpec((1, lb, D_INNER), lambda i, j: (i, jnp.where(j < nb, nb - 1, 2 * nb - 1 - j), 0)),
        out_shape=jax.ShapeDtypeStruct((b, t, D_INNER), BF16),
        scratch_shapes=[pltpu.VMEM((SSD_GROUPS, (SSD_HEADS // SSD_GROUPS) * SSD_HEAD_DIM, LANES), F32),
                        pltpu.VMEM((t, D_INNER), F32),
                        pltpu.VMEM((t, D_INNER), F32),
                        pltpu.VMEM((t, 2 * SSD_GROUPS * SSD_STATE), BF16),
                        pltpu.VMEM((t, SSD_GROUPS * L), F32),
                        pltpu.VMEM((t, DT_PAD), F32)],
        compiler_params=_cparams(("arbitrary", "arbitrary")),
    )(xbc, xbc, xbc, dt, z, conv_w.astype(F32), conv_b.reshape(1, CONV_CH).astype(F32), dtb, alog, dsk,
      ssd_norm.reshape(1, D_INNER).astype(F32), emat)


def _na_bias(rpb, t):
    r = t // GRID_W
    kh = min(NA_MAX_KH, r)
    nsb = r // NA_SUB_ROWS
    rows = np.arange(r)
    row_start = np.clip(rows - NA_MAX_KH // 2, 0, r - kh)
    r0 = np.arange(nsb) * NA_SUB_ROWS
    kr0 = np.clip(r0 - NA_MAX_KH // 2, 0, r - NA_KEY_ROWS)
    qrow = r0[:, None] + np.arange(NA_SUB_ROWS)[None, :]
    krow = kr0[:, None] + np.arange(NA_KEY_ROWS)[None, :]
    rs = row_start[qrow]
    row_ok = (krow[:, None, :] >= rs[:, :, None]) & (krow[:, None, :] < rs[:, :, None] + kh)
    dr = np.clip(krow[:, None, :] - qrow[:, :, None] + (NA_MAX_KH - 1), 0, 2 * NA_MAX_KH - 2)
    cols = np.arange(GRID_W)
    col_start = np.clip(cols - NA_KW // 2, 0, GRID_W - NA_KW)
    col_ok = (cols[None, :] >= col_start[:, None]) & (cols[None, :] < col_start[:, None] + NA_KW)
    dc = np.clip(cols[None, :] - cols[:, None] + (NA_KW - 1), 0, 2 * NA_KW - 2)
    sel_c = (dc[..., None] == np.arange(2 * NA_KW - 1)) & col_ok[..., None]
    nr = 2 * NA_MAX_KH - 1
    tile_r = jnp.einsum("hrc,xyc->hrxy", rpb.astype(F32), jnp.asarray(sel_c, F32), precision=HIGHEST)
    tile_r = jnp.where(jnp.asarray(col_ok)[None, None], tile_r, NEG_BIG)
    tile_r = jnp.concatenate([tile_r, jnp.full((NA_HEADS, 1, GRID_W, GRID_W), NEG_BIG, F32)], axis=1)
    kp = NA_KEY_ROWS // 2
    code = np.where(row_ok, dr, nr).reshape(nsb * NA_SUB_ROWS * kp, 2)
    pairs, inv = np.unique(code, axis=0, return_inverse=True)
    blocks = jnp.concatenate([tile_r[:, pairs[:, 0]], tile_r[:, pairs[:, 1]]], axis=-1)
    blocks = jnp.moveaxis(blocks, 0, 1).reshape(len(pairs), NA_HEADS * GRID_W * 2 * GRID_W)
    onehot = jnp.asarray(inv.reshape(-1, 1) == np.arange(len(pairs))[None, :], F32)
    bias = jnp.dot(onehot, blocks, precision=HIGHEST)
    return bias.reshape(nsb // NA_SUBS, NA_SUBS, NA_SUB_ROWS, kp, NA_HEADS, GRID_W, 2 * GRID_W).astype(BF16)


def _natten_kernel(qkv_ref, bias_ref, qg_ref, kg_ref, seg_ref, o_ref, *, grid_rows):
    nq = NA_SUB_ROWS * GRID_W
    nk = NA_KEY_ROWS * GRID_W
    rb = pl.program_id(0)
    seg = seg_ref[...]
    lane_h = lax.broadcasted_iota(jnp.int32, (1, NA_WIDTH), 1) // NA_HEAD_DIM
    for sub in range(NA_SUBS):
        r0 = rb * NA_QUERY_ROWS + sub * NA_SUB_ROWS
        kr0 = jnp.clip(r0 - NA_MAX_KH // 2, 0, grid_rows - NA_KEY_ROWS)
        q0 = pl.multiple_of(r0 * GRID_W, nq)
        k0 = pl.multiple_of(kr0 * GRID_W, NA_MAX_KH // 2 * GRID_W)
        q = qkv_ref[0, pl.ds(q0, nq), 0:NA_WIDTH].astype(F32)
        k = qkv_ref[0, pl.ds(k0, nk), NA_WIDTH:2 * NA_WIDTH].astype(F32)
        v = qkv_ref[0, pl.ds(k0, nk), 2 * NA_WIDTH:3 * NA_WIDTH]
        qms = sum(jnp.dot(part, seg, preferred_element_type=F32) for part in _split3(q * q)[:2])
        kms = sum(jnp.dot(part, seg, preferred_element_type=F32) for part in _split3(k * k)[:2])
        qn = q * lax.rsqrt(qms + EPS) * (qg_ref[...] * NA_HEAD_DIM ** -0.5)
        kn = (k * lax.rsqrt(kms + EPS) * kg_ref[...]).astype(BF16)
        acc = jnp.zeros((nq, NA_WIDTH), F32)
        for h in range(NA_HEADS):
            hm = lane_h == h
            s = lax.dot_general(jnp.where(hm, qn, 0.0).astype(BF16), kn, (((1,), (1,)), ((), ())),
                                preferred_element_type=F32)
            bias = jnp.concatenate(
                [jnp.concatenate([bias_ref[0, sub, qr, kc, h] for kc in range(NA_KEY_ROWS // 2)], axis=1)
                 for qr in range(NA_SUB_ROWS)], axis=0)
            s = s + bias.astype(F32)
            p = jnp.exp(s - jnp.max(s, axis=-1, keepdims=True))
            l = jnp.sum(p, axis=-1, keepdims=True)
            o = jnp.dot(p.astype(BF16), v, preferred_element_type=F32)
            acc += jnp.where(hm, o / l, 0.0)
        o_ref[0, sub * nq:(sub + 1) * nq, :] = acc.astype(o_ref.dtype)


def _natten(qkv, bias, q_norm, k_norm):
    b, t, _ = qkv.shape
    r = t // GRID_W
    nrb = r // NA_QUERY_ROWS
    nq = NA_QUERY_ROWS * GRID_W
    head = jnp.arange(NA_WIDTH) // NA_HEAD_DIM
    seg = ((head[:, None] == head[None, :]).astype(F32) / NA_HEAD_DIM).astype(BF16)
    return pl.pallas_call(
        functools.partial(_natten_kernel, grid_rows=r),
        grid=(nrb, b),
        in_specs=[pl.BlockSpec((1, t, 3 * NA_WIDTH), lambda i, j: (j, 0, 0)),
                  pl.BlockSpec((1,) + bias.shape[1:], lambda i, j: (i,) + (0,) * (bias.ndim - 1)),
                  _full((1, NA_WIDTH)), _full((1, NA_WIDTH)), _full((NA_WIDTH, NA_WIDTH))],
        out_specs=pl.BlockSpec((1, nq, NA_WIDTH), lambda i, j: (j, i, 0)),
        out_shape=jax.ShapeDtypeStruct((b, t, NA_WIDTH), BF16),
        compiler_params=_cparams(("arbitrary", "arbitrary")),
    )(qkv, bias, jnp.tile(q_norm.astype(F32), NA_HEADS).reshape(1, NA_WIDTH),
      jnp.tile(k_norm.astype(F32), NA_HEADS).reshape(1, NA_WIDTH), seg)


def _memkv_kernel(m_ref, g_ref, w_ref, kg_ref, k_ref, v_ref):
    x = m_ref[0]
    ms = jnp.mean(x * x, axis=-1, keepdims=True)
    h = (x * lax.rsqrt(ms + EPS) * g_ref[...]).astype(BF16)
    kv = jnp.dot(h, w_ref[...], preferred_element_type=F32)
    ks = []
    for hd in range(XA_HEADS):
        kh = kv[:, hd * XA_HEAD_DIM:(hd + 1) * XA_HEAD_DIM]
        ks.append(kh * lax.rsqrt(jnp.mean(kh * kh, axis=-1, keepdims=True) + EPS))
    k_ref[0] = (jnp.concatenate(ks, axis=1) * kg_ref[...]).astype(k_ref.dtype)
    v_ref[0] = kv[:, XA_WIDTH:].astype(v_ref.dtype)


def _memkv(mem, norm_mem, w_xkv, xk_norm):
    b, m, _ = mem.shape
    return pl.pallas_call(
        _memkv_kernel,
        grid=(b,),
        in_specs=[pl.BlockSpec((1, m, D_MODEL), lambda i: (i, 0, 0)),
                  _full((1, D_MODEL)), _full((D_MODEL, 2 * XA_WIDTH)), _full((1, XA_WIDTH))],
        out_specs=[pl.BlockSpec((1, m, XA_WIDTH), lambda i: (i, 0, 0))] * 2,
        out_shape=[jax.ShapeDtypeStruct((b, m, XA_WIDTH), BF16)] * 2,
        compiler_params=_cparams(("arbitrary",)),
    )(mem, norm_mem.reshape(1, D_MODEL).astype(F32), w_xkv,
      jnp.tile(xk_norm.astype(F32), XA_HEADS).reshape(1, XA_WIDTH))


def _mixer_kernel(x_ref, ys_ref, yn_ref, u_ref, up_ref, un_ref, k_ref, v_ref,
                  pw_ref, psc_ref, wo_ref, gxa_ref, wq_ref, qg_ref, wxo_ref, gff_ref,
                  wrb_ref, br_ref, ltri_ref,
                  x2_ref, hf_ref, meta_ref, cnt_ref, carry_ref, *, seq):
    tm = x_ref.shape[1]
    halo = BF16_SUBLANES
    bi = pl.program_id(0)
    i = pl.program_id(1)
    nt = pl.num_programs(1)

    @pl.when(jnp.logical_and(bi == 0, i == 0))
    def _():
        carry_ref[...] = jnp.zeros_like(carry_ref)

    u = u_ref[0].astype(F32)
    up = up_ref[0].astype(F32) * (i > 0).astype(F32)
    un = un_ref[0].astype(F32) * (i < nt - 1).astype(F32)
    cat = jnp.concatenate([up, u, un], axis=0)
    n = tm + 2 * halo

    def sh(a, k):
        return pltpu.roll(a, (-k) % n, 0)

    a2 = cat + sh(cat, -1)
    a4 = sh(a2, 1) + sh(a2, -1)
    a8 = sh(a4, 2) + sh(a4, -2)
    a16 = sh(a8, 4) + sh(a8, -4)
    lane_g = lax.broadcasted_iota(jnp.int32, (1, POOL_WIDTH), 1) // POOL_GROUP_DIM
    wsum = jnp.where(lane_g == 0, a2, jnp.where(lane_g == 1, a4, jnp.where(lane_g == 2, a8, a16)))
    wsum = wsum[halo:halo + tm, :]
    half = jnp.where(lane_g == 0, POOL_WINDOWS[0] // 2,
                     jnp.where(lane_g == 1, POOL_WINDOWS[1] // 2,
                               jnp.where(lane_g == 2, POOL_WINDOWS[2] // 2, POOL_WINDOWS[3] // 2)))
    tpos = i * tm + lax.broadcasted_iota(jnp.int32, (tm, 1), 0)
    cnt = (jnp.minimum(tpos + half, seq) - jnp.maximum(tpos - half, 0)).astype(F32)
    d = wsum / cnt - u
    ypool = jnp.dot(d.astype(BF16), pw_ref[...], preferred_element_type=F32) * psc_ref[...]

    mix = jnp.dot(ys_ref[0], wo_ref[0:D_INNER, :], preferred_element_type=F32)
    mix += jnp.dot(yn_ref[0], wo_ref[D_INNER:D_INNER + NA_WIDTH, :], preferred_element_type=F32)
    mix += jnp.dot(ypool.astype(BF16), wo_ref[D_INNER + NA_WIDTH:, :], preferred_element_type=F32)
    x1 = x_ref[0] + mix

    hn = (x1 * lax.rsqrt(jnp.mean(x1 * x1, axis=-1, keepdims=True) + EPS) * gxa_ref[...]).astype(BF16)
    q = jnp.dot(hn, wq_ref[...], preferred_element_type=F32)
    kk = k_ref[0]
    vv = v_ref[0]
    scale = XA_HEAD_DIM ** -0.5
    outs = []
    for hd in range(XA_HEADS):
        sl = slice(hd * XA_HEAD_DIM, (hd + 1) * XA_HEAD_DIM)
        qh = q[:, sl]
        qh = (qh * lax.rsqrt(jnp.mean(qh * qh, axis=-1, keepdims=True) + EPS) * qg_ref[:, sl]).astype(BF16)
        s = lax.dot_general(qh, kk[:, sl], (((1,), (1,)), ((), ())), preferred_element_type=F32) * scale
        p = jnp.exp(s - jnp.max(s, axis=-1, keepdims=True))
        l = jnp.sum(p, axis=-1, keepdims=True)
        outs.append(jnp.dot(p.astype(BF16), vv[:, sl], preferred_element_type=F32) / l)
    att = jnp.concatenate(outs, axis=1).astype(BF16)
    x2 = x1 + jnp.dot(att, wxo_ref[...], preferred_element_type=F32)
    x2_ref[0] = x2

    hf = x2 * lax.rsqrt(jnp.mean(x2 * x2, axis=-1, keepdims=True) + EPS) * gff_ref[...]
    h_hi = hf.astype(BF16)
    hf_ref[0, :, :HALF] = pltpu.bitcast(_pack_bf16_pairs(h_hi.astype(F32)), jnp.int32)
    h_lo = (hf - h_hi.astype(F32)).astype(BF16)
    both = jnp.dot(h_hi, wrb_ref[...], preferred_element_type=F32)
    logits = (both[:, :LANES] + both[:, LANES:]
              + jnp.dot(h_lo, wrb_ref[:, :LANES], preferred_element_type=F32)) + br_ref[...]
    lane = lax.broadcasted_iota(jnp.int32, (1, LANES), 1)
    lane_f = lane.astype(F32)
    lane_grp = (lane // EXPERTS_PER_GROUP).astype(F32)
    is_g = jnp.logical_and(lane >= N_EXPERTS, lane < N_EXPERTS + N_EXPERT_GROUPS)
    gl = jnp.where(is_g, logits, NEG_BIG)
    gmax = jnp.max(gl, axis=-1, keepdims=True)
    g_sel = jnp.min(jnp.where(gl == gmax, lane_f, float(LANES)), axis=-1, keepdims=True) - N_EXPERTS
    g_gate = 1.0 / jnp.sum(jnp.where(is_g, jnp.exp(gl - gmax), 0.0), axis=-1, keepdims=True)
    in_grp = jnp.logical_and(lane < N_EXPERTS, lane_grp == g_sel)
    el = jnp.where(in_grp, logits, NEG_BIG)
    v1 = jnp.max(el, axis=-1, keepdims=True)
    e0 = jnp.min(jnp.where(el == v1, lane_f, float(LANES)), axis=-1, keepdims=True)
    el2 = jnp.where(lane_f == e0, NEG_BIG, el)
    v2 = jnp.max(el2, axis=-1, keepdims=True)
    e1 = jnp.min(jnp.where(el2 == v2, lane_f, float(LANES)), axis=-1, keepdims=True)
    w1 = jnp.exp(v2 - v1)
    gate0 = g_gate / (1.0 + w1)
    gate1 = g_gate * w1 / (1.0 + w1)

    base = g_sel * EXPERTS_PER_GROUP
    ea = jnp.minimum(e0, e1) - base
    eb = jnp.maximum(e0, e1) - base
    combo = g_sel * PAIRS_PER_GROUP + ea * EXPERTS_PER_GROUP - ea * (ea + 1.0) * 0.5 + (eb - ea - 1.0)
    gate_a = jnp.where(e0 < e1, gate0, gate1)
    gate_b = jnp.where(e0 < e1, gate1, gate0)
    hf_ref[0, :, HALF:] = pltpu.bitcast(jnp.where(lane == 0, gate_a, 0.0) + jnp.where(lane == 1, gate_b, 0.0), jnp.int32)

    oh = lane_f == combo
    cnt_tok = oh.astype(F32)
    before = jnp.dot(ltri_ref[...], cnt_tok.astype(BF16), preferred_element_type=F32) + carry_ref[0:1, :]
    rank = jnp.sum(jnp.where(oh, before, 0.0), axis=-1, keepdims=True)
    new_carry = carry_ref[0:1, :] + jnp.sum(cnt_tok, axis=0, keepdims=True)
    carry_ref[...] = jnp.broadcast_to(new_carry, carry_ref.shape)
    cnt_ref[...] = jnp.broadcast_to(new_carry, cnt_ref.shape)

    slab = jnp.where(lane == 0, combo, 0.0) + jnp.where(lane == 1, rank, 0.0)
    meta_ref[0] = slab.T[0:8, :]


def _mixer(x, y_ssd, y_na, u, kmem, vmem, p):
    b, t, _ = x.shape
    tm = TOKEN_TILE
    nt = t // tm
    hb = tm // BF16_SUBLANES
    nhalo = t // BF16_SUBLANES
    ltri = (jnp.arange(tm)[:, None] > jnp.arange(tm)[None, :]).astype(BF16)
    tok = lambda w: pl.BlockSpec((1, tm, w), lambda i, j: (i, j, 0))
    mem = pl.BlockSpec((1, kmem.shape[1], XA_WIDTH), lambda i, j: (i, 0, 0))
    weights = (p["pool_bd"], p["pool_scale"], p["w_out"], p["norm_xa"], p["w_xq"], p["xq_norm"], p["w_xo"],
               p["norm_ffn"], p["w_r_both"], p["b_r"], ltri)
    return pl.pallas_call(
        functools.partial(_mixer_kernel, seq=t),
        grid=(b, nt),
        in_specs=[tok(D_MODEL), tok(D_INNER), tok(NA_WIDTH), tok(POOL_WIDTH),
                  pl.BlockSpec((1, BF16_SUBLANES, POOL_WIDTH), lambda i, j: (i, jnp.maximum(j * hb - 1, 0), 0)),
                  pl.BlockSpec((1, BF16_SUBLANES, POOL_WIDTH),
                               lambda i, j: (i, jnp.minimum((j + 1) * hb, nhalo - 1), 0)),
                  mem, mem] + [_full(w.shape) for w in weights],
        out_specs=[tok(D_MODEL), tok(ROW_WORDS),
                   pl.BlockSpec((1, 8, tm), lambda i, j: (i * nt + j, 0, 0)),
                   pl.BlockSpec((8, LANES), lambda i, j: (0, 0))],
        out_shape=[jax.ShapeDtypeStruct((b, t, D_MODEL), F32),
                   jax.ShapeDtypeStruct((b, t, ROW_WORDS), jnp.int32),
                   jax.ShapeDtypeStruct((b * nt, 8, tm), F32),
                   jax.ShapeDtypeStruct((8, LANES), F32)],
        scratch_shapes=[pltpu.VMEM((8, LANES), F32)],
        compiler_params=_cparams(("arbitrary", "arbitrary")),
    )(x, y_ssd, y_na, u, u, u, kmem, vmem, *weights)


def _zero_pad_kernel(pstart_ref, cnt_ref, xs_in, zero_ref, xs_ref, sem, *, bm):
    del xs_in

    def fill(cm, wait):
        n_c = cnt_ref[cm]
        npad = (bm - n_c % bm) % bm
        head = (8 - n_c % 8) % 8
        base = pstart_ref[cm] + n_c

        def go(cp):
            if wait:
                cp.wait()
            else:
                cp.start()

        def one(r, _):
            go(pltpu.make_async_copy(zero_ref.at[pl.ds(0, 1)], xs_ref.at[pl.ds(base + r, 1)], sem))
            return 0

        lax.fori_loop(0, head, one, 0)
        rest = npad - head
        off = base + head
        size = bm // 2
        while size >= 8:
            @pl.when(rest & size != 0)
            def _(off=off, size=size):
                go(pltpu.make_async_copy(zero_ref.at[pl.ds(0, size)],
                                         xs_ref.at[pl.ds(pl.multiple_of(off, 8), size)], sem))

            off = off + (rest & size)
            size //= 2
        return 0

    lax.fori_loop(0, N_COMBOS, lambda cm, _: fill(cm, False), 0)
    lax.fori_loop(0, N_COMBOS, lambda cm, _: fill(cm, True), 0)


def _zero_pad_rows(xs, pstart, counts, bm):
    return pl.pallas_call(
        functools.partial(_zero_pad_kernel, bm=bm),
        grid_spec=pltpu.PrefetchScalarGridSpec(
            num_scalar_prefetch=2,
            grid=(1,),
            in_specs=[pl.BlockSpec(memory_space=pl.ANY),
                      pl.BlockSpec((bm // 2, ROW_WORDS), lambda i, *_: (0, 0))],
            out_specs=pl.BlockSpec(memory_space=pl.ANY),
            scratch_shapes=[pltpu.SemaphoreType.DMA(())]),
        out_shape=jax.ShapeDtypeStruct(xs.shape, xs.dtype),
        input_output_aliases={2: 0},
        compiler_params=_cparams(("arbitrary",)),
    )(pstart, counts, xs, jnp.zeros((bm // 2, ROW_WORDS), xs.dtype))


def _sc_rows(kind, src, idx, n_out):
    n = idx.shape[0]
    width = src.shape[1]
    info = plsc.get_sparse_core_info()
    workers = info.num_cores * info.num_subcores
    per_worker = n // workers
    assert per_worker * workers == n and per_worker % SC_GATHER_ROWS == 0, (n, workers)
    mesh = plsc.VectorSubcoreMesh(core_axis_name="core", subcore_axis_name="subcore")

    def body(src_hbm, idx_hbm, out_hbm, idx_v, rows_v, sem):
        base = (lax.axis_index("subcore") * info.num_cores + lax.axis_index("core")) * per_worker

        @pl.loop(0, per_worker // SC_GATHER_ROWS)
        def _(j):
            off = pl.multiple_of(base + j * SC_GATHER_ROWS, SC_GATHER_ROWS)
            pltpu.sync_copy(idx_hbm.at[pl.ds(off, SC_GATHER_ROWS)], idx_v)
            if kind == "gather":
                pltpu.async_copy(src_hbm.at[idx_v], rows_v, sem).wait()
                pltpu.sync_copy(rows_v, out_hbm.at[pl.ds(off, SC_GATHER_ROWS)])
            else:
                pltpu.sync_copy(src_hbm.at[pl.ds(off, SC_GATHER_ROWS)], rows_v)
                pltpu.async_copy(rows_v, out_hbm.at[idx_v], sem).wait()

    return pl.kernel(
        body, mesh=mesh, out_type=jax.ShapeDtypeStruct((n_out, width), src.dtype),
        scratch_types=[pltpu.VMEM((SC_GATHER_ROWS,), jnp.int32), pltpu.VMEM((SC_GATHER_ROWS, width), src.dtype),
                       pltpu.SemaphoreType.DMA])(src, idx)


def _experts_kernel(nused_ref, ea_ref, fa_ref, sa_ref, na_ref, eb_ref, fb_ref, sb_ref, nb_ref,
                    x_ref, wg_hbm, wu_hbm, wd_hbm, y_ref, wga, wua, wda, wgb, wub, wdb, sem_a, sem_b):
    i = pl.program_id(0)

    def stream(e_ref, first_ref, slot_ref, next_ref, bufs, sem):
        def copies(e, s):
            return [pltpu.make_async_copy(w.at[e], buf.at[s], sem.at[s])
                    for w, buf in zip((wg_hbm, wu_hbm, wd_hbm), bufs)]

        @pl.when(i == 0)
        def _():
            for c in copies(e_ref[0], 0):
                c.start()

        @pl.when(first_ref[i] == 1)
        def _():
            s = slot_ref[i]
            for c in copies(e_ref[i], s):
                c.wait()

            @pl.when(next_ref[i] >= 0)
            def _():
                for c in copies(next_ref[i], 1 - s):
                    c.start()

    @pl.when(i < nused_ref[0])
    def _():
        stream(ea_ref, fa_ref, sa_ref, na_ref, (wga, wua, wda), sem_a)
        stream(eb_ref, fb_ref, sb_ref, nb_ref, (wgb, wub, wdb), sem_b)
        x = _unpack_bf16_pairs(pltpu.bitcast(x_ref[:, :HALF], U32)).astype(BF16)
        gates = pltpu.bitcast(x_ref[:, HALF:], F32)

        def mlp(wg_ref, wu_ref, wd_ref, s):
            hg = jnp.dot(x, wg_ref[s], preferred_element_type=F32)
            hu = jnp.dot(x, wu_ref[s], preferred_element_type=F32)
            return jnp.dot((_silu(hg) * hu).astype(BF16), wd_ref[s], preferred_element_type=F32)

        y = mlp(wga, wua, wda, sa_ref[i]) * gates[:, 0:1] + mlp(wgb, wub, wdb, sb_ref[i]) * gates[:, 1:2]
        y_ref[...] = pltpu.bitcast(_pack_bf16_pairs(y.astype(BF16).astype(F32)), jnp.int32)


def _weight_runs(e, nused):
    nblk = e.shape[0]
    j = jnp.arange(nblk, dtype=jnp.int32)
    first = jnp.concatenate([jnp.ones((1,), bool), e[1:] != e[:-1]]) & (j < nused[0])
    slot = (jnp.cumsum(first.astype(jnp.int32)) - 1) % 2
    nxt = jnp.min(jnp.where(first[None, :] & (j[None, :] > j[:, None]), j[None, :], nblk), axis=1)
    nxt_e = jnp.where(nxt < nblk, e[jnp.minimum(nxt, nblk - 1)], -1)
    return first.astype(jnp.int32), slot.astype(jnp.int32), nxt_e.astype(jnp.int32)


def _experts(xs, blk_ea, blk_eb, nused, w_gate, w_up, w_down, bm):
    nblk = blk_ea.shape[0]
    row = lambda j, nu, *_: (jnp.minimum(j, nu[0] - 1), 0)
    row_out = lambda j, nu, *_: (jnp.where(j < nu[0], j, nblk - 1), 0)
    anyspec = pl.BlockSpec(memory_space=pl.ANY)
    up = pltpu.VMEM((2, D_MODEL, D_EXPERT), BF16)
    down = pltpu.VMEM((2, D_EXPERT, D_MODEL), BF16)
    return pl.pallas_call(
        _experts_kernel,
        grid_spec=pltpu.PrefetchScalarGridSpec(
            num_scalar_prefetch=9,
            grid=(nblk,),
            in_specs=[pl.BlockSpec((bm, ROW_WORDS), row), anyspec, anyspec, anyspec],
            out_specs=pl.BlockSpec((bm, HALF), row_out),
            scratch_shapes=[up, up, down, up, up, down, pltpu.SemaphoreType.DMA((2,)), pltpu.SemaphoreType.DMA((2,))]),
        out_shape=jax.ShapeDtypeStruct((nblk * bm, HALF), jnp.int32),
        compiler_params=_cparams(("arbitrary",)),
    )(nused, blk_ea, *_weight_runs(blk_ea, nused), blk_eb, *_weight_runs(blk_eb, nused), xs, w_gate, w_up, w_down)


def _add_rows_kernel(x_ref, yg_ref, o_ref):
    o_ref[...] = x_ref[...] + _unpack_bf16_pairs(pltpu.bitcast(yg_ref[...], U32))


def _add_inproj_kernel(x2_ref, yg_ref, g_ref, w_ref, x_ref, z_ref, xbc_ref, qkv_ref, u_ref, dt_ref):
    _add_rows_kernel(x2_ref, yg_ref, x_ref)
    _inproj_kernel(x_ref, g_ref, w_ref, z_ref, xbc_ref, qkv_ref, u_ref, dt_ref)


def _combine(x2d, yg, gain=None, w_cat=None):
    n = x2d.shape[0]
    tm = TOKEN_TILE
    tile = lambda w: pl.BlockSpec((tm, w), lambda i: (i, 0))
    if w_cat is None:
        return pl.pallas_call(
            _add_rows_kernel, grid=(n // tm,), in_specs=[tile(D_MODEL), tile(HALF)], out_specs=tile(D_MODEL),
            out_shape=jax.ShapeDtypeStruct((n, D_MODEL), F32), compiler_params=_cparams(("arbitrary",)))(x2d, yg)
    widths = (D_MODEL,) + _INPROJ_WIDTHS
    dtypes = (F32,) + _INPROJ_DTYPES
    return pl.pallas_call(
        _add_inproj_kernel, grid=(n // tm,),
        in_specs=[tile(D_MODEL), tile(HALF), _full((1, D_MODEL)), _full(w_cat.shape)],
        out_specs=[tile(w) for w in widths],
        out_shape=[jax.ShapeDtypeStruct((n, w), d) for w, d in zip(widths, dtypes)],
        compiler_params=_cparams(("arbitrary",)),
    )(x2d, yg, gain.reshape(1, D_MODEL), w_cat)


_PAIR_A = np.array([a for a in range(EXPERTS_PER_GROUP) for _ in range(a + 1, EXPERTS_PER_GROUP)], np.int32)
_PAIR_B = np.array([b for a in range(EXPERTS_PER_GROUP) for b in range(a + 1, EXPERTS_PER_GROUP)], np.int32)


def _expert_block(n):
    return 2 * EXPERT_BLOCK_MIN if n >= EXPERT_BLOCK_MIN * N_COMBOS else EXPERT_BLOCK_MIN


def _moe(x2, hf, meta, counts, w_gate, w_up, w_down):
    b, t, _ = x2.shape
    n = b * t
    bm = _expert_block(n)
    nblk = (n + N_COMBOS * (bm - 1) + bm - 1) // bm
    cnt = counts[0, :N_COMBOS].astype(jnp.int32)
    psz = (cnt + bm - 1) // bm * bm
    pend = jnp.cumsum(psz)
    pstart = (pend - psz).astype(jnp.int32)
    nused = jnp.maximum(pend[-1] // bm, 1).astype(jnp.int32).reshape(1)
    blk = jnp.minimum(jnp.arange(nblk, dtype=jnp.int32), nused[0] - 1)
    blk_c = jnp.minimum(jnp.sum(pend[None, :] <= (blk * bm)[:, None], axis=1), N_COMBOS - 1).astype(jnp.int32)
    grp = blk_c // PAIRS_PER_GROUP
    blk_ea = (grp * EXPERTS_PER_GROUP + jnp.asarray(_PAIR_A)[blk_c % PAIRS_PER_GROUP]).astype(jnp.int32)
    blk_eb = (grp * EXPERTS_PER_GROUP + jnp.asarray(_PAIR_B)[blk_c % PAIRS_PER_GROUP]).astype(jnp.int32)
    ids = meta.astype(jnp.int32)
    combo = ids[:, 0, :].reshape(n)
    dest = ids[:, 1, :].reshape(n) + jnp.sum(jnp.where(combo[:, None] == jnp.arange(N_COMBOS)[None, :], pstart[None, :], 0),
                                             axis=1)
    xs = _sc_rows("scatter", hf.reshape(n, ROW_WORDS), dest, nblk * bm)
    xs = _zero_pad_rows(xs, pstart, cnt, bm)
    y = _experts(xs, blk_ea, blk_eb, nused, w_gate, w_up, w_down, bm)
    yg = _sc_rows("gather", y, dest, n)
    return x2.reshape(n, D_MODEL), yg


def _prep_layer(lp):
    w_in = lp["w_in"]
    c0 = D_INNER + CONV_CH
    c1 = c0 + 2 * SSD_HEADS
    w_cat = jnp.concatenate([w_in[:, :c0], w_in[:, c1:], w_in[:, c0:c1],
                             jnp.zeros((D_MODEL, DT_PAD - 2 * SSD_HEADS), w_in.dtype)], axis=1).astype(BF16)
    pool_bd = jnp.zeros((POOL_WIDTH, POOL_WIDTH), F32)
    for g in range(POOL_GROUPS):
        sl = slice(g * POOL_GROUP_DIM, (g + 1) * POOL_GROUP_DIM)
        pool_bd = pool_bd.at[sl, sl].set(lp["pool_w"][g].astype(F32))
    w_r = jnp.concatenate([lp["w_router_expert"], lp["w_router_group"],
                           jnp.zeros((D_MODEL, LANES - N_EXPERTS - N_EXPERT_GROUPS), F32)], axis=1).astype(F32)
    w_r_hi = w_r.astype(BF16)
    w_r_lo = (w_r - w_r_hi.astype(F32)).astype(BF16)
    b_r = jnp.concatenate([lp["b_router_expert"], lp["b_router_group"],
                           jnp.zeros((LANES - N_EXPERTS - N_EXPERT_GROUPS,), F32)]).reshape(1, LANES).astype(F32)
    row = lambda a, w: a.reshape(1, w).astype(F32)
    return dict(
        norm_mix=lp["norm_mix"], w_cat=w_cat,
        conv_w=lp["conv_w"], conv_b=lp["conv_b"], dt_bias=lp["dt_bias"], a_log=lp["a_log"],
        d_skip=lp["d_skip"], ssd_norm=lp["ssd_norm"],
        na_q_norm=lp["na_q_norm"], na_k_norm=lp["na_k_norm"], na_rpb=lp["na_rpb"],
        pool_bd=pool_bd.astype(BF16), pool_scale=row(lp["pool_scale"], POOL_WIDTH),
        w_out=lp["w_out"].astype(BF16), norm_xa=row(lp["norm_xa"], D_MODEL),
        norm_mem=lp["norm_mem"], w_xq=lp["w_xq"].astype(BF16), w_xkv=lp["w_xkv"].astype(BF16),
        xq_norm=row(jnp.tile(lp["xq_norm"], XA_HEADS), XA_WIDTH), xk_norm=lp["xk_norm"],
        w_xo=lp["w_xo"].astype(BF16), norm_ffn=row(lp["norm_ffn"], D_MODEL),
        w_r_both=jnp.concatenate([w_r_hi, w_r_lo], axis=1), b_r=b_r,
        w_e_gate=lp["w_e_gate"].astype(BF16), w_e_up=lp["w_e_up"].astype(BF16),
        w_e_down=lp["w_e_down"].astype(BF16),
    )


def _layer(x, pending, mem, p, na_bias):
    b, m, _ = mem.shape
    if pending is None:
        t = x.shape[1]
        z, xbc, qkv, u, dt = _inproj(x.reshape(b * t, D_MODEL), p["norm_mix"], p["w_cat"])
    else:
        t = pending[0].shape[0] // b
        x, z, xbc, qkv, u, dt = _combine(*pending, p["norm_mix"], p["w_cat"])
        x = x.reshape(b, t, D_MODEL)
    r3 = lambda a: a.reshape(b, t, a.shape[-1])
    y_ssd = _ssd(r3(z), r3(xbc), r3(dt), p["conv_w"], p["conv_b"], p["dt_bias"], p["a_log"], p["d_skip"],
                 p["ssd_norm"])
    y_na = _natten(r3(qkv), na_bias, p["na_q_norm"], p["na_k_norm"])
    kmem, vmem = _memkv(mem, p["norm_mem"], p["w_xkv"], p["xk_norm"])
    x2, hf, meta, counts = _mixer(x, y_ssd, y_na, r3(u), kmem, vmem, p)
    return _moe(x2, hf, meta, counts, p["w_e_gate"], p["w_e_up"], p["w_e_down"])


_LAYER_KEYS = ("norm_mix", "w_in", "conv_w", "conv_b", "dt_bias", "a_log", "d_skip", "ssd_norm", "na_q_norm",
               "na_k_norm", "na_rpb", "pool_w", "pool_scale", "w_out", "norm_xa", "norm_mem", "w_xq", "w_xkv",
               "xq_norm", "xk_norm", "w_xo", "norm_ffn", "w_router_group", "b_router_group", "w_router_expert",
               "b_router_expert", "w_e_gate", "w_e_up", "w_e_down")


def kernel(x_prompt, x_sample, mem_prompt, mem_sample, norm_mix, w_in, conv_w, conv_b, dt_bias, a_log, d_skip, ssd_norm, na_q_norm, na_k_norm, na_rpb, pool_w, pool_scale, w_out, norm_xa, norm_mem, w_xq, w_xkv, xq_norm, xk_norm, w_xo, norm_ffn, w_router_group, b_router_group, w_router_expert, b_router_expert, w_e_gate, w_e_up, w_e_down):
    stacked = dict(zip(_LAYER_KEYS, (norm_mix, w_in, conv_w, conv_b, dt_bias, a_log, d_skip, ssd_norm, na_q_norm,
                                     na_k_norm, na_rpb, pool_w, pool_scale, w_out, norm_xa, norm_mem, w_xq, w_xkv,
                                     xq_norm, xk_norm, w_xo, norm_ffn, w_router_group, b_router_group,
                                     w_router_expert, b_router_expert, w_e_gate, w_e_up, w_e_down)))
    depth = w_in.shape[0]
    layers = [_prep_layer({k: v[l] for k, v in stacked.items()}) for l in range(depth)]

    bias_cache = {}

    def trunk(x, mem):
        t = x.shape[1]
        pending = None
        for l, lp in enumerate(layers):
            if (l, t) not in bias_cache:
                bias_cache[(l, t)] = _na_bias(lp["na_rpb"], t)
            pending = _layer(x, pending, mem, lp, bias_cache[(l, t)])
            x = None
        return _combine(*pending).reshape(mem.shape[0], t, D_MODEL)

    return trunk(x_prompt, mem_prompt), trunk(x_sample, mem_sample)
```

```python
import functools

import jax
import jax.numpy as jnp
import numpy as np
from jax import lax
from jax.experimental import pallas as pl
from jax.experimental.pallas import tpu as pltpu
from jax.experimental.pallas import tpu_sc as plsc

F32 = jnp.float32
BF16 = jnp.bfloat16
U32 = jnp.uint32
HIGHEST = lax.Precision.HIGHEST

D_MODEL = 1024
GRID_W = 64
EPS = 1e-6
SSD_HEAD_DIM = 64
D_INNER = D_MODEL // 2
SSD_HEADS = D_INNER // SSD_HEAD_DIM
SSD_GROUPS = 2
SSD_STATE = 64
SSD_CHUNK = 128
CONV_W = 4
CONV_CH = D_INNER + 2 * SSD_GROUPS * SSD_STATE
NA_HEADS = 4
NA_HEAD_DIM = D_MODEL // 16
NA_WIDTH = NA_HEADS * NA_HEAD_DIM
NA_MAX_KH = 8
NA_KW = 16
POOL_WINDOWS = (2, 4, 8, 16)
POOL_GROUPS = 4
POOL_WIDTH = D_MODEL - D_INNER - NA_WIDTH
POOL_GROUP_DIM = POOL_WIDTH // POOL_GROUPS
XA_HEADS = 4
XA_HEAD_DIM = D_MODEL // 8
XA_WIDTH = XA_HEADS * XA_HEAD_DIM
N_EXPERT_GROUPS = 4
EXPERTS_PER_GROUP = 8
N_EXPERTS = N_EXPERT_GROUPS * EXPERTS_PER_GROUP
D_EXPERT = D_MODEL // 4
PAIRS_PER_GROUP = EXPERTS_PER_GROUP * (EXPERTS_PER_GROUP - 1) // 2
N_COMBOS = N_EXPERT_GROUPS * PAIRS_PER_GROUP

LANES = 128
BF16_SUBLANES = 16
VMEM_LIMIT_BYTES = 56 * 1024 * 1024

TOKEN_TILE = 1024
NA_QUERY_ROWS = 16
NA_SUB_ROWS = 8
NA_SUBS = NA_QUERY_ROWS // NA_SUB_ROWS
NA_KEY_ROWS = NA_SUB_ROWS + NA_MAX_KH
EXPERT_BLOCK_MIN = 128
HALF = D_MODEL // 2
ROW_WORDS = HALF + LANES
SSD_STEP_CHUNKS = 8
DT_PAD = LANES
SC_GATHER_ROWS = 128
NEG_BIG = -1e30


def _cparams(sem):
    return pltpu.CompilerParams(dimension_semantics=sem, vmem_limit_bytes=VMEM_LIMIT_BYTES)


def _sigmoid(x):
    return 1.0 / (1.0 + jnp.exp(-x))


def _silu(x):
    return x * _sigmoid(x)


def _softplus(x):
    return jnp.maximum(x, 0.0) + jnp.log(1.0 + jnp.exp(-jnp.abs(x)))


def _pack_bf16_pairs(v):
    k = v.shape[1] // 2
    bits = pltpu.bitcast(v, U32)
    return (bits[:, :k] >> 16) | (bits[:, k:] & jnp.uint32(0xFFFF0000))


def _unpack_bf16_pairs(w):
    lo = pltpu.bitcast(w << 16, F32)
    hi = pltpu.bitcast(w & jnp.uint32(0xFFFF0000), F32)
    return jnp.concatenate([lo, hi], axis=1)


def _full(shape):
    n = len(shape)
    return pl.BlockSpec(shape, lambda *_: (0,) * n)


def _inproj_kernel(x_ref, g_ref, w_ref, z_ref, xbc_ref, qkv_ref, u_ref, dt_ref):
    x = x_ref[...]
    ms = jnp.mean(x * x, axis=-1, keepdims=True)
    h = (x * lax.rsqrt(ms + EPS) * g_ref[...]).astype(BF16)
    o = 0
    for ref in (z_ref, xbc_ref, qkv_ref, u_ref, dt_ref):
        w = ref.shape[-1]
        ref[...] = jnp.dot(h, w_ref[:, o:o + w], preferred_element_type=F32).astype(ref.dtype)
        o += w


_INPROJ_WIDTHS = (D_INNER, CONV_CH, 3 * NA_WIDTH, POOL_WIDTH, DT_PAD)
_INPROJ_DTYPES = (BF16, BF16, BF16, BF16, F32)


def _inproj(x2d, gain, w_cat):
    n = x2d.shape[0]
    tm = TOKEN_TILE
    widths, dtypes = _INPROJ_WIDTHS, _INPROJ_DTYPES
    return pl.pallas_call(
        _inproj_kernel,
        grid=(n // tm,),
        in_specs=[pl.BlockSpec((tm, D_MODEL), lambda i: (i, 0)),
                  _full((1, D_MODEL)),
                  _full(w_cat.shape)],
        out_specs=[pl.BlockSpec((tm, w), lambda i: (i, 0)) for w in widths],
        out_shape=[jax.ShapeDtypeStruct((n, w), d) for w, d in zip(widths, dtypes)],
        compiler_params=_cparams(("arbitrary",)),
    )(x2d, gain.reshape(1, D_MODEL), w_cat)


def _split3(a):
    hi = a.astype(BF16)
    r = a - hi.astype(F32)
    mid = r.astype(BF16)
    lo = (r - mid.astype(F32)).astype(BF16)
    return hi, mid, lo


def _ssd_kernel(xc_ref, xp_ref, xn_ref, dt_ref, z_ref, cw_ref, cb_ref, dtb_ref, alog_ref, dsk_ref, nrm_ref, emat_ref,
                y_ref, state_ref, yf_ref, xs_c, bc_c, cbm_c, dt_c, *, nblocks):
    L = SSD_CHUNK
    LB = SSD_STEP_CHUNKS * L
    P = SSD_HEAD_DIM
    NS = SSD_STATE
    HG = SSD_HEADS // SSD_GROUPS
    gn = SSD_GROUPS * NS
    j = pl.program_id(1)
    c = jnp.where(j < nblocks, j, 2 * nblocks - 1 - j)
    row0 = pl.multiple_of(c * LB, LB)
    rows = pl.ds(row0, LB)
    lane1 = lax.broadcasted_iota(jnp.int32, (1, LANES), 1)
    lo_half = lane1 < P
    ti = lax.broadcasted_iota(jnp.int32, (L, L), 0)
    si = lax.broadcasted_iota(jnp.int32, (L, L), 1)

    def masked_c(bc):
        return [jnp.where(lane1 // NS == g, bc[:, gn:], 0.0).astype(BF16) for g in range(SSD_GROUPS)]

    def prepare():
        cur = xc_ref[0].astype(F32)
        prev = xp_ref[0].astype(F32)
        nxt = xn_ref[0].astype(F32)
        has_prev = (c > 0).astype(F32)
        has_next = (c < nblocks - 1).astype(F32)
        p_last = prev[BF16_SUBLANES - 1:BF16_SUBLANES, :] * has_prev
        n0 = nxt[0:1, :] * has_next
        n1 = nxt[1:2, :] * has_next
        row = lax.broadcasted_iota(jnp.int32, (LB, 1), 0)
        um1 = jnp.where(row == 0, p_last, pltpu.roll(cur, 1, 0))
        up1 = jnp.where(row == LB - 1, n0, pltpu.roll(cur, LB - 1, 0))
        up2 = jnp.where(row == LB - 2, n0, jnp.where(row == LB - 1, n1, pltpu.roll(cur, LB - 2, 0)))
        cw = cw_ref[...]
        acc = cb_ref[...] + um1 * cw[0:1, :] + cur * cw[1:2, :] + up1 * cw[2:3, :] + up2 * cw[3:4, :]
        xbc = _silu(acc)
        xs = xbc[:, :D_INNER]
        bc = xbc[:, D_INNER:D_INNER + 2 * gn].astype(BF16)
        dt = _softplus(dt_ref[0] + dtb_ref[...])
        xs_c[rows, :] = xs
        bc_c[rows, :] = bc
        dt_c[rows, :] = dt
        ops = []
        for sub in range(SSD_STEP_CHUNKS):
            sl = slice(sub * L, (sub + 1) * L)
            cg = masked_c(bc[sl])
            cb_mat = [lax.dot_general(cg[g], bc[sl, :gn], (((1,), (1,)), ((), ())), preferred_element_type=F32)
                      for g in range(SSD_GROUPS)]
            cbm_c[pl.ds(row0 + sub * L, L), :] = jnp.concatenate(cb_mat, axis=1)
            ops.append((xs[sl], bc[sl, :gn], cg, cb_mat, dt[sl]))
        return ops

    def recall():
        ops = []
        for sub in range(SSD_STEP_CHUNKS):
            r = pl.ds(row0 + sub * L, L)
            bc = bc_c[r, :]
            cbm = cbm_c[r, :]
            ops.append((xs_c[r, :], bc[:, :gn], masked_c(bc), [cbm[:, g * L:(g + 1) * L] for g in range(SSD_GROUPS)],
                        dt_c[r, :]))
        return ops

    def scan_chunk(direction, xs, bfull, cg, cb_mat, dt):
        if direction == 0:
            mask = ti >= si
            edge = L - 1
        else:
            mask = si >= ti
            edge = 0
        la = dt * (-jnp.exp(alog_ref[...]))
        tri = mask.astype(BF16)
        csum = sum(jnp.dot(tri, part, preferred_element_type=F32) for part in _split3(la))
        csum_t = csum.T
        emat = emat_ref[direction]
        colb = sum(jnp.dot(part, emat, preferred_element_type=F32) for part in _split3(csum))
        tot = csum[edge:edge + 1, :]
        e_tot = jnp.exp(tot)
        e_in = jnp.exp(csum)
        e_out = jnp.exp(tot - csum)
        ys = []
        for g in range(SSD_GROUPS):
            s_old = state_ref[g]
            y_off = lax.dot_general(cg[g], s_old.astype(BF16), (((1,), (1,)), ((), ())),
                                    preferred_element_type=F32)
            xw = []
            for pr in range(HG // 2):
                h0 = g * HG + 2 * pr
                l0 = direction * SSD_HEADS + h0

                def col(a, l0=l0):
                    return jnp.where(lo_half, a[:, l0:l0 + 1], a[:, l0 + 1:l0 + 2])

                xdt = xs[:, h0 * P:(h0 + 2) * P] * col(dt)
                y_pair = y_off[:, 2 * pr * P:(2 * pr + 2) * P] * col(e_in)
                for hh, half in ((h0, lo_half), (h0 + 1, jnp.logical_not(lo_half))):
                    ll = direction * SSD_HEADS + hh
                    seg = colb[:, hh * L:(hh + 1) * L] - csum_t[ll:ll + 1, :]
                    dec = jnp.exp(jnp.where(mask, seg, NEG_BIG))
                    m = (cb_mat[g] * dec).astype(BF16)
                    y_pair += jnp.dot(m, jnp.where(half, xdt, 0.0).astype(BF16), preferred_element_type=F32)
                ys.append(y_pair)
                xw.append(xdt * col(e_out))
            xw = jnp.concatenate(xw, axis=1).astype(BF16)
            s_new = lax.dot_general(xw, bfull, (((0,), (0,)), ((), ())), preferred_element_type=F32)
            s_scaled = []
            for hl in range(HG):
                lane = direction * SSD_HEADS + g * HG + hl
                s_scaled.append(s_old[hl * P:(hl + 1) * P, :] * e_tot[:, lane:lane + 1])
            state_ref[g] = jnp.concatenate(s_scaled, axis=0) + s_new
        return jnp.concatenate(ys, axis=1)

    @pl.when(jnp.logical_or(j == 0, j == nblocks))
    def _():
        state_ref[...] = jnp.zeros_like(state_ref)

    @pl.when(j < nblocks)
    def _():
        ops = prepare()
        for sub in range(SSD_STEP_CHUNKS):
            yf_ref[pl.ds(row0 + sub * L, L), :] = scan_chunk(0, *ops[sub])

    @pl.when(j >= nblocks)
    def _():
        ops = recall()
        for sub in reversed(range(SSD_STEP_CHUNKS)):
            sl = slice(sub * L, (sub + 1) * L)
            y = yf_ref[pl.ds(row0 + sub * L, L), :] + scan_chunk(1, *ops[sub]) + dsk_ref[...] * ops[sub][0]
            y = y * _silu(z_ref[0, sl, :].astype(F32))
            gw = D_INNER // SSD_GROUPS
            outs = []
            for g in range(SSD_GROUPS):
                yg = y[:, g * gw:(g + 1) * gw]
                outs.append(yg * lax.rsqrt(jnp.mean(yg * yg, axis=-1, keepdims=True) + EPS))
            y_ref[0, sl, :] = (jnp.concatenate(outs, axis=1) * nrm_ref[...]).astype(y_ref.dtype)


def _ssd(z, xbc, dt, conv_w, conv_b, dt_bias, a_log, d_skip, ssd_norm):
    b, t, _ = z.shape
    L = SSD_CHUNK
    lb = SSD_STEP_CHUNKS * L
    nb = t // lb
    hb = lb // BF16_SUBLANES
    nhalo = t // BF16_SUBLANES

    def blk(j):
        return jnp.where(j < nb, j, 2 * nb - 1 - j)

    pad = DT_PAD - 2 * SSD_HEADS
    dtb = jnp.pad(dt_bias.reshape(1, -1).astype(F32), ((0, 0), (0, pad)))
    alog = jnp.pad(a_log.reshape(1, -1).astype(F32), ((0, 0), (0, pad)))
    dsk = jnp.repeat(d_skip.astype(F32), SSD_HEAD_DIM).reshape(1, D_INNER)
    sel = np.arange(DT_PAD)[None, :, None] == (np.arange(2)[:, None, None] * SSD_HEADS
                                                + np.arange(SSD_HEADS)[None, None, :])
    emat = jnp.asarray(np.repeat(sel, L, axis=2), BF16)
    return pl.pallas_call(
        functools.partial(_ssd_kernel, nblocks=nb),
        grid=(b, 2 * nb),
        in_specs=[
            pl.BlockSpec((1, lb, CONV_CH), lambda i, j: (i, blk(j), 0)),
            pl.BlockSpec((1, BF16_SUBLANES, CONV_CH), lambda i, j: (i, jnp.maximum(blk(j) * hb - 1, 0), 0)),
            pl.BlockSpec((1, BF16_SUBLANES, CONV_CH),
                         lambda i, j: (i, jnp.minimum((blk(j) + 1) * hb, nhalo - 1), 0)),
            pl.BlockSpec((1, lb, DT_PAD), lambda i, j: (i, blk(j), 0)),
            pl.BlockSpec((1, lb, D_INNER), lambda i, j: (i, blk(j), 0)),
            _full((CONV_W, CONV_CH)), _full((1, CONV_CH)), _full((1, DT_PAD)), _full((1, DT_PAD)),
            _full((1, D_INNER)), _full((1, D_INNER)), _full((2, DT_PAD, SSD_HEADS * L)),
        ],
        out_specs=pl.BlockSpec((1, lb, D_INNER), lambda i, j: (i, jnp.where(j < nb, nb - 1, 2 * nb - 1 - j), 0)),
        out_shape=jax.ShapeDtypeStruct((b, t, D_INNER), BF16),
        scratch_shapes=[pltpu.VMEM((SSD_GROUPS, (SSD_HEADS // SSD_GROUPS) * SSD_HEAD_DIM, LANES), F32),
                        pltpu.VMEM((t, D_INNER), F32),
                        pltpu.VMEM((t, D_INNER), F32),
                        pltpu.VMEM((t, 2 * SSD_GROUPS * SSD_STATE), BF16),
                        pltpu.VMEM((t, SSD_GROUPS * L), F32),
                        pltpu.VMEM((t, DT_PAD), F32)],
        compiler_params=_cparams(("arbitrary", "arbitrary")),
    )(xbc, xbc, xbc, dt, z, conv_w.astype(F32), conv_b.reshape(1, CONV_CH).astype(F32), dtb, alog, dsk,
      ssd_norm.reshape(1, D_INNER).astype(F32), emat)


def _na_bias(rpb, t):
    r = t // GRID_W
    kh = min(NA_MAX_KH, r)
    nsb = r // NA_SUB_ROWS
    rows = np.arange(r)
    row_start = np.clip(rows - NA_MAX_KH // 2, 0, r - kh)
    r0 = np.arange(nsb) * NA_SUB_ROWS
    kr0 = np.clip(r0 - NA_MAX_KH // 2, 0, r - NA_KEY_ROWS)
    qrow = r0[:, None] + np.arange(NA_SUB_ROWS)[None, :]
    krow = kr0[:, None] + np.arange(NA_KEY_ROWS)[None, :]
    rs = row_start[qrow]
    row_ok = (krow[:, None, :] >= rs[:, :, None]) & (krow[:, None, :] < rs[:, :, None] + kh)
    dr = np.clip(krow[:, None, :] - qrow[:, :, None] + (NA_MAX_KH - 1), 0, 2 * NA_MAX_KH - 2)
    cols = np.arange(GRID_W)
    col_start = np.clip(cols - NA_KW // 2, 0, GRID_W - NA_KW)
    col_ok = (cols[None, :] >= col_start[:, None]) & (cols[None, :] < col_start[:, None] + NA_KW)
    dc = np.clip(cols[None, :] - cols[:, None] + (NA_KW - 1), 0, 2 * NA_KW - 2)
    sel_c = (dc[..., None] == np.arange(2 * NA_KW - 1)) & col_ok[..., None]
    nr = 2 * NA_MAX_KH - 1
    tile_r = jnp.einsum("hrc,xyc->hrxy", rpb.astype(F32), jnp.asarray(sel_c, F32), precision=HIGHEST)
    tile_r = jnp.where(jnp.asarray(col_ok)[None, None], tile_r, NEG_BIG)
    tile_r = jnp.concatenate([tile_r, jnp.full((NA_HEADS, 1, GRID_W, GRID_W), NEG_BIG, F32)], axis=1)
    kp = NA_KEY_ROWS // 2
    code = np.where(row_ok, dr, nr).reshape(nsb * NA_SUB_ROWS * kp, 2)
    pairs, inv = np.unique(code, axis=0, return_inverse=True)
    blocks = jnp.concatenate([tile_r[:, pairs[:, 0]], tile_r[:, pairs[:, 1]]], axis=-1)
    blocks = jnp.moveaxis(blocks, 0, 1).reshape(len(pairs), NA_HEADS * GRID_W * 2 * GRID_W)
    onehot = jnp.asarray(inv.reshape(-1, 1) == np.arange(len(pairs))[None, :], F32)
    bias = jnp.dot(onehot, blocks, precision=HIGHEST)
    return bias.reshape(nsb // NA_SUBS, NA_SUBS, NA_SUB_ROWS, kp, NA_HEADS, GRID_W, 2 * GRID_W).astype(BF16)


def _natten_kernel(qkv_ref, bias_ref, qg_ref, kg_ref, seg_ref, o_ref, *, grid_rows):
    nq = NA_SUB_ROWS * GRID_W
    nk = NA_KEY_ROWS * GRID_W
    rb = pl.program_id(0)
    seg = seg_ref[...]
    lane_h = lax.broadcasted_iota(jnp.int32, (1, NA_WIDTH), 1) // NA_HEAD_DIM
    for sub in range(NA_SUBS):
        r0 = rb * NA_QUERY_ROWS + sub * NA_SUB_ROWS
        kr0 = jnp.clip(r0 - NA_MAX_KH // 2, 0, grid_rows - NA_KEY_ROWS)
        q0 = pl.multiple_of(r0 * GRID_W, nq)
        k0 = pl.multiple_of(kr0 * GRID_W, NA_MAX_KH // 2 * GRID_W)
        q = qkv_ref[0, pl.ds(q0, nq), 0:NA_WIDTH].astype(F32)
        k = qkv_ref[0, pl.ds(k0, nk), NA_WIDTH:2 * NA_WIDTH].astype(F32)
        v = qkv_ref[0, pl.ds(k0, nk), 2 * NA_WIDTH:3 * NA_WIDTH]
        qms = sum(jnp.dot(part, seg, preferred_element_type=F32) for part in _split3(q * q)[:2])
        kms = sum(jnp.dot(part, seg, preferred_element_type=F32) for part in _split3(k * k)[:2])
        qn = q * lax.rsqrt(qms + EPS) * (qg_ref[...] * NA_HEAD_DIM ** -0.5)
        kn = (k * lax.rsqrt(kms + EPS) * kg_ref[...]).astype(BF16)
        acc = jnp.zeros((nq, NA_WIDTH), F32)
        for h in range(NA_HEADS):
            hm = lane_h == h
            s = lax.dot_general(jnp.where(hm, qn, 0.0).astype(BF16), kn, (((1,), (1,)), ((), ())),
                                preferred_element_type=F32)
            bias = jnp.concatenate(
                [jnp.concatenate([bias_ref[0, sub, qr, kc, h] for kc in range(NA_KEY_ROWS // 2)], axis=1)
                 for qr in range(NA_SUB_ROWS)], axis=0)
            s = s + bias.astype(F32)
            p = jnp.exp(s - jnp.max(s, axis=-1, keepdims=True))
            l = jnp.sum(p, axis=-1, keepdims=True)
            o = jnp.dot(p.astype(BF16), v, preferred_element_type=F32)
            acc += jnp.where(hm, o / l, 0.0)
        o_ref[0, sub * nq:(sub + 1) * nq, :] = acc.astype(o_ref.dtype)


def _natten(qkv, bias, q_norm, k_norm):
    b, t, _ = qkv.shape
    r = t // GRID_W
    nrb = r // NA_QUERY_ROWS
    nq = NA_QUERY_ROWS * GRID_W
    head = jnp.arange(NA_WIDTH) // NA_HEAD_DIM
    seg = ((head[:, None] == head[None, :]).astype(F32) / NA_HEAD_DIM).astype(BF16)
    return pl.pallas_call(
        functools.partial(_natten_kernel, grid_rows=r),
        grid=(nrb, b),
        in_specs=[pl.BlockSpec((1, t, 3 * NA_WIDTH), lambda i, j: (j, 0, 0)),
                  pl.BlockSpec((1,) + bias.shape[1:], lambda i, j: (i,) + (0,) * (bias.ndim - 1)),
                  _full((1, NA_WIDTH)), _full((1, NA_WIDTH)), _full((NA_WIDTH, NA_WIDTH))],
        out_specs=pl.BlockSpec((1, nq, NA_WIDTH), lambda i, j: (j, i, 0)),
        out_shape=jax.ShapeDtypeStruct((b, t, NA_WIDTH), BF16),
        compiler_params=_cparams(("arbitrary", "arbitrary")),
    )(qkv, bias, jnp.tile(q_norm.astype(F32), NA_HEADS).reshape(1, NA_WIDTH),
      jnp.tile(k_norm.astype(F32), NA_HEADS).reshape(1, NA_WIDTH), seg)


def _memkv_kernel(m_ref, g_ref, w_ref, kg_ref, k_ref, v_ref):
    x = m_ref[0]
    ms = jnp.mean(x * x, axis=-1, keepdims=True)
    h = (x * lax.rsqrt(ms + EPS) * g_ref[...]).astype(BF16)
    kv = jnp.dot(h, w_ref[...], preferred_element_type=F32)
    ks = []
    for hd in range(XA_HEADS):
        kh = kv[:, hd * XA_HEAD_DIM:(hd + 1) * XA_HEAD_DIM]
        ks.append(kh * lax.rsqrt(jnp.mean(kh * kh, axis=-1, keepdims=True) + EPS))
    k_ref[0] = (jnp.concatenate(ks, axis=1) * kg_ref[...]).astype(k_ref.dtype)
    v_ref[0] = kv[:, XA_WIDTH:].astype(v_ref.dtype)


def _memkv(mem, norm_mem, w_xkv, xk_norm):
    b, m, _ = mem.shape
    return pl.pallas_call(
        _memkv_kernel,
        grid=(b,),
        in_specs=[pl.BlockSpec((1, m, D_MODEL), lambda i: (i, 0, 0)),
                  _full((1, D_MODEL)), _full((D_MODEL, 2 * XA_WIDTH)), _full((1, XA_WIDTH))],
        out_specs=[pl.BlockSpec((1, m, XA_WIDTH), lambda i: (i, 0, 0))] * 2,
        out_shape=[jax.ShapeDtypeStruct((b, m, XA_WIDTH), BF16)] * 2,
        compiler_params=_cparams(("arbitrary",)),
    )(mem, norm_mem.reshape(1, D_MODEL).astype(F32), w_xkv,
      jnp.tile(xk_norm.astype(F32), XA_HEADS).reshape(1, XA_WIDTH))


def _mixer_kernel(x_ref, ys_ref, yn_ref, u_ref, up_ref, un_ref, k_ref, v_ref,
                  pw_ref, psc_ref, wo_ref, gxa_ref, wq_ref, qg_ref, wxo_ref, gff_ref,
                  wrb_ref, br_ref, ltri_ref,
                  x2_ref, hf_ref, meta_ref, cnt_ref, carry_ref, *, seq):
    tm = x_ref.shape[1]
    halo = BF16_SUBLANES
    bi = pl.program_id(0)
    i = pl.program_id(1)
    nt = pl.num_programs(1)

    @pl.when(jnp.logical_and(bi == 0, i == 0))
    def _():
        carry_ref[...] = jnp.zeros_like(carry_ref)

    u = u_ref[0].astype(F32)
    up = up_ref[0].astype(F32) * (i > 0).astype(F32)
    un = un_ref[0].astype(F32) * (i < nt - 1).astype(F32)
    cat = jnp.concatenate([up, u, un], axis=0)
    n = tm + 2 * halo

    def sh(a, k):
        return pltpu.roll(a, (-k) % n, 0)

    a2 = cat + sh(cat, -1)
    a4 = sh(a2, 1) + sh(a2, -1)
    a8 = sh(a4, 2) + sh(a4, -2)
    a16 = sh(a8, 4) + sh(a8, -4)
    lane_g = lax.broadcasted_iota(jnp.int32, (1, POOL_WIDTH), 1) // POOL_GROUP_DIM
    wsum = jnp.where(lane_g == 0, a2, jnp.where(lane_g == 1, a4, jnp.where(lane_g == 2, a8, a16)))
    wsum = wsum[halo:halo + tm, :]
    half = jnp.where(lane_g == 0, POOL_WINDOWS[0] // 2,
                     jnp.where(lane_g == 1, POOL_WINDOWS[1] // 2,
                               jnp.where(lane_g == 2, POOL_WINDOWS[2] // 2, POOL_WINDOWS[3] // 2)))
    tpos = i * tm + lax.broadcasted_iota(jnp.int32, (tm, 1), 0)
    cnt = (jnp.minimum(tpos + half, seq) - jnp.maximum(tpos - half, 0)).astype(F32)
    d = wsum / cnt - u
    ypool = jnp.dot(d.astype(BF16), pw_ref[...], preferred_element_type=F32) * psc_ref[...]

    mix = jnp.dot(ys_ref[0], wo_ref[0:D_INNER, :], preferred_element_type=F32)
    mix += jnp.dot(yn_ref[0], wo_ref[D_INNER:D_INNER + NA_WIDTH, :], preferred_element_type=F32)
    mix += jnp.dot(ypool.astype(BF16), wo_ref[D_INNER + NA_WIDTH:, :], preferred_element_type=F32)
    x1 = x_ref[0] + mix

    hn = (x1 * lax.rsqrt(jnp.mean(x1 * x1, axis=-1, keepdims=True) + EPS) * gxa_ref[...]).astype(BF16)
    q = jnp.dot(hn, wq_ref[...], preferred_element_type=F32)
    kk = k_ref[0]
    vv = v_ref[0]
    scale = XA_HEAD_DIM ** -0.5
    outs = []
    for hd in range(XA_HEADS):
        sl = slice(hd * XA_HEAD_DIM, (hd + 1) * XA_HEAD_DIM)
        qh = q[:, sl]
        qh = (qh * lax.rsqrt(jnp.mean(qh * qh, axis=-1, keepdims=True) + EPS) * qg_ref[:, sl]).astype(BF16)
        s = lax.dot_general(qh, kk[:, sl], (((1,), (1,)), ((), ())), preferred_element_type=F32) * scale
        p = jnp.exp(s - jnp.max(s, axis=-1, keepdims=True))
        l = jnp.sum(p, axis=-1, keepdims=True)
        outs.append(jnp.dot(p.astype(BF16), vv[:, sl], preferred_element_type=F32) / l)
    att = jnp.concatenate(outs, axis=1).astype(BF16)
    x2 = x1 + jnp.dot(att, wxo_ref[...], preferred_element_type=F32)
    x2_ref[0] = x2

    hf = x2 * lax.rsqrt(jnp.mean(x2 * x2, axis=-1, keepdims=True) + EPS) * gff_ref[...]
    h_hi = hf.astype(BF16)
    hf_ref[0, :, :HALF] = pltpu.bitcast(_pack_bf16_pairs(h_hi.astype(F32)), jnp.int32)
    h_lo = (hf - h_hi.astype(F32)).astype(BF16)
    both = jnp.dot(h_hi, wrb_ref[...], preferred_element_type=F32)
    logits = (both[:, :LANES] + both[:, LANES:]
              + jnp.dot(h_lo, wrb_ref[:, :LANES], preferred_element_type=F32)) + br_ref[...]
    lane = lax.broadcasted_iota(jnp.int32, (1, LANES), 1)
    lane_f = lane.astype(F32)
    lane_grp = (lane // EXPERTS_PER_GROUP).astype(F32)
    is_g = jnp.logical_and(lane >= N_EXPERTS, lane < N_EXPERTS + N_EXPERT_GROUPS)
    gl = jnp.where(is_g, logits, NEG_BIG)
    gmax = jnp.max(gl, axis=-1, keepdims=True)
    g_sel = jnp.min(jnp.where(gl == gmax, lane_f, float(LANES)), axis=-1, keepdims=True) - N_EXPERTS
    g_gate = 1.0 / jnp.sum(jnp.where(is_g, jnp.exp(gl - gmax), 0.0), axis=-1, keepdims=True)
    in_grp = jnp.logical_and(lane < N_EXPERTS, lane_grp == g_sel)
    el = jnp.where(in_grp, logits, NEG_BIG)
    v1 = jnp.max(el, axis=-1, keepdims=True)
    e0 = jnp.min(jnp.where(el == v1, lane_f, float(LANES)), axis=-1, keepdims=True)
    el2 = jnp.where(lane_f == e0, NEG_BIG, el)
    v2 = jnp.max(el2, axis=-1, keepdims=True)
    e1 = jnp.min(jnp.where(el2 == v2, lane_f, float(LANES)), axis=-1, keepdims=True)
    w1 = jnp.exp(v2 - v1)
    gate0 = g_gate / (1.0 + w1)
    gate1 = g_gate * w1 / (1.0 + w1)

    base = g_sel * EXPERTS_PER_GROUP
    ea = jnp.minimum(e0, e1) - base
    eb = jnp.maximum(e0, e1) - base
    combo = g_sel * PAIRS_PER_GROUP + ea * EXPERTS_PER_GROUP - ea * (ea + 1.0) * 0.5 + (eb - ea - 1.0)
    gate_a = jnp.where(e0 < e1, gate0, gate1)
    gate_b = jnp.where(e0 < e1, gate1, gate0)
    hf_ref[0, :, HALF:] = pltpu.bitcast(jnp.where(lane == 0, gate_a, 0.0) + jnp.where(lane == 1, gate_b, 0.0), jnp.int32)

    oh = lane_f == combo
    cnt_tok = oh.astype(F32)
    before = jnp.dot(ltri_ref[...], cnt_tok.astype(BF16), preferred_element_type=F32) + carry_ref[0:1, :]
    rank = jnp.sum(jnp.where(oh, before, 0.0), axis=-1, keepdims=True)
    new_carry = carry_ref[0:1, :] + jnp.sum(cnt_tok, axis=0, keepdims=True)
    carry_ref[...] = jnp.broadcast_to(new_carry, carry_ref.shape)
    cnt_ref[...] = jnp.broadcast_to(new_carry, cnt_ref.shape)

    slab = jnp.where(lane == 0, combo, 0.0) + jnp.where(lane == 1, rank, 0.0)
    meta_ref[0] = slab.T[0:8, :]


def _mixer(x, y_ssd, y_na, u, kmem, vmem, p):
    b, t, _ = x.shape
    tm = TOKEN_TILE
    nt = t // tm
    hb = tm // BF16_SUBLANES
    nhalo = t // BF16_SUBLANES
    ltri = (jnp.arange(tm)[:, None] > jnp.arange(tm)[None, :]).astype(BF16)
    tok = lambda w: pl.BlockSpec((1, tm, w), lambda i, j: (i, j, 0))
    mem = pl.BlockSpec((1, kmem.shape[1], XA_WIDTH), lambda i, j: (i, 0, 0))
    weights = (p["pool_bd"], p["pool_scale"], p["w_out"], p["norm_xa"], p["w_xq"], p["xq_norm"], p["w_xo"],
               p["norm_ffn"], p["w_r_both"], p["b_r"], ltri)
    return pl.pallas_call(
        functools.partial(_mixer_kernel, seq=t),
        grid=(b, nt),
        in_specs=[tok(D_MODEL), tok(D_INNER), tok(NA_WIDTH), tok(POOL_WIDTH),
                  pl.BlockSpec((1, BF16_SUBLANES, POOL_WIDTH), lambda i, j: (i, jnp.maximum(j * hb - 1, 0), 0)),
                  pl.BlockSpec((1, BF16_SUBLANES, POOL_WIDTH),
                               lambda i, j: (i, jnp.minimum((j + 1) * hb, nhalo - 1), 0)),
                  mem, mem] + [_full(w.shape) for w in weights],
        out_specs=[tok(D_MODEL), tok(ROW_WORDS),
                   pl.BlockSpec((1, 8, tm), lambda i, j: (i * nt + j, 0, 0)),
                   pl.BlockSpec((8, LANES), lambda i, j: (0, 0))],
        out_shape=[jax.ShapeDtypeStruct((b, t, D_MODEL), F32),
                   jax.ShapeDtypeStruct((b, t, ROW_WORDS), jnp.int32),
                   jax.ShapeDtypeStruct((b * nt, 8, tm), F32),
                   jax.ShapeDtypeStruct((8, LANES), F32)],
        scratch_shapes=[pltpu.VMEM((8, LANES), F32)],
        compiler_params=_cparams(("arbitrary", "arbitrary")),
    )(x, y_ssd, y_na, u, u, u, kmem, vmem, *weights)


def _zero_pad_kernel(pstart_ref, cnt_ref, xs_in, zero_ref, xs_ref, sem, *, bm):
    del xs_in

    def fill(cm, wait):
        n_c = cnt_ref[cm]
        npad = (bm - n_c % bm) % bm
        head = (8 - n_c % 8) % 8
        base = pstart_ref[cm] + n_c

        def go(cp):
            if wait:
                cp.wait()
            else:
                cp.start()

        def one(r, _):
            go(pltpu.make_async_copy(zero_ref.at[pl.ds(0, 1)], xs_ref.at[pl.ds(base + r, 1)], sem))
            return 0

        lax.fori_loop(0, head, one, 0)
        rest = npad - head
        off = base + head
        size = bm // 2
        while size >= 8:
            @pl.when(rest & size != 0)
            def _(off=off, size=size):
                go(pltpu.make_async_copy(zero_ref.at[pl.ds(0, size)],
                                         xs_ref.at[pl.ds(pl.multiple_of(off, 8), size)], sem))

            off = off + (rest & size)
            size //= 2
        return 0

    lax.fori_loop(0, N_COMBOS, lambda cm, _: fill(cm, False), 0)
    lax.fori_loop(0, N_COMBOS, lambda cm, _: fill(cm, True), 0)


def _zero_pad_rows(xs, pstart, counts, bm):
    return pl.pallas_call(
        functools.partial(_zero_pad_kernel, bm=bm),
        grid_spec=pltpu.PrefetchScalarGridSpec(
            num_scalar_prefetch=2,
            grid=(1,),
            in_specs=[pl.BlockSpec(memory_space=pl.ANY),
                      pl.BlockSpec((bm // 2, ROW_WORDS), lambda i, *_: (0, 0))],
            out_specs=pl.BlockSpec(memory_space=pl.ANY),
            scratch_shapes=[pltpu.SemaphoreType.DMA(())]),
        out_shape=jax.ShapeDtypeStruct(xs.shape, xs.dtype),
        input_output_aliases={2: 0},
        compiler_params=_cparams(("arbitrary",)),
    )(pstart, counts, xs, jnp.zeros((bm // 2, ROW_WORDS), xs.dtype))


def _sc_rows(kind, src, idx, n_out):
    n = idx.shape[0]
    width = src.shape[1]
    info = plsc.get_sparse_core_info()
    workers = info.num_cores * info.num_subcores
    per_worker = n // workers
    assert per_worker * workers == n and per_worker % SC_GATHER_ROWS == 0, (n, workers)
    mesh = plsc.VectorSubcoreMesh(core_axis_name="core", subcore_axis_name="subcore")

    def body(src_hbm, idx_hbm, out_hbm, idx_v, rows_v, sem):
        base = (lax.axis_index("subcore") * info.num_cores + lax.axis_index("core")) * per_worker

        @pl.loop(0, per_worker // SC_GATHER_ROWS)
        def _(j):
            off = pl.multiple_of(base + j * SC_GATHER_ROWS, SC_GATHER_ROWS)
            pltpu.sync_copy(idx_hbm.at[pl.ds(off, SC_GATHER_ROWS)], idx_v)
            if kind == "gather":
                pltpu.async_copy(src_hbm.at[idx_v], rows_v, sem).wait()
                pltpu.sync_copy(rows_v, out_hbm.at[pl.ds(off, SC_GATHER_ROWS)])
            else:
                pltpu.sync_copy(src_hbm.at[pl.ds(off, SC_GATHER_ROWS)], rows_v)
                pltpu.async_copy(rows_v, out_hbm.at[idx_v], sem).wait()

    return pl.kernel(
        body, mesh=mesh, out_type=jax.ShapeDtypeStruct((n_out, width), src.dtype),
        scratch_types=[pltpu.VMEM((SC_GATHER_ROWS,), jnp.int32), pltpu.VMEM((SC_GATHER_ROWS, width), src.dtype),
                       pltpu.SemaphoreType.DMA])(src, idx)


def _experts_kernel(nused_ref, ea_ref, fa_ref, sa_ref, na_ref, eb_ref, fb_ref, sb_ref, nb_ref,
                    x_ref, wg_hbm, wu_hbm, wd_hbm, y_ref, wga, wua, wda, wgb, wub, wdb, sem_a, sem_b):
    i = pl.program_id(0)

    def stream(e_ref, first_ref, slot_ref, next_ref, bufs, sem):
        def copies(e, s):
            return [pltpu.make_async_copy(w.at[e], buf.at[s], sem.at[s])
                    for w, buf in zip((wg_hbm, wu_hbm, wd_hbm), bufs)]

        @pl.when(i == 0)
        def _():
            for c in copies(e_ref[0], 0):
                c.start()

        @pl.when(first_ref[i] == 1)
        def _():
            s = slot_ref[i]
            for c in copies(e_ref[i], s):
                c.wait()

            @pl.when(next_ref[i] >= 0)
            def _():
                for c in copies(next_ref[i], 1 - s):
                    c.start(priority=1)

    @pl.when(i < nused_ref[0])
    def _():
        stream(ea_ref, fa_ref, sa_ref, na_ref, (wga, wua, wda), sem_a)
        stream(eb_ref, fb_ref, sb_ref, nb_ref, (wgb, wub, wdb), sem_b)
        x = _unpack_bf16_pairs(pltpu.bitcast(x_ref[:, :HALF], U32)).astype(BF16)
        gates = pltpu.bitcast(x_ref[:, HALF:], F32)

        def mlp(wg_ref, wu_ref, wd_ref, s):
            hg = jnp.dot(x, wg_ref[s], preferred_element_type=F32)
            hu = jnp.dot(x, wu_ref[s], preferred_element_type=F32)
            return jnp.dot((_silu(hg) * hu).astype(BF16), wd_ref[s], preferred_element_type=F32)

        y = mlp(wga, wua, wda, sa_ref[i]) * gates[:, 0:1] + mlp(wgb, wub, wdb, sb_ref[i]) * gates[:, 1:2]
        y_ref[...] = pltpu.bitcast(_pack_bf16_pairs(y.astype(BF16).astype(F32)), jnp.int32)


def _weight_runs(e, nused):
    nblk = e.shape[0]
    j = jnp.arange(nblk, dtype=jnp.int32)
    first = jnp.concatenate([jnp.ones((1,), bool), e[1:] != e[:-1]]) & (j < nused[0])
    slot = (jnp.cumsum(first.astype(jnp.int32)) - 1) % 2
    nxt = jnp.min(jnp.where(first[None, :] & (j[None, :] > j[:, None]), j[None, :], nblk), axis=1)
    nxt_e = jnp.where(nxt < nblk, e[jnp.minimum(nxt, nblk - 1)], -1)
    return first.astype(jnp.int32), slot.astype(jnp.int32), nxt_e.astype(jnp.int32)


def _experts(xs, blk_ea, blk_eb, nused, w_gate, w_up, w_down, bm):
    nblk = blk_ea.shape[0]
    row = lambda j, nu, *_: (jnp.minimum(j, nu[0] - 1), 0)
    row_out = lambda j, nu, *_: (jnp.where(j < nu[0], j, nblk - 1), 0)
    anyspec = pl.BlockSpec(memory_space=pl.ANY)
    up = pltpu.VMEM((2, D_MODEL, D_EXPERT), BF16)
    down = pltpu.VMEM((2, D_EXPERT, D_MODEL), BF16)
    return pl.pallas_call(
        _experts_kernel,
        grid_spec=pltpu.PrefetchScalarGridSpec(
            num_scalar_prefetch=9,
            grid=(nblk,),
            in_specs=[pl.BlockSpec((bm, ROW_WORDS), row), anyspec, anyspec, anyspec],
            out_specs=pl.BlockSpec((bm, HALF), row_out),
            scratch_shapes=[up, up, down, up, up, down, pltpu.SemaphoreType.DMA((2,)), pltpu.SemaphoreType.DMA((2,))]),
        out_shape=jax.ShapeDtypeStruct((nblk * bm, HALF), jnp.int32),
        compiler_params=_cparams(("arbitrary",)),
    )(nused, blk_ea, *_weight_runs(blk_ea, nused), blk_eb, *_weight_runs(blk_eb, nused), xs, w_gate, w_up, w_down)


def _add_rows_kernel(x_ref, yg_ref, o_ref):
    o_ref[...] = x_ref[...] + _unpack_bf16_pairs(pltpu.bitcast(yg_ref[...], U32))


def _add_inproj_kernel(x2_ref, yg_ref, g_ref, w_ref, x_ref, z_ref, xbc_ref, qkv_ref, u_ref, dt_ref):
    _add_rows_kernel(x2_ref, yg_ref, x_ref)
    _inproj_kernel(x_ref, g_ref, w_ref, z_ref, xbc_ref, qkv_ref, u_ref, dt_ref)


def _combine(x2d, yg, gain=None, w_cat=None):
    n = x2d.shape[0]
    tm = TOKEN_TILE
    tile = lambda w: pl.BlockSpec((tm, w), lambda i: (i, 0))
    if w_cat is None:
        return pl.pallas_call(
            _add_rows_kernel, grid=(n // tm,), in_specs=[tile(D_MODEL), tile(HALF)], out_specs=tile(D_MODEL),
            out_shape=jax.ShapeDtypeStruct((n, D_MODEL), F32), compiler_params=_cparams(("arbitrary",)))(x2d, yg)
    widths = (D_MODEL,) + _INPROJ_WIDTHS
    dtypes = (F32,) + _INPROJ_DTYPES
    return pl.pallas_call(
        _add_inproj_kernel, grid=(n // tm,),
        in_specs=[tile(D_MODEL), tile(HALF), _full((1, D_MODEL)), _full(w_cat.shape)],
        out_specs=[tile(w) for w in widths],
        out_shape=[jax.ShapeDtypeStruct((n, w), d) for w, d in zip(widths, dtypes)],
        compiler_params=_cparams(("arbitrary",)),
    )(x2d, yg, gain.reshape(1, D_MODEL), w_cat)


_PAIR_A = np.array([a for a in range(EXPERTS_PER_GROUP) for _ in range(a + 1, EXPERTS_PER_GROUP)], np.int32)
_PAIR_B = np.array([b for a in range(EXPERTS_PER_GROUP) for b in range(a + 1, EXPERTS_PER_GROUP)], np.int32)


def _expert_block(n):
    return 2 * EXPERT_BLOCK_MIN if n >= EXPERT_BLOCK_MIN * N_COMBOS else EXPERT_BLOCK_MIN


def _moe(x2, hf, meta, counts, w_gate, w_up, w_down):
    b, t, _ = x2.shape
    n = b * t
    bm = _expert_block(n)
    nblk = (n + N_COMBOS * (bm - 1) + bm - 1) // bm
    cnt = counts[0, :N_COMBOS].astype(jnp.int32)
    psz = (cnt + bm - 1) // bm * bm
    pend = jnp.cumsum(psz)
    pstart = (pend - psz).astype(jnp.int32)
    nused = jnp.maximum(pend[-1] // bm, 1).astype(jnp.int32).reshape(1)
    blk = jnp.minimum(jnp.arange(nblk, dtype=jnp.int32), nused[0] - 1)
    blk_c = jnp.minimum(jnp.sum(pend[None, :] <= (blk * bm)[:, None], axis=1), N_COMBOS - 1).astype(jnp.int32)
    grp = blk_c // PAIRS_PER_GROUP
    blk_ea = (grp * EXPERTS_PER_GROUP + jnp.asarray(_PAIR_A)[blk_c % PAIRS_PER_GROUP]).astype(jnp.int32)
    blk_eb = (grp * EXPERTS_PER_GROUP + jnp.asarray(_PAIR_B)[blk_c % PAIRS_PER_GROUP]).astype(jnp.int32)
    ids = meta.astype(jnp.int32)
    combo = ids[:, 0, :].reshape(n)
    dest = ids[:, 1, :].reshape(n) + jnp.sum(jnp.where(combo[:, None] == jnp.arange(N_COMBOS)[None, :], pstart[None, :], 0),
                                             axis=1)
    xs = _sc_rows("scatter", hf.reshape(n, ROW_WORDS), dest, nblk * bm)
    xs = _zero_pad_rows(xs, pstart, cnt, bm)
    y = _experts(xs, blk_ea, blk_eb, nused, w_gate, w_up, w_down, bm)
    yg = _sc_rows("gather", y, dest, n)
    return x2.reshape(n, D_MODEL), yg


def _prep_layer(lp):
    w_in = lp["w_in"]
    c0 = D_INNER + CONV_CH
    c1 = c0 + 2 * SSD_HEADS
    w_cat = jnp.concatenate([w_in[:, :c0], w_in[:, c1:], w_in[:, c0:c1],
                             jnp.zeros((D_MODEL, DT_PAD - 2 * SSD_HEADS), w_in.dtype)], axis=1).astype(BF16)
    pool_bd = jnp.zeros((POOL_WIDTH, POOL_WIDTH), F32)
    for g in range(POOL_GROUPS):
        sl = slice(g * POOL_GROUP_DIM, (g + 1) * POOL_GROUP_DIM)
        pool_bd = pool_bd.at[sl, sl].set(lp["pool_w"][g].astype(F32))
    w_r = jnp.concatenate([lp["w_router_expert"], lp["w_router_group"],
                           jnp.zeros((D_MODEL, LANES - N_EXPERTS - N_EXPERT_GROUPS), F32)], axis=1).astype(F32)
    w_r_hi = w_r.astype(BF16)
    w_r_lo = (w_r - w_r_hi.astype(F32)).astype(BF16)
    b_r = jnp.concatenate([lp["b_router_expert"], lp["b_router_group"],
                           jnp.zeros((LANES - N_EXPERTS - N_EXPERT_GROUPS,), F32)]).reshape(1, LANES).astype(F32)
    row = lambda a, w: a.reshape(1, w).astype(F32)
    return dict(
        norm_mix=lp["norm_mix"], w_cat=w_cat,
        conv_w=lp["conv_w"], conv_b=lp["conv_b"], dt_bias=lp["dt_bias"], a_log=lp["a_log"],
        d_skip=lp["d_skip"], ssd_norm=lp["ssd_norm"],
        na_q_norm=lp["na_q_norm"], na_k_norm=lp["na_k_norm"], na_rpb=lp["na_rpb"],
        pool_bd=pool_bd.astype(BF16), pool_scale=row(lp["pool_scale"], POOL_WIDTH),
        w_out=lp["w_out"].astype(BF16), norm_xa=row(lp["norm_xa"], D_MODEL),
        norm_mem=lp["norm_mem"], w_xq=lp["w_xq"].astype(BF16), w_xkv=lp["w_xkv"].astype(BF16),
        xq_norm=row(jnp.tile(lp["xq_norm"], XA_HEADS), XA_WIDTH), xk_norm=lp["xk_norm"],
        w_xo=lp["w_xo"].astype(BF16), norm_ffn=row(lp["norm_ffn"], D_MODEL),
        w_r_both=jnp.concatenate([w_r_hi, w_r_lo], axis=1), b_r=b_r,
        w_e_gate=lp["w_e_gate"].astype(BF16), w_e_up=lp["w_e_up"].astype(BF16),
        w_e_down=lp["w_e_down"].astype(BF16),
    )


def _layer(x, pending, mem, p, na_bias):
    b, m, _ = mem.shape
    if pending is None:
        t = x.shape[1]
        z, xbc, qkv, u, dt = _inproj(x.reshape(b * t, D_MODEL), p["norm_mix"], p["w_cat"])
    else:
        t = pending[0].shape[0] // b
        x, z, xbc, qkv, u, dt = _combine(*pending, p["norm_mix"], p["w_cat"])
        x = x.reshape(b, t, D_MODEL)
    r3 = lambda a: a.reshape(b, t, a.shape[-1])
    y_ssd = _ssd(r3(z), r3(xbc), r3(dt), p["conv_w"], p["conv_b"], p["dt_bias"], p["a_log"], p["d_skip"],
                 p["ssd_norm"])
    y_na = _natten(r3(qkv), na_bias, p["na_q_norm"], p["na_k_norm"])
    kmem, vmem = _memkv(mem, p["norm_mem"], p["w_xkv"], p["xk_norm"])
    x2, hf, meta, counts = _mixer(x, y_ssd, y_na, r3(u), kmem, vmem, p)
    return _moe(x2, hf, meta, counts, p["w_e_gate"], p["w_e_up"], p["w_e_down"])


_LAYER_KEYS = ("norm_mix", "w_in", "conv_w", "conv_b", "dt_bias", "a_log", "d_skip", "ssd_norm", "na_q_norm",
               "na_k_norm", "na_rpb", "pool_w", "pool_scale", "w_out", "norm_xa", "norm_mem", "w_xq", "w_xkv",
               "xq_norm", "xk_norm", "w_xo", "norm_ffn", "w_router_group", "b_router_group", "w_router_expert",
               "b_router_expert", "w_e_gate", "w_e_up", "w_e_down")


def kernel(x_prompt, x_sample, mem_prompt, mem_sample, norm_mix, w_in, conv_w, conv_b, dt_bias, a_log, d_skip, ssd_norm, na_q_norm, na_k_norm, na_rpb, pool_w, pool_scale, w_out, norm_xa, norm_mem, w_xq, w_xkv, xq_norm, xk_norm, w_xo, norm_ffn, w_router_group, b_router_group, w_router_expert, b_router_expert, w_e_gate, w_e_up, w_e_down):
    stacked = dict(zip(_LAYER_KEYS, (norm_mix, w_in, conv_w, conv_b, dt_bias, a_log, d_skip, ssd_norm, na_q_norm,
                                     na_k_norm, na_rpb, pool_w, pool_scale, w_out, norm_xa, norm_mem, w_xq, w_xkv,
                                     xq_norm, xk_norm, w_xo, norm_ffn, w_router_group, b_router_group,
                                     w_router_expert, b_router_expert, w_e_gate, w_e_up, w_e_down)))
    depth = w_in.shape[0]
    layers = [_prep_layer({k: v[l] for k, v in stacked.items()}) for l in range(depth)]

    bias_cache = {}

    def trunk(x, mem):
        t = x.shape[1]
        pending = None
        for l, lp in enumerate(layers):
            if (l, t) not in bias_cache:
                bias_cache[(l, t)] = _na_bias(lp["na_rpb"], t)
            pending = _layer(x, pending, mem, lp, bias_cache[(l, t)])
            x = None
        return _combine(*pending).reshape(mem.shape[0], t, D_MODEL)

    return trunk(x_prompt, mem_prompt), trunk(x_sample, mem_sample)
```
